```python
import math
import jax, jax.numpy as jnp
from jax import lax
import numpy as np

D_MODEL = 1024
BATCH = 8
SEQ = 16384
DEPTH = 1

CHUNK = 64
D_A = D_MODEL
D_B = D_MODEL
GROUP = 64
CONV_A_WIDTH = 31
CONV_B_WIDTH = 3
D_FF = 4 * D_MODEL
N_ADA = 6
D_IN = 2 * D_A + 3 * D_B + 2 * D_MODEL
LN_EPS = 1e-5
DEEPNORM_ALPHA = (2.0 * DEPTH) ** 0.25
DEEPNORM_BETA = (8.0 * DEPTH) ** -0.25

kernel_name = "hybrid_conformer_shortconv_gated_block"


def _layernorm(x, g=None, b=None):
    xf = x.astype(jnp.float32)
    mu = jnp.mean(xf, axis=-1, keepdims=True)
    var = jnp.mean(jnp.square(xf - mu), axis=-1, keepdims=True)
    y = (xf - mu) * lax.rsqrt(var + LN_EPS)
    if g is not None:
        y = y * g.astype(jnp.float32) + b.astype(jnp.float32)
    return y.astype(x.dtype)


def _causal_depthwise_conv(u, w):
    k = w.shape[0]
    u_pad = jnp.pad(u, ((0, 0), (k - 1, 0), (0, 0)))
    return lax.conv_general_dilated(
        u_pad, w[:, None, :].astype(u.dtype), window_strides=(1,), padding="VALID",
        dimension_numbers=("NWC", "WIO", "NWC"), feature_group_count=u.shape[-1])


def _fwd_setup_inputs(seed: int = 0) -> dict:
    key = jax.random.key(seed)
    ks = jax.random.split(key, 24)
    f32 = jnp.float32
    L, D = DEPTH, D_MODEL

    def nrm(k, shape, scale):
        return jax.random.normal(k, shape, f32) * scale

    return {
        "x": nrm(ks[0], (BATCH, SEQ, D), 1.0),
        "c": nrm(ks[1], (BATCH, D), 1.0),
        "w_ada": nrm(ks[2], (L, D, N_ADA * D), 0.2 * D ** -0.5),
        "b_ada": nrm(ks[3], (L, N_ADA * D), 0.02),
        "w_in": nrm(ks[4], (L, D, D_IN), D ** -0.5),
        "b_in": nrm(ks[5], (L, D_IN), 0.02),
        "conv_a_w": nrm(ks[6], (L, CONV_A_WIDTH, D_A), CONV_A_WIDTH ** -0.5),
        "conv_a_b": nrm(ks[7], (L, D_A), 0.02),
        "ln_a_g": 1.0 + nrm(ks[8], (L, D_A), 0.02),
        "ln_a_b": nrm(ks[9], (L, D_A), 0.02),
        "w_a_out": nrm(ks[10], (L, D_A, D), D_A ** -0.5),
        "b_a_out": nrm(ks[11], (L, D), 0.02),
        "conv_b_w": nrm(ks[12], (L, CONV_B_WIDTH, D_B), CONV_B_WIDTH ** -0.5),
        "w_b_out": nrm(ks[13], (L, D_B, D), D_B ** -0.5),
        "w_o": nrm(ks[14], (L, D, D), DEEPNORM_BETA * D ** -0.5),
        "b_o": nrm(ks[15], (L, D), 0.02),
        "ln1_g": 1.0 + nrm(ks[16], (L, D), 0.02),
        "ln1_b": nrm(ks[17], (L, D), 0.02),
        "w_up": nrm(ks[18], (L, D, D_FF), D ** -0.5),
        "b_up": nrm(ks[19], (L, D_FF), 0.02),
        "w_down": nrm(ks[20], (L, D_FF, D), DEEPNORM_BETA * D_FF ** -0.5),
        "b_down": nrm(ks[21], (L, D), 0.02),
        "ln2_g": 1.0 + nrm(ks[22], (L, D), 0.02),
        "ln2_b": nrm(ks[23], (L, D), 0.02),
    }


def _fwd_reference(x, c, w_ada, b_ada, w_in, b_in, conv_a_w, conv_a_b, ln_a_g, ln_a_b,
              w_a_out, b_a_out, conv_b_w, w_b_out, w_o, b_o, ln1_g, ln1_b,
              w_up, b_up, w_down, b_down, ln2_g, ln2_b):
    split_idx = np.cumsum([D_A, D_A, D_B, D_B, D_B, D_MODEL]).tolist()
    c_act = jax.nn.silu(c)
    for l in range(DEPTH):
        mod = c_act @ w_ada[l] + b_ada[l]
        shift1, scale1, gate1, shift2, scale2, gate2 = [
            m[:, None, :] for m in jnp.split(mod, N_ADA, axis=-1)]

        h = _layernorm(x) * (1.0 + scale1) + shift1
        z = jnp.einsum("bsd,de->bse", h, w_in[l]) + b_in[l]
        a_val, a_gate, b_gb, b_gc, b_x, g_a, g_b = jnp.split(z, split_idx, axis=-1)

        u = a_val * jax.nn.sigmoid(a_gate)
        u = _causal_depthwise_conv(u, conv_a_w[l]) + conv_a_b[l]
        u = jax.nn.silu(_layernorm(u, ln_a_g[l], ln_a_b[l]))
        y_a = jnp.einsum("bsc,cd->bsd", u, w_a_out[l]) + b_a_out[l]

        v = b_gb * _causal_depthwise_conv(b_gc * b_x, conv_b_w[l])
        y_b = jnp.einsum("bsc,cd->bsd", v, w_b_out[l])

        merged = jax.nn.sigmoid(g_a) * y_a + jax.nn.sigmoid(g_b) * y_b
        out = jnp.einsum("bsd,de->bse", merged, w_o[l]) + b_o[l]
        x = _layernorm(DEEPNORM_ALPHA * x + (1.0 + gate1) * out, ln1_g[l], ln1_b[l])

        h = _layernorm(x) * (1.0 + scale2) + shift2
        f = jnp.square(jax.nn.relu(jnp.einsum("bsd,df->bsf", h, w_up[l]) + b_up[l]))
        out = jnp.einsum("bsf,fd->bsd", f, w_down[l]) + b_down[l]
        x = _layernorm(DEEPNORM_ALPHA * x + (1.0 + gate2) * out, ln2_g[l], ln2_b[l])
    return x


import jax as _jax
import jax.numpy as _jnp

TWIN_FORMAT = 'train_step'
FWD_PARAMS = ['x', 'c', 'w_ada', 'b_ada', 'w_in', 'b_in', 'conv_a_w', 'conv_a_b', 'ln_a_g', 'ln_a_b', 'w_a_out', 'b_a_out', 'conv_b_w', 'w_b_out', 'w_o', 'b_o', 'ln1_g', 'ln1_b', 'w_up', 'b_up', 'w_down', 'b_down', 'ln2_g', 'ln2_b']
TWIN_WEIGHTS = ['w_ada', 'b_ada', 'w_in', 'b_in', 'conv_a_w', 'conv_a_b', 'ln_a_g', 'ln_a_b', 'w_a_out', 'b_a_out', 'conv_b_w', 'w_b_out', 'w_o', 'b_o', 'ln1_g', 'ln1_b', 'w_up', 'b_up', 'w_down', 'b_down', 'ln2_g', 'ln2_b']
TWIN_DIFF_INPUT = 'x'
TWIN_INPUTS = ['x', 'c', 'w_ada', 'b_ada', 'w_in', 'b_in', 'conv_a_w', 'conv_a_b', 'ln_a_g', 'ln_a_b', 'w_a_out', 'b_a_out', 'conv_b_w', 'w_b_out', 'w_o', 'b_o', 'ln1_g', 'ln1_b', 'w_up', 'b_up', 'w_down', 'b_down', 'ln2_g', 'ln2_b', 'loss_target', 'm_w_ada', 'm_b_ada', 'm_w_in', 'm_b_in', 'm_conv_a_w', 'm_conv_a_b', 'm_ln_a_g', 'm_ln_a_b', 'm_w_a_out', 'm_b_a_out', 'm_conv_b_w', 'm_w_b_out', 'm_w_o', 'm_b_o', 'm_ln1_g', 'm_ln1_b', 'm_w_up', 'm_b_up', 'm_w_down', 'm_b_down', 'm_ln2_g', 'm_ln2_b', 'v_w_ada', 'v_b_ada', 'v_w_in', 'v_b_in', 'v_conv_a_w', 'v_conv_a_b', 'v_ln_a_g', 'v_ln_a_b', 'v_w_a_out', 'v_b_a_out', 'v_conv_b_w', 'v_w_b_out', 'v_w_o', 'v_b_o', 'v_ln1_g', 'v_ln1_b', 'v_w_up', 'v_b_up', 'v_w_down', 'v_b_down', 'v_ln2_g', 'v_ln2_b']
TWIN_OUTPUTS = ['loss', 'grad_x', 'grad_w_ada', 'grad_b_ada', 'grad_w_in', 'grad_b_in', 'grad_conv_a_w', 'grad_conv_a_b', 'grad_ln_a_g', 'grad_ln_a_b', 'grad_w_a_out', 'grad_b_a_out', 'grad_conv_b_w', 'grad_w_b_out', 'grad_w_o', 'grad_b_o', 'grad_ln1_g', 'grad_ln1_b', 'grad_w_up', 'grad_b_up', 'grad_w_down', 'grad_b_down', 'grad_ln2_g', 'grad_ln2_b', 'delta_w_ada', 'delta_b_ada', 'delta_w_in', 'delta_b_in', 'delta_conv_a_w', 'delta_conv_a_b', 'delta_ln_a_g', 'delta_ln_a_b', 'delta_w_a_out', 'delta_b_a_out', 'delta_conv_b_w', 'delta_w_b_out', 'delta_w_o', 'delta_b_o', 'delta_ln1_g', 'delta_ln1_b', 'delta_w_up', 'delta_b_up', 'delta_w_down', 'delta_b_down', 'delta_ln2_g', 'delta_ln2_b', 'new_m_w_ada', 'new_m_b_ada', 'new_m_w_in', 'new_m_b_in', 'new_m_conv_a_w', 'new_m_conv_a_b', 'new_m_ln_a_g', 'new_m_ln_a_b', 'new_m_w_a_out', 'new_m_b_a_out', 'new_m_conv_b_w', 'new_m_w_b_out', 'new_m_w_o', 'new_m_b_o', 'new_m_ln1_g', 'new_m_ln1_b', 'new_m_w_up', 'new_m_b_up', 'new_m_w_down', 'new_m_b_down', 'new_m_ln2_g', 'new_m_ln2_b', 'new_v_w_ada', 'new_v_b_ada', 'new_v_w_in', 'new_v_b_in', 'new_v_conv_a_w', 'new_v_conv_a_b', 'new_v_ln_a_g', 'new_v_ln_a_b', 'new_v_w_a_out', 'new_v_b_a_out', 'new_v_conv_b_w', 'new_v_w_b_out', 'new_v_w_o', 'new_v_b_o', 'new_v_ln1_g', 'new_v_ln1_b', 'new_v_w_up', 'new_v_b_up', 'new_v_w_down', 'new_v_b_down', 'new_v_ln2_g', 'new_v_ln2_b']
TWIN_LEAF_KINDS = {'loss': 'loss', 'grad_x': 'grad_x', 'grad_w_ada': 'grad_w', 'grad_b_ada': 'grad_w', 'grad_w_in': 'grad_w', 'grad_b_in': 'grad_w', 'grad_conv_a_w': 'grad_w', 'grad_conv_a_b': 'grad_w', 'grad_ln_a_g': 'grad_w', 'grad_ln_a_b': 'grad_w', 'grad_w_a_out': 'grad_w', 'grad_b_a_out': 'grad_w', 'grad_conv_b_w': 'grad_w', 'grad_w_b_out': 'grad_w', 'grad_w_o': 'grad_w', 'grad_b_o': 'grad_w', 'grad_ln1_g': 'grad_w', 'grad_ln1_b': 'grad_w', 'grad_w_up': 'grad_w', 'grad_b_up': 'grad_w', 'grad_w_down': 'grad_w', 'grad_b_down': 'grad_w', 'grad_ln2_g': 'grad_w', 'grad_ln2_b': 'grad_w', 'delta_w_ada': 'delta_w', 'delta_b_ada': 'delta_w', 'delta_w_in': 'delta_w', 'delta_b_in': 'delta_w', 'delta_conv_a_w': 'delta_w', 'delta_conv_a_b': 'delta_w', 'delta_ln_a_g': 'delta_w', 'delta_ln_a_b': 'delta_w', 'delta_w_a_out': 'delta_w', 'delta_b_a_out': 'delta_w', 'delta_conv_b_w': 'delta_w', 'delta_w_b_out': 'delta_w', 'delta_w_o': 'delta_w', 'delta_b_o': 'delta_w', 'delta_ln1_g': 'delta_w', 'delta_ln1_b': 'delta_w', 'delta_w_up': 'delta_w', 'delta_b_up': 'delta_w', 'delta_w_down': 'delta_w', 'delta_b_down': 'delta_w', 'delta_ln2_g': 'delta_w', 'delta_ln2_b': 'delta_w', 'new_m_w_ada': 'new_m', 'new_m_b_ada': 'new_m', 'new_m_w_in': 'new_m', 'new_m_b_in': 'new_m', 'new_m_conv_a_w': 'new_m', 'new_m_conv_a_b': 'new_m', 'new_m_ln_a_g': 'new_m', 'new_m_ln_a_b': 'new_m', 'new_m_w_a_out': 'new_m', 'new_m_b_a_out': 'new_m', 'new_m_conv_b_w': 'new_m', 'new_m_w_b_out': 'new_m', 'new_m_w_o': 'new_m', 'new_m_b_o': 'new_m', 'new_m_ln1_g': 'new_m', 'new_m_ln1_b': 'new_m', 'new_m_w_up': 'new_m', 'new_m_b_up': 'new_m', 'new_m_w_down': 'new_m', 'new_m_b_down': 'new_m', 'new_m_ln2_g': 'new_m', 'new_m_ln2_b': 'new_m', 'new_v_w_ada': 'new_v', 'new_v_b_ada': 'new_v', 'new_v_w_in': 'new_v', 'new_v_b_in': 'new_v', 'new_v_conv_a_w': 'new_v', 'new_v_conv_a_b': 'new_v', 'new_v_ln_a_g': 'new_v', 'new_v_ln_a_b': 'new_v', 'new_v_w_a_out': 'new_v', 'new_v_b_a_out': 'new_v', 'new_v_conv_b_w': 'new_v', 'new_v_w_b_out': 'new_v', 'new_v_w_o': 'new_v', 'new_v_b_o': 'new_v', 'new_v_ln1_g': 'new_v', 'new_v_ln1_b': 'new_v', 'new_v_w_up': 'new_v', 'new_v_b_up': 'new_v', 'new_v_w_down': 'new_v', 'new_v_b_down': 'new_v', 'new_v_ln2_g': 'new_v', 'new_v_ln2_b': 'new_v'}


def _forward(args):
    return _fwd_reference(*[args[k] for k in FWD_PARAMS])


def _output_shape():
    def fwd():
        inp = _fwd_setup_inputs(0)
        return _fwd_reference(*[inp[k] for k in FWD_PARAMS])
    out = _jax.eval_shape(fwd)
    return out.shape, out.dtype

N_MICROBATCH = 1
ADAM_LR = 0.001
ADAM_B1 = 0.9
ADAM_B2 = 0.999
ADAM_EPS = 1e-08
ADAM_WD = 0.01
ADAM_STEP = 10
PER_EXAMPLE_BATCH_AXIS = {'x': 0, 'c': 0, 'loss_target': 0}
SHARED_INPUTS = []
_WEIGHT_DTYPES = {'w_ada': _jnp.float32, 'b_ada': _jnp.float32, 'w_in': _jnp.float32, 'b_in': _jnp.float32, 'conv_a_w': _jnp.float32, 'conv_a_b': _jnp.float32, 'ln_a_g': _jnp.float32, 'ln_a_b': _jnp.float32, 'w_a_out': _jnp.float32, 'b_a_out': _jnp.float32, 'conv_b_w': _jnp.float32, 'w_b_out': _jnp.float32, 'w_o': _jnp.float32, 'b_o': _jnp.float32, 'ln1_g': _jnp.float32, 'ln1_b': _jnp.float32, 'w_up': _jnp.float32, 'b_up': _jnp.float32, 'w_down': _jnp.float32, 'b_down': _jnp.float32, 'ln2_g': _jnp.float32, 'ln2_b': _jnp.float32}
MOMENT_SCALE = {'w_ada': 1.495373e-01, 'b_ada': 4.602021e-01, 'w_in': 7.178978e-02, 'b_in': 8.288334e-02, 'conv_a_w': 6.281137e-02, 'conv_a_b': 2.836384e-01, 'ln_a_g': 1.134234e-01, 'ln_a_b': 1.883946e-01, 'w_a_out': 8.332386e-02, 'b_a_out': 4.261474e-01, 'conv_b_w': 1.016625e-01, 'w_b_out': 9.934052e-02, 'w_o': 2.131733e-01, 'b_o': 1.478891e+00, 'ln1_g': 2.839806e+00, 'ln1_b': 1.823008e+00, 'w_up': 1.117124e-01, 'b_up': 3.080471e-01, 'w_down': 6.121642e-01, 'b_down': 1.480668e+00, 'ln2_g': 1.285460e+02, 'ln2_b': 2.856600e+01}


def _to_microbatches(a, axis):
    t = _jnp.moveaxis(a, axis, 0)
    t = t.reshape((N_MICROBATCH, t.shape[0] // N_MICROBATCH) + t.shape[1:])
    return _jnp.moveaxis(t, 1, axis + 1)


def setup_inputs(seed: int = 0) -> dict:
    inp = _fwd_setup_inputs(seed)
    key = _jax.random.fold_in(_jax.random.key(seed), 7919)
    shape, _ = _output_shape()
    out = dict(inp)
    out["loss_target"] = _jax.random.normal(_jax.random.fold_in(key, 0), shape, _jnp.float32)
    for i, name in enumerate(TWIN_WEIGHTS):
        w = inp[name].astype(_jnp.float32)
        if MOMENT_SCALE is None:
            s = _jnp.sqrt(_jnp.mean(_jnp.square(w)) + 1e-30)
        else:
            s = MOMENT_SCALE[name]
        km, kv = _jax.random.split(_jax.random.fold_in(key, i + 1))
        out[name] = w
        out["m_" + name] = s * _jax.random.normal(km, w.shape, _jnp.float32)
        out["v_" + name] = (s * s) * _jax.random.uniform(kv, w.shape, _jnp.float32, 0.5, 1.5)
    if N_MICROBATCH > 1:
        for name, axis in PER_EXAMPLE_BATCH_AXIS.items():
            out[name] = _to_microbatches(out[name], axis)
    return {'x': out['x'], 'c': out['c'], 'w_ada': out['w_ada'], 'b_ada': out['b_ada'], 'w_in': out['w_in'], 'b_in': out['b_in'], 'conv_a_w': out['conv_a_w'], 'conv_a_b': out['conv_a_b'], 'ln_a_g': out['ln_a_g'], 'ln_a_b': out['ln_a_b'], 'w_a_out': out['w_a_out'], 'b_a_out': out['b_a_out'], 'conv_b_w': out['conv_b_w'], 'w_b_out': out['w_b_out'], 'w_o': out['w_o'], 'b_o': out['b_o'], 'ln1_g': out['ln1_g'], 'ln1_b': out['ln1_b'], 'w_up': out['w_up'], 'b_up': out['b_up'], 'w_down': out['w_down'], 'b_down': out['b_down'], 'ln2_g': out['ln2_g'], 'ln2_b': out['ln2_b'], 'loss_target': out['loss_target'], 'm_w_ada': out['m_w_ada'], 'm_b_ada': out['m_b_ada'], 'm_w_in': out['m_w_in'], 'm_b_in': out['m_b_in'], 'm_conv_a_w': out['m_conv_a_w'], 'm_conv_a_b': out['m_conv_a_b'], 'm_ln_a_g': out['m_ln_a_g'], 'm_ln_a_b': out['m_ln_a_b'], 'm_w_a_out': out['m_w_a_out'], 'm_b_a_out': out['m_b_a_out'], 'm_conv_b_w': out['m_conv_b_w'], 'm_w_b_out': out['m_w_b_out'], 'm_w_o': out['m_w_o'], 'm_b_o': out['m_b_o'], 'm_ln1_g': out['m_ln1_g'], 'm_ln1_b': out['m_ln1_b'], 'm_w_up': out['m_w_up'], 'm_b_up': out['m_b_up'], 'm_w_down': out['m_w_down'], 'm_b_down': out['m_b_down'], 'm_ln2_g': out['m_ln2_g'], 'm_ln2_b': out['m_ln2_b'], 'v_w_ada': out['v_w_ada'], 'v_b_ada': out['v_b_ada'], 'v_w_in': out['v_w_in'], 'v_b_in': out['v_b_in'], 'v_conv_a_w': out['v_conv_a_w'], 'v_conv_a_b': out['v_conv_a_b'], 'v_ln_a_g': out['v_ln_a_g'], 'v_ln_a_b': out['v_ln_a_b'], 'v_w_a_out': out['v_w_a_out'], 'v_b_a_out': out['v_b_a_out'], 'v_conv_b_w': out['v_conv_b_w'], 'v_w_b_out': out['v_w_b_out'], 'v_w_o': out['v_w_o'], 'v_b_o': out['v_b_o'], 'v_ln1_g': out['v_ln1_g'], 'v_ln1_b': out['v_ln1_b'], 'v_w_up': out['v_w_up'], 'v_b_up': out['v_b_up'], 'v_w_down': out['v_w_down'], 'v_b_down': out['v_b_down'], 'v_ln2_g': out['v_ln2_g'], 'v_ln2_b': out['v_ln2_b']}


def _loss(weights, diff, rest, loss_target):
    with _jax.named_scope("forward"):
        args = {**rest, TWIN_DIFF_INPUT: diff, **{k: w.astype(_WEIGHT_DTYPES[k]) for k, w in weights.items()}}
        y = _forward(args)
    with _jax.named_scope("loss_head"):
        err = _jnp.square(y.astype(_jnp.float32) - loss_target)
        return 0.5 * _jnp.sum(_jnp.mean(err, axis=-1)) if err.ndim else 0.5 * err


def _adamw(w, g, m, v):
    m = ADAM_B1 * m + (1.0 - ADAM_B1) * g
    v = ADAM_B2 * v + (1.0 - ADAM_B2) * _jnp.square(g)
    m_hat = m / (1.0 - ADAM_B1 ** ADAM_STEP)
    v_hat = v / (1.0 - ADAM_B2 ** ADAM_STEP)
    delta = -ADAM_LR * (m_hat / (_jnp.sqrt(v_hat) + ADAM_EPS) + ADAM_WD * w)
    return delta, m, v


def reference(x, c, w_ada, b_ada, w_in, b_in, conv_a_w, conv_a_b, ln_a_g, ln_a_b, w_a_out, b_a_out, conv_b_w, w_b_out, w_o, b_o, ln1_g, ln1_b, w_up, b_up, w_down, b_down, ln2_g, ln2_b, loss_target, m_w_ada, m_b_ada, m_w_in, m_b_in, m_conv_a_w, m_conv_a_b, m_ln_a_g, m_ln_a_b, m_w_a_out, m_b_a_out, m_conv_b_w, m_w_b_out, m_w_o, m_b_o, m_ln1_g, m_ln1_b, m_w_up, m_b_up, m_w_down, m_b_down, m_ln2_g, m_ln2_b, v_w_ada, v_b_ada, v_w_in, v_b_in, v_conv_a_w, v_conv_a_b, v_ln_a_g, v_ln_a_b, v_w_a_out, v_b_a_out, v_conv_b_w, v_w_b_out, v_w_o, v_b_o, v_ln1_g, v_ln1_b, v_w_up, v_b_up, v_w_down, v_b_down, v_ln2_g, v_ln2_b):
    given = dict(x=x, c=c, w_ada=w_ada, b_ada=b_ada, w_in=w_in, b_in=b_in, conv_a_w=conv_a_w, conv_a_b=conv_a_b, ln_a_g=ln_a_g, ln_a_b=ln_a_b, w_a_out=w_a_out, b_a_out=b_a_out, conv_b_w=conv_b_w, w_b_out=w_b_out, w_o=w_o, b_o=b_o, ln1_g=ln1_g, ln1_b=ln1_b, w_up=w_up, b_up=b_up, w_down=w_down, b_down=b_down, ln2_g=ln2_g, ln2_b=ln2_b, loss_target=loss_target, m_w_ada=m_w_ada, m_b_ada=m_b_ada, m_w_in=m_w_in, m_b_in=m_b_in, m_conv_a_w=m_conv_a_w, m_conv_a_b=m_conv_a_b, m_ln_a_g=m_ln_a_g, m_ln_a_b=m_ln_a_b, m_w_a_out=m_w_a_out, m_b_a_out=m_b_a_out, m_conv_b_w=m_conv_b_w, m_w_b_out=m_w_b_out, m_w_o=m_w_o, m_b_o=m_b_o, m_ln1_g=m_ln1_g, m_ln1_b=m_ln1_b, m_w_up=m_w_up, m_b_up=m_b_up, m_w_down=m_w_down, m_b_down=m_b_down, m_ln2_g=m_ln2_g, m_ln2_b=m_ln2_b, v_w_ada=v_w_ada, v_b_ada=v_b_ada, v_w_in=v_w_in, v_b_in=v_b_in, v_conv_a_w=v_conv_a_w, v_conv_a_b=v_conv_a_b, v_ln_a_g=v_ln_a_g, v_ln_a_b=v_ln_a_b, v_w_a_out=v_w_a_out, v_b_a_out=v_b_a_out, v_conv_b_w=v_conv_b_w, v_w_b_out=v_w_b_out, v_w_o=v_w_o, v_b_o=v_b_o, v_ln1_g=v_ln1_g, v_ln1_b=v_ln1_b, v_w_up=v_w_up, v_b_up=v_b_up, v_w_down=v_w_down, v_b_down=v_b_down, v_ln2_g=v_ln2_g, v_ln2_b=v_ln2_b)
    weights = {n: given[n] for n in TWIN_WEIGHTS}
    shared = {n: given[n] for n in SHARED_INPUTS}
    per_example = {n: given[n] for n in ['x', 'c']}
    grad_fn = _jax.value_and_grad(_loss, argnums=(0, 1))

    def one_microbatch(ex, loss_target):
        ex = dict(ex)
        diff = ex.pop(TWIN_DIFF_INPUT)
        return grad_fn(weights, diff, {**shared, **ex}, loss_target)

    if N_MICROBATCH == 1:
        loss, (grad_w, grad_x) = one_microbatch(per_example, given["loss_target"])
    else:
        def body(carry, xs):
            loss_sum, grad_sum = carry
            l_k, (gw_k, gx_k) = one_microbatch(xs[0], xs[1])
            with _jax.named_scope("update"):
                return (loss_sum + l_k, _jax.tree.map(_jnp.add, grad_sum, gw_k)), gx_k

        init = (_jnp.zeros((), _jnp.float32), _jax.tree.map(_jnp.zeros_like, weights))
        (loss, grad_w), grad_x = _jax.lax.scan(body, init, (per_example, given["loss_target"]))
    with _jax.named_scope("update"):
        delta_w, new_m, new_v = {}, {}, {}
        for n in TWIN_WEIGHTS:
            delta_w[n], new_m[n], new_v[n] = _adamw(weights[n], grad_w[n], given["m_" + n], given["v_" + n])
    return (loss, grad_x, *[grad_w[n] for n in TWIN_WEIGHTS], *[delta_w[n] for n in TWIN_WEIGHTS],
            *[new_m[n] for n in TWIN_WEIGHTS], *[new_v[n] for n in TWIN_WEIGHTS])
```

```python
import functools

import jax
import jax.numpy as jnp
from jax import lax
from jax.experimental import pallas as pl
from jax.experimental.pallas import tpu as pltpu

F32 = jnp.float32
BF16 = jnp.bfloat16
MESH = pl.DeviceIdType.MESH

N_DEV = 8
LN_EPS = 1e-5
DEPTH = 1
ALPHA = (2.0 * DEPTH) ** 0.25
ADAM_LR, ADAM_B1, ADAM_B2, ADAM_EPS, ADAM_WD, ADAM_STEP = 0.001, 0.9, 0.999, 1e-08, 0.01, 10

VMEM_LIMIT = 60 * 1024 * 1024
TOKEN_TILE = 256
DW_TILE = 1024
HALO_A = 32
HALO_B = 8
CONV_ROWS, CONV_LANES = 32, 512


def _cparams(n_grid):
    return pltpu.CompilerParams(dimension_semantics=("arbitrary",) * n_grid, vmem_limit_bytes=VMEM_LIMIT)


def _full(shape):
    return pl.BlockSpec(shape, lambda *_: (0,) * len(shape))


ANY = pl.BlockSpec(memory_space=pl.ANY)


def _ln(x):
    mu = jnp.mean(x, axis=-1, keepdims=True)
    xc = x - mu
    var = jnp.mean(xc * xc, axis=-1, keepdims=True)
    rstd = lax.rsqrt(var + LN_EPS)
    return xc * rstd, rstd


def _ln_bwd(dxhat, xhat, rstd):
    m1 = jnp.mean(dxhat, axis=-1, keepdims=True)
    m2 = jnp.mean(dxhat * xhat, axis=-1, keepdims=True)
    return rstd * (dxhat - m1 - xhat * m2)


def _sigmoid(x):
    return 1.0 / (1.0 + jnp.exp(-x))


def _colsum(a):
    return jnp.sum(a, axis=0, keepdims=True)


def _dot(a, b):
    return jnp.dot(a, b, preferred_element_type=F32)


def _dot_nt(a, b):
    return lax.dot_general(a, b, (((1,), (1,)), ((), ())), preferred_element_type=F32)


def _dot_tn(a, b):
    return lax.dot_general(a, b, (((0,), (0,)), ((), ())), preferred_element_type=F32)


def _load_cols(src_hbm, dst_vmem, sems):
    nblk, _, w = src_hbm.shape
    cps = [pltpu.make_async_copy(src_hbm.at[j], dst_vmem.at[:, pl.ds(j * w, w)], sems.at[j]) for j in range(nblk)]
    for cp in cps:
        cp.start()
    for cp in cps:
        cp.wait()


def _load_whole(pairs, sems):
    cps = [pltpu.make_async_copy(s, d, sems.at[k]) for k, (s, d) in enumerate(pairs)]
    for cp in cps:
        cp.start()
    for cp in cps:
        cp.wait()


def _mesh_pos():
    return lax.axis_index("x"), lax.axis_index("y"), lax.axis_index("c")


def _all_gather(arrs, name):
    n = len(arrs)

    def body(*refs):
        ins, outs = refs[:n], refs[n:2 * n]
        send_sems, recv_sems, local_sems = refs[2 * n:]
        x, y, c = _mesh_pos()
        me, sibling = (x, y, c), (x, y, 1 - c)
        chips = [(1 - x, y), (x, 1 - y), (1 - x, 1 - y)]

        def slot(a, px, py, pc):
            return outs[a].at[4 * px + 2 * py + pc]

        def copy(a, k, block, to, src=None):
            return pltpu.make_async_remote_copy(
                src_ref=slot(a, *block) if src is None else src, dst_ref=slot(a, *block),
                send_sem=send_sems.at[a, k], recv_sem=recv_sems.at[a, k], device_id=to, device_id_type=MESH)

        mine = [pltpu.make_async_copy(ins[a], slot(a, *me), local_sems.at[a]) for a in range(n)]
        for cp in mine:
            cp.start()
        first = []
        for a in range(n):
            first.append(copy(a, 0, me, sibling, src=ins[a]))
            first += [copy(a, 1 + j, me, (*chip, c), src=ins[a]) for j, chip in enumerate(chips)]
        for cp in first:
            cp.start()
        passed = []
        for a in range(n):
            for j, chip in enumerate(chips):
                copy(a, 1 + j, (*chip, c), me).wait_recv()
                fwd = copy(a, 4 + j, (*chip, c), sibling)
                fwd.start()
                passed.append(fwd)
        for a in range(n):
            copy(a, 0, sibling, me).wait_recv()
            for j, chip in enumerate(chips):
                copy(a, 4 + j, (*chip, 1 - c), me).wait_recv()
        for cp in first + passed:
            cp.wait_send()
        for cp in mine:
            cp.wait()

    outs = pl.pallas_call(
        body, name=name,
        out_shape=[jax.ShapeDtypeStruct((N_DEV,) + a.shape, a.dtype) for a in arrs],
        in_specs=[ANY] * n, out_specs=[ANY] * n,
        scratch_shapes=[pltpu.SemaphoreType.DMA((n, 7)), pltpu.SemaphoreType.DMA((n, 7)), pltpu.SemaphoreType.DMA((n,))],
    )(*arrs)
    return list(outs)


def _shard_of(ref, kind, j):
    if kind == "col":
        w = ref.shape[1] // N_DEV
        return ref.at[:, pl.ds(j * w, w)]
    h = ref.shape[0] // N_DEV
    return ref.at[pl.ds(j * h, h), :]


def _shard_shape(shape, kind):
    return (shape[0], shape[1] // N_DEV) if kind == "col" else (shape[0] // N_DEV, shape[1])


def _exchange_cores(grads, kinds, name):
    n = len(grads)

    def body(*refs):
        ins, outs = refs[:n], refs[n:2 * n]
        send_sems, recv_sems, local_sems = refs[2 * n:]
        x, y, c = _mesh_pos()
        sibling = (x, y, 1 - c)
        local, remote = [], []
        for a in range(n):
            for s in range(4):
                px, py = s // 2, s % 2
                local.append(pltpu.make_async_copy(
                    _shard_of(ins[a], kinds[a], 4 * px + 2 * py + c), outs[a].at[c, s], local_sems.at[a, s]))
                remote.append(pltpu.make_async_remote_copy(
                    src_ref=_shard_of(ins[a], kinds[a], 4 * px + 2 * py + (1 - c)), dst_ref=outs[a].at[c, s],
                    send_sem=send_sems.at[a, s], recv_sem=recv_sems.at[a, s], device_id=sibling, device_id_type=MESH))
        for cp in remote + local:
            cp.start()
        for a in range(n):
            for s in range(4):
                pltpu.make_async_remote_copy(
                    src_ref=outs[a].at[1 - c, s], dst_ref=outs[a].at[1 - c, s],
                    send_sem=send_sems.at[a, s], recv_sem=recv_sems.at[a, s], device_id=sibling, device_id_type=MESH).wait_recv()
        for cp in remote:
            cp.wait_send()
        for cp in local:
            cp.wait()

    outs = pl.pallas_call(
        body, name=name,
        out_shape=[jax.ShapeDtypeStruct((2, 4) + _shard_shape(g.shape, k), F32) for g, k in zip(grads, kinds)],
        in_specs=[ANY] * n, out_specs=[ANY] * n,
        scratch_shapes=[pltpu.SemaphoreType.DMA((n, 4)), pltpu.SemaphoreType.DMA((n, 4)), pltpu.SemaphoreType.DMA((n, 4))],
    )(*grads)
    return list(outs)


def _exchange_chips(parts, name):
    n = len(parts)

    def body(*refs):
        ins, outs = refs[:n], refs[n:2 * n]
        send_sems, recv_sems, local_sems = refs[2 * n:]
        x, y, c = _mesh_pos()
        my_slot = 2 * x + y
        chips = [(1 - x, y), (x, 1 - y), (1 - x, 1 - y)]
        local = [pltpu.make_async_copy(ins[a].at[my_slot], outs[a].at[my_slot], local_sems.at[a]) for a in range(n)]
        sends = []
        for a in range(n):
            for j, (px, py) in enumerate(chips):
                sends.append(pltpu.make_async_remote_copy(
                    src_ref=ins[a].at[2 * px + py], dst_ref=outs[a].at[my_slot],
                    send_sem=send_sems.at[a, j], recv_sem=recv_sems.at[a, j], device_id=(px, py, c), device_id_type=MESH))
        for cp in sends + local:
            cp.start()
        for a in range(n):
            for j, (px, py) in enumerate(chips):
                pltpu.make_async_remote_copy(
                    src_ref=outs[a].at[2 * px + py], dst_ref=outs[a].at[2 * px + py],
                    send_sem=send_sems.at[a, j], recv_sem=recv_sems.at[a, j], device_id=(px, py, c), device_id_type=MESH).wait_recv()
        for cp in sends:
            cp.wait_send()
        for cp in local:
            cp.wait()

    outs = pl.pallas_call(
        body, name=name,
        out_shape=[jax.ShapeDtypeStruct(p.shape, p.dtype) for p in parts],
        in_specs=[ANY] * n, out_specs=[ANY] * n,
        scratch_shapes=[pltpu.SemaphoreType.DMA((n, 3)), pltpu.SemaphoreType.DMA((n, 3)), pltpu.SemaphoreType.DMA((n,))],
    )(*parts)
    return list(outs)


def _row_tile(rows):
    for t in (256, 128, 64, 32, 16, 8):
        if rows % t == 0:
            return t
    return rows


def _add_planes(p, name):
    _, ns, R, C = p.shape
    tr = _row_tile(R)

    def body(p_ref, o_ref):
        o_ref[...] = p_ref[0] + p_ref[1]

    return pl.pallas_call(
        body, name=name, grid=(ns, R // tr),
        in_specs=[pl.BlockSpec((2, None, tr, C), lambda s, r: (0, s, r, 0))],
        out_specs=pl.BlockSpec((None, tr, C), lambda s, r: (s, r, 0)),
        out_shape=jax.ShapeDtypeStruct((ns, R, C), F32), compiler_params=_cparams(2),
    )(p)


def _adamw_math(w, g, m, v):
    m2 = ADAM_B1 * m + (1.0 - ADAM_B1) * g
    v2 = ADAM_B2 * v + (1.0 - ADAM_B2) * (g * g)
    m_hat = m2 / (1.0 - ADAM_B1 ** ADAM_STEP)
    v_hat = v2 / (1.0 - ADAM_B2 ** ADAM_STEP)
    delta = -ADAM_LR * (m_hat / (jnp.sqrt(v_hat) + ADAM_EPS) + ADAM_WD * w)
    return delta, m2, v2


def _sum_adamw(parts, w, m, v, name):
    n, R, C = parts.shape
    tr = _row_tile(R)

    def body(p_ref, w_ref, m_ref, v_ref, g_ref, d_ref, m_out, v_out):
        g = p_ref[0]
        for k in range(1, n):
            g = g + p_ref[k]
        g_ref[...] = g
        d_ref[...], m_out[...], v_out[...] = _adamw_math(w_ref[...], g, m_ref[...], v_ref[...])

    blk = pl.BlockSpec((tr, C), lambda r: (r, 0))
    return pl.pallas_call(
        body, name=name, grid=(R // tr,),
        in_specs=[pl.BlockSpec((n, tr, C), lambda r: (0, r, 0)), blk, blk, blk],
        out_specs=[blk] * 4, out_shape=[jax.ShapeDtypeStruct((R, C), F32)] * 4, compiler_params=_cparams(1),
    )(parts, w, m, v)


def _ada_fwd(c, w_ada_g, b_ada):
    nb, D, W = w_ada_g.shape

    def body(c_ref, w_ref, b_ref, o_ref):
        cv = c_ref[...]
        ca = jnp.broadcast_to(cv * _sigmoid(cv), (8, D)).astype(BF16)
        o_ref[...] = _dot(ca, w_ref[...])[0:1, :] + b_ref[...]

    return pl.pallas_call(
        body, name="ada_fwd", grid=(nb,),
        in_specs=[_full((1, D)), pl.BlockSpec((None, D, W), lambda j: (j, 0, 0)), pl.BlockSpec((1, W), lambda j: (0, j))],
        out_specs=pl.BlockSpec((1, W), lambda j: (0, j)),
        out_shape=jax.ShapeDtypeStruct((1, nb * W), F32), compiler_params=_cparams(1),
    )(c, w_ada_g, b_ada)


def _ada_bwd(me, small_g, D, W, c_block):
    def body(me_ref, c_ref, dm_ref, o_ref):
        cv = c_ref[...]
        ca = (cv * _sigmoid(cv)).astype(BF16).astype(F32)
        dm = dm_ref[...].astype(BF16).astype(F32)
        o_ref[...] = lax.dot_general(ca, dm, (((0,), (0,)), ((), ())), precision=lax.Precision.HIGHEST,
                                     preferred_element_type=F32)

    return pl.pallas_call(
        body, name="ada_bwd",
        grid_spec=pltpu.PrefetchScalarGridSpec(
            num_scalar_prefetch=1, grid=(1,),
            in_specs=[pl.BlockSpec((N_DEV, D), lambda i, me_ref: (0, c_block)),
                      pl.BlockSpec((N_DEV, W), lambda i, me_ref: (0, me_ref[0]))],
            out_specs=pl.BlockSpec((D, W), lambda i, me_ref: (0, 0))),
        out_shape=jax.ShapeDtypeStruct((D, W), F32), compiler_params=_cparams(1),
    )(me, small_g, small_g)


def _fwd_in(x, mod, w_in_g, b_in):
    S, D = x.shape
    nb, _, wb = w_in_g.shape
    N = nb * wb
    TM = TOKEN_TILE

    def body(x_ref, mod_ref, w_hbm, b_ref, z_ref, h_ref, w_v, sems):
        @pl.when(pl.program_id(0) == 0)
        def _():
            _load_cols(w_hbm, w_v, sems)

        xhat, _ = _ln(x_ref[...])
        hb = (xhat * (1.0 + mod_ref[:, D:2 * D]) + mod_ref[:, 0:D]).astype(BF16)
        h_ref[...] = hb
        z_ref[...] = _dot(hb, w_v[...]) + b_ref[...]

    return pl.pallas_call(
        body, name="fwd_in", grid=(S // TM,),
        in_specs=[pl.BlockSpec((TM, D), lambda i: (i, 0)), _full(mod.shape), ANY, _full((1, N))],
        out_specs=[pl.BlockSpec((TM, N), lambda i: (i, 0)), pl.BlockSpec((TM, D), lambda i: (i, 0))],
        out_shape=[jax.ShapeDtypeStruct((S, N), F32), jax.ShapeDtypeStruct((S, D), BF16)],
        scratch_shapes=[pltpu.VMEM((D, N), BF16), pltpu.SemaphoreType.DMA((nb,))],
        compiler_params=_cparams(1),
    )(x, mod, w_in_g, b_in)


def _causal_taps(buf, w_ref, o_ref, K, halo, TM, D):
    R, C = CONV_ROWS, CONV_LANES
    base = halo - (K - 1)

    def step(j, carry):
        r0 = pl.multiple_of(j * R, R)
        for c0 in range(0, D, C):
            win = buf[pl.ds(r0, R + halo), c0:c0 + C]
            acc = jnp.zeros((R, C), F32)
            for k in range(K):
                acc = acc + w_ref[k:k + 1, c0:c0 + C] * win[base + k:base + k + R, :]
            o_ref[pl.ds(r0, R), c0:c0 + C] = acc
        return carry

    lax.fori_loop(0, TM // R, step, 0)


def _fwd_mix(z, x, mod, wa, ba, lag, lab, w_a_out, b_a_out, wb, w_b_out, w_o, b_o):
    S, D = x.shape
    N = z.shape[1]
    TM = TOKEN_TILE
    KA, KB = wa.shape[0], wb.shape[0]

    def body(z_ref, x_ref, mod_ref, wa_ref, ba_ref, lag_ref, lab_ref, wao_hbm, bao_ref, wb_ref, wbo_hbm, wo_hbm, bo_ref,
             u1_ref, ya_ref, yb_ref, q_ref, out_ref, r1_ref, u2_ref, v_ref, mg_ref,
             ubuf, pbuf, wao, wbo, wo, sems):
        i = pl.program_id(0)

        @pl.when(i == 0)
        def _():
            _load_whole([(wao_hbm, wao), (wbo_hbm, wbo), (wo_hbm, wo)], sems)
            ubuf[0:HALO_A, :] = jnp.zeros((HALO_A, D), F32)
            pbuf[0:HALO_B, :] = jnp.zeros((HALO_B, D), F32)

        ubuf[HALO_A:HALO_A + TM, :] = z_ref[:, 0:D] * _sigmoid(z_ref[:, D:2 * D])
        _causal_taps(ubuf, wa_ref, u1_ref, KA, HALO_A, TM, D)
        ubuf[0:HALO_A, :] = ubuf[TM:TM + HALO_A, :]
        u1 = u1_ref[...] + ba_ref[...]
        u1_ref[...] = u1
        xa, _ = _ln(u1)
        l = xa * lag_ref[...] + lab_ref[...]
        u2 = (l * _sigmoid(l)).astype(BF16)
        u2_ref[...] = u2
        ya = _dot(u2, wao[...]) + bao_ref[...]
        ya_ref[...] = ya
        pbuf[HALO_B:HALO_B + TM, :] = z_ref[:, 3 * D:4 * D] * z_ref[:, 4 * D:5 * D]
        _causal_taps(pbuf, wb_ref, q_ref, KB, HALO_B, TM, D)
        pbuf[0:HALO_B, :] = pbuf[TM:TM + HALO_B, :]
        v = (z_ref[:, 2 * D:3 * D] * q_ref[...]).astype(BF16)
        v_ref[...] = v
        yb = _dot(v, wbo[...])
        yb_ref[...] = yb
        mg = (_sigmoid(z_ref[:, 5 * D:6 * D]) * ya + _sigmoid(z_ref[:, 6 * D:7 * D]) * yb).astype(BF16)
        mg_ref[...] = mg
        out = _dot(mg, wo[...]) + bo_ref[...]
        out_ref[...] = out
        r1_ref[...] = ALPHA * x_ref[...] + (1.0 + mod_ref[:, 2 * D:3 * D]) * out

    tile = pl.BlockSpec((TM, D), lambda i: (i, 0))
    vec = _full((1, D))
    return pl.pallas_call(
        body, name="fwd_mix", grid=(S // TM,),
        in_specs=[pl.BlockSpec((TM, N), lambda i: (i, 0)), tile, _full(mod.shape), _full((KA, D)), vec, vec, vec, ANY, vec,
                  _full((KB, D)), ANY, ANY, vec],
        out_specs=[tile] * 9,
        out_shape=[jax.ShapeDtypeStruct((S, D), F32)] * 6 + [jax.ShapeDtypeStruct((S, D), BF16)] * 3,
        scratch_shapes=[pltpu.VMEM((TM + HALO_A, D), F32), pltpu.VMEM((TM + HALO_B, D), F32),
                        pltpu.VMEM((D, D), BF16), pltpu.VMEM((D, D), BF16), pltpu.VMEM((D, D), BF16),
                        pltpu.SemaphoreType.DMA((3,))],
        compiler_params=_cparams(1),
    )(z, x, mod, wa, ba, lag, lab, w_a_out, b_a_out, wb, w_b_out, w_o, b_o)


FFN_ROWS = ("b_down", "ln2_g", "ln2_b", "gate2", "scale2", "shift2", "loss")


def _ffn(r1, tgt, mod, ln1_g, ln1_b, w_up_g, b_up, w_down, b_down, ln2_g, ln2_b):
    S, D = r1.shape
    nb, _, wb = w_up_g.shape
    FF = nb * wb
    TM = TOKEN_TILE
    CF = 1024
    n_tiles = S // TM

    def body(r1_ref, t_ref, mod_ref, g1_ref, b1_ref, wup_hbm, bup_ref, wdn_hbm, bdn_ref, g2_ref, b2_ref,
             dx1_ref, h2_ref, f_ref, dhu_ref, do2_ref, acc_ref, accup_ref,
             wup, wdn, relu_buf, sems, sem2):
        i = pl.program_id(0)

        @pl.when(i == 0)
        def _():
            _load_cols(wup_hbm, wup, sems)
            _load_whole([(wdn_hbm, wdn)], sem2)
            acc_ref[...] = jnp.zeros(acc_ref.shape, F32)
            accup_ref[...] = jnp.zeros(accup_ref.shape, F32)

        scale2, shift2, gate2 = mod_ref[:, 4 * D:5 * D], mod_ref[:, 3 * D:4 * D], mod_ref[:, 5 * D:6 * D]
        xhat1, _ = _ln(r1_ref[...])
        x1 = xhat1 * g1_ref[...] + b1_ref[...]
        xh0, rstd0 = _ln(x1)
        h2 = (xh0 * (1.0 + scale2) + shift2).astype(BF16)
        h2_ref[...] = h2
        out2 = jnp.zeros((TM, D), F32) + bdn_ref[...]
        for c0 in range(0, FF, CF):
            hu = _dot(h2, wup[:, c0:c0 + CF]) + bup_ref[:, c0:c0 + CF]
            rl = jnp.maximum(hu, 0.0)
            relu_buf[:, c0:c0 + CF] = rl
            fb = (rl * rl).astype(BF16)
            f_ref[:, c0:c0 + CF] = fb
            out2 = out2 + _dot(fb, wdn[c0:c0 + CF, :])
        r2 = ALPHA * x1 + (1.0 + gate2) * out2
        xh2, rstd2 = _ln(r2)
        e = xh2 * g2_ref[...] + b2_ref[...] - t_ref[...]
        acc_ref[6:7, :] += _colsum(e * e)
        dy = e * (1.0 / D)
        acc_ref[1:2, :] += _colsum(dy * xh2)
        acc_ref[2:3, :] += _colsum(dy)
        dr2 = _ln_bwd(dy * g2_ref[...], xh2, rstd2)
        acc_ref[3:4, :] += _colsum(dr2 * out2)
        do2 = (1.0 + gate2) * dr2
        acc_ref[0:1, :] += _colsum(do2)
        do2b = do2.astype(BF16)
        do2_ref[...] = do2b
        dh2 = jnp.zeros((TM, D), F32)
        for c0 in range(0, FF, CF):
            dhu = _dot_nt(do2b, wdn[c0:c0 + CF, :]) * (2.0 * relu_buf[:, c0:c0 + CF])
            accup_ref[:, c0:c0 + CF] += _colsum(dhu)
            dhub = dhu.astype(BF16)
            dhu_ref[:, c0:c0 + CF] = dhub
            dh2 = dh2 + _dot_nt(dhub, wup[:, c0:c0 + CF])
        acc_ref[4:5, :] += _colsum(dh2 * xh0)
        acc_ref[5:6, :] += _colsum(dh2)
        dx1_ref[...] = ALPHA * dr2 + _ln_bwd(dh2 * (1.0 + scale2), xh0, rstd0)

        @pl.when(i == n_tiles - 1)
        def _():
            tot = jnp.sum(acc_ref[6:7, :], axis=-1, keepdims=True) * (0.5 / D)
            acc_ref[6:7, :] = jnp.broadcast_to(tot, (1, D))

    tile = pl.BlockSpec((TM, D), lambda i: (i, 0))
    wide = pl.BlockSpec((TM, FF), lambda i: (i, 0))
    vec = _full((1, D))
    return pl.pallas_call(
        body, name="ffn", grid=(n_tiles,),
        in_specs=[tile, tile, _full(mod.shape), vec, vec, ANY, _full((1, FF)), ANY, vec, vec, vec],
        out_specs=[tile, tile, wide, wide, tile, _full((8, D)), _full((1, FF))],
        out_shape=[jax.ShapeDtypeStruct((S, D), F32), jax.ShapeDtypeStruct((S, D), BF16), jax.ShapeDtypeStruct((S, FF), BF16),
                   jax.ShapeDtypeStruct((S, FF), BF16), jax.ShapeDtypeStruct((S, D), BF16),
                   jax.ShapeDtypeStruct((8, D), F32), jax.ShapeDtypeStruct((1, FF), F32)],
        scratch_shapes=[pltpu.VMEM((D, FF), BF16), pltpu.VMEM((FF, D), BF16), pltpu.VMEM((TM, FF), F32),
                        pltpu.SemaphoreType.DMA((nb,)), pltpu.SemaphoreType.DMA((1,))],
        compiler_params=_cparams(1),
    )(r1, tgt, mod, ln1_g, ln1_b, w_up_g, b_up, w_down, b_down, ln2_g, ln2_b)


def _dw(a, b, name, msplit=1, nsplit=1):
    S, M = a.shape
    N = b.shape[1]
    TK = min(DW_TILE, S)
    mb, nbk = M // msplit, N // nsplit

    def body(a_ref, b_ref, o_ref):
        @pl.when(pl.program_id(2) == 0)
        def _():
            o_ref[...] = jnp.zeros(o_ref.shape, F32)

        o_ref[...] += _dot_tn(a_ref[...], b_ref[...])

    return pl.pallas_call(
        body, name=name, grid=(msplit, nsplit, S // TK),
        in_specs=[pl.BlockSpec((TK, mb), lambda i, j, k: (k, i)), pl.BlockSpec((TK, nbk), lambda i, j, k: (k, j))],
        out_specs=pl.BlockSpec((mb, nbk), lambda i, j, k: (i, j)),
        out_shape=jax.ShapeDtypeStruct((M, N), F32), compiler_params=_cparams(3),
    )(a, b)


BWD1_ROWS = ("ln1_g", "ln1_b", "gate1", "b_o", "b_a_out", "ln_a_g", "ln_a_b", "conv_a_b", "b_in_gb", "b_in_ga", "b_in_gbb")


def _bwd_mix1(dx1, r1, out, ya, yb, q, u1, z, mod, ln1_g, lag, lab, w_o, w_a_out, w_b_out):
    S, D = dx1.shape
    TM = TOKEN_TILE

    def body(dx1_ref, r1_ref, out_ref, ya_ref, yb_ref, q_ref, u1_ref, zgb_ref, zga_ref, zgg_ref, mod_ref, g1_ref, lag_ref, lab_ref,
             wo_hbm, wao_hbm, wbo_hbm,
             dxp_ref, du1_ref, dq_ref, dzc_ref, dout_ref, dya_ref, dyb_ref, acc_ref,
             wo, wao, wbo, sems):
        @pl.when(pl.program_id(0) == 0)
        def _():
            _load_whole([(wo_hbm, wo), (wao_hbm, wao), (wbo_hbm, wbo)], sems)
            acc_ref[...] = jnp.zeros(acc_ref.shape, F32)

        dx1v = dx1_ref[...]
        xhat1, rstd1 = _ln(r1_ref[...])
        acc_ref[0:1, :] += _colsum(dx1v * xhat1)
        acc_ref[1:2, :] += _colsum(dx1v)
        dr1 = _ln_bwd(dx1v * g1_ref[...], xhat1, rstd1)
        dxp_ref[...] = ALPHA * dr1
        acc_ref[2:3, :] += _colsum(dr1 * out_ref[...])
        dout = (1.0 + mod_ref[:, 2 * D:3 * D]) * dr1
        acc_ref[3:4, :] += _colsum(dout)
        doutb = dout.astype(BF16)
        dout_ref[...] = doutb
        dmg = _dot_nt(doutb, wo[...])
        sga, sgb = _sigmoid(zga_ref[...]), _sigmoid(zgg_ref[...])
        dga = dmg * ya_ref[...] * sga * (1.0 - sga)
        dgb = dmg * yb_ref[...] * sgb * (1.0 - sgb)
        acc_ref[9:10, :] += _colsum(dga)
        acc_ref[10:11, :] += _colsum(dgb)
        dzc_ref[:, D:2 * D] = dga.astype(BF16)
        dzc_ref[:, 2 * D:3 * D] = dgb.astype(BF16)
        dya = dmg * sga
        acc_ref[4:5, :] += _colsum(dya)
        dyab = dya.astype(BF16)
        dya_ref[...] = dyab
        dybb = (dmg * sgb).astype(BF16)
        dyb_ref[...] = dybb
        du2 = _dot_nt(dyab, wao[...])
        xa, rstda = _ln(u1_ref[...])
        l = xa * lag_ref[...] + lab_ref[...]
        sl = _sigmoid(l)
        dl = du2 * (sl * (1.0 + l * (1.0 - sl)))
        acc_ref[5:6, :] += _colsum(dl * xa)
        acc_ref[6:7, :] += _colsum(dl)
        du1 = _ln_bwd(dl * lag_ref[...], xa, rstda)
        acc_ref[7:8, :] += _colsum(du1)
        du1_ref[...] = du1
        dv = _dot_nt(dybb, wbo[...])
        dgbk = dv * q_ref[...]
        acc_ref[8:9, :] += _colsum(dgbk)
        dzc_ref[:, 0:D] = dgbk.astype(BF16)
        dq_ref[...] = dv * zgb_ref[...]

    tile = pl.BlockSpec((TM, D), lambda i: (i, 0))
    vec = _full((1, D))
    zcol = lambda k: pl.BlockSpec((TM, D), lambda i: (i, k))
    return pl.pallas_call(
        body, name="bwd_mix1", grid=(S // TM,),
        in_specs=[tile] * 7 + [zcol(2), zcol(5), zcol(6), _full(mod.shape), vec, vec, vec, ANY, ANY, ANY],
        out_specs=[tile, tile, tile, pl.BlockSpec((TM, 3 * D), lambda i: (i, 0)), tile, tile, tile, _full((16, D))],
        out_shape=[jax.ShapeDtypeStruct((S, D), F32)] * 3 + [jax.ShapeDtypeStruct((S, 3 * D), BF16)]
        + [jax.ShapeDtypeStruct((S, D), BF16)] * 3 + [jax.ShapeDtypeStruct((16, D), F32)],
        scratch_shapes=[pltpu.VMEM((D, D), BF16)] * 3 + [pltpu.SemaphoreType.DMA((3,))],
        compiler_params=_cparams(1),
    )(dx1, r1, out, ya, yb, q, u1, z, z, z, mod, ln1_g, lag, lab, w_o, w_a_out, w_b_out)


def _anticausal_taps(dbuf, u_ref, w_ref, o_ref, dwacc, K, halo, TM, D):
    R, C = CONV_ROWS, CONV_LANES

    def step(jr, carry):
        r0 = pl.multiple_of(jr * R, R)
        for c0 in range(0, D, C):
            win = dbuf[pl.ds(r0, R + halo), c0:c0 + C]
            uc = u_ref[pl.ds(r0, R), c0:c0 + C]
            acc = jnp.zeros((R, C), F32)
            for j in range(K):
                sh = win[j:j + R, :]
                k = K - 1 - j
                acc = acc + w_ref[k:k + 1, c0:c0 + C] * sh
                pr = uc * sh
                part = pr[0:8, :]
                for s in range(8, R, 8):
                    part = part + pr[s:s + 8, :]
                dwacc[k, :, c0:c0 + C] += part
            o_ref[pl.ds(r0, R), c0:c0 + C] = acc
        return carry

    lax.fori_loop(0, TM // R, step, 0)


def _bwd_mix2(du1, dq, z, dzc, x, dxp, mod, wa, wb, w_in_g):
    S, D = x.shape
    N = z.shape[1]
    nb, _, wbk = w_in_g.shape
    TM = TOKEN_TILE
    KA, KB = wa.shape[0], wb.shape[0]
    n_tiles = S // TM

    def body(du1_ref, dq_ref, zav_ref, zc_ref, zx_ref, dzc_ref, x_ref, dxp_ref, mod_ref, wa_ref, wb_ref, win_hbm,
             dz_ref, gx_ref, dwa_ref, dwb_ref, acc_ref, dbin_ref,
             dbuf, qbuf, ubuf, pbuf, obuf, dwa_acc, dwb_acc, w_v, sems):
        i = pl.program_id(0)

        @pl.when(i == 0)
        def _():
            _load_cols(win_hbm, w_v, sems)
            dbuf[TM:TM + HALO_A, :] = jnp.zeros((HALO_A, D), F32)
            qbuf[TM:TM + HALO_B, :] = jnp.zeros((HALO_B, D), F32)
            dwa_acc[...] = jnp.zeros(dwa_acc.shape, F32)
            dwb_acc[...] = jnp.zeros(dwb_acc.shape, F32)
            acc_ref[...] = jnp.zeros(acc_ref.shape, F32)
            dbin_ref[...] = jnp.zeros(dbin_ref.shape, F32)

        a_val = zav_ref[:, 0:D]
        sa = _sigmoid(zav_ref[:, D:2 * D])
        ubuf[...] = a_val * sa
        dbuf[0:TM, :] = du1_ref[...]
        _anticausal_taps(dbuf, ubuf, wa_ref, obuf, dwa_acc, KA, HALO_A, TM, D)
        dbuf[TM:TM + HALO_A, :] = dbuf[0:HALO_A, :]
        du0 = obuf[...]
        dav = du0 * sa
        dag = du0 * a_val * sa * (1.0 - sa)
        dbin_ref[:, 0:D] += _colsum(dav)
        dbin_ref[:, D:2 * D] += _colsum(dag)
        dz_ref[:, 0:D] = dav.astype(BF16)
        dz_ref[:, D:2 * D] = dag.astype(BF16)
        pbuf[...] = zc_ref[...] * zx_ref[...]
        qbuf[0:TM, :] = dq_ref[...]
        _anticausal_taps(qbuf, pbuf, wb_ref, obuf, dwb_acc, KB, HALO_B, TM, D)
        qbuf[TM:TM + HALO_B, :] = qbuf[0:HALO_B, :]
        dp = obuf[...]
        dgc = dp * zx_ref[...]
        dgx = dp * zc_ref[...]
        dbin_ref[:, 3 * D:4 * D] += _colsum(dgc)
        dbin_ref[:, 4 * D:5 * D] += _colsum(dgx)
        dz_ref[:, 3 * D:4 * D] = dgc.astype(BF16)
        dz_ref[:, 4 * D:5 * D] = dgx.astype(BF16)
        dz_ref[:, 2 * D:3 * D] = dzc_ref[:, 0:D]
        dz_ref[:, 5 * D:7 * D] = dzc_ref[:, D:3 * D]
        dh = _dot_nt(dz_ref[...], w_v[...])
        xhat, rstd = _ln(x_ref[...])
        acc_ref[0:1, :] += _colsum(dh * xhat)
        acc_ref[1:2, :] += _colsum(dh)
        gx_ref[...] = dxp_ref[...] + _ln_bwd(dh * (1.0 + mod_ref[:, D:2 * D]), xhat, rstd)

        @pl.when(i == n_tiles - 1)
        def _():
            for k in range(KA):
                dwa_ref[k:k + 1, :] = jnp.sum(dwa_acc[k], axis=0, keepdims=True)
            for k in range(KB):
                dwb_ref[k:k + 1, :] = jnp.sum(dwb_acc[k], axis=0, keepdims=True)

    rev = lambda i: n_tiles - 1 - i
    tile = pl.BlockSpec((TM, D), lambda i: (rev(i), 0))
    zcol = lambda k: pl.BlockSpec((TM, D), lambda i: (rev(i), k))
    return pl.pallas_call(
        body, name="bwd_mix2", grid=(n_tiles,),
        in_specs=[tile, tile, pl.BlockSpec((TM, 2 * D), lambda i: (rev(i), 0)), zcol(3), zcol(4),
                  pl.BlockSpec((TM, 3 * D), lambda i: (rev(i), 0)), tile, tile, _full(mod.shape), _full((KA, D)), _full((KB, D)), ANY],
        out_specs=[pl.BlockSpec((TM, N), lambda i: (rev(i), 0)), tile, _full((KA, D)), _full((KB, D)), _full((8, D)), _full((1, N))],
        out_shape=[jax.ShapeDtypeStruct((S, N), BF16), jax.ShapeDtypeStruct((S, D), F32), jax.ShapeDtypeStruct((KA, D), F32),
                   jax.ShapeDtypeStruct((KB, D), F32), jax.ShapeDtypeStruct((8, D), F32), jax.ShapeDtypeStruct((1, N), F32)],
        scratch_shapes=[pltpu.VMEM((TM + HALO_A, D), F32), pltpu.VMEM((TM + HALO_B, D), F32), pltpu.VMEM((TM, D), F32),
                        pltpu.VMEM((TM, D), F32), pltpu.VMEM((TM, D), F32), pltpu.VMEM((KA, 8, D), F32), pltpu.VMEM((KB, 8, D), F32),
                        pltpu.VMEM((D, N), BF16), pltpu.SemaphoreType.DMA((nb,))],
        compiler_params=_cparams(1),
    )(du1, dq, z, z, z, dzc, x, dxp, mod, wa, wb, w_in_g)


def _local_grads(x, c, tgt, gw, vecs):
    w_ada_g, w_in_g, w_a_out, w_b_out, w_o, w_up_g, w_down, wa, wb = gw
    D = x.shape[1]
    mod = _ada_fwd(c, w_ada_g, vecs["b_ada"])
    z, h1 = _fwd_in(x, mod, w_in_g, vecs["b_in"])
    u1, ya, yb, q, out, r1, u2, v, mg = _fwd_mix(
        z, x, mod, wa, vecs["conv_a_b"], vecs["ln_a_g"], vecs["ln_a_b"], w_a_out, vecs["b_a_out"], wb, w_b_out, w_o, vecs["b_o"])
    dx1, h2, f, dhu, do2, acc_f, acc_up = _ffn(
        r1, tgt, mod, vecs["ln1_g"], vecs["ln1_b"], w_up_g, vecs["b_up"], w_down, vecs["b_down"], vecs["ln2_g"], vecs["ln2_b"])
    g_up = _dw(h2, dhu, "dw_up", nsplit=2)
    g_down = _dw(f, do2, "dw_down", msplit=2)
    dxp, du1, dq, dzc, doutb, dyab, dybb, acc_1 = _bwd_mix1(
        dx1, r1, out, ya, yb, q, u1, z, mod, vecs["ln1_g"], vecs["ln_a_g"], vecs["ln_a_b"], w_o, w_a_out, w_b_out)
    g_o = _dw(mg, doutb, "dw_o")
    g_a_out = _dw(u2, dyab, "dw_a_out")
    g_b_out = _dw(v, dybb, "dw_b_out")
    dz, gx, g_wa, g_wb, acc_2, db_in = _bwd_mix2(du1, dq, z, dzc, x, dxp, mod, wa, wb, w_in_g)
    g_in = _dw(h1, dz, "dw_in", nsplit=4)
    row = lambda acc, k: acc[k:k + 1, :]
    dmod = jnp.concatenate([row(acc_2, 1), row(acc_2, 0), row(acc_1, 2), row(acc_f, 5), row(acc_f, 4), row(acc_f, 3)], axis=1)
    db_in = jnp.concatenate([db_in[:, 0:2 * D], row(acc_1, 8), db_in[:, 3 * D:5 * D], row(acc_1, 9), row(acc_1, 10)], axis=1)
    gvec = {
        "b_ada": dmod, "b_in": db_in, "conv_a_b": row(acc_1, 7), "ln_a_g": row(acc_1, 5), "ln_a_b": row(acc_1, 6),
        "b_a_out": row(acc_1, 4), "b_o": row(acc_1, 3), "ln1_g": row(acc_1, 0), "ln1_b": row(acc_1, 1), "b_up": acc_up,
        "b_down": row(acc_f, 0), "ln2_g": row(acc_f, 1), "ln2_b": row(acc_f, 2),
    }
    loss = acc_f[6, 0]
    gmat = {"w_in": g_in, "w_a_out": g_a_out, "w_b_out": g_b_out, "w_o": g_o, "w_up": g_up, "w_down": g_down,
            "conv_a_w": g_wa, "conv_b_w": g_wb}
    return loss, gx, gmat, gvec


WEIGHTS = ("w_ada", "b_ada", "w_in", "b_in", "conv_a_w", "conv_a_b", "ln_a_g", "ln_a_b", "w_a_out", "b_a_out", "conv_b_w",
           "w_b_out", "w_o", "b_o", "ln1_g", "ln1_b", "w_up", "b_up", "w_down", "b_down", "ln2_g", "ln2_b")
VECTORS = ("b_ada", "b_in", "conv_a_b", "ln_a_g", "ln_a_b", "b_a_out", "b_o", "ln1_g", "ln1_b", "b_up", "b_down", "ln2_g", "ln2_b")
MATRICES = ("w_in", "w_a_out", "w_b_out", "w_o", "w_up", "w_down", "conv_a_w", "conv_b_w")
SHARD_KIND = {"w_in": "col", "w_a_out": "row", "w_b_out": "row", "w_o": "row", "w_up": "col", "w_down": "row",
              "conv_a_w": "col", "conv_b_w": "col"}


def kernel(x, c, w_ada, b_ada, w_in, b_in, conv_a_w, conv_a_b, ln_a_g, ln_a_b, w_a_out, b_a_out, conv_b_w, w_b_out, w_o, b_o, ln1_g, ln1_b, w_up, b_up, w_down, b_down, ln2_g, ln2_b, loss_target, m_w_ada, m_b_ada, m_w_in, m_b_in, m_conv_a_w, m_conv_a_b, m_ln_a_g, m_ln_a_b, m_w_a_out, m_b_a_out, m_conv_b_w, m_w_b_out, m_w_o, m_b_o, m_ln1_g, m_ln1_b, m_w_up, m_b_up, m_w_down, m_b_down, m_ln2_g, m_ln2_b, v_w_ada, v_b_ada, v_w_in, v_b_in, v_conv_a_w, v_conv_a_b, v_ln_a_g, v_ln_a_b, v_w_a_out, v_b_a_out, v_conv_b_w, v_w_b_out, v_w_o, v_b_o, v_ln1_g, v_ln1_b, v_w_up, v_b_up, v_w_down, v_b_down, v_ln2_g, v_ln2_b):
    args = dict(locals())
    w = {n: args[n][0] for n in WEIGHTS}
    m = {n: args["m_" + n][0] for n in WEIGHTS}
    v = {n: args["v_" + n][0] for n in WEIGHTS}
    w = {n: (a[None, :] if a.ndim == 1 else a) for n, a in w.items()}
    m = {n: (a[None, :] if a.ndim == 1 else a) for n, a in m.items()}
    v = {n: (a[None, :] if a.ndim == 1 else a) for n, a in v.items()}
    xs, tgt = x[0], loss_target[0]
    S, D = xs.shape
    me = (4 * lax.axis_index("x") + 2 * lax.axis_index("y") + lax.axis_index("c")).astype(jnp.int32).reshape(1)

    bf = lambda n: w[n].astype(BF16)
    g = _all_gather([bf("w_ada"), bf("w_in"), bf("w_a_out"), bf("w_b_out"), bf("w_o"), bf("w_up"), bf("w_down"),
                     w["conv_a_w"], w["conv_b_w"]], "gather_weights")
    w_ada_g, w_in_g, w_a_out_g, w_b_out_g, w_o_g, w_up_g, w_down_g, wa_g, wb_g = g
    rows = lambda a: a.reshape(a.shape[0] * a.shape[1], a.shape[2])
    cols = lambda a: jnp.transpose(a, (1, 0, 2)).reshape(a.shape[1], a.shape[0] * a.shape[2])
    gw = (w_ada_g, w_in_g, rows(w_a_out_g), rows(w_b_out_g), rows(w_o_g), w_up_g, rows(w_down_g), cols(wa_g), cols(wb_g))
    vecs = {n: w[n] for n in VECTORS}

    loss, gx, gmat, gvec = _local_grads(xs, c, tgt, gw, vecs)

    small = jnp.concatenate([gvec[n] for n in VECTORS] + [c], axis=1)
    (small_g,) = _all_gather([small], "gather_vectors")
    small_g = small_g.reshape(N_DEV, small.shape[1])
    planes = _exchange_cores([gmat[n] for n in MATRICES], [SHARD_KIND[n] for n in MATRICES], "reduce_cores")
    chip_sums = [_add_planes(p, "add_" + n) for p, n in zip(planes, MATRICES)]
    quarters = _exchange_chips(chip_sums, "reduce_chips")

    res = {}
    for n, parts in zip(MATRICES, quarters):
        res[n] = _sum_adamw(parts, w[n], m[n], v[n], "adamw_" + n)
    W = w["w_ada"].shape[1]
    n_vec = small.shape[1] - D
    g_ada = _ada_bwd(me, small_g, D, W, n_vec // D)
    res["w_ada"] = _sum_adamw(g_ada[None], w["w_ada"], m["w_ada"], v["w_ada"], "adamw_w_ada")
    cat = lambda d: jnp.concatenate([d[n] for n in VECTORS], axis=1)
    vg, vd, vm, vv = _sum_adamw(small_g[:, None, 0:n_vec], cat(w), cat(m), cat(v), "adamw_vectors")
    off = 0
    for n in VECTORS:
        k = w[n].shape[1]
        res[n] = tuple(a[:, off:off + k] for a in (vg, vd, vm, vv))
        off += k

    loss = lax.psum(loss, ("x", "y", "c"))
    outs = [loss, gx[None]]
    for t in range(4):
        outs += [res[n][t].reshape(args[n].shape) for n in WEIGHTS]
    return tuple(outs)
```

```python
import functools

import jax
import jax.numpy as jnp
from jax import lax
from jax.experimental import pallas as pl
from jax.experimental.pallas import tpu as pltpu

F32 = jnp.float32
BF16 = jnp.bfloat16
MESH = pl.DeviceIdType.MESH

N_DEV = 8
LN_EPS = 1e-5
DEPTH = 1
ALPHA = (2.0 * DEPTH) ** 0.25
ADAM_LR, ADAM_B1, ADAM_B2, ADAM_EPS, ADAM_WD, ADAM_STEP = 0.001, 0.9, 0.999, 1e-08, 0.01, 10

VMEM_LIMIT = 60 * 1024 * 1024
TOKEN_TILE = 256
DW_TILE = 1024
HALO_A = 32
HALO_B = 8
CONV_ROWS, CONV_LANES = 32, 512


def _cparams(n_grid):
    return pltpu.CompilerParams(dimension_semantics=("arbitrary",) * n_grid, vmem_limit_bytes=VMEM_LIMIT)


def _full(shape):
    return pl.BlockSpec(shape, lambda *_: (0,) * len(shape))


ANY = pl.BlockSpec(memory_space=pl.ANY)


def _ln(x):
    mu = jnp.mean(x, axis=-1, keepdims=True)
    xc = x - mu
    var = jnp.mean(xc * xc, axis=-1, keepdims=True)
    rstd = lax.rsqrt(var + LN_EPS)
    return xc * rstd, rstd


def _ln_bwd(dxhat, xhat, rstd):
    m1 = jnp.mean(dxhat, axis=-1, keepdims=True)
    m2 = jnp.mean(dxhat * xhat, axis=-1, keepdims=True)
    return rstd * (dxhat - m1 - xhat * m2)


def _sigmoid(x):
    return 1.0 / (1.0 + jnp.exp(-x))


def _colsum(a):
    return jnp.sum(a, axis=0, keepdims=True)


def _dot(a, b):
    return jnp.dot(a, b, preferred_element_type=F32)


def _dot_nt(a, b):
    return lax.dot_general(a, b, (((1,), (1,)), ((), ())), preferred_element_type=F32)


def _dot_tn(a, b):
    return lax.dot_general(a, b, (((0,), (0,)), ((), ())), preferred_element_type=F32)


def _load_cols(src_hbm, dst_vmem, sems):
    nblk, _, w = src_hbm.shape
    cps = [pltpu.make_async_copy(src_hbm.at[j], dst_vmem.at[:, pl.ds(j * w, w)], sems.at[j]) for j in range(nblk)]
    for cp in cps:
        cp.start()
    for cp in cps:
        cp.wait()


def _load_whole(pairs, sems):
    cps = [pltpu.make_async_copy(s, d, sems.at[k]) for k, (s, d) in enumerate(pairs)]
    for cp in cps:
        cp.start()
    for cp in cps:
        cp.wait()


def _mesh_pos():
    return lax.axis_index("x"), lax.axis_index("y"), lax.axis_index("c")


def _all_gather(arrs, name):
    n = len(arrs)

    def body(*refs):
        ins, outs = refs[:n], refs[n:2 * n]
        send_sems, recv_sems, local_sems = refs[2 * n:]
        x, y, c = _mesh_pos()
        me, sibling = (x, y, c), (x, y, 1 - c)
        chips = [(1 - x, y), (x, 1 - y), (1 - x, 1 - y)]

        def slot(a, px, py, pc):
            return outs[a].at[4 * px + 2 * py + pc]

        def copy(a, k, block, to, src=None):
            return pltpu.make_async_remote_copy(
                src_ref=slot(a, *block) if src is None else src, dst_ref=slot(a, *block),
                send_sem=send_sems.at[a, k], recv_sem=recv_sems.at[a, k], device_id=to, device_id_type=MESH)

        mine = [pltpu.make_async_copy(ins[a], slot(a, *me), local_sems.at[a]) for a in range(n)]
        for cp in mine:
            cp.start()
        first = []
        for a in range(n):
            first.append(copy(a, 0, me, sibling, src=ins[a]))
            first += [copy(a, 1 + j, me, (*chip, c), src=ins[a]) for j, chip in enumerate(chips)]
        for cp in first:
            cp.start()
        passed = []
        for a in range(n):
            for j, chip in enumerate(chips):
                copy(a, 1 + j, (*chip, c), me).wait_recv()
                fwd = copy(a, 4 + j, (*chip, c), sibling)
                fwd.start()
                passed.append(fwd)
        for a in range(n):
            copy(a, 0, sibling, me).wait_recv()
            for j, chip in enumerate(chips):
                copy(a, 4 + j, (*chip, 1 - c), me).wait_recv()
        for cp in first + passed:
            cp.wait_send()
        for cp in mine:
            cp.wait()

    outs = pl.pallas_call(
        body, name=name,
        out_shape=[jax.ShapeDtypeStruct((N_DEV,) + a.shape, a.dtype) for a in arrs],
        in_specs=[ANY] * n, out_specs=[ANY] * n,
        scratch_shapes=[pltpu.SemaphoreType.DMA((n, 7)), pltpu.SemaphoreType.DMA((n, 7)), pltpu.SemaphoreType.DMA((n,))],
    )(*arrs)
    return list(outs)


def _shard_of(ref, kind, j):
    if kind == "col":
        w = ref.shape[1] // N_DEV
        return ref.at[:, pl.ds(j * w, w)]
    h = ref.shape[0] // N_DEV
    return ref.at[pl.ds(j * h, h), :]


def _shard_shape(shape, kind):
    return (shape[0], shape[1] // N_DEV) if kind == "col" else (shape[0] // N_DEV, shape[1])


def _exchange_cores(grads, kinds, name):
    n = len(grads)

    def body(*refs):
        ins, outs = refs[:n], refs[n:2 * n]
        send_sems, recv_sems = refs[2 * n:]
        x, y, c = _mesh_pos()
        sibling = (x, y, 1 - c)
        sends = []
        for a in range(n):
            for s in range(4):
                sends.append(pltpu.make_async_remote_copy(
                    src_ref=_shard_of(ins[a], kinds[a], 2 * s + (1 - c)), dst_ref=outs[a].at[s],
                    send_sem=send_sems.at[a, s], recv_sem=recv_sems.at[a, s], device_id=sibling, device_id_type=MESH))
        for cp in sends:
            cp.start()
        for a in range(n):
            for s in range(4):
                pltpu.make_async_remote_copy(
                    src_ref=outs[a].at[s], dst_ref=outs[a].at[s],
                    send_sem=send_sems.at[a, s], recv_sem=recv_sems.at[a, s], device_id=sibling, device_id_type=MESH).wait_recv()
        for cp in sends:
            cp.wait_send()

    outs = pl.pallas_call(
        body, name=name,
        out_shape=[jax.ShapeDtypeStruct((4,) + _shard_shape(g.shape, k), F32) for g, k in zip(grads, kinds)],
        in_specs=[ANY] * n, out_specs=[ANY] * n,
        scratch_shapes=[pltpu.SemaphoreType.DMA((n, 4)), pltpu.SemaphoreType.DMA((n, 4))],
    )(*grads)
    return list(outs)


def _exchange_chips(parts, name):
    n = len(parts)

    def body(*refs):
        ins, outs = refs[:n], refs[n:2 * n]
        send_sems, recv_sems = refs[2 * n:]
        x, y, c = _mesh_pos()
        my_slot = 2 * x + y
        chips = [(1 - x, y), (x, 1 - y), (1 - x, 1 - y)]
        sends = []
        for a in range(n):
            for j, (px, py) in enumerate(chips):
                sends.append(pltpu.make_async_remote_copy(
                    src_ref=ins[a].at[2 * px + py], dst_ref=outs[a].at[my_slot],
                    send_sem=send_sems.at[a, j], recv_sem=recv_sems.at[a, j], device_id=(px, py, c), device_id_type=MESH))
        for cp in sends:
            cp.start()
        for a in range(n):
            for j, (px, py) in enumerate(chips):
                pltpu.make_async_remote_copy(
                    src_ref=outs[a].at[2 * px + py], dst_ref=outs[a].at[2 * px + py],
                    send_sem=send_sems.at[a, j], recv_sem=recv_sems.at[a, j], device_id=(px, py, c), device_id_type=MESH).wait_recv()
        for cp in sends:
            cp.wait_send()

    outs = pl.pallas_call(
        body, name=name,
        out_shape=[jax.ShapeDtypeStruct(p.shape, p.dtype) for p in parts],
        in_specs=[ANY] * n, out_specs=[ANY] * n,
        scratch_shapes=[pltpu.SemaphoreType.DMA((n, 3)), pltpu.SemaphoreType.DMA((n, 3))],
    )(*parts)
    return list(outs)


def _row_tile(rows):
    for t in (256, 128, 64, 32, 16, 8):
        if rows % t == 0:
            return t
    return rows


def _wire_dtype(rows):
    return BF16 if rows % 16 == 0 else F32


def _add_own(core, g, recv, kind, name):
    ns, R, C = recv.shape
    tr = _row_tile(R)
    nr = R // tr
    if kind == "col":
        g_spec = pl.BlockSpec((tr, C), lambda s, r, c_ref: (r, 2 * s + c_ref[0]))
    else:
        g_spec = pl.BlockSpec((tr, C), lambda s, r, c_ref: ((2 * s + c_ref[0]) * nr + r, 0))
    slab = pl.BlockSpec((None, tr, C), lambda s, r, c_ref: (s, r, 0))

    def body(c_ref, g_ref, r_ref, o32_ref, o16_ref):
        t = g_ref[...] + r_ref[...]
        o32_ref[...] = t
        o16_ref[...] = t.astype(o16_ref.dtype)

    return pl.pallas_call(
        body, name=name,
        grid_spec=pltpu.PrefetchScalarGridSpec(num_scalar_prefetch=1, grid=(ns, nr), in_specs=[g_spec, slab], out_specs=[slab, slab]),
        out_shape=[jax.ShapeDtypeStruct((ns, R, C), F32), jax.ShapeDtypeStruct((ns, R, C), _wire_dtype(R))],
        compiler_params=_cparams(2),
    )(core, g, recv)


def _adamw_math(w, g, m, v):
    m2 = ADAM_B1 * m + (1.0 - ADAM_B1) * g
    v2 = ADAM_B2 * v + (1.0 - ADAM_B2) * (g * g)
    m_hat = m2 / (1.0 - ADAM_B1 ** ADAM_STEP)
    v_hat = v2 / (1.0 - ADAM_B2 ** ADAM_STEP)
    delta = -ADAM_LR * (m_hat / (jnp.sqrt(v_hat) + ADAM_EPS) + ADAM_WD * w)
    return delta, m2, v2


def _sum_adamw(parts, w, m, v, name):
    n, R, C = parts.shape
    tr = _row_tile(R)

    def body(p_ref, w_ref, m_ref, v_ref, g_ref, d_ref, m_out, v_out):
        g = p_ref[0]
        for k in range(1, n):
            g = g + p_ref[k]
        g_ref[...] = g
        d_ref[...], m_out[...], v_out[...] = _adamw_math(w_ref[...], g, m_ref[...], v_ref[...])

    blk = pl.BlockSpec((tr, C), lambda r: (r, 0))
    return pl.pallas_call(
        body, name=name, grid=(R // tr,),
        in_specs=[pl.BlockSpec((n, tr, C), lambda r: (0, r, 0)), blk, blk, blk],
        out_specs=[blk] * 4, out_shape=[jax.ShapeDtypeStruct((R, C), F32)] * 4, compiler_params=_cparams(1),
    )(parts, w, m, v)


def _sum_chips_adamw(chip, own, recv, w, m, v, name):
    n, R, C = recv.shape
    tr = _row_tile(R)

    def body(chip_ref, own_ref, r_ref, w_ref, m_ref, v_ref, g_ref, d_ref, m_out, v_out):
        g = None
        for k in range(n):
            term = jnp.where(chip_ref[0] == k, own_ref[...], r_ref[k].astype(F32))
            g = term if g is None else g + term
        g_ref[...] = g
        d_ref[...], m_out[...], v_out[...] = _adamw_math(w_ref[...], g, m_ref[...], v_ref[...])

    blk = pl.BlockSpec((tr, C), lambda r, chip_ref: (r, 0))
    return pl.pallas_call(
        body, name=name,
        grid_spec=pltpu.PrefetchScalarGridSpec(
            num_scalar_prefetch=1, grid=(R // tr,),
            in_specs=[pl.BlockSpec((None, tr, C), lambda r, chip_ref: (chip_ref[0], r, 0)),
                      pl.BlockSpec((n, tr, C), lambda r, chip_ref: (0, r, 0)), blk, blk, blk],
            out_specs=[blk] * 4),
        out_shape=[jax.ShapeDtypeStruct((R, C), F32)] * 4, compiler_params=_cparams(1),
    )(chip, own, recv, w, m, v)


def _ada_fwd(c, w_ada_g, b_ada):
    nb, D, W = w_ada_g.shape

    def body(c_ref, w_ref, b_ref, o_ref):
        cv = c_ref[...]
        ca = jnp.broadcast_to(cv * _sigmoid(cv), (8, D)).astype(BF16)
        o_ref[...] = _dot(ca, w_ref[...])[0:1, :] + b_ref[...]

    return pl.pallas_call(
        body, name="ada_fwd", grid=(nb,),
        in_specs=[_full((1, D)), pl.BlockSpec((None, D, W), lambda j: (j, 0, 0)), pl.BlockSpec((1, W), lambda j: (0, j))],
        out_specs=pl.BlockSpec((1, W), lambda j: (0, j)),
        out_shape=jax.ShapeDtypeStruct((1, nb * W), F32), compiler_params=_cparams(1),
    )(c, w_ada_g, b_ada)


def _ada_bwd(me, small_g, D, W, c_block):
    def body(me_ref, c_ref, dm_ref, o_ref):
        cv = c_ref[...]
        ca = (cv * _sigmoid(cv)).astype(BF16).astype(F32)
        dm = dm_ref[...].astype(BF16).astype(F32)
        o_ref[...] = lax.dot_general(ca, dm, (((0,), (0,)), ((), ())), precision=lax.Precision.HIGHEST,
                                     preferred_element_type=F32)

    return pl.pallas_call(
        body, name="ada_bwd",
        grid_spec=pltpu.PrefetchScalarGridSpec(
            num_scalar_prefetch=1, grid=(1,),
            in_specs=[pl.BlockSpec((N_DEV, D), lambda i, me_ref: (0, c_block)),
                      pl.BlockSpec((N_DEV, W), lambda i, me_ref: (0, me_ref[0]))],
            out_specs=pl.BlockSpec((D, W), lambda i, me_ref: (0, 0))),
        out_shape=jax.ShapeDtypeStruct((D, W), F32), compiler_params=_cparams(1),
    )(me, small_g, small_g)


def _fwd_in(x, mod, w_in_g, b_in):
    S, D = x.shape
    nb, _, wb = w_in_g.shape
    N = nb * wb
    TM = TOKEN_TILE

    def body(x_ref, mod_ref, w_hbm, b_ref, z_ref, h_ref, w_v, sems):
        @pl.when(pl.program_id(0) == 0)
        def _():
            _load_cols(w_hbm, w_v, sems)

        xhat, _ = _ln(x_ref[...])
        hb = (xhat * (1.0 + mod_ref[:, D:2 * D]) + mod_ref[:, 0:D]).astype(BF16)
        h_ref[...] = hb
        z_ref[...] = _dot(hb, w_v[...]) + b_ref[...]

    return pl.pallas_call(
        body, name="fwd_in", grid=(S // TM,),
        in_specs=[pl.BlockSpec((TM, D), lambda i: (i, 0)), _full(mod.shape), ANY, _full((1, N))],
        out_specs=[pl.BlockSpec((TM, N), lambda i: (i, 0)), pl.BlockSpec((TM, D), lambda i: (i, 0))],
        out_shape=[jax.ShapeDtypeStruct((S, N), F32), jax.ShapeDtypeStruct((S, D), BF16)],
        scratch_shapes=[pltpu.VMEM((D, N), BF16), pltpu.SemaphoreType.DMA((nb,))],
        compiler_params=_cparams(1),
    )(x, mod, w_in_g, b_in)


def _causal_taps(buf, w_ref, o_ref, K, halo, TM, D):
    R, C = CONV_ROWS, CONV_LANES
    base = halo - (K - 1)

    def step(j, carry):
        r0 = pl.multiple_of(j * R, R)
        for c0 in range(0, D, C):
            win = buf[pl.ds(r0, R + halo), c0:c0 + C]
            acc = jnp.zeros((R, C), F32)
            for k in range(K):
                acc = acc + w_ref[k:k + 1, c0:c0 + C] * win[base + k:base + k + R, :]
            o_ref[pl.ds(r0, R), c0:c0 + C] = acc
        return carry

    lax.fori_loop(0, TM // R, step, 0)


def _fwd_mix(z, x, mod, wa, ba, lag, lab, w_a_out, b_a_out, wb, w_b_out, w_o, b_o):
    S, D = x.shape
    N = z.shape[1]
    TM = TOKEN_TILE
    KA, KB = wa.shape[0], wb.shape[0]

    def body(z_ref, x_ref, mod_ref, wa_ref, ba_ref, lag_ref, lab_ref, wao_hbm, bao_ref, wb_ref, wbo_hbm, wo_hbm, bo_ref,
             u1_ref, ya_ref, yb_ref, q_ref, out_ref, r1_ref, u2_ref, v_ref, mg_ref,
             ubuf, pbuf, wao, wbo, wo, sems):
        i = pl.program_id(0)

        @pl.when(i == 0)
        def _():
            _load_whole([(wao_hbm, wao), (wbo_hbm, wbo), (wo_hbm, wo)], sems)
            ubuf[0:HALO_A, :] = jnp.zeros((HALO_A, D), F32)
            pbuf[0:HALO_B, :] = jnp.zeros((HALO_B, D), F32)

        ubuf[HALO_A:HALO_A + TM, :] = z_ref[:, 0:D] * _sigmoid(z_ref[:, D:2 * D])
        _causal_taps(ubuf, wa_ref, u1_ref, KA, HALO_A, TM, D)
        ubuf[0:HALO_A, :] = ubuf[TM:TM + HALO_A, :]
        u1 = u1_ref[...] + ba_ref[...]
        u1_ref[...] = u1
        xa, _ = _ln(u1)
        l = xa * lag_ref[...] + lab_ref[...]
        u2 = (l * _sigmoid(l)).astype(BF16)
        u2_ref[...] = u2
        ya = _dot(u2, wao[...]) + bao_ref[...]
        ya_ref[...] = ya
        pbuf[HALO_B:HALO_B + TM, :] = z_ref[:, 3 * D:4 * D] * z_ref[:, 4 * D:5 * D]
        _causal_taps(pbuf, wb_ref, q_ref, KB, HALO_B, TM, D)
        pbuf[0:HALO_B, :] = pbuf[TM:TM + HALO_B, :]
        v = (z_ref[:, 2 * D:3 * D] * q_ref[...]).astype(BF16)
        v_ref[...] = v
        yb = _dot(v, wbo[...])
        yb_ref[...] = yb
        mg = (_sigmoid(z_ref[:, 5 * D:6 * D]) * ya + _sigmoid(z_ref[:, 6 * D:7 * D]) * yb).astype(BF16)
        mg_ref[...] = mg
        out = _dot(mg, wo[...]) + bo_ref[...]
        out_ref[...] = out
        r1_ref[...] = ALPHA * x_ref[...] + (1.0 + mod_ref[:, 2 * D:3 * D]) * out

    tile = pl.BlockSpec((TM, D), lambda i: (i, 0))
    vec = _full((1, D))
    return pl.pallas_call(
        body, name="fwd_mix", grid=(S // TM,),
        in_specs=[pl.BlockSpec((TM, N), lambda i: (i, 0)), tile, _full(mod.shape), _full((KA, D)), vec, vec, vec, ANY, vec,
                  _full((KB, D)), ANY, ANY, vec],
        out_specs=[tile] * 9,
        out_shape=[jax.ShapeDtypeStruct((S, D), F32)] * 6 + [jax.ShapeDtypeStruct((S, D), BF16)] * 3,
        scratch_shapes=[pltpu.VMEM((TM + HALO_A, D), F32), pltpu.VMEM((TM + HALO_B, D), F32),
                        pltpu.VMEM((D, D), BF16), pltpu.VMEM((D, D), BF16), pltpu.VMEM((D, D), BF16),
                        pltpu.SemaphoreType.DMA((3,))],
        compiler_params=_cparams(1),
    )(z, x, mod, wa, ba, lag, lab, w_a_out, b_a_out, wb, w_b_out, w_o, b_o)


FFN_ROWS = ("b_down", "ln2_g", "ln2_b", "gate2", "scale2", "shift2", "loss")


def _ffn(r1, tgt, mod, ln1_g, ln1_b, w_up_g, b_up, w_down, b_down, ln2_g, ln2_b):
    S, D = r1.shape
    nb, _, wb = w_up_g.shape
    FF = nb * wb
    TM = TOKEN_TILE
    CF = 1024
    n_tiles = S // TM

    def body(r1_ref, t_ref, mod_ref, g1_ref, b1_ref, wup_hbm, bup_ref, wdn_hbm, bdn_ref, g2_ref, b2_ref,
             dx1_ref, h2_ref, f_ref, dhu_ref, do2_ref, acc_ref, accup_ref,
             wup, wdn, relu_buf, sems, sem2):
        i = pl.program_id(0)

        @pl.when(i == 0)
        def _():
            _load_cols(wup_hbm, wup, sems)
            _load_whole([(wdn_hbm, wdn)], sem2)
            acc_ref[...] = jnp.zeros(acc_ref.shape, F32)
            accup_ref[...] = jnp.zeros(accup_ref.shape, F32)

        scale2, shift2, gate2 = mod_ref[:, 4 * D:5 * D], mod_ref[:, 3 * D:4 * D], mod_ref[:, 5 * D:6 * D]
        xhat1, _ = _ln(r1_ref[...])
        x1 = xhat1 * g1_ref[...] + b1_ref[...]
        xh0, rstd0 = _ln(x1)
        h2 = (xh0 * (1.0 + scale2) + shift2).astype(BF16)
        h2_ref[...] = h2
        out2 = jnp.zeros((TM, D), F32) + bdn_ref[...]
        for c0 in range(0, FF, CF):
            hu = _dot(h2, wup[:, c0:c0 + CF]) + bup_ref[:, c0:c0 + CF]
            rl = jnp.maximum(hu, 0.0)
            relu_buf[:, c0:c0 + CF] = rl
            fb = (rl * rl).astype(BF16)
            f_ref[:, c0:c0 + CF] = fb
            out2 = out2 + _dot(fb, wdn[c0:c0 + CF, :])
        r2 = ALPHA * x1 + (1.0 + gate2) * out2
        xh2, rstd2 = _ln(r2)
        e = xh2 * g2_ref[...] + b2_ref[...] - t_ref[...]
        acc_ref[6:7, :] += _colsum(e * e)
        dy = e * (1.0 / D)
        acc_ref[1:2, :] += _colsum(dy * xh2)
        acc_ref[2:3, :] += _colsum(dy)
        dr2 = _ln_bwd(dy * g2_ref[...], xh2, rstd2)
        acc_ref[3:4, :] += _colsum(dr2 * out2)
        do2 = (1.0 + gate2) * dr2
        acc_ref[0:1, :] += _colsum(do2)
        do2b = do2.astype(BF16)
        do2_ref[...] = do2b
        dh2 = jnp.zeros((TM, D), F32)
        for c0 in range(0, FF, CF):
            dhu = _dot_nt(do2b, wdn[c0:c0 + CF, :]) * (2.0 * relu_buf[:, c0:c0 + CF])
            accup_ref[:, c0:c0 + CF] += _colsum(dhu)
            dhub = dhu.astype(BF16)
            dhu_ref[:, c0:c0 + CF] = dhub
            dh2 = dh2 + _dot_nt(dhub, wup[:, c0:c0 + CF])
        acc_ref[4:5, :] += _colsum(dh2 * xh0)
        acc_ref[5:6, :] += _colsum(dh2)
        dx1_ref[...] = ALPHA * dr2 + _ln_bwd(dh2 * (1.0 + scale2), xh0, rstd0)

        @pl.when(i == n_tiles - 1)
        def _():
            tot = jnp.sum(acc_ref[6:7, :], axis=-1, keepdims=True) * (0.5 / D)
            acc_ref[6:7, :] = jnp.broadcast_to(tot, (1, D))

    tile = pl.BlockSpec((TM, D), lambda i: (i, 0))
    wide = pl.BlockSpec((TM, FF), lambda i: (i, 0))
    vec = _full((1, D))
    return pl.pallas_call(
        body, name="ffn", grid=(n_tiles,),
        in_specs=[tile, tile, _full(mod.shape), vec, vec, ANY, _full((1, FF)), ANY, vec, vec, vec],
        out_specs=[tile, tile, wide, wide, tile, _full((8, D)), _full((1, FF))],
        out_shape=[jax.ShapeDtypeStruct((S, D), F32), jax.ShapeDtypeStruct((S, D), BF16), jax.ShapeDtypeStruct((S, FF), BF16),
                   jax.ShapeDtypeStruct((S, FF), BF16), jax.ShapeDtypeStruct((S, D), BF16),
                   jax.ShapeDtypeStruct((8, D), F32), jax.ShapeDtypeStruct((1, FF), F32)],
        scratch_shapes=[pltpu.VMEM((D, FF), BF16), pltpu.VMEM((FF, D), BF16), pltpu.VMEM((TM, FF), F32),
                        pltpu.SemaphoreType.DMA((nb,)), pltpu.SemaphoreType.DMA((1,))],
        compiler_params=_cparams(1),
    )(r1, tgt, mod, ln1_g, ln1_b, w_up_g, b_up, w_down, b_down, ln2_g, ln2_b)


def _dw(a, b, name, msplit=1, nsplit=1):
    S, M = a.shape
    N = b.shape[1]
    TK = min(DW_TILE, S)
    mb, nbk = M // msplit, N // nsplit

    def body(a_ref, b_ref, o_ref):
        @pl.when(pl.program_id(2) == 0)
        def _():
            o_ref[...] = jnp.zeros(o_ref.shape, F32)

        o_ref[...] += _dot_tn(a_ref[...], b_ref[...])

    return pl.pallas_call(
        body, name=name, grid=(msplit, nsplit, S // TK),
        in_specs=[pl.BlockSpec((TK, mb), lambda i, j, k: (k, i)), pl.BlockSpec((TK, nbk), lambda i, j, k: (k, j))],
        out_specs=pl.BlockSpec((mb, nbk), lambda i, j, k: (i, j)),
        out_shape=jax.ShapeDtypeStruct((M, N), F32), compiler_params=_cparams(3),
    )(a, b)


BWD1_ROWS = ("ln1_g", "ln1_b", "gate1", "b_o", "b_a_out", "ln_a_g", "ln_a_b", "conv_a_b", "b_in_gb", "b_in_ga", "b_in_gbb")


def _bwd_mix1(dx1, r1, out, ya, yb, q, u1, z, mod, ln1_g, lag, lab, w_o, w_a_out, w_b_out):
    S, D = dx1.shape
    TM = TOKEN_TILE

    def body(dx1_ref, r1_ref, out_ref, ya_ref, yb_ref, q_ref, u1_ref, zgb_ref, zga_ref, zgg_ref, mod_ref, g1_ref, lag_ref, lab_ref,
             wo_hbm, wao_hbm, wbo_hbm,
             dxp_ref, du1_ref, dq_ref, dzc_ref, dout_ref, dya_ref, dyb_ref, acc_ref,
             wo, wao, wbo, sems):
        @pl.when(pl.program_id(0) == 0)
        def _():
            _load_whole([(wo_hbm, wo), (wao_hbm, wao), (wbo_hbm, wbo)], sems)
            acc_ref[...] = jnp.zeros(acc_ref.shape, F32)

        dx1v = dx1_ref[...]
        xhat1, rstd1 = _ln(r1_ref[...])
        acc_ref[0:1, :] += _colsum(dx1v * xhat1)
        acc_ref[1:2, :] += _colsum(dx1v)
        dr1 = _ln_bwd(dx1v * g1_ref[...], xhat1, rstd1)
        dxp_ref[...] = ALPHA * dr1
        acc_ref[2:3, :] += _colsum(dr1 * out_ref[...])
        dout = (1.0 + mod_ref[:, 2 * D:3 * D]) * dr1
        acc_ref[3:4, :] += _colsum(dout)
        doutb = dout.astype(BF16)
        dout_ref[...] = doutb
        dmg = _dot_nt(doutb, wo[...])
        sga, sgb = _sigmoid(zga_ref[...]), _sigmoid(zgg_ref[...])
        dga = dmg * ya_ref[...] * sga * (1.0 - sga)
        dgb = dmg * yb_ref[...] * sgb * (1.0 - sgb)
        acc_ref[9:10, :] += _colsum(dga)
        acc_ref[10:11, :] += _colsum(dgb)
        dzc_ref[:, D:2 * D] = dga.astype(BF16)
        dzc_ref[:, 2 * D:3 * D] = dgb.astype(BF16)
        dya = dmg * sga
        acc_ref[4:5, :] += _colsum(dya)
        dyab = dya.astype(BF16)
        dya_ref[...] = dyab
        dybb = (dmg * sgb).astype(BF16)
        dyb_ref[...] = dybb
        du2 = _dot_nt(dyab, wao[...])
        xa, rstda = _ln(u1_ref[...])
        l = xa * lag_ref[...] + lab_ref[...]
        sl = _sigmoid(l)
        dl = du2 * (sl * (1.0 + l * (1.0 - sl)))
        acc_ref[5:6, :] += _colsum(dl * xa)
        acc_ref[6:7, :] += _colsum(dl)
        du1 = _ln_bwd(dl * lag_ref[...], xa, rstda)
        acc_ref[7:8, :] += _colsum(du1)
        du1_ref[...] = du1
        dv = _dot_nt(dybb, wbo[...])
        dgbk = dv * q_ref[...]
        acc_ref[8:9, :] += _colsum(dgbk)
        dzc_ref[:, 0:D] = dgbk.astype(BF16)
        dq_ref[...] = dv * zgb_ref[...]

    tile = pl.BlockSpec((TM, D), lambda i: (i, 0))
    vec = _full((1, D))
    zcol = lambda k: pl.BlockSpec((TM, D), lambda i: (i, k))
    return pl.pallas_call(
        body, name="bwd_mix1", grid=(S // TM,),
        in_specs=[tile] * 7 + [zcol(2), zcol(5), zcol(6), _full(mod.shape), vec, vec, vec, ANY, ANY, ANY],
        out_specs=[tile, tile, tile, pl.BlockSpec((TM, 3 * D), lambda i: (i, 0)), tile, tile, tile, _full((16, D))],
        out_shape=[jax.ShapeDtypeStruct((S, D), F32)] * 3 + [jax.ShapeDtypeStruct((S, 3 * D), BF16)]
        + [jax.ShapeDtypeStruct((S, D), BF16)] * 3 + [jax.ShapeDtypeStruct((16, D), F32)],
        scratch_shapes=[pltpu.VMEM((D, D), BF16)] * 3 + [pltpu.SemaphoreType.DMA((3,))],
        compiler_params=_cparams(1),
    )(dx1, r1, out, ya, yb, q, u1, z, z, z, mod, ln1_g, lag, lab, w_o, w_a_out, w_b_out)


def _anticausal_taps(dbuf, u_ref, w_ref, o_ref, dwacc, K, halo, TM, D):
    R, C = CONV_ROWS, CONV_LANES

    def step(jr, carry):
        r0 = pl.multiple_of(jr * R, R)
        for c0 in range(0, D, C):
            win = dbuf[pl.ds(r0, R + halo), c0:c0 + C]
            uc = u_ref[pl.ds(r0, R), c0:c0 + C]
            acc = jnp.zeros((R, C), F32)
            for j in range(K):
                sh = win[j:j + R, :]
                k = K - 1 - j
                acc = acc + w_ref[k:k + 1, c0:c0 + C] * sh
                pr = uc * sh
                part = pr[0:8, :]
                for s in range(8, R, 8):
                    part = part + pr[s:s + 8, :]
                dwacc[k, :, c0:c0 + C] += part
            o_ref[pl.ds(r0, R), c0:c0 + C] = acc
        return carry

    lax.fori_loop(0, TM // R, step, 0)


def _bwd_mix2(du1, dq, z, dzc, x, dxp, mod, wa, wb, w_in_g):
    S, D = x.shape
    N = z.shape[1]
    nb, _, wbk = w_in_g.shape
    TM = TOKEN_TILE
    KA, KB = wa.shape[0], wb.shape[0]
    n_tiles = S // TM

    def body(du1_ref, dq_ref, zav_ref, zc_ref, zx_ref, dzc_ref, x_ref, dxp_ref, mod_ref, wa_ref, wb_ref, win_hbm,
             dz_ref, gx_ref, dwa_ref, dwb_ref, acc_ref, dbin_ref,
             dbuf, qbuf, ubuf, pbuf, obuf, dwa_acc, dwb_acc, w_v, sems):
        i = pl.program_id(0)

        @pl.when(i == 0)
        def _():
            _load_cols(win_hbm, w_v, sems)
            dbuf[TM:TM + HALO_A, :] = jnp.zeros((HALO_A, D), F32)
            qbuf[TM:TM + HALO_B, :] = jnp.zeros((HALO_B, D), F32)
            dwa_acc[...] = jnp.zeros(dwa_acc.shape, F32)
            dwb_acc[...] = jnp.zeros(dwb_acc.shape, F32)
            acc_ref[...] = jnp.zeros(acc_ref.shape, F32)
            dbin_ref[...] = jnp.zeros(dbin_ref.shape, F32)

        a_val = zav_ref[:, 0:D]
        sa = _sigmoid(zav_ref[:, D:2 * D])
        ubuf[...] = a_val * sa
        dbuf[0:TM, :] = du1_ref[...]
        _anticausal_taps(dbuf, ubuf, wa_ref, obuf, dwa_acc, KA, HALO_A, TM, D)
        dbuf[TM:TM + HALO_A, :] = dbuf[0:HALO_A, :]
        du0 = obuf[...]
        dav = du0 * sa
        dag = du0 * a_val * sa * (1.0 - sa)
        dbin_ref[:, 0:D] += _colsum(dav)
        dbin_ref[:, D:2 * D] += _colsum(dag)
        dz_ref[:, 0:D] = dav.astype(BF16)
        dz_ref[:, D:2 * D] = dag.astype(BF16)
        pbuf[...] = zc_ref[...] * zx_ref[...]
        qbuf[0:TM, :] = dq_ref[...]
        _anticausal_taps(qbuf, pbuf, wb_ref, obuf, dwb_acc, KB, HALO_B, TM, D)
        qbuf[TM:TM + HALO_B, :] = qbuf[0:HALO_B, :]
        dp = obuf[...]
        dgc = dp * zx_ref[...]
        dgx = dp * zc_ref[...]
        dbin_ref[:, 3 * D:4 * D] += _colsum(dgc)
        dbin_ref[:, 4 * D:5 * D] += _colsum(dgx)
        dz_ref[:, 3 * D:4 * D] = dgc.astype(BF16)
        dz_ref[:, 4 * D:5 * D] = dgx.astype(BF16)
        dz_ref[:, 2 * D:3 * D] = dzc_ref[:, 0:D]
        dz_ref[:, 5 * D:7 * D] = dzc_ref[:, D:3 * D]
        dh = _dot_nt(dz_ref[...], w_v[...])
        xhat, rstd = _ln(x_ref[...])
        acc_ref[0:1, :] += _colsum(dh * xhat)
        acc_ref[1:2, :] += _colsum(dh)
        gx_ref[...] = dxp_ref[...] + _ln_bwd(dh * (1.0 + mod_ref[:, D:2 * D]), xhat, rstd)

        @pl.when(i == n_tiles - 1)
        def _():
            for k in range(KA):
                dwa_ref[k:k + 1, :] = jnp.sum(dwa_acc[k], axis=0, keepdims=True)
            for k in range(KB):
                dwb_ref[k:k + 1, :] = jnp.sum(dwb_acc[k], axis=0, keepdims=True)

    rev = lambda i: n_tiles - 1 - i
    tile = pl.BlockSpec((TM, D), lambda i: (rev(i), 0))
    zcol = lambda k: pl.BlockSpec((TM, D), lambda i: (rev(i), k))
    return pl.pallas_call(
        body, name="bwd_mix2", grid=(n_tiles,),
        in_specs=[tile, tile, pl.BlockSpec((TM, 2 * D), lambda i: (rev(i), 0)), zcol(3), zcol(4),
                  pl.BlockSpec((TM, 3 * D), lambda i: (rev(i), 0)), tile, tile, _full(mod.shape), _full((KA, D)), _full((KB, D)), ANY],
        out_specs=[pl.BlockSpec((TM, N), lambda i: (rev(i), 0)), tile, _full((KA, D)), _full((KB, D)), _full((8, D)), _full((1, N))],
        out_shape=[jax.ShapeDtypeStruct((S, N), BF16), jax.ShapeDtypeStruct((S, D), F32), jax.ShapeDtypeStruct((KA, D), F32),
                   jax.ShapeDtypeStruct((KB, D), F32), jax.ShapeDtypeStruct((8, D), F32), jax.ShapeDtypeStruct((1, N), F32)],
        scratch_shapes=[pltpu.VMEM((TM + HALO_A, D), F32), pltpu.VMEM((TM + HALO_B, D), F32), pltpu.VMEM((TM, D), F32),
                        pltpu.VMEM((TM, D), F32), pltpu.VMEM((TM, D), F32), pltpu.VMEM((KA, 8, D), F32), pltpu.VMEM((KB, 8, D), F32),
                        pltpu.VMEM((D, N), BF16), pltpu.SemaphoreType.DMA((nb,))],
        compiler_params=_cparams(1),
    )(du1, dq, z, z, z, dzc, x, dxp, mod, wa, wb, w_in_g)


def _local_grads(x, c, tgt, gw, vecs):
    w_ada_g, w_in_g, w_a_out, w_b_out, w_o, w_up_g, w_down, wa, wb = gw
    D = x.shape[1]
    mod = _ada_fwd(c, w_ada_g, vecs["b_ada"])
    z, h1 = _fwd_in(x, mod, w_in_g, vecs["b_in"])
    u1, ya, yb, q, out, r1, u2, v, mg = _fwd_mix(
        z, x, mod, wa, vecs["conv_a_b"], vecs["ln_a_g"], vecs["ln_a_b"], w_a_out, vecs["b_a_out"], wb, w_b_out, w_o, vecs["b_o"])
    dx1, h2, f, dhu, do2, acc_f, acc_up = _ffn(
        r1, tgt, mod, vecs["ln1_g"], vecs["ln1_b"], w_up_g, vecs["b_up"], w_down, vecs["b_down"], vecs["ln2_g"], vecs["ln2_b"])
    g_up = _dw(h2, dhu, "dw_up", nsplit=2)
    g_down = _dw(f, do2, "dw_down", msplit=2)
    dxp, du1, dq, dzc, doutb, dyab, dybb, acc_1 = _bwd_mix1(
        dx1, r1, out, ya, yb, q, u1, z, mod, vecs["ln1_g"], vecs["ln_a_g"], vecs["ln_a_b"], w_o, w_a_out, w_b_out)
    g_o = _dw(mg, doutb, "dw_o")
    g_a_out = _dw(u2, dyab, "dw_a_out")
    g_b_out = _dw(v, dybb, "dw_b_out")
    dz, gx, g_wa, g_wb, acc_2, db_in = _bwd_mix2(du1, dq, z, dzc, x, dxp, mod, wa, wb, w_in_g)
    g_in = _dw(h1, dz, "dw_in", nsplit=4)
    row = lambda acc, k: acc[k:k + 1, :]
    dmod = jnp.concatenate([row(acc_2, 1), row(acc_2, 0), row(acc_1, 2), row(acc_f, 5), row(acc_f, 4), row(acc_f, 3)], axis=1)
    db_in = jnp.concatenate([db_in[:, 0:2 * D], row(acc_1, 8), db_in[:, 3 * D:5 * D], row(acc_1, 9), row(acc_1, 10)], axis=1)
    gvec = {
        "b_ada": dmod, "b_in": db_in, "conv_a_b": row(acc_1, 7), "ln_a_g": row(acc_1, 5), "ln_a_b": row(acc_1, 6),
        "b_a_out": row(acc_1, 4), "b_o": row(acc_1, 3), "ln1_g": row(acc_1, 0), "ln1_b": row(acc_1, 1), "b_up": acc_up,
        "b_down": row(acc_f, 0), "ln2_g": row(acc_f, 1), "ln2_b": row(acc_f, 2),
    }
    loss = acc_f[6, 0]
    gmat = {"w_in": g_in, "w_a_out": g_a_out, "w_b_out": g_b_out, "w_o": g_o, "w_up": g_up, "w_down": g_down,
            "conv_a_w": g_wa, "conv_b_w": g_wb}
    return loss, gx, gmat, gvec


WEIGHTS = ("w_ada", "b_ada", "w_in", "b_in", "conv_a_w", "conv_a_b", "ln_a_g", "ln_a_b", "w_a_out", "b_a_out", "conv_b_w",
           "w_b_out", "w_o", "b_o", "ln1_g", "ln1_b", "w_up", "b_up", "w_down", "b_down", "ln2_g", "ln2_b")
VECTORS = ("b_ada", "b_in", "conv_a_b", "ln_a_g", "ln_a_b", "b_a_out", "b_o", "ln1_g", "ln1_b", "b_up", "b_down", "ln2_g", "ln2_b")
MATRICES = ("w_in", "w_a_out", "w_b_out", "w_o", "w_up", "w_down", "conv_a_w", "conv_b_w")
SHARD_KIND = {"w_in": "col", "w_a_out": "row", "w_b_out": "row", "w_o": "row", "w_up": "col", "w_down": "row",
              "conv_a_w": "col", "conv_b_w": "col"}


def kernel(x, c, w_ada, b_ada, w_in, b_in, conv_a_w, conv_a_b, ln_a_g, ln_a_b, w_a_out, b_a_out, conv_b_w, w_b_out, w_o, b_o, ln1_g, ln1_b, w_up, b_up, w_down, b_down, ln2_g, ln2_b, loss_target, m_w_ada, m_b_ada, m_w_in, m_b_in, m_conv_a_w, m_conv_a_b, m_ln_a_g, m_ln_a_b, m_w_a_out, m_b_a_out, m_conv_b_w, m_w_b_out, m_w_o, m_b_o, m_ln1_g, m_ln1_b, m_w_up, m_b_up, m_w_down, m_b_down, m_ln2_g, m_ln2_b, v_w_ada, v_b_ada, v_w_in, v_b_in, v_conv_a_w, v_conv_a_b, v_ln_a_g, v_ln_a_b, v_w_a_out, v_b_a_out, v_conv_b_w, v_w_b_out, v_w_o, v_b_o, v_ln1_g, v_ln1_b, v_w_up, v_b_up, v_w_down, v_b_down, v_ln2_g, v_ln2_b):
    args = dict(locals())
    w = {n: args[n][0] for n in WEIGHTS}
    m = {n: args["m_" + n][0] for n in WEIGHTS}
    v = {n: args["v_" + n][0] for n in WEIGHTS}
    w = {n: (a[None, :] if a.ndim == 1 else a) for n, a in w.items()}
    m = {n: (a[None, :] if a.ndim == 1 else a) for n, a in m.items()}
    v = {n: (a[None, :] if a.ndim == 1 else a) for n, a in v.items()}
    xs, tgt = x[0], loss_target[0]
    S, D = xs.shape
    me = (4 * lax.axis_index("x") + 2 * lax.axis_index("y") + lax.axis_index("c")).astype(jnp.int32).reshape(1)
    core = lax.axis_index("c").astype(jnp.int32).reshape(1)
    chip = (2 * lax.axis_index("x") + lax.axis_index("y")).astype(jnp.int32).reshape(1)

    bf = lambda n: w[n].astype(BF16)
    g = _all_gather([bf("w_ada"), bf("w_in"), bf("w_a_out"), bf("w_b_out"), bf("w_o"), bf("w_up"), bf("w_down"),
                     w["conv_a_w"], w["conv_b_w"]], "gather_weights")
    w_ada_g, w_in_g, w_a_out_g, w_b_out_g, w_o_g, w_up_g, w_down_g, wa_g, wb_g = g
    rows = lambda a: a.reshape(a.shape[0] * a.shape[1], a.shape[2])
    cols = lambda a: jnp.transpose(a, (1, 0, 2)).reshape(a.shape[1], a.shape[0] * a.shape[2])
    gw = (w_ada_g, w_in_g, rows(w_a_out_g), rows(w_b_out_g), rows(w_o_g), w_up_g, rows(w_down_g), cols(wa_g), cols(wb_g))
    vecs = {n: w[n] for n in VECTORS}

    loss, gx, gmat, gvec = _local_grads(xs, c, tgt, gw, vecs)

    small = jnp.concatenate([gvec[n] for n in VECTORS] + [c], axis=1)
    (small_g,) = _all_gather([small], "gather_vectors")
    small_g = small_g.reshape(N_DEV, small.shape[1])
    from_sibling = _exchange_cores([gmat[n] for n in MATRICES], [SHARD_KIND[n] for n in MATRICES], "reduce_cores")
    chip_sums = [_add_own(core, gmat[n], r, SHARD_KIND[n], "add_" + n) for r, n in zip(from_sibling, MATRICES)]
    from_chips = _exchange_chips([wire for _, wire in chip_sums], "reduce_chips")

    res = {}
    for n, (own, _), recv in zip(MATRICES, chip_sums, from_chips):
        res[n] = _sum_chips_adamw(chip, own, recv, w[n], m[n], v[n], "adamw_" + n)
    W = w["w_ada"].shape[1]
    n_vec = small.shape[1] - D
    g_ada = _ada_bwd(me, small_g, D, W, n_vec // D)
    res["w_ada"] = _sum_adamw(g_ada[None], w["w_ada"], m["w_ada"], v["w_ada"], "adamw_w_ada")
    cat = lambda d: jnp.concatenate([d[n] for n in VECTORS], axis=1)
    vg, vd, vm, vv = _sum_adamw(small_g[:, None, 0:n_vec], cat(w), cat(m), cat(v), "adamw_vectors")
    off = 0
    for n in VECTORS:
        k = w[n].shape[1]
        res[n] = tuple(a[:, off:off + k] for a in (vg, vd, vm, vv))
        off += k

    loss = lax.psum(loss, ("x", "y", "c"))
    outs = [loss, gx[None]]
    for t in range(4):
        outs += [res[n][t].reshape(args[n].shape) for n in WEIGHTS]
    return tuple(outs)
```

```python
import functools

import jax
import jax.numpy as jnp
from jax import lax
from jax.experimental import pallas as pl
from jax.experimental.pallas import tpu as pltpu

F32 = jnp.float32
BF16 = jnp.bfloat16
MESH = pl.DeviceIdType.MESH

N_DEV = 8
LN_EPS = 1e-5
DEPTH = 1
ALPHA = (2.0 * DEPTH) ** 0.25
ADAM_LR, ADAM_B1, ADAM_B2, ADAM_EPS, ADAM_WD, ADAM_STEP = 0.001, 0.9, 0.999, 1e-08, 0.01, 10

VMEM_LIMIT = 60 * 1024 * 1024
TOKEN_TILE = 256
DW_TILE = 1024
HALO_A = 32
HALO_B = 8
CONV_ROWS, CONV_LANES = 32, 512


def _cparams(n_grid):
    return pltpu.CompilerParams(dimension_semantics=("arbitrary",) * n_grid, vmem_limit_bytes=VMEM_LIMIT)


def _full(shape):
    return pl.BlockSpec(shape, lambda *_: (0,) * len(shape))


ANY = pl.BlockSpec(memory_space=pl.ANY)


def _ln(x):
    mu = jnp.mean(x, axis=-1, keepdims=True)
    xc = x - mu
    var = jnp.mean(xc * xc, axis=-1, keepdims=True)
    rstd = lax.rsqrt(var + LN_EPS)
    return xc * rstd, rstd


def _ln_bwd(dxhat, xhat, rstd):
    m1 = jnp.mean(dxhat, axis=-1, keepdims=True)
    m2 = jnp.mean(dxhat * xhat, axis=-1, keepdims=True)
    return rstd * (dxhat - m1 - xhat * m2)


def _sigmoid(x):
    return 1.0 / (1.0 + jnp.exp(-x))


def _colsum(a):
    return jnp.sum(a, axis=0, keepdims=True)


def _dot(a, b):
    return jnp.dot(a, b, preferred_element_type=F32)


def _dot_nt(a, b):
    return lax.dot_general(a, b, (((1,), (1,)), ((), ())), preferred_element_type=F32)


def _dot_tn(a, b):
    return lax.dot_general(a, b, (((0,), (0,)), ((), ())), preferred_element_type=F32)


def _load_cols(src_hbm, dst_vmem, sems):
    nblk, _, w = src_hbm.shape
    cps = [pltpu.make_async_copy(src_hbm.at[j], dst_vmem.at[:, pl.ds(j * w, w)], sems.at[j]) for j in range(nblk)]
    for cp in cps:
        cp.start()
    for cp in cps:
        cp.wait()


def _load_whole(pairs, sems):
    cps = [pltpu.make_async_copy(s, d, sems.at[k]) for k, (s, d) in enumerate(pairs)]
    for cp in cps:
        cp.start()
    for cp in cps:
        cp.wait()


def _mesh_pos():
    return lax.axis_index("x"), lax.axis_index("y"), lax.axis_index("c")


def _all_gather(arrs, name):
    n = len(arrs)

    def body(*refs):
        ins, outs = refs[:n], refs[n:2 * n]
        send_sems, recv_sems, local_sems = refs[2 * n:]
        x, y, c = _mesh_pos()
        me, sibling = (x, y, c), (x, y, 1 - c)
        chips = [(1 - x, y), (x, 1 - y), (1 - x, 1 - y)]

        def slot(a, px, py, pc):
            return outs[a].at[4 * px + 2 * py + pc]

        def copy(a, k, block, to, src=None):
            return pltpu.make_async_remote_copy(
                src_ref=slot(a, *block) if src is None else src, dst_ref=slot(a, *block),
                send_sem=send_sems.at[a, k], recv_sem=recv_sems.at[a, k], device_id=to, device_id_type=MESH)

        mine = [pltpu.make_async_copy(ins[a], slot(a, *me), local_sems.at[a]) for a in range(n)]
        for cp in mine:
            cp.start()
        first = []
        for a in range(n):
            first.append(copy(a, 0, me, sibling, src=ins[a]))
            first += [copy(a, 1 + j, me, (*chip, c), src=ins[a]) for j, chip in enumerate(chips)]
        for cp in first:
            cp.start()
        passed = []
        for a in range(n):
            for j, chip in enumerate(chips):
                copy(a, 1 + j, (*chip, c), me).wait_recv()
                fwd = copy(a, 4 + j, (*chip, c), sibling)
                fwd.start()
                passed.append(fwd)
        for a in range(n):
            copy(a, 0, sibling, me).wait_recv()
            for j, chip in enumerate(chips):
                copy(a, 4 + j, (*chip, 1 - c), me).wait_recv()
        for cp in first + passed:
            cp.wait_send()
        for cp in mine:
            cp.wait()

    outs = pl.pallas_call(
        body, name=name,
        out_shape=[jax.ShapeDtypeStruct((N_DEV,) + a.shape, a.dtype) for a in arrs],
        in_specs=[ANY] * n, out_specs=[ANY] * n,
        scratch_shapes=[pltpu.SemaphoreType.DMA((n, 7)), pltpu.SemaphoreType.DMA((n, 7)), pltpu.SemaphoreType.DMA((n,))],
    )(*arrs)
    return list(outs)


def _shard_of(ref, kind, j):
    if kind == "col":
        w = ref.shape[1] // N_DEV
        return ref.at[:, pl.ds(j * w, w)]
    h = ref.shape[0] // N_DEV
    return ref.at[pl.ds(j * h, h), :]


def _shard_shape(shape, kind):
    return (shape[0], shape[1] // N_DEV) if kind == "col" else (shape[0] // N_DEV, shape[1])


def _exchange_cores(grads, kinds, name):
    n = len(grads)

    def body(*refs):
        ins, outs = refs[:n], refs[n:2 * n]
        send_sems, recv_sems = refs[2 * n:]
        x, y, c = _mesh_pos()
        sibling = (x, y, 1 - c)
        sends = []
        for a in range(n):
            for s in range(4):
                sends.append(pltpu.make_async_remote_copy(
                    src_ref=_shard_of(ins[a], kinds[a], 2 * s + (1 - c)), dst_ref=outs[a].at[s],
                    send_sem=send_sems.at[a, s], recv_sem=recv_sems.at[a, s], device_id=sibling, device_id_type=MESH))
        for cp in sends:
            cp.start()
        for a in range(n):
            for s in range(4):
                pltpu.make_async_remote_copy(
                    src_ref=outs[a].at[s], dst_ref=outs[a].at[s],
                    send_sem=send_sems.at[a, s], recv_sem=recv_sems.at[a, s], device_id=sibling, device_id_type=MESH).wait_recv()
        for cp in sends:
            cp.wait_send()

    outs = pl.pallas_call(
        body, name=name,
        out_shape=[jax.ShapeDtypeStruct((4,) + _shard_shape(g.shape, k), F32) for g, k in zip(grads, kinds)],
        in_specs=[ANY] * n, out_specs=[ANY] * n,
        scratch_shapes=[pltpu.SemaphoreType.DMA((n, 4)), pltpu.SemaphoreType.DMA((n, 4))],
    )(*grads)
    return list(outs)


def _exchange_chips(parts, name):
    n = len(parts)

    def body(*refs):
        ins, outs = refs[:n], refs[n:2 * n]
        send_sems, recv_sems = refs[2 * n:]
        x, y, c = _mesh_pos()
        my_slot = 2 * x + y
        chips = [(1 - x, y), (x, 1 - y), (1 - x, 1 - y)]
        sends = []
        for a in range(n):
            for j, (px, py) in enumerate(chips):
                sends.append(pltpu.make_async_remote_copy(
                    src_ref=ins[a].at[2 * px + py], dst_ref=outs[a].at[my_slot],
                    send_sem=send_sems.at[a, j], recv_sem=recv_sems.at[a, j], device_id=(px, py, c), device_id_type=MESH))
        for cp in sends:
            cp.start()
        for a in range(n):
            for j, (px, py) in enumerate(chips):
                pltpu.make_async_remote_copy(
                    src_ref=outs[a].at[2 * px + py], dst_ref=outs[a].at[2 * px + py],
                    send_sem=send_sems.at[a, j], recv_sem=recv_sems.at[a, j], device_id=(px, py, c), device_id_type=MESH).wait_recv()
        for cp in sends:
            cp.wait_send()

    outs = pl.pallas_call(
        body, name=name,
        out_shape=[jax.ShapeDtypeStruct(p.shape, p.dtype) for p in parts],
        in_specs=[ANY] * n, out_specs=[ANY] * n,
        scratch_shapes=[pltpu.SemaphoreType.DMA((n, 3)), pltpu.SemaphoreType.DMA((n, 3))],
    )(*parts)
    return list(outs)


def _row_tile(rows):
    for t in (256, 128, 64, 32, 16, 8):
        if rows % t == 0:
            return t
    return rows


def _wire_dtype(rows):
    return BF16 if rows % 16 == 0 else F32


def _add_own(core, g, recv, kind, name):
    ns, R, C = recv.shape
    tr = _row_tile(R)
    nr = R // tr
    if kind == "col":
        g_spec = pl.BlockSpec((tr, C), lambda s, r, c_ref: (r, 2 * s + c_ref[0]))
    else:
        g_spec = pl.BlockSpec((tr, C), lambda s, r, c_ref: ((2 * s + c_ref[0]) * nr + r, 0))
    slab = pl.BlockSpec((None, tr, C), lambda s, r, c_ref: (s, r, 0))

    def body(c_ref, g_ref, r_ref, o32_ref, o16_ref):
        t = g_ref[...] + r_ref[...]
        o32_ref[...] = t
        o16_ref[...] = t.astype(o16_ref.dtype)

    return pl.pallas_call(
        body, name=name,
        grid_spec=pltpu.PrefetchScalarGridSpec(num_scalar_prefetch=1, grid=(ns, nr), in_specs=[g_spec, slab], out_specs=[slab, slab]),
        out_shape=[jax.ShapeDtypeStruct((ns, R, C), F32), jax.ShapeDtypeStruct((ns, R, C), _wire_dtype(R))],
        compiler_params=_cparams(2),
    )(core, g, recv)


def _adamw_math(w, g, m, v):
    m2 = ADAM_B1 * m + (1.0 - ADAM_B1) * g
    v2 = ADAM_B2 * v + (1.0 - ADAM_B2) * (g * g)
    m_hat = m2 / (1.0 - ADAM_B1 ** ADAM_STEP)
    v_hat = v2 / (1.0 - ADAM_B2 ** ADAM_STEP)
    delta = -ADAM_LR * (m_hat / (jnp.sqrt(v_hat) + ADAM_EPS) + ADAM_WD * w)
    return delta, m2, v2


def _sum_adamw(parts, w, m, v, name):
    n, R, C = parts.shape
    tr = _row_tile(R)

    def body(p_ref, w_ref, m_ref, v_ref, g_ref, d_ref, m_out, v_out):
        g = p_ref[0]
        for k in range(1, n):
            g = g + p_ref[k]
        g_ref[...] = g
        d_ref[...], m_out[...], v_out[...] = _adamw_math(w_ref[...], g, m_ref[...], v_ref[...])

    blk = pl.BlockSpec((tr, C), lambda r: (r, 0))
    return pl.pallas_call(
        body, name=name, grid=(R // tr,),
        in_specs=[pl.BlockSpec((n, tr, C), lambda r: (0, r, 0)), blk, blk, blk],
        out_specs=[blk] * 4, out_shape=[jax.ShapeDtypeStruct((R, C), F32)] * 4, compiler_params=_cparams(1),
    )(parts, w, m, v)


def _sum_chips_adamw(chip, own, recv, w, m, v, name):
    n, R, C = recv.shape
    tr = _row_tile(R)

    def body(chip_ref, own_ref, r_ref, w_ref, m_ref, v_ref, g_ref, d_ref, m_out, v_out):
        g = None
        for k in range(n):
            term = jnp.where(chip_ref[0] == k, own_ref[...], r_ref[k].astype(F32))
            g = term if g is None else g + term
        g_ref[...] = g
        d_ref[...], m_out[...], v_out[...] = _adamw_math(w_ref[...], g, m_ref[...], v_ref[...])

    blk = pl.BlockSpec((tr, C), lambda r, chip_ref: (r, 0))
    return pl.pallas_call(
        body, name=name,
        grid_spec=pltpu.PrefetchScalarGridSpec(
            num_scalar_prefetch=1, grid=(R // tr,),
            in_specs=[pl.BlockSpec((None, tr, C), lambda r, chip_ref: (chip_ref[0], r, 0)),
                      pl.BlockSpec((n, tr, C), lambda r, chip_ref: (0, r, 0)), blk, blk, blk],
            out_specs=[blk] * 4),
        out_shape=[jax.ShapeDtypeStruct((R, C), F32)] * 4, compiler_params=_cparams(1),
    )(chip, own, recv, w, m, v)


def _ada_fwd(c, w_ada_g, b_ada):
    nb, D, W = w_ada_g.shape

    def body(c_ref, w_ref, b_ref, o_ref):
        cv = c_ref[...]
        ca = jnp.broadcast_to(cv * _sigmoid(cv), (8, D)).astype(BF16)
        o_ref[...] = _dot(ca, w_ref[...])[0:1, :] + b_ref[...]

    return pl.pallas_call(
        body, name="ada_fwd", grid=(nb,),
        in_specs=[_full((1, D)), pl.BlockSpec((None, D, W), lambda j: (j, 0, 0)), pl.BlockSpec((1, W), lambda j: (0, j))],
        out_specs=pl.BlockSpec((1, W), lambda j: (0, j)),
        out_shape=jax.ShapeDtypeStruct((1, nb * W), F32), compiler_params=_cparams(1),
    )(c, w_ada_g, b_ada)


def _ada_bwd(me, small_g, D, W, c_block):
    def body(me_ref, c_ref, dm_ref, o_ref):
        cv = c_ref[...]
        ca = (cv * _sigmoid(cv)).astype(BF16).astype(F32)
        dm = dm_ref[...].astype(BF16).astype(F32)
        o_ref[...] = lax.dot_general(ca, dm, (((0,), (0,)), ((), ())), precision=lax.Precision.HIGHEST,
                                     preferred_element_type=F32)

    return pl.pallas_call(
        body, name="ada_bwd",
        grid_spec=pltpu.PrefetchScalarGridSpec(
            num_scalar_prefetch=1, grid=(1,),
            in_specs=[pl.BlockSpec((N_DEV, D), lambda i, me_ref: (0, c_block)),
                      pl.BlockSpec((N_DEV, W), lambda i, me_ref: (0, me_ref[0]))],
            out_specs=pl.BlockSpec((D, W), lambda i, me_ref: (0, 0))),
        out_shape=jax.ShapeDtypeStruct((D, W), F32), compiler_params=_cparams(1),
    )(me, small_g, small_g)


def _fwd_in(x, mod, w_in_g, b_in):
    S, D = x.shape
    nb, _, wb = w_in_g.shape
    N = nb * wb
    TM = TOKEN_TILE

    def body(x_ref, mod_ref, w_hbm, b_ref, z_ref, h_ref, w_v, sems):
        @pl.when(pl.program_id(0) == 0)
        def _():
            _load_cols(w_hbm, w_v, sems)

        xhat, _ = _ln(x_ref[...])
        hb = (xhat * (1.0 + mod_ref[:, D:2 * D]) + mod_ref[:, 0:D]).astype(BF16)
        h_ref[...] = hb
        z_ref[...] = _dot(hb, w_v[...]) + b_ref[...]

    return pl.pallas_call(
        body, name="fwd_in", grid=(S // TM,),
        in_specs=[pl.BlockSpec((TM, D), lambda i: (i, 0)), _full(mod.shape), ANY, _full((1, N))],
        out_specs=[pl.BlockSpec((TM, N), lambda i: (i, 0)), pl.BlockSpec((TM, D), lambda i: (i, 0))],
        out_shape=[jax.ShapeDtypeStruct((S, N), F32), jax.ShapeDtypeStruct((S, D), BF16)],
        scratch_shapes=[pltpu.VMEM((D, N), BF16), pltpu.SemaphoreType.DMA((nb,))],
        compiler_params=_cparams(1),
    )(x, mod, w_in_g, b_in)


def _rows_from(win, o, R):
    if o % 8 == 0:
        return win[o:o + R, :]
    return pltpu.roll(win, win.shape[0] - o, axis=0)[0:R, :]


def _causal_taps(buf, w_ref, o_ref, K, halo, TM, D):
    R, C = CONV_ROWS, CONV_LANES
    base = halo - (K - 1)

    def step(j, carry):
        r0 = pl.multiple_of(j * R, R)
        for c0 in range(0, D, C):
            win = buf[pl.ds(r0, R + halo), c0:c0 + C]
            acc = jnp.zeros((R, C), F32)
            for k in range(K):
                acc = acc + w_ref[k:k + 1, c0:c0 + C] * _rows_from(win, base + k, R)
            o_ref[pl.ds(r0, R), c0:c0 + C] = acc
        return carry

    lax.fori_loop(0, TM // R, step, 0)


def _fwd_mix(z, x, mod, wa, ba, lag, lab, w_a_out, b_a_out, wb, w_b_out, w_o, b_o):
    S, D = x.shape
    N = z.shape[1]
    TM = TOKEN_TILE
    KA, KB = wa.shape[0], wb.shape[0]

    def body(z_ref, x_ref, mod_ref, wa_ref, ba_ref, lag_ref, lab_ref, wao_hbm, bao_ref, wb_ref, wbo_hbm, wo_hbm, bo_ref,
             u1_ref, ya_ref, yb_ref, q_ref, out_ref, r1_ref, u2_ref, v_ref, mg_ref,
             ubuf, pbuf, wao, wbo, wo, sems):
        i = pl.program_id(0)

        @pl.when(i == 0)
        def _():
            _load_whole([(wao_hbm, wao), (wbo_hbm, wbo), (wo_hbm, wo)], sems)
            ubuf[0:HALO_A, :] = jnp.zeros((HALO_A, D), F32)
            pbuf[0:HALO_B, :] = jnp.zeros((HALO_B, D), F32)

        ubuf[HALO_A:HALO_A + TM, :] = z_ref[:, 0:D] * _sigmoid(z_ref[:, D:2 * D])
        _causal_taps(ubuf, wa_ref, u1_ref, KA, HALO_A, TM, D)
        ubuf[0:HALO_A, :] = ubuf[TM:TM + HALO_A, :]
        u1 = u1_ref[...] + ba_ref[...]
        u1_ref[...] = u1
        xa, _ = _ln(u1)
        l = xa * lag_ref[...] + lab_ref[...]
        u2 = (l * _sigmoid(l)).astype(BF16)
        u2_ref[...] = u2
        ya = _dot(u2, wao[...]) + bao_ref[...]
        ya_ref[...] = ya.astype(BF16)
        pbuf[HALO_B:HALO_B + TM, :] = z_ref[:, 3 * D:4 * D] * z_ref[:, 4 * D:5 * D]
        _causal_taps(pbuf, wb_ref, q_ref, KB, HALO_B, TM, D)
        pbuf[0:HALO_B, :] = pbuf[TM:TM + HALO_B, :]
        v = (z_ref[:, 2 * D:3 * D] * q_ref[...]).astype(BF16)
        v_ref[...] = v
        yb = _dot(v, wbo[...])
        yb_ref[...] = yb.astype(BF16)
        mg = (_sigmoid(z_ref[:, 5 * D:6 * D]) * ya + _sigmoid(z_ref[:, 6 * D:7 * D]) * yb).astype(BF16)
        mg_ref[...] = mg
        out = _dot(mg, wo[...]) + bo_ref[...]
        out_ref[...] = out.astype(BF16)
        r1_ref[...] = ALPHA * x_ref[...] + (1.0 + mod_ref[:, 2 * D:3 * D]) * out

    tile = pl.BlockSpec((TM, D), lambda i: (i, 0))
    vec = _full((1, D))
    return pl.pallas_call(
        body, name="fwd_mix", grid=(S // TM,),
        in_specs=[pl.BlockSpec((TM, N), lambda i: (i, 0)), tile, _full(mod.shape), _full((KA, D)), vec, vec, vec, ANY, vec,
                  _full((KB, D)), ANY, ANY, vec],
        out_specs=[tile] * 9,
        out_shape=[jax.ShapeDtypeStruct((S, D), dt) for dt in (F32, BF16, BF16, F32, BF16, F32, BF16, BF16, BF16)],
        scratch_shapes=[pltpu.VMEM((TM + HALO_A, D), F32), pltpu.VMEM((TM + HALO_B, D), F32),
                        pltpu.VMEM((D, D), BF16), pltpu.VMEM((D, D), BF16), pltpu.VMEM((D, D), BF16),
                        pltpu.SemaphoreType.DMA((3,))],
        compiler_params=_cparams(1),
    )(z, x, mod, wa, ba, lag, lab, w_a_out, b_a_out, wb, w_b_out, w_o, b_o)


FFN_ROWS = ("b_down", "ln2_g", "ln2_b", "gate2", "scale2", "shift2", "loss")


def _ffn(r1, tgt, mod, ln1_g, ln1_b, w_up_g, b_up, w_down, b_down, ln2_g, ln2_b):
    S, D = r1.shape
    nb, _, wb = w_up_g.shape
    FF = nb * wb
    TM = TOKEN_TILE
    CF = 1024
    n_tiles = S // TM

    def body(r1_ref, t_ref, mod_ref, g1_ref, b1_ref, wup_hbm, bup_ref, wdn_hbm, bdn_ref, g2_ref, b2_ref,
             dx1_ref, h2_ref, f_ref, dhu_ref, do2_ref, acc_ref, accup_ref,
             wup, wdn, relu_buf, sems, sem2):
        i = pl.program_id(0)

        @pl.when(i == 0)
        def _():
            _load_cols(wup_hbm, wup, sems)
            _load_whole([(wdn_hbm, wdn)], sem2)
            acc_ref[...] = jnp.zeros(acc_ref.shape, F32)
            accup_ref[...] = jnp.zeros(accup_ref.shape, F32)

        scale2, shift2, gate2 = mod_ref[:, 4 * D:5 * D], mod_ref[:, 3 * D:4 * D], mod_ref[:, 5 * D:6 * D]
        xhat1, _ = _ln(r1_ref[...])
        x1 = xhat1 * g1_ref[...] + b1_ref[...]
        xh0, rstd0 = _ln(x1)
        h2 = (xh0 * (1.0 + scale2) + shift2).astype(BF16)
        h2_ref[...] = h2
        out2 = jnp.zeros((TM, D), F32) + bdn_ref[...]
        for c0 in range(0, FF, CF):
            hu = _dot(h2, wup[:, c0:c0 + CF]) + bup_ref[:, c0:c0 + CF]
            rl = jnp.maximum(hu, 0.0)
            relu_buf[:, c0:c0 + CF] = rl
            fb = (rl * rl).astype(BF16)
            f_ref[:, c0:c0 + CF] = fb
            out2 = out2 + _dot(fb, wdn[c0:c0 + CF, :])
        r2 = ALPHA * x1 + (1.0 + gate2) * out2
        xh2, rstd2 = _ln(r2)
        e = xh2 * g2_ref[...] + b2_ref[...] - t_ref[...]
        acc_ref[6:7, :] += _colsum(e * e)
        dy = e * (1.0 / D)
        acc_ref[1:2, :] += _colsum(dy * xh2)
        acc_ref[2:3, :] += _colsum(dy)
        dr2 = _ln_bwd(dy * g2_ref[...], xh2, rstd2)
        acc_ref[3:4, :] += _colsum(dr2 * out2)
        do2 = (1.0 + gate2) * dr2
        acc_ref[0:1, :] += _colsum(do2)
        do2b = do2.astype(BF16)
        do2_ref[...] = do2b
        dh2 = jnp.zeros((TM, D), F32)
        for c0 in range(0, FF, CF):
            dhu = _dot_nt(do2b, wdn[c0:c0 + CF, :]) * (2.0 * relu_buf[:, c0:c0 + CF])
            accup_ref[:, c0:c0 + CF] += _colsum(dhu)
            dhub = dhu.astype(BF16)
            dhu_ref[:, c0:c0 + CF] = dhub
            dh2 = dh2 + _dot_nt(dhub, wup[:, c0:c0 + CF])
        acc_ref[4:5, :] += _colsum(dh2 * xh0)
        acc_ref[5:6, :] += _colsum(dh2)
        dx1_ref[...] = ALPHA * dr2 + _ln_bwd(dh2 * (1.0 + scale2), xh0, rstd0)

        @pl.when(i == n_tiles - 1)
        def _():
            tot = jnp.sum(acc_ref[6:7, :], axis=-1, keepdims=True) * (0.5 / D)
            acc_ref[6:7, :] = jnp.broadcast_to(tot, (1, D))

    tile = pl.BlockSpec((TM, D), lambda i: (i, 0))
    wide = pl.BlockSpec((TM, FF), lambda i: (i, 0))
    vec = _full((1, D))
    return pl.pallas_call(
        body, name="ffn", grid=(n_tiles,),
        in_specs=[tile, tile, _full(mod.shape), vec, vec, ANY, _full((1, FF)), ANY, vec, vec, vec],
        out_specs=[tile, tile, wide, wide, tile, _full((8, D)), _full((1, FF))],
        out_shape=[jax.ShapeDtypeStruct((S, D), F32), jax.ShapeDtypeStruct((S, D), BF16), jax.ShapeDtypeStruct((S, FF), BF16),
                   jax.ShapeDtypeStruct((S, FF), BF16), jax.ShapeDtypeStruct((S, D), BF16),
                   jax.ShapeDtypeStruct((8, D), F32), jax.ShapeDtypeStruct((1, FF), F32)],
        scratch_shapes=[pltpu.VMEM((D, FF), BF16), pltpu.VMEM((FF, D), BF16), pltpu.VMEM((TM, FF), F32),
                        pltpu.SemaphoreType.DMA((nb,)), pltpu.SemaphoreType.DMA((1,))],
        compiler_params=_cparams(1),
    )(r1, tgt, mod, ln1_g, ln1_b, w_up_g, b_up, w_down, b_down, ln2_g, ln2_b)


def _dw(a, b, name, msplit=1, nsplit=1):
    S, M = a.shape
    N = b.shape[1]
    TK = min(DW_TILE, S)
    mb, nbk = M // msplit, N // nsplit

    def body(a_ref, b_ref, o_ref):
        @pl.when(pl.program_id(2) == 0)
        def _():
            o_ref[...] = jnp.zeros(o_ref.shape, F32)

        o_ref[...] += _dot_tn(a_ref[...], b_ref[...])

    return pl.pallas_call(
        body, name=name, grid=(msplit, nsplit, S // TK),
        in_specs=[pl.BlockSpec((TK, mb), lambda i, j, k: (k, i)), pl.BlockSpec((TK, nbk), lambda i, j, k: (k, j))],
        out_specs=pl.BlockSpec((mb, nbk), lambda i, j, k: (i, j)),
        out_shape=jax.ShapeDtypeStruct((M, N), F32), compiler_params=_cparams(3),
    )(a, b)


BWD1_ROWS = ("ln1_g", "ln1_b", "gate1", "b_o", "b_a_out", "ln_a_g", "ln_a_b", "conv_a_b", "b_in_gb", "b_in_ga", "b_in_gbb")


def _bwd_mix1(dx1, r1, out, ya, yb, q, u1, z, mod, ln1_g, lag, lab, w_o, w_a_out, w_b_out):
    S, D = dx1.shape
    TM = TOKEN_TILE

    def body(dx1_ref, r1_ref, out_ref, ya_ref, yb_ref, q_ref, u1_ref, zgb_ref, zga_ref, zgg_ref, mod_ref, g1_ref, lag_ref, lab_ref,
             wo_hbm, wao_hbm, wbo_hbm,
             dxp_ref, du1_ref, dq_ref, dzc_ref, dout_ref, dya_ref, dyb_ref, acc_ref,
             wo, wao, wbo, sems):
        @pl.when(pl.program_id(0) == 0)
        def _():
            _load_whole([(wo_hbm, wo), (wao_hbm, wao), (wbo_hbm, wbo)], sems)
            acc_ref[...] = jnp.zeros(acc_ref.shape, F32)

        dx1v = dx1_ref[...]
        xhat1, rstd1 = _ln(r1_ref[...])
        acc_ref[0:1, :] += _colsum(dx1v * xhat1)
        acc_ref[1:2, :] += _colsum(dx1v)
        dr1 = _ln_bwd(dx1v * g1_ref[...], xhat1, rstd1)
        dxp_ref[...] = ALPHA * dr1
        acc_ref[2:3, :] += _colsum(dr1 * out_ref[...].astype(F32))
        dout = (1.0 + mod_ref[:, 2 * D:3 * D]) * dr1
        acc_ref[3:4, :] += _colsum(dout)
        doutb = dout.astype(BF16)
        dout_ref[...] = doutb
        dmg = _dot_nt(doutb, wo[...])
        sga, sgb = _sigmoid(zga_ref[...]), _sigmoid(zgg_ref[...])
        dga = dmg * ya_ref[...].astype(F32) * sga * (1.0 - sga)
        dgb = dmg * yb_ref[...].astype(F32) * sgb * (1.0 - sgb)
        acc_ref[9:10, :] += _colsum(dga)
        acc_ref[10:11, :] += _colsum(dgb)
        dzc_ref[:, D:2 * D] = dga.astype(BF16)
        dzc_ref[:, 2 * D:3 * D] = dgb.astype(BF16)
        dya = dmg * sga
        acc_ref[4:5, :] += _colsum(dya)
        dyab = dya.astype(BF16)
        dya_ref[...] = dyab
        dybb = (dmg * sgb).astype(BF16)
        dyb_ref[...] = dybb
        du2 = _dot_nt(dyab, wao[...])
        xa, rstda = _ln(u1_ref[...])
        l = xa * lag_ref[...] + lab_ref[...]
        sl = _sigmoid(l)
        dl = du2 * (sl * (1.0 + l * (1.0 - sl)))
        acc_ref[5:6, :] += _colsum(dl * xa)
        acc_ref[6:7, :] += _colsum(dl)
        du1 = _ln_bwd(dl * lag_ref[...], xa, rstda)
        acc_ref[7:8, :] += _colsum(du1)
        du1_ref[...] = du1
        dv = _dot_nt(dybb, wbo[...])
        dgbk = dv * q_ref[...]
        acc_ref[8:9, :] += _colsum(dgbk)
        dzc_ref[:, 0:D] = dgbk.astype(BF16)
        dq_ref[...] = dv * zgb_ref[...]

    tile = pl.BlockSpec((TM, D), lambda i: (i, 0))
    vec = _full((1, D))
    zcol = lambda k: pl.BlockSpec((TM, D), lambda i: (i, k))
    return pl.pallas_call(
        body, name="bwd_mix1", grid=(S // TM,),
        in_specs=[tile] * 7 + [zcol(2), zcol(5), zcol(6), _full(mod.shape), vec, vec, vec, ANY, ANY, ANY],
        out_specs=[tile, tile, tile, pl.BlockSpec((TM, 3 * D), lambda i: (i, 0)), tile, tile, tile, _full((16, D))],
        out_shape=[jax.ShapeDtypeStruct((S, D), F32)] * 3 + [jax.ShapeDtypeStruct((S, 3 * D), BF16)]
        + [jax.ShapeDtypeStruct((S, D), BF16)] * 3 + [jax.ShapeDtypeStruct((16, D), F32)],
        scratch_shapes=[pltpu.VMEM((D, D), BF16)] * 3 + [pltpu.SemaphoreType.DMA((3,))],
        compiler_params=_cparams(1),
    )(dx1, r1, out, ya, yb, q, u1, z, z, z, mod, ln1_g, lag, lab, w_o, w_a_out, w_b_out)


def _anticausal_taps(dbuf, u_ref, w_ref, o_ref, dwacc, K, halo, TM, D):
    R, C = CONV_ROWS, CONV_LANES

    def step(jr, carry):
        r0 = pl.multiple_of(jr * R, R)
        for c0 in range(0, D, C):
            win = dbuf[pl.ds(r0, R + halo), c0:c0 + C]
            uc = u_ref[pl.ds(r0, R), c0:c0 + C]
            acc = jnp.zeros((R, C), F32)
            for j in range(K):
                sh = _rows_from(win, j, R)
                k = K - 1 - j
                acc = acc + w_ref[k:k + 1, c0:c0 + C] * sh
                pr = uc * sh
                part = pr[0:8, :]
                for s in range(8, R, 8):
                    part = part + pr[s:s + 8, :]
                dwacc[k, :, c0:c0 + C] += part
            o_ref[pl.ds(r0, R), c0:c0 + C] = acc
        return carry

    lax.fori_loop(0, TM // R, step, 0)


def _bwd_mix2(du1, dq, z, dzc, x, dxp, mod, wa, wb, w_in_g):
    S, D = x.shape
    N = z.shape[1]
    nb, _, wbk = w_in_g.shape
    TM = TOKEN_TILE
    KA, KB = wa.shape[0], wb.shape[0]
    n_tiles = S // TM

    def body(du1_ref, dq_ref, zav_ref, zc_ref, zx_ref, dzc_ref, x_ref, dxp_ref, mod_ref, wa_ref, wb_ref, win_hbm,
             dz_ref, gx_ref, dwa_ref, dwb_ref, acc_ref, dbin_ref,
             dbuf, qbuf, ubuf, pbuf, obuf, dwa_acc, dwb_acc, w_v, sems):
        i = pl.program_id(0)

        @pl.when(i == 0)
        def _():
            _load_cols(win_hbm, w_v, sems)
            dbuf[TM:TM + HALO_A, :] = jnp.zeros((HALO_A, D), F32)
            qbuf[TM:TM + HALO_B, :] = jnp.zeros((HALO_B, D), F32)
            dwa_acc[...] = jnp.zeros(dwa_acc.shape, F32)
            dwb_acc[...] = jnp.zeros(dwb_acc.shape, F32)
            acc_ref[...] = jnp.zeros(acc_ref.shape, F32)
            dbin_ref[...] = jnp.zeros(dbin_ref.shape, F32)

        a_val = zav_ref[:, 0:D]
        sa = _sigmoid(zav_ref[:, D:2 * D])
        ubuf[...] = a_val * sa
        dbuf[0:TM, :] = du1_ref[...]
        _anticausal_taps(dbuf, ubuf, wa_ref, obuf, dwa_acc, KA, HALO_A, TM, D)
        dbuf[TM:TM + HALO_A, :] = dbuf[0:HALO_A, :]
        du0 = obuf[...]
        dav = du0 * sa
        dag = du0 * a_val * sa * (1.0 - sa)
        dbin_ref[:, 0:D] += _colsum(dav)
        dbin_ref[:, D:2 * D] += _colsum(dag)
        dz_ref[:, 0:D] = dav.astype(BF16)
        dz_ref[:, D:2 * D] = dag.astype(BF16)
        pbuf[...] = zc_ref[...] * zx_ref[...]
        qbuf[0:TM, :] = dq_ref[...]
        _anticausal_taps(qbuf, pbuf, wb_ref, obuf, dwb_acc, KB, HALO_B, TM, D)
        qbuf[TM:TM + HALO_B, :] = qbuf[0:HALO_B, :]
        dp = obuf[...]
        dgc = dp * zx_ref[...]
        dgx = dp * zc_ref[...]
        dbin_ref[:, 3 * D:4 * D] += _colsum(dgc)
        dbin_ref[:, 4 * D:5 * D] += _colsum(dgx)
        dz_ref[:, 3 * D:4 * D] = dgc.astype(BF16)
        dz_ref[:, 4 * D:5 * D] = dgx.astype(BF16)
        dz_ref[:, 2 * D:3 * D] = dzc_ref[:, 0:D]
        dz_ref[:, 5 * D:7 * D] = dzc_ref[:, D:3 * D]
        dh = _dot_nt(dz_ref[...], w_v[...])
        xhat, rstd = _ln(x_ref[...])
        acc_ref[0:1, :] += _colsum(dh * xhat)
        acc_ref[1:2, :] += _colsum(dh)
        gx_ref[...] = dxp_ref[...] + _ln_bwd(dh * (1.0 + mod_ref[:, D:2 * D]), xhat, rstd)

        @pl.when(i == n_tiles - 1)
        def _():
            for k in range(KA):
                dwa_ref[k:k + 1, :] = jnp.sum(dwa_acc[k], axis=0, keepdims=True)
            for k in range(KB):
                dwb_ref[k:k + 1, :] = jnp.sum(dwb_acc[k], axis=0, keepdims=True)

    rev = lambda i: n_tiles - 1 - i
    tile = pl.BlockSpec((TM, D), lambda i: (rev(i), 0))
    zcol = lambda k: pl.BlockSpec((TM, D), lambda i: (rev(i), k))
    return pl.pallas_call(
        body, name="bwd_mix2", grid=(n_tiles,),
        in_specs=[tile, tile, pl.BlockSpec((TM, 2 * D), lambda i: (rev(i), 0)), zcol(3), zcol(4),
                  pl.BlockSpec((TM, 3 * D), lambda i: (rev(i), 0)), tile, tile, _full(mod.shape), _full((KA, D)), _full((KB, D)), ANY],
        out_specs=[pl.BlockSpec((TM, N), lambda i: (rev(i), 0)), tile, _full((KA, D)), _full((KB, D)), _full((8, D)), _full((1, N))],
        out_shape=[jax.ShapeDtypeStruct((S, N), BF16), jax.ShapeDtypeStruct((S, D), F32), jax.ShapeDtypeStruct((KA, D), F32),
                   jax.ShapeDtypeStruct((KB, D), F32), jax.ShapeDtypeStruct((8, D), F32), jax.ShapeDtypeStruct((1, N), F32)],
        scratch_shapes=[pltpu.VMEM((TM + HALO_A, D), F32), pltpu.VMEM((TM + HALO_B, D), F32), pltpu.VMEM((TM, D), F32),
                        pltpu.VMEM((TM, D), F32), pltpu.VMEM((TM, D), F32), pltpu.VMEM((KA, 8, D), F32), pltpu.VMEM((KB, 8, D), F32),
                        pltpu.VMEM((D, N), BF16), pltpu.SemaphoreType.DMA((nb,))],
        compiler_params=_cparams(1),
    )(du1, dq, z, z, z, dzc, x, dxp, mod, wa, wb, w_in_g)


def _local_grads(x, c, tgt, gw, vecs):
    w_ada_g, w_in_g, w_a_out, w_b_out, w_o, w_up_g, w_down, wa, wb = gw
    D = x.shape[1]
    mod = _ada_fwd(c, w_ada_g, vecs["b_ada"])
    z, h1 = _fwd_in(x, mod, w_in_g, vecs["b_in"])
    u1, ya, yb, q, out, r1, u2, v, mg = _fwd_mix(
        z, x, mod, wa, vecs["conv_a_b"], vecs["ln_a_g"], vecs["ln_a_b"], w_a_out, vecs["b_a_out"], wb, w_b_out, w_o, vecs["b_o"])
    dx1, h2, f, dhu, do2, acc_f, acc_up = _ffn(
        r1, tgt, mod, vecs["ln1_g"], vecs["ln1_b"], w_up_g, vecs["b_up"], w_down, vecs["b_down"], vecs["ln2_g"], vecs["ln2_b"])
    g_up = _dw(h2, dhu, "dw_up", nsplit=2)
    g_down = _dw(f, do2, "dw_down", msplit=2)
    dxp, du1, dq, dzc, doutb, dyab, dybb, acc_1 = _bwd_mix1(
        dx1, r1, out, ya, yb, q, u1, z, mod, vecs["ln1_g"], vecs["ln_a_g"], vecs["ln_a_b"], w_o, w_a_out, w_b_out)
    g_o = _dw(mg, doutb, "dw_o")
    g_a_out = _dw(u2, dyab, "dw_a_out")
    g_b_out = _dw(v, dybb, "dw_b_out")
    dz, gx, g_wa, g_wb, acc_2, db_in = _bwd_mix2(du1, dq, z, dzc, x, dxp, mod, wa, wb, w_in_g)
    g_in = _dw(h1, dz, "dw_in", nsplit=4)
    row = lambda acc, k: acc[k:k + 1, :]
    dmod = jnp.concatenate([row(acc_2, 1), row(acc_2, 0), row(acc_1, 2), row(acc_f, 5), row(acc_f, 4), row(acc_f, 3)], axis=1)
    db_in = jnp.concatenate([db_in[:, 0:2 * D], row(acc_1, 8), db_in[:, 3 * D:5 * D], row(acc_1, 9), row(acc_1, 10)], axis=1)
    gvec = {
        "b_ada": dmod, "b_in": db_in, "conv_a_b": row(acc_1, 7), "ln_a_g": row(acc_1, 5), "ln_a_b": row(acc_1, 6),
        "b_a_out": row(acc_1, 4), "b_o": row(acc_1, 3), "ln1_g": row(acc_1, 0), "ln1_b": row(acc_1, 1), "b_up": acc_up,
        "b_down": row(acc_f, 0), "ln2_g": row(acc_f, 1), "ln2_b": row(acc_f, 2),
    }
    loss = acc_f[6, 0]
    gmat = {"w_in": g_in, "w_a_out": g_a_out, "w_b_out": g_b_out, "w_o": g_o, "w_up": g_up, "w_down": g_down,
            "conv_a_w": g_wa, "conv_b_w": g_wb}
    return loss, gx, gmat, gvec


WEIGHTS = ("w_ada", "b_ada", "w_in", "b_in", "conv_a_w", "conv_a_b", "ln_a_g", "ln_a_b", "w_a_out", "b_a_out", "conv_b_w",
           "w_b_out", "w_o", "b_o", "ln1_g", "ln1_b", "w_up", "b_up", "w_down", "b_down", "ln2_g", "ln2_b")
VECTORS = ("b_ada", "b_in", "conv_a_b", "ln_a_g", "ln_a_b", "b_a_out", "b_o", "ln1_g", "ln1_b", "b_up", "b_down", "ln2_g", "ln2_b")
MATRICES = ("w_in", "w_a_out", "w_b_out", "w_o", "w_up", "w_down", "conv_a_w", "conv_b_w")
SHARD_KIND = {"w_in": "col", "w_a_out": "row", "w_b_out": "row", "w_o": "row", "w_up": "col", "w_down": "row",
              "conv_a_w": "col", "conv_b_w": "col"}


def kernel(x, c, w_ada, b_ada, w_in, b_in, conv_a_w, conv_a_b, ln_a_g, ln_a_b, w_a_out, b_a_out, conv_b_w, w_b_out, w_o, b_o, ln1_g, ln1_b, w_up, b_up, w_down, b_down, ln2_g, ln2_b, loss_target, m_w_ada, m_b_ada, m_w_in, m_b_in, m_conv_a_w, m_conv_a_b, m_ln_a_g, m_ln_a_b, m_w_a_out, m_b_a_out, m_conv_b_w, m_w_b_out, m_w_o, m_b_o, m_ln1_g, m_ln1_b, m_w_up, m_b_up, m_w_down, m_b_down, m_ln2_g, m_ln2_b, v_w_ada, v_b_ada, v_w_in, v_b_in, v_conv_a_w, v_conv_a_b, v_ln_a_g, v_ln_a_b, v_w_a_out, v_b_a_out, v_conv_b_w, v_w_b_out, v_w_o, v_b_o, v_ln1_g, v_ln1_b, v_w_up, v_b_up, v_w_down, v_b_down, v_ln2_g, v_ln2_b):
    args = dict(locals())
    w = {n: args[n][0] for n in WEIGHTS}
    m = {n: args["m_" + n][0] for n in WEIGHTS}
    v = {n: args["v_" + n][0] for n in WEIGHTS}
    w = {n: (a[None, :] if a.ndim == 1 else a) for n, a in w.items()}
    m = {n: (a[None, :] if a.ndim == 1 else a) for n, a in m.items()}
    v = {n: (a[None, :] if a.ndim == 1 else a) for n, a in v.items()}
    xs, tgt = x[0], loss_target[0]
    S, D = xs.shape
    me = (4 * lax.axis_index("x") + 2 * lax.axis_index("y") + lax.axis_index("c")).astype(jnp.int32).reshape(1)
    core = lax.axis_index("c").astype(jnp.int32).reshape(1)
    chip = (2 * lax.axis_index("x") + lax.axis_index("y")).astype(jnp.int32).reshape(1)

    bf = lambda n: w[n].astype(BF16)
    g = _all_gather([bf("w_ada"), bf("w_in"), bf("w_a_out"), bf("w_b_out"), bf("w_o"), bf("w_up"), bf("w_down"),
                     w["conv_a_w"], w["conv_b_w"]], "gather_weights")
    w_ada_g, w_in_g, w_a_out_g, w_b_out_g, w_o_g, w_up_g, w_down_g, wa_g, wb_g = g
    rows = lambda a: a.reshape(a.shape[0] * a.shape[1], a.shape[2])
    cols = lambda a: jnp.transpose(a, (1, 0, 2)).reshape(a.shape[1], a.shape[0] * a.shape[2])
    gw = (w_ada_g, w_in_g, rows(w_a_out_g), rows(w_b_out_g), rows(w_o_g), w_up_g, rows(w_down_g), cols(wa_g), cols(wb_g))
    vecs = {n: w[n] for n in VECTORS}

    loss, gx, gmat, gvec = _local_grads(xs, c, tgt, gw, vecs)

    small = jnp.concatenate([gvec[n] for n in VECTORS] + [c], axis=1)
    (small_g,) = _all_gather([small], "gather_vectors")
    small_g = small_g.reshape(N_DEV, small.shape[1])
    from_sibling = _exchange_cores([gmat[n] for n in MATRICES], [SHARD_KIND[n] for n in MATRICES], "reduce_cores")
    chip_sums = [_add_own(core, gmat[n], r, SHARD_KIND[n], "add_" + n) for r, n in zip(from_sibling, MATRICES)]
    from_chips = _exchange_chips([wire for _, wire in chip_sums], "reduce_chips")

    res = {}
    for n, (own, _), recv in zip(MATRICES, chip_sums, from_chips):
        res[n] = _sum_chips_adamw(chip, own, recv, w[n], m[n], v[n], "adamw_" + n)
    W = w["w_ada"].shape[1]
    n_vec = small.shape[1] - D
    g_ada = _ada_bwd(me, small_g, D, W, n_vec // D)
    res["w_ada"] = _sum_adamw(g_ada[None], w["w_ada"], m["w_ada"], v["w_ada"], "adamw_w_ada")
    cat = lambda d: jnp.concatenate([d[n] for n in VECTORS], axis=1)
    vg, vd, vm, vv = _sum_adamw(small_g[:, None, 0:n_vec], cat(w), cat(m), cat(v), "adamw_vectors")
    off = 0
    for n in VECTORS:
        k = w[n].shape[1]
        res[n] = tuple(a[:, off:off + k] for a in (vg, vd, vm, vv))
        off += k

    loss = lax.psum(loss, ("x", "y", "c"))
    outs = [loss, gx[None]]
    for t in range(4):
        outs += [res[n][t].reshape(args[n].shape) for n in WEIGHTS]
    return tuple(outs)
```

```python
import functools

import jax
import jax.numpy as jnp
from jax import lax
from jax.experimental import pallas as pl
from jax.experimental.pallas import tpu as pltpu

F32 = jnp.float32
BF16 = jnp.bfloat16
MESH = pl.DeviceIdType.MESH

N_DEV = 8
LN_EPS = 1e-5
DEPTH = 1
ALPHA = (2.0 * DEPTH) ** 0.25
ADAM_LR, ADAM_B1, ADAM_B2, ADAM_EPS, ADAM_WD, ADAM_STEP = 0.001, 0.9, 0.999, 1e-08, 0.01, 10

VMEM_LIMIT = 60 * 1024 * 1024
TOKEN_TILE = 256
DW_TILE = 1024
HALO_A = 32
HALO_B = 8
CONV_ROWS, CONV_LANES = 32, 512


def _cparams(n_grid):
    return pltpu.CompilerParams(dimension_semantics=("arbitrary",) * n_grid, vmem_limit_bytes=VMEM_LIMIT)


def _full(shape):
    return pl.BlockSpec(shape, lambda *_: (0,) * len(shape))


ANY = pl.BlockSpec(memory_space=pl.ANY)


def _ln(x):
    mu = jnp.mean(x, axis=-1, keepdims=True)
    xc = x - mu
    var = jnp.mean(xc * xc, axis=-1, keepdims=True)
    rstd = lax.rsqrt(var + LN_EPS)
    return xc * rstd, rstd


def _ln_bwd(dxhat, xhat, rstd):
    m1 = jnp.mean(dxhat, axis=-1, keepdims=True)
    m2 = jnp.mean(dxhat * xhat, axis=-1, keepdims=True)
    return rstd * (dxhat - m1 - xhat * m2)


def _sigmoid(x):
    return 1.0 / (1.0 + jnp.exp(-x))


def _colsum(a):
    return jnp.sum(a, axis=0, keepdims=True)


def _dot(a, b):
    return jnp.dot(a, b, preferred_element_type=F32)


def _dot_nt(a, b):
    return lax.dot_general(a, b, (((1,), (1,)), ((), ())), preferred_element_type=F32)


def _dot_tn(a, b):
    return lax.dot_general(a, b, (((0,), (0,)), ((), ())), preferred_element_type=F32)


def _load_cols(src_hbm, dst_vmem, sems):
    nblk, _, w = src_hbm.shape
    cps = [pltpu.make_async_copy(src_hbm.at[j], dst_vmem.at[:, pl.ds(j * w, w)], sems.at[j]) for j in range(nblk)]
    for cp in cps:
        cp.start()
    for cp in cps:
        cp.wait()


def _load_whole(pairs, sems):
    cps = [pltpu.make_async_copy(s, d, sems.at[k]) for k, (s, d) in enumerate(pairs)]
    for cp in cps:
        cp.start()
    for cp in cps:
        cp.wait()


def _mesh_pos():
    return lax.axis_index("x"), lax.axis_index("y"), lax.axis_index("c")


def _all_gather(arrs, name):
    n = len(arrs)

    def body(*refs):
        ins, outs = refs[:n], refs[n:2 * n]
        send_sems, recv_sems, local_sems = refs[2 * n:]
        x, y, c = _mesh_pos()
        me, sibling = (x, y, c), (x, y, 1 - c)
        chips = [(1 - x, y), (x, 1 - y), (1 - x, 1 - y)]

        def slot(a, px, py, pc):
            return outs[a].at[4 * px + 2 * py + pc]

        def copy(a, k, block, to, src=None):
            return pltpu.make_async_remote_copy(
                src_ref=slot(a, *block) if src is None else src, dst_ref=slot(a, *block),
                send_sem=send_sems.at[a, k], recv_sem=recv_sems.at[a, k], device_id=to, device_id_type=MESH)

        mine = [pltpu.make_async_copy(ins[a], slot(a, *me), local_sems.at[a]) for a in range(n)]
        for cp in mine:
            cp.start()
        first = []
        for a in range(n):
            first.append(copy(a, 0, me, sibling, src=ins[a]))
            first += [copy(a, 1 + j, me, (*chip, c), src=ins[a]) for j, chip in enumerate(chips)]
        for cp in first:
            cp.start()
        passed = []
        for a in range(n):
            for j, chip in enumerate(chips):
                copy(a, 1 + j, (*chip, c), me).wait_recv()
                fwd = copy(a, 4 + j, (*chip, c), sibling)
                fwd.start()
                passed.append(fwd)
        for a in range(n):
            copy(a, 0, sibling, me).wait_recv()
            for j, chip in enumerate(chips):
                copy(a, 4 + j, (*chip, 1 - c), me).wait_recv()
        for cp in first + passed:
            cp.wait_send()
        for cp in mine:
            cp.wait()

    outs = pl.pallas_call(
        body, name=name,
        out_shape=[jax.ShapeDtypeStruct((N_DEV,) + a.shape, a.dtype) for a in arrs],
        in_specs=[ANY] * n, out_specs=[ANY] * n,
        scratch_shapes=[pltpu.SemaphoreType.DMA((n, 7)), pltpu.SemaphoreType.DMA((n, 7)), pltpu.SemaphoreType.DMA((n,))],
    )(*arrs)
    return list(outs)


def _peer_copies(src_of, dsts, send_sems, recv_sems):
    x, y, c = _mesh_pos()
    me = 4 * x + 2 * y + c
    sends, arrivals = [], []
    for a in range(len(dsts)):
        for k in range(1, N_DEV):
            px, py, pc = (1 - x if k & 4 else x), (1 - y if k & 2 else y), (1 - c if k & 1 else c)
            p = 4 * px + 2 * py + pc
            common = dict(send_sem=send_sems.at[a, k - 1], recv_sem=recv_sems.at[a, k - 1],
                          device_id=(px, py, pc), device_id_type=MESH)
            sends.append(pltpu.make_async_remote_copy(src_ref=src_of(a, p), dst_ref=dsts[a].at[me], **common))
            arrivals.append(pltpu.make_async_remote_copy(src_ref=dsts[a].at[p], dst_ref=dsts[a].at[p], **common))
    return sends, arrivals


def _peer_sems(n):
    return [pltpu.SemaphoreType.DMA((n, N_DEV - 1)), pltpu.SemaphoreType.DMA((n, N_DEV - 1))]


def _shard_of(ref, kind, j):
    if kind == "col":
        w = ref.shape[1] // N_DEV
        return ref.at[:, pl.ds(j * w, w)]
    h = ref.shape[0] // N_DEV
    return ref.at[pl.ds(j * h, h), :]


def _shard_shape(shape, kind):
    return (shape[0], shape[1] // N_DEV) if kind == "col" else (shape[0] // N_DEV, shape[1])


def _exchange_cores(grads, kinds, name):
    n = len(grads)

    def body(*refs):
        ins, outs = refs[:n], refs[n:2 * n]
        send_sems, recv_sems = refs[2 * n:]
        x, y, c = _mesh_pos()
        sibling = (x, y, 1 - c)
        sends = []
        for a in range(n):
            for s in range(4):
                sends.append(pltpu.make_async_remote_copy(
                    src_ref=_shard_of(ins[a], kinds[a], 2 * s + (1 - c)), dst_ref=outs[a].at[s],
                    send_sem=send_sems.at[a, s], recv_sem=recv_sems.at[a, s], device_id=sibling, device_id_type=MESH))
        for cp in sends:
            cp.start()
        for a in range(n):
            for s in range(4):
                pltpu.make_async_remote_copy(
                    src_ref=outs[a].at[s], dst_ref=outs[a].at[s],
                    send_sem=send_sems.at[a, s], recv_sem=recv_sems.at[a, s], device_id=sibling, device_id_type=MESH).wait_recv()
        for cp in sends:
            cp.wait_send()

    outs = pl.pallas_call(
        body, name=name,
        out_shape=[jax.ShapeDtypeStruct((4,) + _shard_shape(g.shape, k), F32) for g, k in zip(grads, kinds)],
        in_specs=[ANY] * n, out_specs=[ANY] * n,
        scratch_shapes=[pltpu.SemaphoreType.DMA((n, 4)), pltpu.SemaphoreType.DMA((n, 4))],
    )(*grads)
    return list(outs)


def _exchange_chips(parts, name):
    n = len(parts)

    def body(*refs):
        ins, outs = refs[:n], refs[n:2 * n]
        send_sems, recv_sems = refs[2 * n:]
        x, y, c = _mesh_pos()
        my_slot = 2 * x + y
        chips = [(1 - x, y), (x, 1 - y), (1 - x, 1 - y)]
        sends = []
        for a in range(n):
            for j, (px, py) in enumerate(chips):
                sends.append(pltpu.make_async_remote_copy(
                    src_ref=ins[a].at[2 * px + py], dst_ref=outs[a].at[my_slot],
                    send_sem=send_sems.at[a, j], recv_sem=recv_sems.at[a, j], device_id=(px, py, c), device_id_type=MESH))
        for cp in sends:
            cp.start()
        for a in range(n):
            for j, (px, py) in enumerate(chips):
                pltpu.make_async_remote_copy(
                    src_ref=outs[a].at[2 * px + py], dst_ref=outs[a].at[2 * px + py],
                    send_sem=send_sems.at[a, j], recv_sem=recv_sems.at[a, j], device_id=(px, py, c), device_id_type=MESH).wait_recv()
        for cp in sends:
            cp.wait_send()

    outs = pl.pallas_call(
        body, name=name,
        out_shape=[jax.ShapeDtypeStruct(p.shape, p.dtype) for p in parts],
        in_specs=[ANY] * n, out_specs=[ANY] * n,
        scratch_shapes=[pltpu.SemaphoreType.DMA((n, 3)), pltpu.SemaphoreType.DMA((n, 3))],
    )(*parts)
    return list(outs)


def _row_tile(rows):
    for t in (256, 128, 64, 32, 16, 8):
        if rows % t == 0:
            return t
    return rows


def _wire_dtype(rows):
    return BF16 if rows % 16 == 0 else F32


def _add_own(core, g, recv, kind, name):
    ns, R, C = recv.shape
    tr = _row_tile(R)
    nr = R // tr
    if kind == "col":
        g_spec = pl.BlockSpec((tr, C), lambda s, r, c_ref: (r, 2 * s + c_ref[0]))
    else:
        g_spec = pl.BlockSpec((tr, C), lambda s, r, c_ref: ((2 * s + c_ref[0]) * nr + r, 0))
    slab = pl.BlockSpec((None, tr, C), lambda s, r, c_ref: (s, r, 0))

    def body(c_ref, g_ref, r_ref, o32_ref, o16_ref):
        t = g_ref[...] + r_ref[...]
        o32_ref[...] = t
        o16_ref[...] = t.astype(o16_ref.dtype)

    return pl.pallas_call(
        body, name=name,
        grid_spec=pltpu.PrefetchScalarGridSpec(num_scalar_prefetch=1, grid=(ns, nr), in_specs=[g_spec, slab], out_specs=[slab, slab]),
        out_shape=[jax.ShapeDtypeStruct((ns, R, C), F32), jax.ShapeDtypeStruct((ns, R, C), _wire_dtype(R))],
        compiler_params=_cparams(2),
    )(core, g, recv)


def _adamw_math(w, g, m, v):
    m2 = ADAM_B1 * m + (1.0 - ADAM_B1) * g
    v2 = ADAM_B2 * v + (1.0 - ADAM_B2) * (g * g)
    m_hat = m2 / (1.0 - ADAM_B1 ** ADAM_STEP)
    v_hat = v2 / (1.0 - ADAM_B2 ** ADAM_STEP)
    delta = -ADAM_LR * (m_hat / (jnp.sqrt(v_hat) + ADAM_EPS) + ADAM_WD * w)
    return delta, m2, v2


def _sum_adamw(parts, w, m, v, name):
    n, R, C = parts.shape
    tr = _row_tile(R)

    def body(p_ref, w_ref, m_ref, v_ref, g_ref, d_ref, m_out, v_out):
        g = p_ref[0]
        for k in range(1, n):
            g = g + p_ref[k]
        g_ref[...] = g
        d_ref[...], m_out[...], v_out[...] = _adamw_math(w_ref[...], g, m_ref[...], v_ref[...])

    blk = pl.BlockSpec((tr, C), lambda r: (r, 0))
    return pl.pallas_call(
        body, name=name, grid=(R // tr,),
        in_specs=[pl.BlockSpec((n, tr, C), lambda r: (0, r, 0)), blk, blk, blk],
        out_specs=[blk] * 4, out_shape=[jax.ShapeDtypeStruct((R, C), F32)] * 4, compiler_params=_cparams(1),
    )(parts, w, m, v)


def _sum_chips_adamw(chip, own, recv, w, m, v, name):
    n, R, C = recv.shape
    tr = _row_tile(R)

    def body(chip_ref, own_ref, r_ref, w_ref, m_ref, v_ref, g_ref, d_ref, m_out, v_out):
        g = None
        for k in range(n):
            term = jnp.where(chip_ref[0] == k, own_ref[...], r_ref[k].astype(F32))
            g = term if g is None else g + term
        g_ref[...] = g
        d_ref[...], m_out[...], v_out[...] = _adamw_math(w_ref[...], g, m_ref[...], v_ref[...])

    blk = pl.BlockSpec((tr, C), lambda r, chip_ref: (r, 0))
    return pl.pallas_call(
        body, name=name,
        grid_spec=pltpu.PrefetchScalarGridSpec(
            num_scalar_prefetch=1, grid=(R // tr,),
            in_specs=[pl.BlockSpec((None, tr, C), lambda r, chip_ref: (chip_ref[0], r, 0)),
                      pl.BlockSpec((n, tr, C), lambda r, chip_ref: (0, r, 0)), blk, blk, blk],
            out_specs=[blk] * 4),
        out_shape=[jax.ShapeDtypeStruct((R, C), F32)] * 4, compiler_params=_cparams(1),
    )(chip, own, recv, w, m, v)


def _sum_peers_adamw(me, g, recv, kind, w, m, v, name):
    n, R, C = recv.shape
    tr = _row_tile(R)
    nr = R // tr
    if kind == "col":
        g_spec = pl.BlockSpec((tr, C), lambda r, me_ref: (r, me_ref[0]))
    else:
        g_spec = pl.BlockSpec((tr, C), lambda r, me_ref: (me_ref[0] * nr + r, 0))

    def body(me_ref, own_ref, r_ref, w_ref, m_ref, v_ref, g_ref, d_ref, m_out, v_out):
        acc = None
        for k in range(n):
            term = jnp.where(me_ref[0] == k, own_ref[...], r_ref[k].astype(F32))
            acc = term if acc is None else acc + term
        g_ref[...] = acc
        d_ref[...], m_out[...], v_out[...] = _adamw_math(w_ref[...], acc, m_ref[...], v_ref[...])

    blk = pl.BlockSpec((tr, C), lambda r, me_ref: (r, 0))
    return pl.pallas_call(
        body, name=name,
        grid_spec=pltpu.PrefetchScalarGridSpec(
            num_scalar_prefetch=1, grid=(nr,),
            in_specs=[g_spec, pl.BlockSpec((n, tr, C), lambda r, me_ref: (0, r, 0)), blk, blk, blk], out_specs=[blk] * 4),
        out_shape=[jax.ShapeDtypeStruct((R, C), F32)] * 4, compiler_params=_cparams(1),
    )(me, g, recv, w, m, v)


def _ada_fwd(c, w_ada_g, b_ada):
    nb, D, W = w_ada_g.shape

    def body(c_ref, w_ref, b_ref, o_ref):
        cv = c_ref[...]
        ca = jnp.broadcast_to(cv * _sigmoid(cv), (8, D)).astype(BF16)
        o_ref[...] = _dot(ca, w_ref[...])[0:1, :] + b_ref[...]

    return pl.pallas_call(
        body, name="ada_fwd", grid=(nb,),
        in_specs=[_full((1, D)), pl.BlockSpec((None, D, W), lambda j: (j, 0, 0)), pl.BlockSpec((1, W), lambda j: (0, j))],
        out_specs=pl.BlockSpec((1, W), lambda j: (0, j)),
        out_shape=jax.ShapeDtypeStruct((1, nb * W), F32), compiler_params=_cparams(1),
    )(c, w_ada_g, b_ada)


def _ada_bwd(me, small_g, D, W, c_block):
    def body(me_ref, c_ref, dm_ref, o_ref):
        cv = c_ref[...]
        ca = (cv * _sigmoid(cv)).astype(BF16).astype(F32)
        dm = dm_ref[...].astype(BF16).astype(F32)
        o_ref[...] = lax.dot_general(ca, dm, (((0,), (0,)), ((), ())), precision=lax.Precision.HIGHEST,
                                     preferred_element_type=F32)

    return pl.pallas_call(
        body, name="ada_bwd",
        grid_spec=pltpu.PrefetchScalarGridSpec(
            num_scalar_prefetch=1, grid=(1,),
            in_specs=[pl.BlockSpec((N_DEV, D), lambda i, me_ref: (0, c_block)),
                      pl.BlockSpec((N_DEV, W), lambda i, me_ref: (0, me_ref[0]))],
            out_specs=pl.BlockSpec((D, W), lambda i, me_ref: (0, 0))),
        out_shape=jax.ShapeDtypeStruct((D, W), F32), compiler_params=_cparams(1),
    )(me, small_g, small_g)


def _fwd_in(x, mod, w_in_g, b_in, shards):
    S, D = x.shape
    nb, _, wb = w_in_g.shape
    N = nb * wb
    TM = TOKEN_TILE
    n_tiles = S // TM
    ns = len(shards)

    def body(*refs):
        x_ref, mod_ref, w_hbm, b_ref = refs[:4]
        ins = refs[4:4 + ns]
        z_ref, h_ref = refs[4 + ns:6 + ns]
        outs = refs[6 + ns:6 + 2 * ns]
        w_v, sems, send_sems, recv_sems, local_sems = refs[6 + 2 * ns:]
        i = pl.program_id(0)
        x_, y_, c_ = _mesh_pos()
        me = 4 * x_ + 2 * y_ + c_
        mine = [pltpu.make_async_copy(ins[a], outs[a].at[me], local_sems.at[a]) for a in range(ns)]
        sends, arrivals = _peer_copies(lambda a, p: ins[a], outs, send_sems, recv_sems)

        @pl.when(i == 0)
        def _():
            for cp in sends + mine:
                cp.start()
            _load_cols(w_hbm, w_v, sems)

        xhat, _ = _ln(x_ref[...])
        hb = (xhat * (1.0 + mod_ref[:, D:2 * D]) + mod_ref[:, 0:D]).astype(BF16)
        h_ref[...] = hb
        z_ref[...] = _dot(hb, w_v[...]) + b_ref[...]

        @pl.when(i == n_tiles - 1)
        def _():
            for cp in arrivals:
                cp.wait_recv()
            for cp in sends:
                cp.wait_send()
            for cp in mine:
                cp.wait()

    res = pl.pallas_call(
        body, name="fwd_in", grid=(n_tiles,),
        in_specs=[pl.BlockSpec((TM, D), lambda i: (i, 0)), _full(mod.shape), ANY, _full((1, N))] + [ANY] * ns,
        out_specs=[pl.BlockSpec((TM, N), lambda i: (i, 0)), pl.BlockSpec((TM, D), lambda i: (i, 0))] + [ANY] * ns,
        out_shape=[jax.ShapeDtypeStruct((S, N), F32), jax.ShapeDtypeStruct((S, D), BF16)]
        + [jax.ShapeDtypeStruct((N_DEV,) + a.shape, a.dtype) for a in shards],
        scratch_shapes=[pltpu.VMEM((D, N), BF16), pltpu.SemaphoreType.DMA((nb,))] + _peer_sems(ns) + [pltpu.SemaphoreType.DMA((ns,))],
        compiler_params=_cparams(1),
    )(x, mod, w_in_g, b_in, *shards)
    return res[0], res[1], list(res[2:])


def _rows_from(win, o, R):
    if o % 8 == 0:
        return win[o:o + R, :]
    return pltpu.roll(win, win.shape[0] - o, axis=0)[0:R, :]


def _causal_taps(buf, w_ref, o_ref, K, halo, TM, D):
    R, C = CONV_ROWS, CONV_LANES
    base = halo - (K - 1)

    def step(j, carry):
        r0 = pl.multiple_of(j * R, R)
        for c0 in range(0, D, C):
            win = buf[pl.ds(r0, R + halo), c0:c0 + C]
            acc = jnp.zeros((R, C), F32)
            for k in range(K):
                acc = acc + w_ref[k:k + 1, c0:c0 + C] * _rows_from(win, base + k, R)
            o_ref[pl.ds(r0, R), c0:c0 + C] = acc
        return carry

    lax.fori_loop(0, TM // R, step, 0)


def _fwd_mix(z, x, mod, wa, ba, lag, lab, w_a_out, b_a_out, wb, w_b_out, w_o, b_o):
    S, D = x.shape
    N = z.shape[1]
    TM = TOKEN_TILE
    KA, KB = wa.shape[0], wb.shape[0]

    def body(z_ref, x_ref, mod_ref, wa_ref, ba_ref, lag_ref, lab_ref, wao_hbm, bao_ref, wb_ref, wbo_hbm, wo_hbm, bo_ref,
             u1_ref, ya_ref, yb_ref, q_ref, out_ref, r1_ref, u2_ref, v_ref, mg_ref,
             ubuf, pbuf, wao, wbo, wo, sems):
        i = pl.program_id(0)

        @pl.when(i == 0)
        def _():
            _load_whole([(wao_hbm, wao), (wbo_hbm, wbo), (wo_hbm, wo)], sems)
            ubuf[0:HALO_A, :] = jnp.zeros((HALO_A, D), F32)
            pbuf[0:HALO_B, :] = jnp.zeros((HALO_B, D), F32)

        ubuf[HALO_A:HALO_A + TM, :] = z_ref[:, 0:D] * _sigmoid(z_ref[:, D:2 * D])
        _causal_taps(ubuf, wa_ref, u1_ref, KA, HALO_A, TM, D)
        ubuf[0:HALO_A, :] = ubuf[TM:TM + HALO_A, :]
        u1 = u1_ref[...] + ba_ref[...]
        u1_ref[...] = u1
        xa, _ = _ln(u1)
        l = xa * lag_ref[...] + lab_ref[...]
        u2 = (l * _sigmoid(l)).astype(BF16)
        u2_ref[...] = u2
        ya = _dot(u2, wao[...]) + bao_ref[...]
        ya_ref[...] = ya.astype(BF16)
        pbuf[HALO_B:HALO_B + TM, :] = z_ref[:, 3 * D:4 * D] * z_ref[:, 4 * D:5 * D]
        _causal_taps(pbuf, wb_ref, q_ref, KB, HALO_B, TM, D)
        pbuf[0:HALO_B, :] = pbuf[TM:TM + HALO_B, :]
        v = (z_ref[:, 2 * D:3 * D] * q_ref[...]).astype(BF16)
        v_ref[...] = v
        yb = _dot(v, wbo[...])
        yb_ref[...] = yb.astype(BF16)
        mg = (_sigmoid(z_ref[:, 5 * D:6 * D]) * ya + _sigmoid(z_ref[:, 6 * D:7 * D]) * yb).astype(BF16)
        mg_ref[...] = mg
        out = _dot(mg, wo[...]) + bo_ref[...]
        out_ref[...] = out.astype(BF16)
        r1_ref[...] = ALPHA * x_ref[...] + (1.0 + mod_ref[:, 2 * D:3 * D]) * out

    tile = pl.BlockSpec((TM, D), lambda i: (i, 0))
    vec = _full((1, D))
    return pl.pallas_call(
        body, name="fwd_mix", grid=(S // TM,),
        in_specs=[pl.BlockSpec((TM, N), lambda i: (i, 0)), tile, _full(mod.shape), _full((KA, D)), vec, vec, vec, ANY, vec,
                  _full((KB, D)), ANY, ANY, vec],
        out_specs=[tile] * 9,
        out_shape=[jax.ShapeDtypeStruct((S, D), dt) for dt in (F32, BF16, BF16, F32, BF16, F32, BF16, BF16, BF16)],
        scratch_shapes=[pltpu.VMEM((TM + HALO_A, D), F32), pltpu.VMEM((TM + HALO_B, D), F32),
                        pltpu.VMEM((D, D), BF16), pltpu.VMEM((D, D), BF16), pltpu.VMEM((D, D), BF16),
                        pltpu.SemaphoreType.DMA((3,))],
        compiler_params=_cparams(1),
    )(z, x, mod, wa, ba, lag, lab, w_a_out, b_a_out, wb, w_b_out, w_o, b_o)


FFN_ROWS = ("b_down", "ln2_g", "ln2_b", "gate2", "scale2", "shift2", "loss")


def _ffn(r1, tgt, mod, ln1_g, ln1_b, w_up_g, b_up, w_down, b_down, ln2_g, ln2_b):
    S, D = r1.shape
    nb, _, wb = w_up_g.shape
    FF = nb * wb
    TM = TOKEN_TILE
    CF = 1024
    n_tiles = S // TM

    def body(r1_ref, t_ref, mod_ref, g1_ref, b1_ref, wup_hbm, bup_ref, wdn_hbm, bdn_ref, g2_ref, b2_ref,
             dx1_ref, h2_ref, f_ref, dhu_ref, do2_ref, acc_ref, accup_ref,
             wup, wdn, relu_buf, sems, sem2):
        i = pl.program_id(0)

        @pl.when(i == 0)
        def _():
            _load_cols(wup_hbm, wup, sems)
            _load_whole([(wdn_hbm, wdn)], sem2)
            acc_ref[...] = jnp.zeros(acc_ref.shape, F32)
            accup_ref[...] = jnp.zeros(accup_ref.shape, F32)

        scale2, shift2, gate2 = mod_ref[:, 4 * D:5 * D], mod_ref[:, 3 * D:4 * D], mod_ref[:, 5 * D:6 * D]
        xhat1, _ = _ln(r1_ref[...])
        x1 = xhat1 * g1_ref[...] + b1_ref[...]
        xh0, rstd0 = _ln(x1)
        h2 = (xh0 * (1.0 + scale2) + shift2).astype(BF16)
        h2_ref[...] = h2
        out2 = jnp.zeros((TM, D), F32) + bdn_ref[...]
        for c0 in range(0, FF, CF):
            hu = _dot(h2, wup[:, c0:c0 + CF]) + bup_ref[:, c0:c0 + CF]
            rl = jnp.maximum(hu, 0.0)
            relu_buf[:, c0:c0 + CF] = rl
            fb = (rl * rl).astype(BF16)
            f_ref[:, c0:c0 + CF] = fb
            out2 = out2 + _dot(fb, wdn[c0:c0 + CF, :])
        r2 = ALPHA * x1 + (1.0 + gate2) * out2
        xh2, rstd2 = _ln(r2)
        e = xh2 * g2_ref[...] + b2_ref[...] - t_ref[...]
        acc_ref[6:7, :] += _colsum(e * e)
        dy = e * (1.0 / D)
        acc_ref[1:2, :] += _colsum(dy * xh2)
        acc_ref[2:3, :] += _colsum(dy)
        dr2 = _ln_bwd(dy * g2_ref[...], xh2, rstd2)
        acc_ref[3:4, :] += _colsum(dr2 * out2)
        do2 = (1.0 + gate2) * dr2
        acc_ref[0:1, :] += _colsum(do2)
        do2b = do2.astype(BF16)
        do2_ref[...] = do2b
        dh2 = jnp.zeros((TM, D), F32)
        for c0 in range(0, FF, CF):
            dhu = _dot_nt(do2b, wdn[c0:c0 + CF, :]) * (2.0 * relu_buf[:, c0:c0 + CF])
            accup_ref[:, c0:c0 + CF] += _colsum(dhu)
            dhub = dhu.astype(BF16)
            dhu_ref[:, c0:c0 + CF] = dhub
            dh2 = dh2 + _dot_nt(dhub, wup[:, c0:c0 + CF])
        acc_ref[4:5, :] += _colsum(dh2 * xh0)
        acc_ref[5:6, :] += _colsum(dh2)
        dx1_ref[...] = ALPHA * dr2 + _ln_bwd(dh2 * (1.0 + scale2), xh0, rstd0)

        @pl.when(i == n_tiles - 1)
        def _():
            tot = jnp.sum(acc_ref[6:7, :], axis=-1, keepdims=True) * (0.5 / D)
            acc_ref[6:7, :] = jnp.broadcast_to(tot, (1, D))

    tile = pl.BlockSpec((TM, D), lambda i: (i, 0))
    wide = pl.BlockSpec((TM, FF), lambda i: (i, 0))
    vec = _full((1, D))
    return pl.pallas_call(
        body, name="ffn", grid=(n_tiles,),
        in_specs=[tile, tile, _full(mod.shape), vec, vec, ANY, _full((1, FF)), ANY, vec, vec, vec],
        out_specs=[tile, tile, wide, wide, tile, _full((8, D)), _full((1, FF))],
        out_shape=[jax.ShapeDtypeStruct((S, D), F32), jax.ShapeDtypeStruct((S, D), BF16), jax.ShapeDtypeStruct((S, FF), BF16),
                   jax.ShapeDtypeStruct((S, FF), BF16), jax.ShapeDtypeStruct((S, D), BF16),
                   jax.ShapeDtypeStruct((8, D), F32), jax.ShapeDtypeStruct((1, FF), F32)],
        scratch_shapes=[pltpu.VMEM((D, FF), BF16), pltpu.VMEM((FF, D), BF16), pltpu.VMEM((TM, FF), F32),
                        pltpu.SemaphoreType.DMA((nb,)), pltpu.SemaphoreType.DMA((1,))],
        compiler_params=_cparams(1),
    )(r1, tgt, mod, ln1_g, ln1_b, w_up_g, b_up, w_down, b_down, ln2_g, ln2_b)


def _dw(a, b, name, msplit=1, nsplit=1, wire=False, exchange=None):
    S, M = a.shape
    N = b.shape[1]
    TK = min(DW_TILE, S)
    mb, nbk = M // msplit, N // nsplit
    nk = S // TK
    srcs, kinds = exchange if exchange else ((), ())
    ns = len(srcs)

    def body(*refs):
        a_ref, b_ref = refs[:2]
        ins = refs[2:2 + ns]
        o_ref = refs[2 + ns]
        o16_ref = refs[3 + ns] if wire else None
        rest = refs[3 + ns + (1 if wire else 0):]
        k = pl.program_id(2)
        if ns:
            outs, (send_sems, recv_sems) = rest[:ns], rest[ns:]
            sends, arrivals = _peer_copies(lambda q, p: _shard_of(ins[q], kinds[q], p), outs, send_sems, recv_sems)
            first = (pl.program_id(0) == 0) & (pl.program_id(1) == 0) & (k == 0)
            last = (pl.program_id(0) == msplit - 1) & (pl.program_id(1) == nsplit - 1) & (k == nk - 1)

            @pl.when(first)
            def _():
                for cp in sends:
                    cp.start()

        @pl.when(k == 0)
        def _():
            o_ref[...] = jnp.zeros(o_ref.shape, F32)

        o_ref[...] += _dot_tn(a_ref[...], b_ref[...])

        if wire:
            @pl.when(k == nk - 1)
            def _():
                o16_ref[...] = o_ref[...].astype(BF16)

        if ns:
            @pl.when(last)
            def _():
                for cp in arrivals:
                    cp.wait_recv()
                for cp in sends:
                    cp.wait_send()

    oblk = pl.BlockSpec((mb, nbk), lambda i, j, k: (i, j))
    res = pl.pallas_call(
        body, name=name, grid=(msplit, nsplit, nk),
        in_specs=[pl.BlockSpec((TK, mb), lambda i, j, k: (k, i)), pl.BlockSpec((TK, nbk), lambda i, j, k: (k, j))] + [ANY] * ns,
        out_specs=[oblk] + ([oblk] if wire else []) + [ANY] * ns,
        out_shape=[jax.ShapeDtypeStruct((M, N), F32)] + ([jax.ShapeDtypeStruct((M, N), BF16)] if wire else [])
        + [jax.ShapeDtypeStruct((N_DEV,) + _shard_shape(g.shape, kd), g.dtype) for g, kd in zip(srcs, kinds)],
        scratch_shapes=_peer_sems(ns) if ns else [],
        compiler_params=_cparams(3),
    )(a, b, *srcs)
    return res if (wire or ns) else res[0]


BWD1_ROWS = ("ln1_g", "ln1_b", "gate1", "b_o", "b_a_out", "ln_a_g", "ln_a_b", "conv_a_b", "b_in_gb", "b_in_ga", "b_in_gbb")


def _bwd_mix1(dx1, r1, out, ya, yb, q, u1, z, mod, ln1_g, lag, lab, w_o, w_a_out, w_b_out):
    S, D = dx1.shape
    TM = TOKEN_TILE

    def body(dx1_ref, r1_ref, out_ref, ya_ref, yb_ref, q_ref, u1_ref, zgb_ref, zga_ref, zgg_ref, mod_ref, g1_ref, lag_ref, lab_ref,
             wo_hbm, wao_hbm, wbo_hbm,
             dxp_ref, du1_ref, dq_ref, dzc_ref, dout_ref, dya_ref, dyb_ref, acc_ref,
             wo, wao, wbo, sems):
        @pl.when(pl.program_id(0) == 0)
        def _():
            _load_whole([(wo_hbm, wo), (wao_hbm, wao), (wbo_hbm, wbo)], sems)
            acc_ref[...] = jnp.zeros(acc_ref.shape, F32)

        dx1v = dx1_ref[...]
        xhat1, rstd1 = _ln(r1_ref[...])
        acc_ref[0:1, :] += _colsum(dx1v * xhat1)
        acc_ref[1:2, :] += _colsum(dx1v)
        dr1 = _ln_bwd(dx1v * g1_ref[...], xhat1, rstd1)
        dxp_ref[...] = ALPHA * dr1
        acc_ref[2:3, :] += _colsum(dr1 * out_ref[...].astype(F32))
        dout = (1.0 + mod_ref[:, 2 * D:3 * D]) * dr1
        acc_ref[3:4, :] += _colsum(dout)
        doutb = dout.astype(BF16)
        dout_ref[...] = doutb
        dmg = _dot_nt(doutb, wo[...])
        sga, sgb = _sigmoid(zga_ref[...]), _sigmoid(zgg_ref[...])
        dga = dmg * ya_ref[...].astype(F32) * sga * (1.0 - sga)
        dgb = dmg * yb_ref[...].astype(F32) * sgb * (1.0 - sgb)
        acc_ref[9:10, :] += _colsum(dga)
        acc_ref[10:11, :] += _colsum(dgb)
        dzc_ref[:, D:2 * D] = dga.astype(BF16)
        dzc_ref[:, 2 * D:3 * D] = dgb.astype(BF16)
        dya = dmg * sga
        acc_ref[4:5, :] += _colsum(dya)
        dyab = dya.astype(BF16)
        dya_ref[...] = dyab
        dybb = (dmg * sgb).astype(BF16)
        dyb_ref[...] = dybb
        du2 = _dot_nt(dyab, wao[...])
        xa, rstda = _ln(u1_ref[...])
        l = xa * lag_ref[...] + lab_ref[...]
        sl = _sigmoid(l)
        dl = du2 * (sl * (1.0 + l * (1.0 - sl)))
        acc_ref[5:6, :] += _colsum(dl * xa)
        acc_ref[6:7, :] += _colsum(dl)
        du1 = _ln_bwd(dl * lag_ref[...], xa, rstda)
        acc_ref[7:8, :] += _colsum(du1)
        du1_ref[...] = du1
        dv = _dot_nt(dybb, wbo[...])
        dgbk = dv * q_ref[...]
        acc_ref[8:9, :] += _colsum(dgbk)
        dzc_ref[:, 0:D] = dgbk.astype(BF16)
        dq_ref[...] = dv * zgb_ref[...]

    tile = pl.BlockSpec((TM, D), lambda i: (i, 0))
    vec = _full((1, D))
    zcol = lambda k: pl.BlockSpec((TM, D), lambda i: (i, k))
    return pl.pallas_call(
        body, name="bwd_mix1", grid=(S // TM,),
        in_specs=[tile] * 7 + [zcol(2), zcol(5), zcol(6), _full(mod.shape), vec, vec, vec, ANY, ANY, ANY],
        out_specs=[tile, tile, tile, pl.BlockSpec((TM, 3 * D), lambda i: (i, 0)), tile, tile, tile, _full((16, D))],
        out_shape=[jax.ShapeDtypeStruct((S, D), F32)] * 3 + [jax.ShapeDtypeStruct((S, 3 * D), BF16)]
        + [jax.ShapeDtypeStruct((S, D), BF16)] * 3 + [jax.ShapeDtypeStruct((16, D), F32)],
        scratch_shapes=[pltpu.VMEM((D, D), BF16)] * 3 + [pltpu.SemaphoreType.DMA((3,))],
        compiler_params=_cparams(1),
    )(dx1, r1, out, ya, yb, q, u1, z, z, z, mod, ln1_g, lag, lab, w_o, w_a_out, w_b_out)


def _anticausal_taps(dbuf, u_ref, w_ref, o_ref, dwacc, K, halo, TM, D):
    R, C = CONV_ROWS, CONV_LANES

    def step(jr, carry):
        r0 = pl.multiple_of(jr * R, R)
        for c0 in range(0, D, C):
            win = dbuf[pl.ds(r0, R + halo), c0:c0 + C]
            uc = u_ref[pl.ds(r0, R), c0:c0 + C]
            acc = jnp.zeros((R, C), F32)
            for j in range(K):
                sh = _rows_from(win, j, R)
                k = K - 1 - j
                acc = acc + w_ref[k:k + 1, c0:c0 + C] * sh
                pr = uc * sh
                part = pr[0:8, :]
                for s in range(8, R, 8):
                    part = part + pr[s:s + 8, :]
                dwacc[k, :, c0:c0 + C] += part
            o_ref[pl.ds(r0, R), c0:c0 + C] = acc
        return carry

    lax.fori_loop(0, TM // R, step, 0)


def _bwd_mix2(du1, dq, z, dzc, x, dxp, mod, wa, wb, w_in_g):
    S, D = x.shape
    N = z.shape[1]
    nb, _, wbk = w_in_g.shape
    TM = TOKEN_TILE
    KA, KB = wa.shape[0], wb.shape[0]
    n_tiles = S // TM

    def body(du1_ref, dq_ref, zav_ref, zc_ref, zx_ref, dzc_ref, x_ref, dxp_ref, mod_ref, wa_ref, wb_ref, win_hbm,
             dz_ref, gx_ref, dwa_ref, dwb_ref, acc_ref, dbin_ref,
             dbuf, qbuf, ubuf, pbuf, obuf, dwa_acc, dwb_acc, w_v, sems):
        i = pl.program_id(0)

        @pl.when(i == 0)
        def _():
            _load_cols(win_hbm, w_v, sems)
            dbuf[TM:TM + HALO_A, :] = jnp.zeros((HALO_A, D), F32)
            qbuf[TM:TM + HALO_B, :] = jnp.zeros((HALO_B, D), F32)
            dwa_acc[...] = jnp.zeros(dwa_acc.shape, F32)
            dwb_acc[...] = jnp.zeros(dwb_acc.shape, F32)
            acc_ref[...] = jnp.zeros(acc_ref.shape, F32)
            dbin_ref[...] = jnp.zeros(dbin_ref.shape, F32)

        a_val = zav_ref[:, 0:D]
        sa = _sigmoid(zav_ref[:, D:2 * D])
        ubuf[...] = a_val * sa
        dbuf[0:TM, :] = du1_ref[...]
        _anticausal_taps(dbuf, ubuf, wa_ref, obuf, dwa_acc, KA, HALO_A, TM, D)
        dbuf[TM:TM + HALO_A, :] = dbuf[0:HALO_A, :]
        du0 = obuf[...]
        dav = du0 * sa
        dag = du0 * a_val * sa * (1.0 - sa)
        dbin_ref[:, 0:D] += _colsum(dav)
        dbin_ref[:, D:2 * D] += _colsum(dag)
        dz_ref[:, 0:D] = dav.astype(BF16)
        dz_ref[:, D:2 * D] = dag.astype(BF16)
        pbuf[...] = zc_ref[...] * zx_ref[...]
        qbuf[0:TM, :] = dq_ref[...]
        _anticausal_taps(qbuf, pbuf, wb_ref, obuf, dwb_acc, KB, HALO_B, TM, D)
        qbuf[TM:TM + HALO_B, :] = qbuf[0:HALO_B, :]
        dp = obuf[...]
        dgc = dp * zx_ref[...]
        dgx = dp * zc_ref[...]
        dbin_ref[:, 3 * D:4 * D] += _colsum(dgc)
        dbin_ref[:, 4 * D:5 * D] += _colsum(dgx)
        dz_ref[:, 3 * D:4 * D] = dgc.astype(BF16)
        dz_ref[:, 4 * D:5 * D] = dgx.astype(BF16)
        dz_ref[:, 2 * D:3 * D] = dzc_ref[:, 0:D]
        dz_ref[:, 5 * D:7 * D] = dzc_ref[:, D:3 * D]
        dh = _dot_nt(dz_ref[...], w_v[...])
        xhat, rstd = _ln(x_ref[...])
        acc_ref[0:1, :] += _colsum(dh * xhat)
        acc_ref[1:2, :] += _colsum(dh)
        gx_ref[...] = dxp_ref[...] + _ln_bwd(dh * (1.0 + mod_ref[:, D:2 * D]), xhat, rstd)

        @pl.when(i == n_tiles - 1)
        def _():
            for k in range(KA):
                dwa_ref[k:k + 1, :] = jnp.sum(dwa_acc[k], axis=0, keepdims=True)
            for k in range(KB):
                dwb_ref[k:k + 1, :] = jnp.sum(dwb_acc[k], axis=0, keepdims=True)

    rev = lambda i: n_tiles - 1 - i
    tile = pl.BlockSpec((TM, D), lambda i: (rev(i), 0))
    zcol = lambda k: pl.BlockSpec((TM, D), lambda i: (rev(i), k))
    return pl.pallas_call(
        body, name="bwd_mix2", grid=(n_tiles,),
        in_specs=[tile, tile, pl.BlockSpec((TM, 2 * D), lambda i: (rev(i), 0)), zcol(3), zcol(4),
                  pl.BlockSpec((TM, 3 * D), lambda i: (rev(i), 0)), tile, tile, _full(mod.shape), _full((KA, D)), _full((KB, D)), ANY],
        out_specs=[pl.BlockSpec((TM, N), lambda i: (rev(i), 0)), tile, _full((KA, D)), _full((KB, D)), _full((8, D)), _full((1, N))],
        out_shape=[jax.ShapeDtypeStruct((S, N), BF16), jax.ShapeDtypeStruct((S, D), F32), jax.ShapeDtypeStruct((KA, D), F32),
                   jax.ShapeDtypeStruct((KB, D), F32), jax.ShapeDtypeStruct((8, D), F32), jax.ShapeDtypeStruct((1, N), F32)],
        scratch_shapes=[pltpu.VMEM((TM + HALO_A, D), F32), pltpu.VMEM((TM + HALO_B, D), F32), pltpu.VMEM((TM, D), F32),
                        pltpu.VMEM((TM, D), F32), pltpu.VMEM((TM, D), F32), pltpu.VMEM((KA, 8, D), F32), pltpu.VMEM((KB, 8, D), F32),
                        pltpu.VMEM((D, N), BF16), pltpu.SemaphoreType.DMA((nb,))],
        compiler_params=_cparams(1),
    )(du1, dq, z, z, z, dzc, x, dxp, mod, wa, wb, w_in_g)


PEER_REDUCED = ("w_up", "w_down", "w_o", "w_a_out", "w_b_out", "conv_a_w", "conv_b_w")


def _local_grads(x, c, tgt, w_ada_g, w_in_g, shards, vecs):
    D = x.shape[1]
    rows = lambda a: a.reshape(a.shape[0] * a.shape[1], a.shape[2])
    cols = lambda a: jnp.transpose(a, (1, 0, 2)).reshape(a.shape[1], a.shape[0] * a.shape[2])
    mod = _ada_fwd(c, w_ada_g, vecs["b_ada"])
    z, h1, gathered = _fwd_in(x, mod, w_in_g, vecs["b_in"], shards)
    w_a_out, w_b_out, w_o, w_up_g, w_down = rows(gathered[0]), rows(gathered[1]), rows(gathered[2]), gathered[3], rows(gathered[4])
    wa, wb = cols(gathered[5]), cols(gathered[6])
    u1, ya, yb, q, out, r1, u2, v, mg = _fwd_mix(
        z, x, mod, wa, vecs["conv_a_b"], vecs["ln_a_g"], vecs["ln_a_b"], w_a_out, vecs["b_a_out"], wb, w_b_out, w_o, vecs["b_o"])
    dx1, h2, f, dhu, do2, acc_f, acc_up = _ffn(
        r1, tgt, mod, vecs["ln1_g"], vecs["ln1_b"], w_up_g, vecs["b_up"], w_down, vecs["b_down"], vecs["ln2_g"], vecs["ln2_b"])
    g_up, g_up16 = _dw(h2, dhu, "dw_up", nsplit=2, wire=True)
    g_down, g_down16 = _dw(f, do2, "dw_down", msplit=2, wire=True)
    dxp, du1, dq, dzc, doutb, dyab, dybb, acc_1 = _bwd_mix1(
        dx1, r1, out, ya, yb, q, u1, z, mod, vecs["ln1_g"], vecs["ln_a_g"], vecs["ln_a_b"], w_o, w_a_out, w_b_out)
    g_o, g_o16 = _dw(mg, doutb, "dw_o", wire=True)
    g_a_out, g_a_out16 = _dw(u2, dyab, "dw_a_out", wire=True)
    g_b_out, g_b_out16 = _dw(v, dybb, "dw_b_out", wire=True)
    dz, gx, g_wa, g_wb, acc_2, db_in = _bwd_mix2(du1, dq, z, dzc, x, dxp, mod, wa, wb, w_in_g)
    full = dict(zip(PEER_REDUCED, (g_up, g_down, g_o, g_a_out, g_b_out, g_wa, g_wb)))
    res = _dw(h1, dz, "dw_in", nsplit=4,
              exchange=((g_up16, g_down16, g_o16, g_a_out16, g_b_out16, g_wa, g_wb), [SHARD_KIND[n] for n in PEER_REDUCED]))
    g_in = res[0]
    peers = {n: (full[n], r) for n, r in zip(PEER_REDUCED, res[1:])}
    row = lambda acc, k: acc[k:k + 1, :]
    dmod = jnp.concatenate([row(acc_2, 1), row(acc_2, 0), row(acc_1, 2), row(acc_f, 5), row(acc_f, 4), row(acc_f, 3)], axis=1)
    db_in = jnp.concatenate([db_in[:, 0:2 * D], row(acc_1, 8), db_in[:, 3 * D:5 * D], row(acc_1, 9), row(acc_1, 10)], axis=1)
    gvec = {
        "b_ada": dmod, "b_in": db_in, "conv_a_b": row(acc_1, 7), "ln_a_g": row(acc_1, 5), "ln_a_b": row(acc_1, 6),
        "b_a_out": row(acc_1, 4), "b_o": row(acc_1, 3), "ln1_g": row(acc_1, 0), "ln1_b": row(acc_1, 1), "b_up": acc_up,
        "b_down": row(acc_f, 0), "ln2_g": row(acc_f, 1), "ln2_b": row(acc_f, 2),
    }
    return acc_f[6, 0], gx, g_in, peers, gvec


WEIGHTS = ("w_ada", "b_ada", "w_in", "b_in", "conv_a_w", "conv_a_b", "ln_a_g", "ln_a_b", "w_a_out", "b_a_out", "conv_b_w",
           "w_b_out", "w_o", "b_o", "ln1_g", "ln1_b", "w_up", "b_up", "w_down", "b_down", "ln2_g", "ln2_b")
VECTORS = ("b_ada", "b_in", "conv_a_b", "ln_a_g", "ln_a_b", "b_a_out", "b_o", "ln1_g", "ln1_b", "b_up", "b_down", "ln2_g", "ln2_b")
SHARD_KIND = {"w_in": "col", "w_a_out": "row", "w_b_out": "row", "w_o": "row", "w_up": "col", "w_down": "row",
              "conv_a_w": "col", "conv_b_w": "col"}


def kernel(x, c, w_ada, b_ada, w_in, b_in, conv_a_w, conv_a_b, ln_a_g, ln_a_b, w_a_out, b_a_out, conv_b_w, w_b_out, w_o, b_o, ln1_g, ln1_b, w_up, b_up, w_down, b_down, ln2_g, ln2_b, loss_target, m_w_ada, m_b_ada, m_w_in, m_b_in, m_conv_a_w, m_conv_a_b, m_ln_a_g, m_ln_a_b, m_w_a_out, m_b_a_out, m_conv_b_w, m_w_b_out, m_w_o, m_b_o, m_ln1_g, m_ln1_b, m_w_up, m_b_up, m_w_down, m_b_down, m_ln2_g, m_ln2_b, v_w_ada, v_b_ada, v_w_in, v_b_in, v_conv_a_w, v_conv_a_b, v_ln_a_g, v_ln_a_b, v_w_a_out, v_b_a_out, v_conv_b_w, v_w_b_out, v_w_o, v_b_o, v_ln1_g, v_ln1_b, v_w_up, v_b_up, v_w_down, v_b_down, v_ln2_g, v_ln2_b):
    args = dict(locals())
    w = {n: args[n][0] for n in WEIGHTS}
    m = {n: args["m_" + n][0] for n in WEIGHTS}
    v = {n: args["v_" + n][0] for n in WEIGHTS}
    w = {n: (a[None, :] if a.ndim == 1 else a) for n, a in w.items()}
    m = {n: (a[None, :] if a.ndim == 1 else a) for n, a in m.items()}
    v = {n: (a[None, :] if a.ndim == 1 else a) for n, a in v.items()}
    xs, tgt = x[0], loss_target[0]
    S, D = xs.shape
    me = (4 * lax.axis_index("x") + 2 * lax.axis_index("y") + lax.axis_index("c")).astype(jnp.int32).reshape(1)
    core = lax.axis_index("c").astype(jnp.int32).reshape(1)
    chip = (2 * lax.axis_index("x") + lax.axis_index("y")).astype(jnp.int32).reshape(1)

    bf = lambda n: w[n].astype(BF16)
    w_ada_g, w_in_g = _all_gather([bf("w_ada"), bf("w_in")], "gather_weights")
    shards = [bf("w_a_out"), bf("w_b_out"), bf("w_o"), bf("w_up"), bf("w_down"), w["conv_a_w"], w["conv_b_w"]]
    vecs = {n: w[n] for n in VECTORS}

    loss, gx, g_in, peers, gvec = _local_grads(xs, c, tgt, w_ada_g, w_in_g, shards, vecs)

    small = jnp.concatenate([gvec[n] for n in VECTORS] + [c], axis=1)
    (small_g,) = _all_gather([small], "gather_vectors")
    small_g = small_g.reshape(N_DEV, small.shape[1])
    (from_sibling,) = _exchange_cores([g_in], [SHARD_KIND["w_in"]], "reduce_cores")
    own_in, wire_in = _add_own(core, g_in, from_sibling, SHARD_KIND["w_in"], "add_w_in")
    (from_chips,) = _exchange_chips([wire_in], "reduce_chips")

    res = {"w_in": _sum_chips_adamw(chip, own_in, from_chips, w["w_in"], m["w_in"], v["w_in"], "adamw_w_in")}
    for n, (g_full, recv) in peers.items():
        res[n] = _sum_peers_adamw(me, g_full, recv, SHARD_KIND[n], w[n], m[n], v[n], "adamw_" + n)
    W = w["w_ada"].shape[1]
    n_vec = small.shape[1] - D
    g_ada = _ada_bwd(me, small_g, D, W, n_vec // D)
    res["w_ada"] = _sum_adamw(g_ada[None], w["w_ada"], m["w_ada"], v["w_ada"], "adamw_w_ada")
    cat = lambda d: jnp.concatenate([d[n] for n in VECTORS], axis=1)
    vg, vd, vm, vv = _sum_adamw(small_g[:, None, 0:n_vec], cat(w), cat(m), cat(v), "adamw_vectors")
    off = 0
    for n in VECTORS:
        k = w[n].shape[1]
        res[n] = tuple(a[:, off:off + k] for a in (vg, vd, vm, vv))
        off += k

    loss = lax.psum(loss, ("x", "y", "c"))
    outs = [loss, gx[None]]
    for t in range(4):
        outs += [res[n][t].reshape(args[n].shape) for n in WEIGHTS]
    return tuple(outs)
```

```python
import functools

import jax
import jax.numpy as jnp
from jax import lax
from jax.experimental import pallas as pl
from jax.experimental.pallas import tpu as pltpu

F32 = jnp.float32
BF16 = jnp.bfloat16
MESH = pl.DeviceIdType.MESH

N_DEV = 8
LN_EPS = 1e-5
DEPTH = 1
ALPHA = (2.0 * DEPTH) ** 0.25
ADAM_LR, ADAM_B1, ADAM_B2, ADAM_EPS, ADAM_WD, ADAM_STEP = 0.001, 0.9, 0.999, 1e-08, 0.01, 10

VMEM_LIMIT = 60 * 1024 * 1024
TOKEN_TILE = 256
DW_TILE = 1024
HALO_A = 32
HALO_B = 8
CONV_ROWS, CONV_LANES = 32, 512


def _cparams(n_grid):
    return pltpu.CompilerParams(dimension_semantics=("arbitrary",) * n_grid, vmem_limit_bytes=VMEM_LIMIT)


def _full(shape):
    return pl.BlockSpec(shape, lambda *_: (0,) * len(shape))


ANY = pl.BlockSpec(memory_space=pl.ANY)


def _ln(x):
    mu = jnp.mean(x, axis=-1, keepdims=True)
    xc = x - mu
    var = jnp.mean(xc * xc, axis=-1, keepdims=True)
    rstd = lax.rsqrt(var + LN_EPS)
    return xc * rstd, rstd


def _ln_bwd(dxhat, xhat, rstd):
    m1 = jnp.mean(dxhat, axis=-1, keepdims=True)
    m2 = jnp.mean(dxhat * xhat, axis=-1, keepdims=True)
    return rstd * (dxhat - m1 - xhat * m2)


def _sigmoid(x):
    return 0.5 * jnp.tanh(0.5 * x) + 0.5


def _colsum(a):
    return jnp.sum(a, axis=0, keepdims=True)


def _dot(a, b):
    return jnp.dot(a, b, preferred_element_type=F32)


def _dot_nt(a, b):
    return lax.dot_general(a, b, (((1,), (1,)), ((), ())), preferred_element_type=F32)


def _dot_tn(a, b):
    return lax.dot_general(a, b, (((0,), (0,)), ((), ())), preferred_element_type=F32)


def _load_cols(src_hbm, dst_vmem, sems):
    nblk, _, w = src_hbm.shape
    cps = [pltpu.make_async_copy(src_hbm.at[j], dst_vmem.at[:, pl.ds(j * w, w)], sems.at[j]) for j in range(nblk)]
    for cp in cps:
        cp.start()
    for cp in cps:
        cp.wait()


def _load_whole(pairs, sems):
    cps = [pltpu.make_async_copy(s, d, sems.at[k]) for k, (s, d) in enumerate(pairs)]
    for cp in cps:
        cp.start()
    for cp in cps:
        cp.wait()


def _mesh_pos():
    return lax.axis_index("x"), lax.axis_index("y"), lax.axis_index("c")


def _all_gather(arrs, name):
    n = len(arrs)

    def body(*refs):
        ins, outs = refs[:n], refs[n:2 * n]
        send_sems, recv_sems, local_sems = refs[2 * n:]
        x, y, c = _mesh_pos()
        me, sibling = (x, y, c), (x, y, 1 - c)
        chips = [(1 - x, y), (x, 1 - y), (1 - x, 1 - y)]

        def slot(a, px, py, pc):
            return outs[a].at[4 * px + 2 * py + pc]

        def copy(a, k, block, to, src=None):
            return pltpu.make_async_remote_copy(
                src_ref=slot(a, *block) if src is None else src, dst_ref=slot(a, *block),
                send_sem=send_sems.at[a, k], recv_sem=recv_sems.at[a, k], device_id=to, device_id_type=MESH)

        mine = [pltpu.make_async_copy(ins[a], slot(a, *me), local_sems.at[a]) for a in range(n)]
        for cp in mine:
            cp.start()
        first = []
        for a in range(n):
            first.append(copy(a, 0, me, sibling, src=ins[a]))
            first += [copy(a, 1 + j, me, (*chip, c), src=ins[a]) for j, chip in enumerate(chips)]
        for cp in first:
            cp.start()
        passed = []
        for a in range(n):
            for j, chip in enumerate(chips):
                copy(a, 1 + j, (*chip, c), me).wait_recv()
                fwd = copy(a, 4 + j, (*chip, c), sibling)
                fwd.start()
                passed.append(fwd)
        for a in range(n):
            copy(a, 0, sibling, me).wait_recv()
            for j, chip in enumerate(chips):
                copy(a, 4 + j, (*chip, 1 - c), me).wait_recv()
        for cp in first + passed:
            cp.wait_send()
        for cp in mine:
            cp.wait()

    outs = pl.pallas_call(
        body, name=name,
        out_shape=[jax.ShapeDtypeStruct((N_DEV,) + a.shape, a.dtype) for a in arrs],
        in_specs=[ANY] * n, out_specs=[ANY] * n,
        scratch_shapes=[pltpu.SemaphoreType.DMA((n, 7)), pltpu.SemaphoreType.DMA((n, 7)), pltpu.SemaphoreType.DMA((n,))],
    )(*arrs)
    return list(outs)


def _peer_copies(src_of, dsts, send_sems, recv_sems):
    x, y, c = _mesh_pos()
    me = 4 * x + 2 * y + c
    sends, arrivals = [], []
    for a in range(len(dsts)):
        for k in range(1, N_DEV):
            px, py, pc = (1 - x if k & 4 else x), (1 - y if k & 2 else y), (1 - c if k & 1 else c)
            p = 4 * px + 2 * py + pc
            common = dict(send_sem=send_sems.at[a, k - 1], recv_sem=recv_sems.at[a, k - 1],
                          device_id=(px, py, pc), device_id_type=MESH)
            sends.append(pltpu.make_async_remote_copy(src_ref=src_of(a, p), dst_ref=dsts[a].at[me], **common))
            arrivals.append(pltpu.make_async_remote_copy(src_ref=dsts[a].at[p], dst_ref=dsts[a].at[p], **common))
    return sends, arrivals


def _peer_sems(n):
    return [pltpu.SemaphoreType.DMA((n, N_DEV - 1)), pltpu.SemaphoreType.DMA((n, N_DEV - 1))]


def _shard_of(ref, kind, j):
    if kind == "col":
        w = ref.shape[1] // N_DEV
        return ref.at[:, pl.ds(j * w, w)]
    h = ref.shape[0] // N_DEV
    return ref.at[pl.ds(j * h, h), :]


def _shard_shape(shape, kind):
    return (shape[0], shape[1] // N_DEV) if kind == "col" else (shape[0] // N_DEV, shape[1])


def _exchange_cores(grads, kinds, name):
    n = len(grads)

    def body(*refs):
        ins, outs = refs[:n], refs[n:2 * n]
        send_sems, recv_sems = refs[2 * n:]
        x, y, c = _mesh_pos()
        sibling = (x, y, 1 - c)
        sends = []
        for a in range(n):
            for s in range(4):
                sends.append(pltpu.make_async_remote_copy(
                    src_ref=_shard_of(ins[a], kinds[a], 2 * s + (1 - c)), dst_ref=outs[a].at[s],
                    send_sem=send_sems.at[a, s], recv_sem=recv_sems.at[a, s], device_id=sibling, device_id_type=MESH))
        for cp in sends:
            cp.start()
        for a in range(n):
            for s in range(4):
                pltpu.make_async_remote_copy(
                    src_ref=outs[a].at[s], dst_ref=outs[a].at[s],
                    send_sem=send_sems.at[a, s], recv_sem=recv_sems.at[a, s], device_id=sibling, device_id_type=MESH).wait_recv()
        for cp in sends:
            cp.wait_send()

    outs = pl.pallas_call(
        body, name=name,
        out_shape=[jax.ShapeDtypeStruct((4,) + _shard_shape(g.shape, k), F32) for g, k in zip(grads, kinds)],
        in_specs=[ANY] * n, out_specs=[ANY] * n,
        scratch_shapes=[pltpu.SemaphoreType.DMA((n, 4)), pltpu.SemaphoreType.DMA((n, 4))],
    )(*grads)
    return list(outs)


def _exchange_chips(parts, name):
    n = len(parts)

    def body(*refs):
        ins, outs = refs[:n], refs[n:2 * n]
        send_sems, recv_sems = refs[2 * n:]
        x, y, c = _mesh_pos()
        my_slot = 2 * x + y
        chips = [(1 - x, y), (x, 1 - y), (1 - x, 1 - y)]
        sends = []
        for a in range(n):
            for j, (px, py) in enumerate(chips):
                sends.append(pltpu.make_async_remote_copy(
                    src_ref=ins[a].at[2 * px + py], dst_ref=outs[a].at[my_slot],
                    send_sem=send_sems.at[a, j], recv_sem=recv_sems.at[a, j], device_id=(px, py, c), device_id_type=MESH))
        for cp in sends:
            cp.start()
        for a in range(n):
            for j, (px, py) in enumerate(chips):
                pltpu.make_async_remote_copy(
                    src_ref=outs[a].at[2 * px + py], dst_ref=outs[a].at[2 * px + py],
                    send_sem=send_sems.at[a, j], recv_sem=recv_sems.at[a, j], device_id=(px, py, c), device_id_type=MESH).wait_recv()
        for cp in sends:
            cp.wait_send()

    outs = pl.pallas_call(
        body, name=name,
        out_shape=[jax.ShapeDtypeStruct(p.shape, p.dtype) for p in parts],
        in_specs=[ANY] * n, out_specs=[ANY] * n,
        scratch_shapes=[pltpu.SemaphoreType.DMA((n, 3)), pltpu.SemaphoreType.DMA((n, 3))],
    )(*parts)
    return list(outs)


def _row_tile(rows):
    for t in (256, 128, 64, 32, 16, 8):
        if rows % t == 0:
            return t
    return rows


def _wire_dtype(rows):
    return BF16 if rows % 16 == 0 else F32


def _add_own(core, g, recv, kind, name):
    ns, R, C = recv.shape
    tr = _row_tile(R)
    nr = R // tr
    if kind == "col":
        g_spec = pl.BlockSpec((tr, C), lambda s, r, c_ref: (r, 2 * s + c_ref[0]))
    else:
        g_spec = pl.BlockSpec((tr, C), lambda s, r, c_ref: ((2 * s + c_ref[0]) * nr + r, 0))
    slab = pl.BlockSpec((None, tr, C), lambda s, r, c_ref: (s, r, 0))

    def body(c_ref, g_ref, r_ref, o32_ref, o16_ref):
        t = g_ref[...] + r_ref[...]
        o32_ref[...] = t
        o16_ref[...] = t.astype(o16_ref.dtype)

    return pl.pallas_call(
        body, name=name,
        grid_spec=pltpu.PrefetchScalarGridSpec(num_scalar_prefetch=1, grid=(ns, nr), in_specs=[g_spec, slab], out_specs=[slab, slab]),
        out_shape=[jax.ShapeDtypeStruct((ns, R, C), F32), jax.ShapeDtypeStruct((ns, R, C), _wire_dtype(R))],
        compiler_params=_cparams(2),
    )(core, g, recv)


def _adamw_math(w, g, m, v):
    m2 = ADAM_B1 * m + (1.0 - ADAM_B1) * g
    v2 = ADAM_B2 * v + (1.0 - ADAM_B2) * (g * g)
    m_hat = m2 / (1.0 - ADAM_B1 ** ADAM_STEP)
    v_hat = v2 / (1.0 - ADAM_B2 ** ADAM_STEP)
    delta = -ADAM_LR * (m_hat / (jnp.sqrt(v_hat) + ADAM_EPS) + ADAM_WD * w)
    return delta, m2, v2


def _sum_adamw(parts, w, m, v, name):
    n, R, C = parts.shape
    tr = _row_tile(R)

    def body(p_ref, w_ref, m_ref, v_ref, g_ref, d_ref, m_out, v_out):
        g = p_ref[0]
        for k in range(1, n):
            g = g + p_ref[k]
        g_ref[...] = g
        d_ref[...], m_out[...], v_out[...] = _adamw_math(w_ref[...], g, m_ref[...], v_ref[...])

    blk = pl.BlockSpec((tr, C), lambda r: (r, 0))
    return pl.pallas_call(
        body, name=name, grid=(R // tr,),
        in_specs=[pl.BlockSpec((n, tr, C), lambda r: (0, r, 0)), blk, blk, blk],
        out_specs=[blk] * 4, out_shape=[jax.ShapeDtypeStruct((R, C), F32)] * 4, compiler_params=_cparams(1),
    )(parts, w, m, v)


def _sum_chips_adamw(chip, own, recv, w, m, v, name):
    n, R, C = recv.shape
    tr = _row_tile(R)

    def body(chip_ref, own_ref, r_ref, w_ref, m_ref, v_ref, g_ref, d_ref, m_out, v_out):
        g = None
        for k in range(n):
            term = jnp.where(chip_ref[0] == k, own_ref[...], r_ref[k].astype(F32))
            g = term if g is None else g + term
        g_ref[...] = g
        d_ref[...], m_out[...], v_out[...] = _adamw_math(w_ref[...], g, m_ref[...], v_ref[...])

    blk = pl.BlockSpec((tr, C), lambda r, chip_ref: (r, 0))
    return pl.pallas_call(
        body, name=name,
        grid_spec=pltpu.PrefetchScalarGridSpec(
            num_scalar_prefetch=1, grid=(R // tr,),
            in_specs=[pl.BlockSpec((None, tr, C), lambda r, chip_ref: (chip_ref[0], r, 0)),
                      pl.BlockSpec((n, tr, C), lambda r, chip_ref: (0, r, 0)), blk, blk, blk],
            out_specs=[blk] * 4),
        out_shape=[jax.ShapeDtypeStruct((R, C), F32)] * 4, compiler_params=_cparams(1),
    )(chip, own, recv, w, m, v)


def _pack_rows(pieces, name):
    arrs = []
    for a, _, _, _ in pieces:
        if not any(a is b for b in arrs):
            arrs.append(a)
    which = [next(i for i, b in enumerate(arrs) if b is a) for a, _, _, _ in pieces]
    total = sum(n for _, _, _, n in pieces)

    def body(*refs):
        o_ref = refs[len(arrs)]
        off = 0
        for (_, r, c0, n), i in zip(pieces, which):
            o_ref[:, off:off + n] = refs[i][r:r + 1, c0:c0 + n]
            off += n

    return pl.pallas_call(
        body, name=name, in_specs=[_full(a.shape) for a in arrs], out_specs=_full((1, total)),
        out_shape=jax.ShapeDtypeStruct((1, total), F32), grid=(1,), compiler_params=_cparams(1),
    )(*arrs)


def _adamw_vectors(parts, ws, ms, vs):
    nv = len(ws)
    widths = [a.shape[1] for a in ws]

    def body(*refs):
        p_ref = refs[0]
        w_refs, m_refs, v_refs = refs[1:1 + nv], refs[1 + nv:1 + 2 * nv], refs[1 + 2 * nv:1 + 3 * nv]
        outs = refs[1 + 3 * nv:]
        off = 0
        for i, n in enumerate(widths):
            g = p_ref[0:1, off:off + n]
            for k in range(1, p_ref.shape[0]):
                g = g + p_ref[k:k + 1, off:off + n]
            outs[i][...] = g
            outs[nv + i][...], outs[2 * nv + i][...], outs[3 * nv + i][...] = _adamw_math(w_refs[i][...], g, m_refs[i][...], v_refs[i][...])
            off += n

    vec_specs = [_full((1, n)) for n in widths]
    res = pl.pallas_call(
        body, name="adamw_vectors", grid=(1,),
        in_specs=[_full(parts.shape)] + vec_specs * 3, out_specs=vec_specs * 4,
        out_shape=[jax.ShapeDtypeStruct((1, n), F32) for n in widths] * 4, compiler_params=_cparams(1),
    )(parts, *ws, *ms, *vs)
    return [res[t * nv:(t + 1) * nv] for t in range(4)]


def _sum_peers_adamw(me, g, recv, kind, w, m, v, name):
    n, R, C = recv.shape
    tr = _row_tile(R)
    nr = R // tr
    if kind == "col":
        g_spec = pl.BlockSpec((tr, C), lambda r, me_ref: (r, me_ref[0]))
    else:
        g_spec = pl.BlockSpec((tr, C), lambda r, me_ref: (me_ref[0] * nr + r, 0))

    def body(me_ref, own_ref, r_ref, w_ref, m_ref, v_ref, g_ref, d_ref, m_out, v_out):
        acc = None
        for k in range(n):
            term = jnp.where(me_ref[0] == k, own_ref[...], r_ref[k].astype(F32))
            acc = term if acc is None else acc + term
        g_ref[...] = acc
        d_ref[...], m_out[...], v_out[...] = _adamw_math(w_ref[...], acc, m_ref[...], v_ref[...])

    blk = pl.BlockSpec((tr, C), lambda r, me_ref: (r, 0))
    return pl.pallas_call(
        body, name=name,
        grid_spec=pltpu.PrefetchScalarGridSpec(
            num_scalar_prefetch=1, grid=(nr,),
            in_specs=[g_spec, pl.BlockSpec((n, tr, C), lambda r, me_ref: (0, r, 0)), blk, blk, blk], out_specs=[blk] * 4),
        out_shape=[jax.ShapeDtypeStruct((R, C), F32)] * 4, compiler_params=_cparams(1),
    )(me, g, recv, w, m, v)


def _ada_fwd(c, w_ada_g, b_ada):
    nb, D, W = w_ada_g.shape

    def body(c_ref, w_ref, b_ref, o_ref):
        cv = c_ref[...]
        ca = jnp.broadcast_to(cv * _sigmoid(cv), (8, D)).astype(BF16)
        o_ref[...] = _dot(ca, w_ref[...])[0:1, :] + b_ref[...]

    return pl.pallas_call(
        body, name="ada_fwd", grid=(nb,),
        in_specs=[_full((1, D)), pl.BlockSpec((None, D, W), lambda j: (j, 0, 0)), pl.BlockSpec((1, W), lambda j: (0, j))],
        out_specs=pl.BlockSpec((1, W), lambda j: (0, j)),
        out_shape=jax.ShapeDtypeStruct((1, nb * W), F32), compiler_params=_cparams(1),
    )(c, w_ada_g, b_ada)


def _ada_bwd(me, small_g, D, W, c_block):
    def body(me_ref, c_ref, dm_ref, o_ref):
        cv = c_ref[...]
        ca = (cv * _sigmoid(cv)).astype(BF16).astype(F32)
        dm = dm_ref[...].astype(BF16).astype(F32)
        o_ref[...] = lax.dot_general(ca, dm, (((0,), (0,)), ((), ())), precision=lax.Precision.HIGHEST,
                                     preferred_element_type=F32)

    return pl.pallas_call(
        body, name="ada_bwd",
        grid_spec=pltpu.PrefetchScalarGridSpec(
            num_scalar_prefetch=1, grid=(1,),
            in_specs=[pl.BlockSpec((N_DEV, D), lambda i, me_ref: (0, c_block)),
                      pl.BlockSpec((N_DEV, W), lambda i, me_ref: (0, me_ref[0]))],
            out_specs=pl.BlockSpec((D, W), lambda i, me_ref: (0, 0))),
        out_shape=jax.ShapeDtypeStruct((D, W), F32), compiler_params=_cparams(1),
    )(me, small_g, small_g)


def _fwd_in(x, mod, w_in_g, b_in, shards):
    S, D = x.shape
    nb, _, wb = w_in_g.shape
    N = nb * wb
    TM = TOKEN_TILE
    n_tiles = S // TM
    ns = len(shards)

    def body(*refs):
        x_ref, mod_ref, w_hbm, b_ref = refs[:4]
        ins = refs[4:4 + ns]
        z_ref, h_ref = refs[4 + ns:6 + ns]
        outs = refs[6 + ns:6 + 2 * ns]
        w_v, sems, send_sems, recv_sems, local_sems = refs[6 + 2 * ns:]
        i = pl.program_id(0)
        x_, y_, c_ = _mesh_pos()
        me = 4 * x_ + 2 * y_ + c_
        mine = [pltpu.make_async_copy(ins[a], outs[a].at[me], local_sems.at[a]) for a in range(ns)]
        sends, arrivals = _peer_copies(lambda a, p: ins[a], outs, send_sems, recv_sems)

        @pl.when(i == 0)
        def _():
            for cp in sends + mine:
                cp.start()
            _load_cols(w_hbm, w_v, sems)

        xhat, _ = _ln(x_ref[...])
        hb = (xhat * (1.0 + mod_ref[:, D:2 * D]) + mod_ref[:, 0:D]).astype(BF16)
        h_ref[...] = hb
        z_ref[...] = _dot(hb, w_v[...]) + b_ref[...]

        @pl.when(i == n_tiles - 1)
        def _():
            for cp in arrivals:
                cp.wait_recv()
            for cp in sends:
                cp.wait_send()
            for cp in mine:
                cp.wait()

    res = pl.pallas_call(
        body, name="fwd_in", grid=(n_tiles,),
        in_specs=[pl.BlockSpec((TM, D), lambda i: (i, 0)), _full(mod.shape), ANY, _full((1, N))] + [ANY] * ns,
        out_specs=[pl.BlockSpec((TM, N), lambda i: (i, 0)), pl.BlockSpec((TM, D), lambda i: (i, 0))] + [ANY] * ns,
        out_shape=[jax.ShapeDtypeStruct((S, N), F32), jax.ShapeDtypeStruct((S, D), BF16)]
        + [jax.ShapeDtypeStruct((N_DEV,) + a.shape, a.dtype) for a in shards],
        scratch_shapes=[pltpu.VMEM((D, N), BF16), pltpu.SemaphoreType.DMA((nb,))] + _peer_sems(ns) + [pltpu.SemaphoreType.DMA((ns,))],
        compiler_params=_cparams(1),
    )(x, mod, w_in_g, b_in, *shards)
    return res[0], res[1], list(res[2:])


def _rows_from(win, o, R):
    if o % 8 == 0:
        return win[o:o + R, :]
    return pltpu.roll(win, win.shape[0] - o, axis=0)[0:R, :]


def _causal_taps(buf, w_ref, o_ref, K, halo, TM, D):
    R, C = CONV_ROWS, CONV_LANES
    base = halo - (K - 1)

    for r0 in range(0, TM, R):
        for c0 in range(0, D, C):
            win = buf[r0:r0 + R + halo, c0:c0 + C]
            acc = jnp.zeros((R, C), F32)
            for k in range(K):
                acc = acc + w_ref[k:k + 1, c0:c0 + C] * _rows_from(win, base + k, R)
            o_ref[r0:r0 + R, c0:c0 + C] = acc


def _fwd_mix(z, x, mod, wa, ba, lag, lab, w_a_out, b_a_out, wb, w_b_out, w_o, b_o):
    S, D = x.shape
    N = z.shape[1]
    TM = TOKEN_TILE
    KA, KB = wa.shape[0], wb.shape[0]

    def body(z_ref, x_ref, mod_ref, wa_ref, ba_ref, lag_ref, lab_ref, wao_hbm, bao_ref, wb_ref, wbo_hbm, wo_hbm, bo_ref,
             u1_ref, ya_ref, yb_ref, q_ref, out_ref, r1_ref, u2_ref, v_ref, mg_ref,
             ubuf, pbuf, wao, wbo, wo, sems):
        i = pl.program_id(0)

        @pl.when(i == 0)
        def _():
            _load_whole([(wao_hbm, wao), (wbo_hbm, wbo), (wo_hbm, wo)], sems)
            ubuf[0:HALO_A, :] = jnp.zeros((HALO_A, D), F32)
            pbuf[0:HALO_B, :] = jnp.zeros((HALO_B, D), F32)

        ubuf[HALO_A:HALO_A + TM, :] = z_ref[:, 0:D] * _sigmoid(z_ref[:, D:2 * D])
        _causal_taps(ubuf, wa_ref, u1_ref, KA, HALO_A, TM, D)
        ubuf[0:HALO_A, :] = ubuf[TM:TM + HALO_A, :]
        u1 = u1_ref[...] + ba_ref[...]
        u1_ref[...] = u1
        xa, _ = _ln(u1)
        l = xa * lag_ref[...] + lab_ref[...]
        u2 = (l * _sigmoid(l)).astype(BF16)
        u2_ref[...] = u2
        ya = _dot(u2, wao[...]) + bao_ref[...]
        ya_ref[...] = ya.astype(BF16)
        pbuf[HALO_B:HALO_B + TM, :] = z_ref[:, 3 * D:4 * D] * z_ref[:, 4 * D:5 * D]
        _causal_taps(pbuf, wb_ref, q_ref, KB, HALO_B, TM, D)
        pbuf[0:HALO_B, :] = pbuf[TM:TM + HALO_B, :]
        v = (z_ref[:, 2 * D:3 * D] * q_ref[...]).astype(BF16)
        v_ref[...] = v
        yb = _dot(v, wbo[...])
        yb_ref[...] = yb.astype(BF16)
        mg = (_sigmoid(z_ref[:, 5 * D:6 * D]) * ya + _sigmoid(z_ref[:, 6 * D:7 * D]) * yb).astype(BF16)
        mg_ref[...] = mg
        out = _dot(mg, wo[...]) + bo_ref[...]
        out_ref[...] = out.astype(BF16)
        r1_ref[...] = ALPHA * x_ref[...] + (1.0 + mod_ref[:, 2 * D:3 * D]) * out

    tile = pl.BlockSpec((TM, D), lambda i: (i, 0))
    vec = _full((1, D))
    return pl.pallas_call(
        body, name="fwd_mix", grid=(S // TM,),
        in_specs=[pl.BlockSpec((TM, N), lambda i: (i, 0)), tile, _full(mod.shape), _full((KA, D)), vec, vec, vec, ANY, vec,
                  _full((KB, D)), ANY, ANY, vec],
        out_specs=[tile] * 9,
        out_shape=[jax.ShapeDtypeStruct((S, D), dt) for dt in (F32, BF16, BF16, F32, BF16, F32, BF16, BF16, BF16)],
        scratch_shapes=[pltpu.VMEM((TM + HALO_A, D), F32), pltpu.VMEM((TM + HALO_B, D), F32),
                        pltpu.VMEM((D, D), BF16), pltpu.VMEM((D, D), BF16), pltpu.VMEM((D, D), BF16),
                        pltpu.SemaphoreType.DMA((3,))],
        compiler_params=_cparams(1),
    )(z, x, mod, wa, ba, lag, lab, w_a_out, b_a_out, wb, w_b_out, w_o, b_o)


FFN_ROWS = ("b_down", "ln2_g", "ln2_b", "gate2", "scale2", "shift2", "loss")


def _ffn(r1, tgt, mod, ln1_g, ln1_b, w_up_g, b_up, w_down, b_down, ln2_g, ln2_b):
    S, D = r1.shape
    nb, _, wb = w_up_g.shape
    FF = nb * wb
    TM = TOKEN_TILE
    CF = 1024
    n_tiles = S // TM

    def body(r1_ref, t_ref, mod_ref, g1_ref, b1_ref, wup_hbm, bup_ref, wdn_hbm, bdn_ref, g2_ref, b2_ref,
             dx1_ref, h2_ref, f_ref, dhu_ref, do2_ref, acc_ref, accup_ref,
             wup, wdn, relu_buf, sems, sem2):
        i = pl.program_id(0)

        @pl.when(i == 0)
        def _():
            _load_cols(wup_hbm, wup, sems)
            _load_whole([(wdn_hbm, wdn)], sem2)
            acc_ref[...] = jnp.zeros(acc_ref.shape, F32)
            accup_ref[...] = jnp.zeros(accup_ref.shape, F32)

        scale2, shift2, gate2 = mod_ref[:, 4 * D:5 * D], mod_ref[:, 3 * D:4 * D], mod_ref[:, 5 * D:6 * D]
        xhat1, _ = _ln(r1_ref[...])
        x1 = xhat1 * g1_ref[...] + b1_ref[...]
        xh0, rstd0 = _ln(x1)
        h2 = (xh0 * (1.0 + scale2) + shift2).astype(BF16)
        h2_ref[...] = h2
        out2 = jnp.zeros((TM, D), F32) + bdn_ref[...]
        for c0 in range(0, FF, CF):
            hu = _dot(h2, wup[:, c0:c0 + CF]) + bup_ref[:, c0:c0 + CF]
            rl = jnp.maximum(hu, 0.0)
            relu_buf[:, c0:c0 + CF] = rl
            fb = (rl * rl).astype(BF16)
            f_ref[:, c0:c0 + CF] = fb
            out2 = out2 + _dot(fb, wdn[c0:c0 + CF, :])
        r2 = ALPHA * x1 + (1.0 + gate2) * out2
        xh2, rstd2 = _ln(r2)
        e = xh2 * g2_ref[...] + b2_ref[...] - t_ref[...]
        acc_ref[6:7, :] += _colsum(e * e)
        dy = e * (1.0 / D)
        acc_ref[1:2, :] += _colsum(dy * xh2)
        acc_ref[2:3, :] += _colsum(dy)
        dr2 = _ln_bwd(dy * g2_ref[...], xh2, rstd2)
        acc_ref[3:4, :] += _colsum(dr2 * out2)
        do2 = (1.0 + gate2) * dr2
        acc_ref[0:1, :] += _colsum(do2)
        do2b = do2.astype(BF16)
        do2_ref[...] = do2b
        dh2 = jnp.zeros((TM, D), F32)
        for c0 in range(0, FF, CF):
            dhu = _dot_nt(do2b, wdn[c0:c0 + CF, :]) * (2.0 * relu_buf[:, c0:c0 + CF])
            accup_ref[:, c0:c0 + CF] += _colsum(dhu)
            dhub = dhu.astype(BF16)
            dhu_ref[:, c0:c0 + CF] = dhub
            dh2 = dh2 + _dot_nt(dhub, wup[:, c0:c0 + CF])
        acc_ref[4:5, :] += _colsum(dh2 * xh0)
        acc_ref[5:6, :] += _colsum(dh2)
        dx1_ref[...] = ALPHA * dr2 + _ln_bwd(dh2 * (1.0 + scale2), xh0, rstd0)

        @pl.when(i == n_tiles - 1)
        def _():
            tot = jnp.sum(acc_ref[6:7, :], axis=-1, keepdims=True) * (0.5 / D)
            acc_ref[6:7, :] = jnp.broadcast_to(tot, (1, D))

    tile = pl.BlockSpec((TM, D), lambda i: (i, 0))
    wide = pl.BlockSpec((TM, FF), lambda i: (i, 0))
    vec = _full((1, D))
    return pl.pallas_call(
        body, name="ffn", grid=(n_tiles,),
        in_specs=[tile, tile, _full(mod.shape), vec, vec, ANY, _full((1, FF)), ANY, vec, vec, vec],
        out_specs=[tile, tile, wide, wide, tile, _full((8, D)), _full((1, FF))],
        out_shape=[jax.ShapeDtypeStruct((S, D), F32), jax.ShapeDtypeStruct((S, D), BF16), jax.ShapeDtypeStruct((S, FF), BF16),
                   jax.ShapeDtypeStruct((S, FF), BF16), jax.ShapeDtypeStruct((S, D), BF16),
                   jax.ShapeDtypeStruct((8, D), F32), jax.ShapeDtypeStruct((1, FF), F32)],
        scratch_shapes=[pltpu.VMEM((D, FF), BF16), pltpu.VMEM((FF, D), BF16), pltpu.VMEM((TM, FF), F32),
                        pltpu.SemaphoreType.DMA((nb,)), pltpu.SemaphoreType.DMA((1,))],
        compiler_params=_cparams(1),
    )(r1, tgt, mod, ln1_g, ln1_b, w_up_g, b_up, w_down, b_down, ln2_g, ln2_b)


def _dw(a, b, name, msplit=1, nsplit=1, wire=False, exchange=None):
    S, M = a.shape
    N = b.shape[1]
    TK = min(DW_TILE, S)
    mb, nbk = M // msplit, N // nsplit
    nk = S // TK
    srcs, kinds = exchange if exchange else ((), ())
    ns = len(srcs)

    def body(*refs):
        a_ref, b_ref = refs[:2]
        ins = refs[2:2 + ns]
        o_ref = refs[2 + ns]
        o16_ref = refs[3 + ns] if wire else None
        rest = refs[3 + ns + (1 if wire else 0):]
        k = pl.program_id(2)
        if ns:
            outs, (send_sems, recv_sems) = rest[:ns], rest[ns:]
            sends, arrivals = _peer_copies(lambda q, p: _shard_of(ins[q], kinds[q], p), outs, send_sems, recv_sems)
            first = (pl.program_id(0) == 0) & (pl.program_id(1) == 0) & (k == 0)
            last = (pl.program_id(0) == msplit - 1) & (pl.program_id(1) == nsplit - 1) & (k == nk - 1)

            @pl.when(first)
            def _():
                for cp in sends:
                    cp.start()

        @pl.when(k == 0)
        def _():
            o_ref[...] = jnp.zeros(o_ref.shape, F32)

        o_ref[...] += _dot_tn(a_ref[...], b_ref[...])

        if wire:
            @pl.when(k == nk - 1)
            def _():
                o16_ref[...] = o_ref[...].astype(BF16)

        if ns:
            @pl.when(last)
            def _():
                for cp in arrivals:
                    cp.wait_recv()
                for cp in sends:
                    cp.wait_send()

    oblk = pl.BlockSpec((mb, nbk), lambda i, j, k: (i, j))
    res = pl.pallas_call(
        body, name=name, grid=(msplit, nsplit, nk),
        in_specs=[pl.BlockSpec((TK, mb), lambda i, j, k: (k, i)), pl.BlockSpec((TK, nbk), lambda i, j, k: (k, j))] + [ANY] * ns,
        out_specs=[oblk] + ([oblk] if wire else []) + [ANY] * ns,
        out_shape=[jax.ShapeDtypeStruct((M, N), F32)] + ([jax.ShapeDtypeStruct((M, N), BF16)] if wire else [])
        + [jax.ShapeDtypeStruct((N_DEV,) + _shard_shape(g.shape, kd), g.dtype) for g, kd in zip(srcs, kinds)],
        scratch_shapes=_peer_sems(ns) if ns else [],
        compiler_params=_cparams(3),
    )(a, b, *srcs)
    return res if (wire or ns) else res[0]


BWD1_ROWS = ("ln1_g", "ln1_b", "gate1", "b_o", "b_a_out", "ln_a_g", "ln_a_b", "conv_a_b", "b_in_gb", "b_in_ga", "b_in_gbb")


def _bwd_mix1(dx1, r1, out, ya, yb, q, u1, z, mod, ln1_g, lag, lab, w_o, w_a_out, w_b_out):
    S, D = dx1.shape
    TM = TOKEN_TILE

    def body(dx1_ref, r1_ref, out_ref, ya_ref, yb_ref, q_ref, u1_ref, zgb_ref, zga_ref, zgg_ref, mod_ref, g1_ref, lag_ref, lab_ref,
             wo_hbm, wao_hbm, wbo_hbm,
             dxp_ref, du1_ref, dq_ref, dzc_ref, dout_ref, dya_ref, dyb_ref, acc_ref,
             wo, wao, wbo, sems):
        @pl.when(pl.program_id(0) == 0)
        def _():
            _load_whole([(wo_hbm, wo), (wao_hbm, wao), (wbo_hbm, wbo)], sems)
            acc_ref[...] = jnp.zeros(acc_ref.shape, F32)

        dx1v = dx1_ref[...]
        xhat1, rstd1 = _ln(r1_ref[...])
        acc_ref[0:1, :] += _colsum(dx1v * xhat1)
        acc_ref[1:2, :] += _colsum(dx1v)
        dr1 = _ln_bwd(dx1v * g1_ref[...], xhat1, rstd1)
        dxp_ref[...] = ALPHA * dr1
        acc_ref[2:3, :] += _colsum(dr1 * out_ref[...].astype(F32))
        dout = (1.0 + mod_ref[:, 2 * D:3 * D]) * dr1
        acc_ref[3:4, :] += _colsum(dout)
        doutb = dout.astype(BF16)
        dout_ref[...] = doutb
        dmg = _dot_nt(doutb, wo[...])
        sga, sgb = _sigmoid(zga_ref[...]), _sigmoid(zgg_ref[...])
        dga = dmg * ya_ref[...].astype(F32) * sga * (1.0 - sga)
        dgb = dmg * yb_ref[...].astype(F32) * sgb * (1.0 - sgb)
        acc_ref[9:10, :] += _colsum(dga)
        acc_ref[10:11, :] += _colsum(dgb)
        dzc_ref[:, D:2 * D] = dga.astype(BF16)
        dzc_ref[:, 2 * D:3 * D] = dgb.astype(BF16)
        dya = dmg * sga
        acc_ref[4:5, :] += _colsum(dya)
        dyab = dya.astype(BF16)
        dya_ref[...] = dyab
        dybb = (dmg * sgb).astype(BF16)
        dyb_ref[...] = dybb
        du2 = _dot_nt(dyab, wao[...])
        xa, rstda = _ln(u1_ref[...])
        l = xa * lag_ref[...] + lab_ref[...]
        sl = _sigmoid(l)
        dl = du2 * (sl * (1.0 + l * (1.0 - sl)))
        acc_ref[5:6, :] += _colsum(dl * xa)
        acc_ref[6:7, :] += _colsum(dl)
        du1 = _ln_bwd(dl * lag_ref[...], xa, rstda)
        acc_ref[7:8, :] += _colsum(du1)
        du1_ref[...] = du1
        dv = _dot_nt(dybb, wbo[...])
        dgbk = dv * q_ref[...]
        acc_ref[8:9, :] += _colsum(dgbk)
        dzc_ref[:, 0:D] = dgbk.astype(BF16)
        dq_ref[...] = dv * zgb_ref[...]

    tile = pl.BlockSpec((TM, D), lambda i: (i, 0))
    vec = _full((1, D))
    zcol = lambda k: pl.BlockSpec((TM, D), lambda i: (i, k))
    return pl.pallas_call(
        body, name="bwd_mix1", grid=(S // TM,),
        in_specs=[tile] * 7 + [zcol(2), zcol(5), zcol(6), _full(mod.shape), vec, vec, vec, ANY, ANY, ANY],
        out_specs=[tile, tile, tile, pl.BlockSpec((TM, 3 * D), lambda i: (i, 0)), tile, tile, tile, _full((16, D))],
        out_shape=[jax.ShapeDtypeStruct((S, D), F32)] * 3 + [jax.ShapeDtypeStruct((S, 3 * D), BF16)]
        + [jax.ShapeDtypeStruct((S, D), BF16)] * 3 + [jax.ShapeDtypeStruct((16, D), F32)],
        scratch_shapes=[pltpu.VMEM((D, D), BF16)] * 3 + [pltpu.SemaphoreType.DMA((3,))],
        compiler_params=_cparams(1),
    )(dx1, r1, out, ya, yb, q, u1, z, z, z, mod, ln1_g, lag, lab, w_o, w_a_out, w_b_out)


def _anticausal_taps(dbuf, u_ref, w_ref, o_ref, dwacc, K, halo, TM, D):
    R, C = CONV_ROWS, CONV_LANES

    for r0 in range(0, TM, R):
        for c0 in range(0, D, C):
            win = dbuf[r0:r0 + R + halo, c0:c0 + C]
            uc = u_ref[r0:r0 + R, c0:c0 + C]
            acc = jnp.zeros((R, C), F32)
            for j in range(K):
                sh = _rows_from(win, j, R)
                k = K - 1 - j
                acc = acc + w_ref[k:k + 1, c0:c0 + C] * sh
                pr = uc * sh
                part = pr[0:8, :]
                for s in range(8, R, 8):
                    part = part + pr[s:s + 8, :]
                dwacc[k, :, c0:c0 + C] += part
            o_ref[r0:r0 + R, c0:c0 + C] = acc


def _bwd_mix2(du1, dq, z, dzc, x, dxp, mod, wa, wb, w_in_g):
    S, D = x.shape
    N = z.shape[1]
    nb, _, wbk = w_in_g.shape
    TM = TOKEN_TILE
    KA, KB = wa.shape[0], wb.shape[0]
    n_tiles = S // TM

    def body(du1_ref, dq_ref, zav_ref, zc_ref, zx_ref, dzc_ref, x_ref, dxp_ref, mod_ref, wa_ref, wb_ref, win_hbm,
             dz_ref, gx_ref, dwa_ref, dwb_ref, acc_ref, dbin_ref,
             dbuf, qbuf, ubuf, pbuf, obuf, dwa_acc, dwb_acc, w_v, sems):
        i = pl.program_id(0)

        @pl.when(i == 0)
        def _():
            _load_cols(win_hbm, w_v, sems)
            dbuf[TM:TM + HALO_A, :] = jnp.zeros((HALO_A, D), F32)
            qbuf[TM:TM + HALO_B, :] = jnp.zeros((HALO_B, D), F32)
            dwa_acc[...] = jnp.zeros(dwa_acc.shape, F32)
            dwb_acc[...] = jnp.zeros(dwb_acc.shape, F32)
            acc_ref[...] = jnp.zeros(acc_ref.shape, F32)
            dbin_ref[...] = jnp.zeros(dbin_ref.shape, F32)

        a_val = zav_ref[:, 0:D]
        sa = _sigmoid(zav_ref[:, D:2 * D])
        ubuf[...] = a_val * sa
        dbuf[0:TM, :] = du1_ref[...]
        _anticausal_taps(dbuf, ubuf, wa_ref, obuf, dwa_acc, KA, HALO_A, TM, D)
        dbuf[TM:TM + HALO_A, :] = dbuf[0:HALO_A, :]
        du0 = obuf[...]
        dav = du0 * sa
        dag = du0 * a_val * sa * (1.0 - sa)
        dbin_ref[:, 0:D] += _colsum(dav)
        dbin_ref[:, D:2 * D] += _colsum(dag)
        dz_ref[:, 0:D] = dav.astype(BF16)
        dz_ref[:, D:2 * D] = dag.astype(BF16)
        pbuf[...] = zc_ref[...] * zx_ref[...]
        qbuf[0:TM, :] = dq_ref[...]
        _anticausal_taps(qbuf, pbuf, wb_ref, obuf, dwb_acc, KB, HALO_B, TM, D)
        qbuf[TM:TM + HALO_B, :] = qbuf[0:HALO_B, :]
        dp = obuf[...]
        dgc = dp * zx_ref[...]
        dgx = dp * zc_ref[...]
        dbin_ref[:, 3 * D:4 * D] += _colsum(dgc)
        dbin_ref[:, 4 * D:5 * D] += _colsum(dgx)
        dz_ref[:, 3 * D:4 * D] = dgc.astype(BF16)
        dz_ref[:, 4 * D:5 * D] = dgx.astype(BF16)
        dz_ref[:, 2 * D:3 * D] = dzc_ref[:, 0:D]
        dz_ref[:, 5 * D:7 * D] = dzc_ref[:, D:3 * D]
        dh = _dot_nt(dz_ref[...], w_v[...])
        xhat, rstd = _ln(x_ref[...])
        acc_ref[0:1, :] += _colsum(dh * xhat)
        acc_ref[1:2, :] += _colsum(dh)
        gx_ref[...] = dxp_ref[...] + _ln_bwd(dh * (1.0 + mod_ref[:, D:2 * D]), xhat, rstd)

        @pl.when(i == n_tiles - 1)
        def _():
            for k in range(KA):
                dwa_ref[k:k + 1, :] = jnp.sum(dwa_acc[k], axis=0, keepdims=True)
            for k in range(KB):
                dwb_ref[k:k + 1, :] = jnp.sum(dwb_acc[k], axis=0, keepdims=True)

    rev = lambda i: n_tiles - 1 - i
    tile = pl.BlockSpec((TM, D), lambda i: (rev(i), 0))
    zcol = lambda k: pl.BlockSpec((TM, D), lambda i: (rev(i), k))
    return pl.pallas_call(
        body, name="bwd_mix2", grid=(n_tiles,),
        in_specs=[tile, tile, pl.BlockSpec((TM, 2 * D), lambda i: (rev(i), 0)), zcol(3), zcol(4),
                  pl.BlockSpec((TM, 3 * D), lambda i: (rev(i), 0)), tile, tile, _full(mod.shape), _full((KA, D)), _full((KB, D)), ANY],
        out_specs=[pl.BlockSpec((TM, N), lambda i: (rev(i), 0)), tile, _full((KA, D)), _full((KB, D)), _full((8, D)), _full((1, N))],
        out_shape=[jax.ShapeDtypeStruct((S, N), BF16), jax.ShapeDtypeStruct((S, D), F32), jax.ShapeDtypeStruct((KA, D), F32),
                   jax.ShapeDtypeStruct((KB, D), F32), jax.ShapeDtypeStruct((8, D), F32), jax.ShapeDtypeStruct((1, N), F32)],
        scratch_shapes=[pltpu.VMEM((TM + HALO_A, D), F32), pltpu.VMEM((TM + HALO_B, D), F32), pltpu.VMEM((TM, D), F32),
                        pltpu.VMEM((TM, D), F32), pltpu.VMEM((TM, D), F32), pltpu.VMEM((KA, 8, D), F32), pltpu.VMEM((KB, 8, D), F32),
                        pltpu.VMEM((D, N), BF16), pltpu.SemaphoreType.DMA((nb,))],
        compiler_params=_cparams(1),
    )(du1, dq, z, z, z, dzc, x, dxp, mod, wa, wb, w_in_g)


PEER_REDUCED = ("w_up", "w_down", "w_o", "w_a_out", "w_b_out", "conv_a_w", "conv_b_w")


def _local_grads(x, c, tgt, w_ada_g, w_in_g, shards, vecs):
    D = x.shape[1]
    rows = lambda a: a.reshape(a.shape[0] * a.shape[1], a.shape[2])
    cols = lambda a: jnp.transpose(a, (1, 0, 2)).reshape(a.shape[1], a.shape[0] * a.shape[2])
    mod = _ada_fwd(c, w_ada_g, vecs["b_ada"])
    z, h1, gathered = _fwd_in(x, mod, w_in_g, vecs["b_in"], shards)
    w_a_out, w_b_out, w_o, w_up_g, w_down = rows(gathered[0]), rows(gathered[1]), rows(gathered[2]), gathered[3], rows(gathered[4])
    wa, wb = cols(gathered[5]), cols(gathered[6])
    u1, ya, yb, q, out, r1, u2, v, mg = _fwd_mix(
        z, x, mod, wa, vecs["conv_a_b"], vecs["ln_a_g"], vecs["ln_a_b"], w_a_out, vecs["b_a_out"], wb, w_b_out, w_o, vecs["b_o"])
    dx1, h2, f, dhu, do2, acc_f, acc_up = _ffn(
        r1, tgt, mod, vecs["ln1_g"], vecs["ln1_b"], w_up_g, vecs["b_up"], w_down, vecs["b_down"], vecs["ln2_g"], vecs["ln2_b"])
    g_up, g_up16 = _dw(h2, dhu, "dw_up", nsplit=2, wire=True)
    g_down, g_down16 = _dw(f, do2, "dw_down", msplit=2, wire=True)
    dxp, du1, dq, dzc, doutb, dyab, dybb, acc_1 = _bwd_mix1(
        dx1, r1, out, ya, yb, q, u1, z, mod, vecs["ln1_g"], vecs["ln_a_g"], vecs["ln_a_b"], w_o, w_a_out, w_b_out)
    g_o, g_o16 = _dw(mg, doutb, "dw_o", wire=True)
    g_a_out, g_a_out16 = _dw(u2, dyab, "dw_a_out", wire=True)
    g_b_out, g_b_out16 = _dw(v, dybb, "dw_b_out", wire=True)
    dz, gx, g_wa, g_wb, acc_2, db_in = _bwd_mix2(du1, dq, z, dzc, x, dxp, mod, wa, wb, w_in_g)
    full = dict(zip(PEER_REDUCED, (g_up, g_down, g_o, g_a_out, g_b_out, g_wa, g_wb)))
    res = _dw(h1, dz, "dw_in", nsplit=4,
              exchange=((g_up16, g_down16, g_o16, g_a_out16, g_b_out16, g_wa, g_wb), [SHARD_KIND[n] for n in PEER_REDUCED]))
    g_in = res[0]
    peers = {n: (full[n], r) for n, r in zip(PEER_REDUCED, res[1:])}
    row = lambda acc, k: (acc, k, 0, D)
    small = _pack_rows(
        [row(acc_2, 1), row(acc_2, 0), row(acc_1, 2), row(acc_f, 5), row(acc_f, 4), row(acc_f, 3),
         (db_in, 0, 0, 2 * D), row(acc_1, 8), (db_in, 0, 3 * D, 2 * D), row(acc_1, 9), row(acc_1, 10),
         row(acc_1, 7), row(acc_1, 5), row(acc_1, 6), row(acc_1, 4), row(acc_1, 3), row(acc_1, 0), row(acc_1, 1),
         (acc_up, 0, 0, acc_up.shape[1]), row(acc_f, 0), row(acc_f, 1), row(acc_f, 2), (c, 0, 0, D)],
        "pack_vectors")
    return acc_f[6, 0], gx, g_in, peers, small


WEIGHTS = ("w_ada", "b_ada", "w_in", "b_in", "conv_a_w", "conv_a_b", "ln_a_g", "ln_a_b", "w_a_out", "b_a_out", "conv_b_w",
           "w_b_out", "w_o", "b_o", "ln1_g", "ln1_b", "w_up", "b_up", "w_down", "b_down", "ln2_g", "ln2_b")
VECTORS = ("b_ada", "b_in", "conv_a_b", "ln_a_g", "ln_a_b", "b_a_out", "b_o", "ln1_g", "ln1_b", "b_up", "b_down", "ln2_g", "ln2_b")
SHARD_KIND = {"w_in": "col", "w_a_out": "row", "w_b_out": "row", "w_o": "row", "w_up": "col", "w_down": "row",
              "conv_a_w": "col", "conv_b_w": "col"}


def kernel(x, c, w_ada, b_ada, w_in, b_in, conv_a_w, conv_a_b, ln_a_g, ln_a_b, w_a_out, b_a_out, conv_b_w, w_b_out, w_o, b_o, ln1_g, ln1_b, w_up, b_up, w_down, b_down, ln2_g, ln2_b, loss_target, m_w_ada, m_b_ada, m_w_in, m_b_in, m_conv_a_w, m_conv_a_b, m_ln_a_g, m_ln_a_b, m_w_a_out, m_b_a_out, m_conv_b_w, m_w_b_out, m_w_o, m_b_o, m_ln1_g, m_ln1_b, m_w_up, m_b_up, m_w_down, m_b_down, m_ln2_g, m_ln2_b, v_w_ada, v_b_ada, v_w_in, v_b_in, v_conv_a_w, v_conv_a_b, v_ln_a_g, v_ln_a_b, v_w_a_out, v_b_a_out, v_conv_b_w, v_w_b_out, v_w_o, v_b_o, v_ln1_g, v_ln1_b, v_w_up, v_b_up, v_w_down, v_b_down, v_ln2_g, v_ln2_b):
    args = dict(locals())
    w = {n: args[n][0] for n in WEIGHTS}
    m = {n: args["m_" + n][0] for n in WEIGHTS}
    v = {n: args["v_" + n][0] for n in WEIGHTS}
    w = {n: (a[None, :] if a.ndim == 1 else a) for n, a in w.items()}
    m = {n: (a[None, :] if a.ndim == 1 else a) for n, a in m.items()}
    v = {n: (a[None, :] if a.ndim == 1 else a) for n, a in v.items()}
    xs, tgt = x[0], loss_target[0]
    S, D = xs.shape
    me = (4 * lax.axis_index("x") + 2 * lax.axis_index("y") + lax.axis_index("c")).astype(jnp.int32).reshape(1)
    core = lax.axis_index("c").astype(jnp.int32).reshape(1)
    chip = (2 * lax.axis_index("x") + lax.axis_index("y")).astype(jnp.int32).reshape(1)

    bf = lambda n: w[n].astype(BF16)
    w_ada_g, w_in_g = _all_gather([bf("w_ada"), bf("w_in")], "gather_weights")
    shards = [bf("w_a_out"), bf("w_b_out"), bf("w_o"), bf("w_up"), bf("w_down"), w["conv_a_w"], w["conv_b_w"]]
    vecs = {n: w[n] for n in VECTORS}

    loss, gx, g_in, peers, small = _local_grads(xs, c, tgt, w_ada_g, w_in_g, shards, vecs)

    (small_g,) = _all_gather([small], "gather_vectors")
    small_g = small_g.reshape(N_DEV, small.shape[1])
    (from_sibling,) = _exchange_cores([g_in], [SHARD_KIND["w_in"]], "reduce_cores")
    own_in, wire_in = _add_own(core, g_in, from_sibling, SHARD_KIND["w_in"], "add_w_in")
    (from_chips,) = _exchange_chips([wire_in], "reduce_chips")

    res = {"w_in": _sum_chips_adamw(chip, own_in, from_chips, w["w_in"], m["w_in"], v["w_in"], "adamw_w_in")}
    for n, (g_full, recv) in peers.items():
        res[n] = _sum_peers_adamw(me, g_full, recv, SHARD_KIND[n], w[n], m[n], v[n], "adamw_" + n)
    W = w["w_ada"].shape[1]
    n_vec = small.shape[1] - D
    g_ada = _ada_bwd(me, small_g, D, W, n_vec // D)
    res["w_ada"] = _sum_adamw(g_ada[None], w["w_ada"], m["w_ada"], v["w_ada"], "adamw_w_ada")
    by_kind = _adamw_vectors(small_g, [w[n] for n in VECTORS], [m[n] for n in VECTORS], [v[n] for n in VECTORS])
    for i, n in enumerate(VECTORS):
        res[n] = tuple(by_kind[t][i] for t in range(4))

    loss = lax.psum(loss, ("x", "y", "c"))
    outs = [loss, gx[None]]
    for t in range(4):
        outs += [res[n][t].reshape(args[n].shape) for n in WEIGHTS]
    return tuple(outs)
```

```python
import functools

import jax
import jax.numpy as jnp
from jax import lax
from jax.experimental import pallas as pl
from jax.experimental.pallas import tpu as pltpu

F32 = jnp.float32
BF16 = jnp.bfloat16
MESH = pl.DeviceIdType.MESH

N_DEV = 8
LN_EPS = 1e-5
DEPTH = 1
ALPHA = (2.0 * DEPTH) ** 0.25
ADAM_LR, ADAM_B1, ADAM_B2, ADAM_EPS, ADAM_WD, ADAM_STEP = 0.001, 0.9, 0.999, 1e-08, 0.01, 10

VMEM_LIMIT = 60 * 1024 * 1024
TOKEN_TILE = 256
DW_TILE = 1024
HALO_A = 32
HALO_B = 8
CONV_ROWS, CONV_LANES = 32, 512


def _cparams(n_grid):
    return pltpu.CompilerParams(dimension_semantics=("arbitrary",) * n_grid, vmem_limit_bytes=VMEM_LIMIT)


def _full(shape):
    return pl.BlockSpec(shape, lambda *_: (0,) * len(shape))


ANY = pl.BlockSpec(memory_space=pl.ANY)


def _ln(x):
    mu = jnp.mean(x, axis=-1, keepdims=True)
    xc = x - mu
    var = jnp.mean(xc * xc, axis=-1, keepdims=True)
    rstd = lax.rsqrt(var + LN_EPS)
    return xc * rstd, rstd


def _ln_bwd(dxhat, xhat, rstd):
    m1 = jnp.mean(dxhat, axis=-1, keepdims=True)
    m2 = jnp.mean(dxhat * xhat, axis=-1, keepdims=True)
    return rstd * (dxhat - m1 - xhat * m2)


def _sigmoid(x):
    return 0.5 * jnp.tanh(0.5 * x) + 0.5


def _colsum(a):
    return jnp.sum(a, axis=0, keepdims=True)


def _dot(a, b):
    return jnp.dot(a, b, preferred_element_type=F32)


def _dot_nt(a, b):
    return lax.dot_general(a, b, (((1,), (1,)), ((), ())), preferred_element_type=F32)


def _dot_tn(a, b):
    return lax.dot_general(a, b, (((0,), (0,)), ((), ())), preferred_element_type=F32)


def _load_cols(src_hbm, dst_vmem, sems):
    nblk, _, w = src_hbm.shape
    cps = [pltpu.make_async_copy(src_hbm.at[j], dst_vmem.at[:, pl.ds(j * w, w)], sems.at[j]) for j in range(nblk)]
    for cp in cps:
        cp.start()
    for cp in cps:
        cp.wait()


def _load_whole(pairs, sems):
    cps = [pltpu.make_async_copy(s, d, sems.at[k]) for k, (s, d) in enumerate(pairs)]
    for cp in cps:
        cp.start()
    for cp in cps:
        cp.wait()


def _mesh_pos():
    return lax.axis_index("x"), lax.axis_index("y"), lax.axis_index("c")


def _all_gather(arrs, name):
    n = len(arrs)

    def body(*refs):
        ins, outs = refs[:n], refs[n:2 * n]
        send_sems, recv_sems, local_sems = refs[2 * n:]
        x, y, c = _mesh_pos()
        me, sibling = (x, y, c), (x, y, 1 - c)
        chips = [(1 - x, y), (x, 1 - y), (1 - x, 1 - y)]

        def slot(a, px, py, pc):
            return outs[a].at[4 * px + 2 * py + pc]

        def copy(a, k, block, to, src=None):
            return pltpu.make_async_remote_copy(
                src_ref=slot(a, *block) if src is None else src, dst_ref=slot(a, *block),
                send_sem=send_sems.at[a, k], recv_sem=recv_sems.at[a, k], device_id=to, device_id_type=MESH)

        mine = [pltpu.make_async_copy(ins[a], slot(a, *me), local_sems.at[a]) for a in range(n)]
        for cp in mine:
            cp.start()
        first = []
        for a in range(n):
            first.append(copy(a, 0, me, sibling, src=ins[a]))
            first += [copy(a, 1 + j, me, (*chip, c), src=ins[a]) for j, chip in enumerate(chips)]
        for cp in first:
            cp.start()
        passed = []
        for a in range(n):
            for j, chip in enumerate(chips):
                copy(a, 1 + j, (*chip, c), me).wait_recv()
                fwd = copy(a, 4 + j, (*chip, c), sibling)
                fwd.start()
                passed.append(fwd)
        for a in range(n):
            copy(a, 0, sibling, me).wait_recv()
            for j, chip in enumerate(chips):
                copy(a, 4 + j, (*chip, 1 - c), me).wait_recv()
        for cp in first + passed:
            cp.wait_send()
        for cp in mine:
            cp.wait()

    outs = pl.pallas_call(
        body, name=name,
        out_shape=[jax.ShapeDtypeStruct((N_DEV,) + a.shape, a.dtype) for a in arrs],
        in_specs=[ANY] * n, out_specs=[ANY] * n,
        scratch_shapes=[pltpu.SemaphoreType.DMA((n, 7)), pltpu.SemaphoreType.DMA((n, 7)), pltpu.SemaphoreType.DMA((n,))],
    )(*arrs)
    return list(outs)


def _peer_copies(src_of, dsts, send_sems, recv_sems):
    x, y, c = _mesh_pos()
    me = 4 * x + 2 * y + c
    sends, arrivals = [], []
    for a in range(len(dsts)):
        for k in range(1, N_DEV):
            px, py, pc = (1 - x if k & 4 else x), (1 - y if k & 2 else y), (1 - c if k & 1 else c)
            p = 4 * px + 2 * py + pc
            common = dict(send_sem=send_sems.at[a, k - 1], recv_sem=recv_sems.at[a, k - 1],
                          device_id=(px, py, pc), device_id_type=MESH)
            sends.append(pltpu.make_async_remote_copy(src_ref=src_of(a, p), dst_ref=dsts[a].at[me], **common))
            arrivals.append(pltpu.make_async_remote_copy(src_ref=dsts[a].at[p], dst_ref=dsts[a].at[p], **common))
    return sends, arrivals


def _peer_sems(n):
    return [pltpu.SemaphoreType.DMA((n, N_DEV - 1)), pltpu.SemaphoreType.DMA((n, N_DEV - 1))]


def _shard_of(ref, kind, j):
    if kind == "col":
        w = ref.shape[1] // N_DEV
        return ref.at[:, pl.ds(j * w, w)]
    h = ref.shape[0] // N_DEV
    return ref.at[pl.ds(j * h, h), :]


def _shard_shape(shape, kind):
    return (shape[0], shape[1] // N_DEV) if kind == "col" else (shape[0] // N_DEV, shape[1])


def _exchange_cores(grads, kinds, name):
    n = len(grads)

    def body(*refs):
        ins, outs = refs[:n], refs[n:2 * n]
        send_sems, recv_sems = refs[2 * n:]
        x, y, c = _mesh_pos()
        sibling = (x, y, 1 - c)
        sends = []
        for a in range(n):
            for s in range(4):
                sends.append(pltpu.make_async_remote_copy(
                    src_ref=_shard_of(ins[a], kinds[a], 2 * s + (1 - c)), dst_ref=outs[a].at[s],
                    send_sem=send_sems.at[a, s], recv_sem=recv_sems.at[a, s], device_id=sibling, device_id_type=MESH))
        for cp in sends:
            cp.start()
        for a in range(n):
            for s in range(4):
                pltpu.make_async_remote_copy(
                    src_ref=outs[a].at[s], dst_ref=outs[a].at[s],
                    send_sem=send_sems.at[a, s], recv_sem=recv_sems.at[a, s], device_id=sibling, device_id_type=MESH).wait_recv()
        for cp in sends:
            cp.wait_send()

    outs = pl.pallas_call(
        body, name=name,
        out_shape=[jax.ShapeDtypeStruct((4,) + _shard_shape(g.shape, k), F32) for g, k in zip(grads, kinds)],
        in_specs=[ANY] * n, out_specs=[ANY] * n,
        scratch_shapes=[pltpu.SemaphoreType.DMA((n, 4)), pltpu.SemaphoreType.DMA((n, 4))],
    )(*grads)
    return list(outs)


def _exchange_chips(parts, name):
    n = len(parts)

    def body(*refs):
        ins, outs = refs[:n], refs[n:2 * n]
        send_sems, recv_sems = refs[2 * n:]
        x, y, c = _mesh_pos()
        my_slot = 2 * x + y
        chips = [(1 - x, y), (x, 1 - y), (1 - x, 1 - y)]
        sends = []
        for a in range(n):
            for j, (px, py) in enumerate(chips):
                sends.append(pltpu.make_async_remote_copy(
                    src_ref=ins[a].at[2 * px + py], dst_ref=outs[a].at[my_slot],
                    send_sem=send_sems.at[a, j], recv_sem=recv_sems.at[a, j], device_id=(px, py, c), device_id_type=MESH))
        for cp in sends:
            cp.start()
        for a in range(n):
            for j, (px, py) in enumerate(chips):
                pltpu.make_async_remote_copy(
                    src_ref=outs[a].at[2 * px + py], dst_ref=outs[a].at[2 * px + py],
                    send_sem=send_sems.at[a, j], recv_sem=recv_sems.at[a, j], device_id=(px, py, c), device_id_type=MESH).wait_recv()
        for cp in sends:
            cp.wait_send()

    outs = pl.pallas_call(
        body, name=name,
        out_shape=[jax.ShapeDtypeStruct(p.shape, p.dtype) for p in parts],
        in_specs=[ANY] * n, out_specs=[ANY] * n,
        scratch_shapes=[pltpu.SemaphoreType.DMA((n, 3)), pltpu.SemaphoreType.DMA((n, 3))],
    )(*parts)
    return list(outs)


def _row_tile(rows):
    for t in (256, 128, 64, 32, 16, 8):
        if rows % t == 0:
            return t
    return rows


def _wire_dtype(rows):
    return BF16 if rows % 16 == 0 else F32


def _add_own(core, g, recv, kind, name):
    ns, R, C = recv.shape
    tr = _row_tile(R)
    nr = R // tr
    if kind == "col":
        g_spec = pl.BlockSpec((tr, C), lambda s, r, c_ref: (r, 2 * s + c_ref[0]))
    else:
        g_spec = pl.BlockSpec((tr, C), lambda s, r, c_ref: ((2 * s + c_ref[0]) * nr + r, 0))
    slab = pl.BlockSpec((None, tr, C), lambda s, r, c_ref: (s, r, 0))

    def body(c_ref, g_ref, r_ref, o32_ref, o16_ref):
        t = g_ref[...] + r_ref[...]
        o32_ref[...] = t
        o16_ref[...] = t.astype(o16_ref.dtype)

    return pl.pallas_call(
        body, name=name,
        grid_spec=pltpu.PrefetchScalarGridSpec(num_scalar_prefetch=1, grid=(ns, nr), in_specs=[g_spec, slab], out_specs=[slab, slab]),
        out_shape=[jax.ShapeDtypeStruct((ns, R, C), F32), jax.ShapeDtypeStruct((ns, R, C), _wire_dtype(R))],
        compiler_params=_cparams(2),
    )(core, g, recv)


def _adamw_math(w, g, m, v):
    m2 = ADAM_B1 * m + (1.0 - ADAM_B1) * g
    v2 = ADAM_B2 * v + (1.0 - ADAM_B2) * (g * g)
    m_hat = m2 / (1.0 - ADAM_B1 ** ADAM_STEP)
    v_hat = v2 / (1.0 - ADAM_B2 ** ADAM_STEP)
    delta = -ADAM_LR * (m_hat / (jnp.sqrt(v_hat) + ADAM_EPS) + ADAM_WD * w)
    return delta, m2, v2


def _sum_adamw(parts, w, m, v, name):
    n, R, C = parts.shape
    tr = _row_tile(R)

    def body(p_ref, w_ref, m_ref, v_ref, g_ref, d_ref, m_out, v_out):
        g = p_ref[0]
        for k in range(1, n):
            g = g + p_ref[k]
        g_ref[...] = g
        d_ref[...], m_out[...], v_out[...] = _adamw_math(w_ref[...], g, m_ref[...], v_ref[...])

    blk = pl.BlockSpec((tr, C), lambda r: (r, 0))
    return pl.pallas_call(
        body, name=name, grid=(R // tr,),
        in_specs=[pl.BlockSpec((n, tr, C), lambda r: (0, r, 0)), blk, blk, blk],
        out_specs=[blk] * 4, out_shape=[jax.ShapeDtypeStruct((R, C), F32)] * 4, compiler_params=_cparams(1),
    )(parts, w, m, v)


def _sum_chips_adamw(chip, own, recv, w, m, v, name):
    n, R, C = recv.shape
    tr = _row_tile(R)

    def body(chip_ref, own_ref, r_ref, w_ref, m_ref, v_ref, g_ref, d_ref, m_out, v_out):
        g = None
        for k in range(n):
            term = jnp.where(chip_ref[0] == k, own_ref[...], r_ref[k].astype(F32))
            g = term if g is None else g + term
        g_ref[...] = g
        d_ref[...], m_out[...], v_out[...] = _adamw_math(w_ref[...], g, m_ref[...], v_ref[...])

    blk = pl.BlockSpec((tr, C), lambda r, chip_ref: (r, 0))
    return pl.pallas_call(
        body, name=name,
        grid_spec=pltpu.PrefetchScalarGridSpec(
            num_scalar_prefetch=1, grid=(R // tr,),
            in_specs=[pl.BlockSpec((None, tr, C), lambda r, chip_ref: (chip_ref[0], r, 0)),
                      pl.BlockSpec((n, tr, C), lambda r, chip_ref: (0, r, 0)), blk, blk, blk],
            out_specs=[blk] * 4),
        out_shape=[jax.ShapeDtypeStruct((R, C), F32)] * 4, compiler_params=_cparams(1),
    )(chip, own, recv, w, m, v)


def _pack_rows(pieces, name):
    arrs = []
    for a, _, _, _ in pieces:
        if not any(a is b for b in arrs):
            arrs.append(a)
    which = [next(i for i, b in enumerate(arrs) if b is a) for a, _, _, _ in pieces]
    total = sum(n for _, _, _, n in pieces)

    def body(*refs):
        o_ref = refs[len(arrs)]
        off = 0
        for (_, r, c0, n), i in zip(pieces, which):
            o_ref[:, off:off + n] = refs[i][r:r + 1, c0:c0 + n]
            off += n

    return pl.pallas_call(
        body, name=name, in_specs=[_full(a.shape) for a in arrs], out_specs=_full((1, total)),
        out_shape=jax.ShapeDtypeStruct((1, total), F32), grid=(1,), compiler_params=_cparams(1),
    )(*arrs)


def _adamw_vectors(parts, ws, ms, vs):
    nv = len(ws)
    widths = [a.shape[1] for a in ws]

    def body(*refs):
        p_ref = refs[0]
        w_refs, m_refs, v_refs = refs[1:1 + nv], refs[1 + nv:1 + 2 * nv], refs[1 + 2 * nv:1 + 3 * nv]
        outs = refs[1 + 3 * nv:]
        off = 0
        for i, n in enumerate(widths):
            g = p_ref[0:1, off:off + n]
            for k in range(1, p_ref.shape[0]):
                g = g + p_ref[k:k + 1, off:off + n]
            outs[i][...] = g
            outs[nv + i][...], outs[2 * nv + i][...], outs[3 * nv + i][...] = _adamw_math(w_refs[i][...], g, m_refs[i][...], v_refs[i][...])
            off += n

    vec_specs = [_full((1, n)) for n in widths]
    res = pl.pallas_call(
        body, name="adamw_vectors", grid=(1,),
        in_specs=[_full(parts.shape)] + vec_specs * 3, out_specs=vec_specs * 4,
        out_shape=[jax.ShapeDtypeStruct((1, n), F32) for n in widths] * 4, compiler_params=_cparams(1),
    )(parts, *ws, *ms, *vs)
    return [res[t * nv:(t + 1) * nv] for t in range(4)]


def _sum_peers_adamw(me, g, recv, kind, w, m, v, name):
    n, R, C = recv.shape
    tr = _row_tile(R)
    nr = R // tr
    if kind == "col":
        g_spec = pl.BlockSpec((tr, C), lambda r, me_ref: (r, me_ref[0]))
    else:
        g_spec = pl.BlockSpec((tr, C), lambda r, me_ref: (me_ref[0] * nr + r, 0))

    def body(me_ref, own_ref, r_ref, w_ref, m_ref, v_ref, g_ref, d_ref, m_out, v_out):
        acc = None
        for k in range(n):
            term = jnp.where(me_ref[0] == k, own_ref[...], r_ref[k].astype(F32))
            acc = term if acc is None else acc + term
        g_ref[...] = acc
        d_ref[...], m_out[...], v_out[...] = _adamw_math(w_ref[...], acc, m_ref[...], v_ref[...])

    blk = pl.BlockSpec((tr, C), lambda r, me_ref: (r, 0))
    return pl.pallas_call(
        body, name=name,
        grid_spec=pltpu.PrefetchScalarGridSpec(
            num_scalar_prefetch=1, grid=(nr,),
            in_specs=[g_spec, pl.BlockSpec((n, tr, C), lambda r, me_ref: (0, r, 0)), blk, blk, blk], out_specs=[blk] * 4),
        out_shape=[jax.ShapeDtypeStruct((R, C), F32)] * 4, compiler_params=_cparams(1),
    )(me, g, recv, w, m, v)


def _ada_cols(me, c_all, w_ada16, b_ada):
    nd, D = c_all.shape
    W = w_ada16.shape[1]

    def body(me_ref, c_ref, w_ref, b_ref, o_ref):
        cv = c_ref[...]
        r = _dot((cv * _sigmoid(cv)).astype(BF16), w_ref[...]) + b_ref[...]
        for b in range(nd):
            o_ref[b] = r[b:b + 1, :]

    return pl.pallas_call(
        body, name="ada_cols",
        grid_spec=pltpu.PrefetchScalarGridSpec(
            num_scalar_prefetch=1, grid=(1,),
            in_specs=[_full((nd, D)), _full((D, W)), pl.BlockSpec((1, W), lambda i, me_ref: (0, me_ref[0]))],
            out_specs=_full((nd, 1, W))),
        out_shape=jax.ShapeDtypeStruct((nd, 1, W), F32), compiler_params=_cparams(1),
    )(me, c_all, w_ada16, b_ada)


def _rows_to_owners(part, name):
    def body(p_ref, o_ref, send_sems, recv_sems, local_sem):
        x, y, c = _mesh_pos()
        me = 4 * x + 2 * y + c
        mine = pltpu.make_async_copy(p_ref.at[me], o_ref.at[me], local_sem.at[0])
        sends, arrivals = _peer_copies(lambda a, p: p_ref.at[p], [o_ref], send_sems, recv_sems)
        mine.start()
        for cp in sends:
            cp.start()
        for cp in arrivals:
            cp.wait_recv()
        for cp in sends:
            cp.wait_send()
        mine.wait()

    return pl.pallas_call(
        body, name=name, in_specs=[ANY], out_specs=ANY, out_shape=jax.ShapeDtypeStruct(part.shape, part.dtype),
        scratch_shapes=_peer_sems(1) + [pltpu.SemaphoreType.DMA((1,))],
    )(part)


def _ada_mod(me, c, w_ada16, b_ada):
    (c_all,) = _all_gather([c], "gather_c")
    part = _ada_cols(me, c_all.reshape(N_DEV, c.shape[1]), w_ada16, b_ada)
    return _rows_to_owners(part, "scatter_mod").reshape(1, N_DEV * w_ada16.shape[1])


def _ada_bwd(me, small_g, D, W, c_block):
    def body(me_ref, c_ref, dm_ref, o_ref):
        cv = c_ref[...]
        ca = (cv * _sigmoid(cv)).astype(BF16).astype(F32)
        dm = dm_ref[...].astype(BF16).astype(F32)
        o_ref[...] = lax.dot_general(ca, dm, (((0,), (0,)), ((), ())), precision=lax.Precision.HIGHEST,
                                     preferred_element_type=F32)

    return pl.pallas_call(
        body, name="ada_bwd",
        grid_spec=pltpu.PrefetchScalarGridSpec(
            num_scalar_prefetch=1, grid=(1,),
            in_specs=[pl.BlockSpec((N_DEV, D), lambda i, me_ref: (0, c_block)),
                      pl.BlockSpec((N_DEV, W), lambda i, me_ref: (0, me_ref[0]))],
            out_specs=pl.BlockSpec((D, W), lambda i, me_ref: (0, 0))),
        out_shape=jax.ShapeDtypeStruct((D, W), F32), compiler_params=_cparams(1),
    )(me, small_g, small_g)


def _fwd_in(x, mod, w_in_g, b_in, shards):
    S, D = x.shape
    nb, _, wb = w_in_g.shape
    N = nb * wb
    TM = TOKEN_TILE
    n_tiles = S // TM
    ns = len(shards)

    def body(*refs):
        x_ref, mod_ref, w_hbm, b_ref = refs[:4]
        ins = refs[4:4 + ns]
        z_ref, h_ref = refs[4 + ns:6 + ns]
        outs = refs[6 + ns:6 + 2 * ns]
        w_v, sems, send_sems, recv_sems, local_sems = refs[6 + 2 * ns:]
        i = pl.program_id(0)
        x_, y_, c_ = _mesh_pos()
        me = 4 * x_ + 2 * y_ + c_
        mine = [pltpu.make_async_copy(ins[a], outs[a].at[me], local_sems.at[a]) for a in range(ns)]
        sends, arrivals = _peer_copies(lambda a, p: ins[a], outs, send_sems, recv_sems)

        @pl.when(i == 0)
        def _():
            for cp in sends + mine:
                cp.start()
            _load_cols(w_hbm, w_v, sems)

        xhat, _ = _ln(x_ref[...])
        hb = (xhat * (1.0 + mod_ref[:, D:2 * D]) + mod_ref[:, 0:D]).astype(BF16)
        h_ref[...] = hb
        z_ref[...] = _dot(hb, w_v[...]) + b_ref[...]

        @pl.when(i == n_tiles - 1)
        def _():
            for cp in arrivals:
                cp.wait_recv()
            for cp in sends:
                cp.wait_send()
            for cp in mine:
                cp.wait()

    res = pl.pallas_call(
        body, name="fwd_in", grid=(n_tiles,),
        in_specs=[pl.BlockSpec((TM, D), lambda i: (i, 0)), _full(mod.shape), ANY, _full((1, N))] + [ANY] * ns,
        out_specs=[pl.BlockSpec((TM, N), lambda i: (i, 0)), pl.BlockSpec((TM, D), lambda i: (i, 0))] + [ANY] * ns,
        out_shape=[jax.ShapeDtypeStruct((S, N), F32), jax.ShapeDtypeStruct((S, D), BF16)]
        + [jax.ShapeDtypeStruct((N_DEV,) + a.shape, a.dtype) for a in shards],
        scratch_shapes=[pltpu.VMEM((D, N), BF16), pltpu.SemaphoreType.DMA((nb,))] + _peer_sems(ns) + [pltpu.SemaphoreType.DMA((ns,))],
        compiler_params=_cparams(1),
    )(x, mod, w_in_g, b_in, *shards)
    return res[0], res[1], list(res[2:])


def _rows_from(win, o, R):
    if o % 8 == 0:
        return win[o:o + R, :]
    return pltpu.roll(win, win.shape[0] - o, axis=0)[0:R, :]


def _causal_taps(buf, w_ref, o_ref, K, halo, TM, D):
    R, C = CONV_ROWS, CONV_LANES
    base = halo - (K - 1)

    for r0 in range(0, TM, R):
        for c0 in range(0, D, C):
            win = buf[r0:r0 + R + halo, c0:c0 + C]
            acc = jnp.zeros((R, C), F32)
            for k in range(K):
                acc = acc + w_ref[k:k + 1, c0:c0 + C] * _rows_from(win, base + k, R)
            o_ref[r0:r0 + R, c0:c0 + C] = acc


def _fwd_mix(z, x, mod, wa, ba, lag, lab, w_a_out, b_a_out, wb, w_b_out, w_o, b_o):
    S, D = x.shape
    N = z.shape[1]
    TM = TOKEN_TILE
    KA, KB = wa.shape[0], wb.shape[0]

    def body(z_ref, x_ref, mod_ref, wa_ref, ba_ref, lag_ref, lab_ref, wao_hbm, bao_ref, wb_ref, wbo_hbm, wo_hbm, bo_ref,
             u1_ref, ya_ref, yb_ref, q_ref, out_ref, r1_ref, u2_ref, v_ref, mg_ref,
             ubuf, pbuf, qbuf, wao, wbo, wo, sems):
        i = pl.program_id(0)

        @pl.when(i == 0)
        def _():
            _load_whole([(wao_hbm, wao), (wbo_hbm, wbo), (wo_hbm, wo)], sems)
            ubuf[0:HALO_A, :] = jnp.zeros((HALO_A, D), F32)
            pbuf[0:HALO_B, :] = jnp.zeros((HALO_B, D), F32)

        ubuf[HALO_A:HALO_A + TM, :] = z_ref[:, 0:D] * _sigmoid(z_ref[:, D:2 * D])
        _causal_taps(ubuf, wa_ref, u1_ref, KA, HALO_A, TM, D)
        ubuf[0:HALO_A, :] = ubuf[TM:TM + HALO_A, :]
        u1 = u1_ref[...] + ba_ref[...]
        u1_ref[...] = u1
        xa, _ = _ln(u1)
        l = xa * lag_ref[...] + lab_ref[...]
        u2 = (l * _sigmoid(l)).astype(BF16)
        u2_ref[...] = u2
        ya = _dot(u2, wao[...]) + bao_ref[...]
        ya_ref[...] = ya.astype(BF16)
        pbuf[HALO_B:HALO_B + TM, :] = z_ref[:, 3 * D:4 * D] * z_ref[:, 4 * D:5 * D]
        _causal_taps(pbuf, wb_ref, qbuf, KB, HALO_B, TM, D)
        pbuf[0:HALO_B, :] = pbuf[TM:TM + HALO_B, :]
        q_ref[...] = qbuf[...].astype(BF16)
        v = (z_ref[:, 2 * D:3 * D] * qbuf[...]).astype(BF16)
        v_ref[...] = v
        yb = _dot(v, wbo[...])
        yb_ref[...] = yb.astype(BF16)
        mg = (_sigmoid(z_ref[:, 5 * D:6 * D]) * ya + _sigmoid(z_ref[:, 6 * D:7 * D]) * yb).astype(BF16)
        mg_ref[...] = mg
        out = _dot(mg, wo[...]) + bo_ref[...]
        out_ref[...] = out.astype(BF16)
        r1_ref[...] = ALPHA * x_ref[...] + (1.0 + mod_ref[:, 2 * D:3 * D]) * out

    tile = pl.BlockSpec((TM, D), lambda i: (i, 0))
    vec = _full((1, D))
    return pl.pallas_call(
        body, name="fwd_mix", grid=(S // TM,),
        in_specs=[pl.BlockSpec((TM, N), lambda i: (i, 0)), tile, _full(mod.shape), _full((KA, D)), vec, vec, vec, ANY, vec,
                  _full((KB, D)), ANY, ANY, vec],
        out_specs=[tile] * 9,
        out_shape=[jax.ShapeDtypeStruct((S, D), dt) for dt in (F32, BF16, BF16, BF16, BF16, F32, BF16, BF16, BF16)],
        scratch_shapes=[pltpu.VMEM((TM + HALO_A, D), F32), pltpu.VMEM((TM + HALO_B, D), F32), pltpu.VMEM((TM, D), F32),
                        pltpu.VMEM((D, D), BF16), pltpu.VMEM((D, D), BF16), pltpu.VMEM((D, D), BF16),
                        pltpu.SemaphoreType.DMA((3,))],
        compiler_params=_cparams(1),
    )(z, x, mod, wa, ba, lag, lab, w_a_out, b_a_out, wb, w_b_out, w_o, b_o)


FFN_ROWS = ("b_down", "ln2_g", "ln2_b", "gate2", "scale2", "shift2", "loss")


def _ffn(r1, tgt, mod, ln1_g, ln1_b, w_up_g, b_up, w_down, b_down, ln2_g, ln2_b):
    S, D = r1.shape
    nb, _, wb = w_up_g.shape
    FF = nb * wb
    TM = TOKEN_TILE
    CF = 1024
    n_tiles = S // TM

    def body(r1_ref, t_ref, mod_ref, g1_ref, b1_ref, wup_hbm, bup_ref, wdn_hbm, bdn_ref, g2_ref, b2_ref,
             dx1_ref, h2_ref, f_ref, dhu_ref, do2_ref, acc_ref, accup_ref,
             wup, wdn, relu_buf, sems, sem2):
        i = pl.program_id(0)

        @pl.when(i == 0)
        def _():
            _load_cols(wup_hbm, wup, sems)
            _load_whole([(wdn_hbm, wdn)], sem2)
            acc_ref[...] = jnp.zeros(acc_ref.shape, F32)
            accup_ref[...] = jnp.zeros(accup_ref.shape, F32)

        scale2, shift2, gate2 = mod_ref[:, 4 * D:5 * D], mod_ref[:, 3 * D:4 * D], mod_ref[:, 5 * D:6 * D]
        xhat1, _ = _ln(r1_ref[...])
        x1 = xhat1 * g1_ref[...] + b1_ref[...]
        xh0, rstd0 = _ln(x1)
        h2 = (xh0 * (1.0 + scale2) + shift2).astype(BF16)
        h2_ref[...] = h2
        out2 = jnp.zeros((TM, D), F32) + bdn_ref[...]
        for c0 in range(0, FF, CF):
            hu = _dot(h2, wup[:, c0:c0 + CF]) + bup_ref[:, c0:c0 + CF]
            rl = jnp.maximum(hu, 0.0)
            relu_buf[:, c0:c0 + CF] = rl
            fb = (rl * rl).astype(BF16)
            f_ref[:, c0:c0 + CF] = fb
            out2 = out2 + _dot(fb, wdn[c0:c0 + CF, :])
        r2 = ALPHA * x1 + (1.0 + gate2) * out2
        xh2, rstd2 = _ln(r2)
        e = xh2 * g2_ref[...] + b2_ref[...] - t_ref[...]
        acc_ref[6:7, :] += _colsum(e * e)
        dy = e * (1.0 / D)
        acc_ref[1:2, :] += _colsum(dy * xh2)
        acc_ref[2:3, :] += _colsum(dy)
        dr2 = _ln_bwd(dy * g2_ref[...], xh2, rstd2)
        acc_ref[3:4, :] += _colsum(dr2 * out2)
        do2 = (1.0 + gate2) * dr2
        acc_ref[0:1, :] += _colsum(do2)
        do2b = do2.astype(BF16)
        do2_ref[...] = do2b
        dh2 = jnp.zeros((TM, D), F32)
        for c0 in range(0, FF, CF):
            dhu = _dot_nt(do2b, wdn[c0:c0 + CF, :]) * (2.0 * relu_buf[:, c0:c0 + CF])
            accup_ref[:, c0:c0 + CF] += _colsum(dhu)
            dhub = dhu.astype(BF16)
            dhu_ref[:, c0:c0 + CF] = dhub
            dh2 = dh2 + _dot_nt(dhub, wup[:, c0:c0 + CF])
        acc_ref[4:5, :] += _colsum(dh2 * xh0)
        acc_ref[5:6, :] += _colsum(dh2)
        dx1_ref[...] = ALPHA * dr2 + _ln_bwd(dh2 * (1.0 + scale2), xh0, rstd0)

        @pl.when(i == n_tiles - 1)
        def _():
            tot = jnp.sum(acc_ref[6:7, :], axis=-1, keepdims=True) * (0.5 / D)
            acc_ref[6:7, :] = jnp.broadcast_to(tot, (1, D))

    tile = pl.BlockSpec((TM, D), lambda i: (i, 0))
    wide = pl.BlockSpec((TM, FF), lambda i: (i, 0))
    vec = _full((1, D))
    return pl.pallas_call(
        body, name="ffn", grid=(n_tiles,),
        in_specs=[tile, tile, _full(mod.shape), vec, vec, ANY, _full((1, FF)), ANY, vec, vec, vec],
        out_specs=[tile, tile, wide, wide, tile, _full((8, D)), _full((1, FF))],
        out_shape=[jax.ShapeDtypeStruct((S, D), F32), jax.ShapeDtypeStruct((S, D), BF16), jax.ShapeDtypeStruct((S, FF), BF16),
                   jax.ShapeDtypeStruct((S, FF), BF16), jax.ShapeDtypeStruct((S, D), BF16),
                   jax.ShapeDtypeStruct((8, D), F32), jax.ShapeDtypeStruct((1, FF), F32)],
        scratch_shapes=[pltpu.VMEM((D, FF), BF16), pltpu.VMEM((FF, D), BF16), pltpu.VMEM((TM, FF), F32),
                        pltpu.SemaphoreType.DMA((nb,)), pltpu.SemaphoreType.DMA((1,))],
        compiler_params=_cparams(1),
    )(r1, tgt, mod, ln1_g, ln1_b, w_up_g, b_up, w_down, b_down, ln2_g, ln2_b)


def _dw(a, b, name, msplit=1, nsplit=1, wire=False, exchange=None):
    S, M = a.shape
    N = b.shape[1]
    TK = min(DW_TILE, S)
    mb, nbk = M // msplit, N // nsplit
    nk = S // TK
    srcs, kinds = exchange if exchange else ((), ())
    ns = len(srcs)

    def body(*refs):
        a_ref, b_ref = refs[:2]
        ins = refs[2:2 + ns]
        o_ref = refs[2 + ns]
        o16_ref = refs[3 + ns] if wire else None
        rest = refs[3 + ns + (1 if wire else 0):]
        k = pl.program_id(2)
        if ns:
            outs, (send_sems, recv_sems) = rest[:ns], rest[ns:]
            sends, arrivals = _peer_copies(lambda q, p: _shard_of(ins[q], kinds[q], p), outs, send_sems, recv_sems)
            first = (pl.program_id(0) == 0) & (pl.program_id(1) == 0) & (k == 0)
            last = (pl.program_id(0) == msplit - 1) & (pl.program_id(1) == nsplit - 1) & (k == nk - 1)

            @pl.when(first)
            def _():
                for cp in sends:
                    cp.start()

        @pl.when(k == 0)
        def _():
            o_ref[...] = jnp.zeros(o_ref.shape, F32)

        o_ref[...] += _dot_tn(a_ref[...], b_ref[...])

        if wire:
            @pl.when(k == nk - 1)
            def _():
                o16_ref[...] = o_ref[...].astype(BF16)

        if ns:
            @pl.when(last)
            def _():
                for cp in arrivals:
                    cp.wait_recv()
                for cp in sends:
                    cp.wait_send()

    oblk = pl.BlockSpec((mb, nbk), lambda i, j, k: (i, j))
    res = pl.pallas_call(
        body, name=name, grid=(msplit, nsplit, nk),
        in_specs=[pl.BlockSpec((TK, mb), lambda i, j, k: (k, i)), pl.BlockSpec((TK, nbk), lambda i, j, k: (k, j))] + [ANY] * ns,
        out_specs=[oblk] + ([oblk] if wire else []) + [ANY] * ns,
        out_shape=[jax.ShapeDtypeStruct((M, N), F32)] + ([jax.ShapeDtypeStruct((M, N), BF16)] if wire else [])
        + [jax.ShapeDtypeStruct((N_DEV,) + _shard_shape(g.shape, kd), g.dtype) for g, kd in zip(srcs, kinds)],
        scratch_shapes=_peer_sems(ns) if ns else [],
        compiler_params=_cparams(3),
    )(a, b, *srcs)
    return res if (wire or ns) else res[0]


BWD1_ROWS = ("ln1_g", "ln1_b", "gate1", "b_o", "b_a_out", "ln_a_g", "ln_a_b", "conv_a_b", "b_in_gb", "b_in_ga", "b_in_gbb")


def _bwd_mix1(dx1, r1, out, ya, yb, q, u1, z, mod, ln1_g, lag, lab, w_o, w_a_out, w_b_out):
    S, D = dx1.shape
    TM = TOKEN_TILE

    def body(dx1_ref, r1_ref, out_ref, ya_ref, yb_ref, q_ref, u1_ref, zgb_ref, zga_ref, zgg_ref, mod_ref, g1_ref, lag_ref, lab_ref,
             wo_hbm, wao_hbm, wbo_hbm,
             dxp_ref, du1_ref, dq_ref, dzc_ref, dout_ref, dya_ref, dyb_ref, acc_ref,
             wo, wao, wbo, sems):
        @pl.when(pl.program_id(0) == 0)
        def _():
            _load_whole([(wo_hbm, wo), (wao_hbm, wao), (wbo_hbm, wbo)], sems)
            acc_ref[...] = jnp.zeros(acc_ref.shape, F32)

        dx1v = dx1_ref[...]
        xhat1, rstd1 = _ln(r1_ref[...])
        acc_ref[0:1, :] += _colsum(dx1v * xhat1)
        acc_ref[1:2, :] += _colsum(dx1v)
        dr1 = _ln_bwd(dx1v * g1_ref[...], xhat1, rstd1)
        dxp_ref[...] = (ALPHA * dr1).astype(BF16)
        acc_ref[2:3, :] += _colsum(dr1 * out_ref[...].astype(F32))
        dout = (1.0 + mod_ref[:, 2 * D:3 * D]) * dr1
        acc_ref[3:4, :] += _colsum(dout)
        doutb = dout.astype(BF16)
        dout_ref[...] = doutb
        dmg = _dot_nt(doutb, wo[...])
        sga, sgb = _sigmoid(zga_ref[...]), _sigmoid(zgg_ref[...])
        dga = dmg * ya_ref[...].astype(F32) * sga * (1.0 - sga)
        dgb = dmg * yb_ref[...].astype(F32) * sgb * (1.0 - sgb)
        acc_ref[9:10, :] += _colsum(dga)
        acc_ref[10:11, :] += _colsum(dgb)
        dzc_ref[:, D:2 * D] = dga.astype(BF16)
        dzc_ref[:, 2 * D:3 * D] = dgb.astype(BF16)
        dya = dmg * sga
        acc_ref[4:5, :] += _colsum(dya)
        dyab = dya.astype(BF16)
        dya_ref[...] = dyab
        dybb = (dmg * sgb).astype(BF16)
        dyb_ref[...] = dybb
        du2 = _dot_nt(dyab, wao[...])
        xa, rstda = _ln(u1_ref[...])
        l = xa * lag_ref[...] + lab_ref[...]
        sl = _sigmoid(l)
        dl = du2 * (sl * (1.0 + l * (1.0 - sl)))
        acc_ref[5:6, :] += _colsum(dl * xa)
        acc_ref[6:7, :] += _colsum(dl)
        du1 = _ln_bwd(dl * lag_ref[...], xa, rstda)
        acc_ref[7:8, :] += _colsum(du1)
        du1_ref[...] = du1.astype(BF16)
        dv = _dot_nt(dybb, wbo[...])
        dgbk = dv * q_ref[...].astype(F32)
        acc_ref[8:9, :] += _colsum(dgbk)
        dzc_ref[:, 0:D] = dgbk.astype(BF16)
        dq_ref[...] = (dv * zgb_ref[...]).astype(BF16)

    tile = pl.BlockSpec((TM, D), lambda i: (i, 0))
    vec = _full((1, D))
    zcol = lambda k: pl.BlockSpec((TM, D), lambda i: (i, k))
    return pl.pallas_call(
        body, name="bwd_mix1", grid=(S // TM,),
        in_specs=[tile] * 7 + [zcol(2), zcol(5), zcol(6), _full(mod.shape), vec, vec, vec, ANY, ANY, ANY],
        out_specs=[tile, tile, tile, pl.BlockSpec((TM, 3 * D), lambda i: (i, 0)), tile, tile, tile, _full((16, D))],
        out_shape=[jax.ShapeDtypeStruct((S, D), BF16)] * 3 + [jax.ShapeDtypeStruct((S, 3 * D), BF16)]
        + [jax.ShapeDtypeStruct((S, D), BF16)] * 3 + [jax.ShapeDtypeStruct((16, D), F32)],
        scratch_shapes=[pltpu.VMEM((D, D), BF16)] * 3 + [pltpu.SemaphoreType.DMA((3,))],
        compiler_params=_cparams(1),
    )(dx1, r1, out, ya, yb, q, u1, z, z, z, mod, ln1_g, lag, lab, w_o, w_a_out, w_b_out)


def _anticausal_taps(dbuf, u_ref, w_ref, o_ref, dwacc, K, halo, TM, D):
    R, C = CONV_ROWS, CONV_LANES

    for r0 in range(0, TM, R):
        for c0 in range(0, D, C):
            win = dbuf[r0:r0 + R + halo, c0:c0 + C]
            uc = u_ref[r0:r0 + R, c0:c0 + C]
            acc = jnp.zeros((R, C), F32)
            for j in range(K):
                sh = _rows_from(win, j, R)
                k = K - 1 - j
                acc = acc + w_ref[k:k + 1, c0:c0 + C] * sh
                pr = uc * sh
                part = pr[0:8, :]
                for s in range(8, R, 8):
                    part = part + pr[s:s + 8, :]
                dwacc[k, :, c0:c0 + C] += part
            o_ref[r0:r0 + R, c0:c0 + C] = acc


def _bwd_mix2(du1, dq, z, dzc, x, dxp, mod, wa, wb, w_in_g):
    S, D = x.shape
    N = z.shape[1]
    nb, _, wbk = w_in_g.shape
    TM = TOKEN_TILE
    KA, KB = wa.shape[0], wb.shape[0]
    n_tiles = S // TM

    def body(du1_ref, dq_ref, zav_ref, zc_ref, zx_ref, dzc_ref, x_ref, dxp_ref, mod_ref, wa_ref, wb_ref, win_hbm,
             dz_ref, gx_ref, dwa_ref, dwb_ref, acc_ref, dbin_ref,
             dbuf, qbuf, ubuf, pbuf, obuf, dwa_acc, dwb_acc, w_v, sems):
        i = pl.program_id(0)

        @pl.when(i == 0)
        def _():
            _load_cols(win_hbm, w_v, sems)
            dbuf[TM:TM + HALO_A, :] = jnp.zeros((HALO_A, D), F32)
            qbuf[TM:TM + HALO_B, :] = jnp.zeros((HALO_B, D), F32)
            dwa_acc[...] = jnp.zeros(dwa_acc.shape, F32)
            dwb_acc[...] = jnp.zeros(dwb_acc.shape, F32)
            acc_ref[...] = jnp.zeros(acc_ref.shape, F32)
            dbin_ref[...] = jnp.zeros(dbin_ref.shape, F32)

        a_val = zav_ref[:, 0:D]
        sa = _sigmoid(zav_ref[:, D:2 * D])
        ubuf[...] = a_val * sa
        dbuf[0:TM, :] = du1_ref[...].astype(F32)
        _anticausal_taps(dbuf, ubuf, wa_ref, obuf, dwa_acc, KA, HALO_A, TM, D)
        dbuf[TM:TM + HALO_A, :] = dbuf[0:HALO_A, :]
        du0 = obuf[...]
        dav = du0 * sa
        dag = du0 * a_val * sa * (1.0 - sa)
        dbin_ref[:, 0:D] += _colsum(dav)
        dbin_ref[:, D:2 * D] += _colsum(dag)
        dz_ref[:, 0:D] = dav.astype(BF16)
        dz_ref[:, D:2 * D] = dag.astype(BF16)
        pbuf[...] = zc_ref[...] * zx_ref[...]
        qbuf[0:TM, :] = dq_ref[...].astype(F32)
        _anticausal_taps(qbuf, pbuf, wb_ref, obuf, dwb_acc, KB, HALO_B, TM, D)
        qbuf[TM:TM + HALO_B, :] = qbuf[0:HALO_B, :]
        dp = obuf[...]
        dgc = dp * zx_ref[...]
        dgx = dp * zc_ref[...]
        dbin_ref[:, 3 * D:4 * D] += _colsum(dgc)
        dbin_ref[:, 4 * D:5 * D] += _colsum(dgx)
        dz_ref[:, 3 * D:4 * D] = dgc.astype(BF16)
        dz_ref[:, 4 * D:5 * D] = dgx.astype(BF16)
        dz_ref[:, 2 * D:3 * D] = dzc_ref[:, 0:D]
        dz_ref[:, 5 * D:7 * D] = dzc_ref[:, D:3 * D]
        dh = _dot_nt(dz_ref[...], w_v[...])
        xhat, rstd = _ln(x_ref[...])
        acc_ref[0:1, :] += _colsum(dh * xhat)
        acc_ref[1:2, :] += _colsum(dh)
        gx_ref[...] = dxp_ref[...].astype(F32) + _ln_bwd(dh * (1.0 + mod_ref[:, D:2 * D]), xhat, rstd)

        @pl.when(i == n_tiles - 1)
        def _():
            for k in range(KA):
                dwa_ref[k:k + 1, :] = jnp.sum(dwa_acc[k], axis=0, keepdims=True)
            for k in range(KB):
                dwb_ref[k:k + 1, :] = jnp.sum(dwb_acc[k], axis=0, keepdims=True)

    rev = lambda i: n_tiles - 1 - i
    tile = pl.BlockSpec((TM, D), lambda i: (rev(i), 0))
    zcol = lambda k: pl.BlockSpec((TM, D), lambda i: (rev(i), k))
    return pl.pallas_call(
        body, name="bwd_mix2", grid=(n_tiles,),
        in_specs=[tile, tile, pl.BlockSpec((TM, 2 * D), lambda i: (rev(i), 0)), zcol(3), zcol(4),
                  pl.BlockSpec((TM, 3 * D), lambda i: (rev(i), 0)), tile, tile, _full(mod.shape), _full((KA, D)), _full((KB, D)), ANY],
        out_specs=[pl.BlockSpec((TM, N), lambda i: (rev(i), 0)), tile, _full((KA, D)), _full((KB, D)), _full((8, D)), _full((1, N))],
        out_shape=[jax.ShapeDtypeStruct((S, N), BF16), jax.ShapeDtypeStruct((S, D), F32), jax.ShapeDtypeStruct((KA, D), F32),
                   jax.ShapeDtypeStruct((KB, D), F32), jax.ShapeDtypeStruct((8, D), F32), jax.ShapeDtypeStruct((1, N), F32)],
        scratch_shapes=[pltpu.VMEM((TM + HALO_A, D), F32), pltpu.VMEM((TM + HALO_B, D), F32), pltpu.VMEM((TM, D), F32),
                        pltpu.VMEM((TM, D), F32), pltpu.VMEM((TM, D), F32), pltpu.VMEM((KA, 8, D), F32), pltpu.VMEM((KB, 8, D), F32),
                        pltpu.VMEM((D, N), BF16), pltpu.SemaphoreType.DMA((nb,))],
        compiler_params=_cparams(1),
    )(du1, dq, z, z, z, dzc, x, dxp, mod, wa, wb, w_in_g)


PEER_REDUCED = ("w_up", "w_down", "w_o", "w_a_out", "w_b_out", "conv_a_w", "conv_b_w")


def _local_grads(me, x, c, tgt, w_ada16, w_in_g, shards, vecs):
    D = x.shape[1]
    rows = lambda a: a.reshape(a.shape[0] * a.shape[1], a.shape[2])
    cols = lambda a: jnp.transpose(a, (1, 0, 2)).reshape(a.shape[1], a.shape[0] * a.shape[2])
    mod = _ada_mod(me, c, w_ada16, vecs["b_ada"])
    z, h1, gathered = _fwd_in(x, mod, w_in_g, vecs["b_in"], shards)
    w_a_out, w_b_out, w_o, w_up_g, w_down = rows(gathered[0]), rows(gathered[1]), rows(gathered[2]), gathered[3], rows(gathered[4])
    wa, wb = cols(gathered[5]), cols(gathered[6])
    u1, ya, yb, q, out, r1, u2, v, mg = _fwd_mix(
        z, x, mod, wa, vecs["conv_a_b"], vecs["ln_a_g"], vecs["ln_a_b"], w_a_out, vecs["b_a_out"], wb, w_b_out, w_o, vecs["b_o"])
    dx1, h2, f, dhu, do2, acc_f, acc_up = _ffn(
        r1, tgt, mod, vecs["ln1_g"], vecs["ln1_b"], w_up_g, vecs["b_up"], w_down, vecs["b_down"], vecs["ln2_g"], vecs["ln2_b"])
    g_up, g_up16 = _dw(h2, dhu, "dw_up", nsplit=2, wire=True)
    g_down, g_down16 = _dw(f, do2, "dw_down", msplit=2, wire=True)
    dxp, du1, dq, dzc, doutb, dyab, dybb, acc_1 = _bwd_mix1(
        dx1, r1, out, ya, yb, q, u1, z, mod, vecs["ln1_g"], vecs["ln_a_g"], vecs["ln_a_b"], w_o, w_a_out, w_b_out)
    g_o, g_o16 = _dw(mg, doutb, "dw_o", wire=True)
    g_a_out, g_a_out16 = _dw(u2, dyab, "dw_a_out", wire=True)
    g_b_out, g_b_out16 = _dw(v, dybb, "dw_b_out", wire=True)
    dz, gx, g_wa, g_wb, acc_2, db_in = _bwd_mix2(du1, dq, z, dzc, x, dxp, mod, wa, wb, w_in_g)
    full = dict(zip(PEER_REDUCED, (g_up, g_down, g_o, g_a_out, g_b_out, g_wa, g_wb)))
    res = _dw(h1, dz, "dw_in", nsplit=4,
              exchange=((g_up16, g_down16, g_o16, g_a_out16, g_b_out16, g_wa, g_wb), [SHARD_KIND[n] for n in PEER_REDUCED]))
    g_in = res[0]
    peers = {n: (full[n], r) for n, r in zip(PEER_REDUCED, res[1:])}
    row = lambda acc, k: (acc, k, 0, D)
    small = _pack_rows(
        [row(acc_2, 1), row(acc_2, 0), row(acc_1, 2), row(acc_f, 5), row(acc_f, 4), row(acc_f, 3),
         (db_in, 0, 0, 2 * D), row(acc_1, 8), (db_in, 0, 3 * D, 2 * D), row(acc_1, 9), row(acc_1, 10),
         row(acc_1, 7), row(acc_1, 5), row(acc_1, 6), row(acc_1, 4), row(acc_1, 3), row(acc_1, 0), row(acc_1, 1),
         (acc_up, 0, 0, acc_up.shape[1]), row(acc_f, 0), row(acc_f, 1), row(acc_f, 2), (c, 0, 0, D)],
        "pack_vectors")
    return acc_f[6, 0], gx, g_in, peers, small


WEIGHTS = ("w_ada", "b_ada", "w_in", "b_in", "conv_a_w", "conv_a_b", "ln_a_g", "ln_a_b", "w_a_out", "b_a_out", "conv_b_w",
           "w_b_out", "w_o", "b_o", "ln1_g", "ln1_b", "w_up", "b_up", "w_down", "b_down", "ln2_g", "ln2_b")
VECTORS = ("b_ada", "b_in", "conv_a_b", "ln_a_g", "ln_a_b", "b_a_out", "b_o", "ln1_g", "ln1_b", "b_up", "b_down", "ln2_g", "ln2_b")
SHARD_KIND = {"w_in": "col", "w_a_out": "row", "w_b_out": "row", "w_o": "row", "w_up": "col", "w_down": "row",
              "conv_a_w": "col", "conv_b_w": "col"}


def kernel(x, c, w_ada, b_ada, w_in, b_in, conv_a_w, conv_a_b, ln_a_g, ln_a_b, w_a_out, b_a_out, conv_b_w, w_b_out, w_o, b_o, ln1_g, ln1_b, w_up, b_up, w_down, b_down, ln2_g, ln2_b, loss_target, m_w_ada, m_b_ada, m_w_in, m_b_in, m_conv_a_w, m_conv_a_b, m_ln_a_g, m_ln_a_b, m_w_a_out, m_b_a_out, m_conv_b_w, m_w_b_out, m_w_o, m_b_o, m_ln1_g, m_ln1_b, m_w_up, m_b_up, m_w_down, m_b_down, m_ln2_g, m_ln2_b, v_w_ada, v_b_ada, v_w_in, v_b_in, v_conv_a_w, v_conv_a_b, v_ln_a_g, v_ln_a_b, v_w_a_out, v_b_a_out, v_conv_b_w, v_w_b_out, v_w_o, v_b_o, v_ln1_g, v_ln1_b, v_w_up, v_b_up, v_w_down, v_b_down, v_ln2_g, v_ln2_b):
    args = dict(locals())
    w = {n: args[n][0] for n in WEIGHTS}
    m = {n: args["m_" + n][0] for n in WEIGHTS}
    v = {n: args["v_" + n][0] for n in WEIGHTS}
    w = {n: (a[None, :] if a.ndim == 1 else a) for n, a in w.items()}
    m = {n: (a[None, :] if a.ndim == 1 else a) for n, a in m.items()}
    v = {n: (a[None, :] if a.ndim == 1 else a) for n, a in v.items()}
    xs, tgt = x[0], loss_target[0]
    S, D = xs.shape
    me = (4 * lax.axis_index("x") + 2 * lax.axis_index("y") + lax.axis_index("c")).astype(jnp.int32).reshape(1)
    core = lax.axis_index("c").astype(jnp.int32).reshape(1)
    chip = (2 * lax.axis_index("x") + lax.axis_index("y")).astype(jnp.int32).reshape(1)

    bf = lambda n: w[n].astype(BF16)
    (w_in_g,) = _all_gather([bf("w_in")], "gather_weights")
    shards = [bf("w_a_out"), bf("w_b_out"), bf("w_o"), bf("w_up"), bf("w_down"), w["conv_a_w"], w["conv_b_w"]]
    vecs = {n: w[n] for n in VECTORS}

    loss, gx, g_in, peers, small = _local_grads(me, xs, c, tgt, bf("w_ada"), w_in_g, shards, vecs)

    (small_g,) = _all_gather([small], "gather_vectors")
    small_g = small_g.reshape(N_DEV, small.shape[1])
    (from_sibling,) = _exchange_cores([g_in], [SHARD_KIND["w_in"]], "reduce_cores")
    own_in, wire_in = _add_own(core, g_in, from_sibling, SHARD_KIND["w_in"], "add_w_in")
    (from_chips,) = _exchange_chips([wire_in], "reduce_chips")

    res = {"w_in": _sum_chips_adamw(chip, own_in, from_chips, w["w_in"], m["w_in"], v["w_in"], "adamw_w_in")}
    for n, (g_full, recv) in peers.items():
        res[n] = _sum_peers_adamw(me, g_full, recv, SHARD_KIND[n], w[n], m[n], v[n], "adamw_" + n)
    W = w["w_ada"].shape[1]
    n_vec = small.shape[1] - D
    g_ada = _ada_bwd(me, small_g, D, W, n_vec // D)
    res["w_ada"] = _sum_adamw(g_ada[None], w["w_ada"], m["w_ada"], v["w_ada"], "adamw_w_ada")
    by_kind = _adamw_vectors(small_g, [w[n] for n in VECTORS], [m[n] for n in VECTORS], [v[n] for n in VECTORS])
    for i, n in enumerate(VECTORS):
        res[n] = tuple(by_kind[t][i] for t in range(4))

    loss = lax.psum(loss, ("x", "y", "c"))
    outs = [loss, gx[None]]
    for t in range(4):
        outs += [res[n][t].reshape(args[n].shape) for n in WEIGHTS]
    return tuple(outs)
```

```python
import functools

import jax
import jax.numpy as jnp
from jax import lax
from jax.experimental import pallas as pl
from jax.experimental.pallas import tpu as pltpu

F32 = jnp.float32
BF16 = jnp.bfloat16
MESH = pl.DeviceIdType.MESH

N_DEV = 8
LN_EPS = 1e-5
DEPTH = 1
ALPHA = (2.0 * DEPTH) ** 0.25
ADAM_LR, ADAM_B1, ADAM_B2, ADAM_EPS, ADAM_WD, ADAM_STEP = 0.001, 0.9, 0.999, 1e-08, 0.01, 10

VMEM_LIMIT = 60 * 1024 * 1024
TOKEN_TILE = 256
DW_TILE = 1024
HALO_A = 32
HALO_B = 8
CONV_ROWS, CONV_LANES = 32, 512


def _cparams(n_grid):
    return pltpu.CompilerParams(dimension_semantics=("arbitrary",) * n_grid, vmem_limit_bytes=VMEM_LIMIT)


def _full(shape):
    return pl.BlockSpec(shape, lambda *_: (0,) * len(shape))


ANY = pl.BlockSpec(memory_space=pl.ANY)


def _ln(x):
    mu = jnp.mean(x, axis=-1, keepdims=True)
    xc = x - mu
    var = jnp.mean(xc * xc, axis=-1, keepdims=True)
    rstd = lax.rsqrt(var + LN_EPS)
    return xc * rstd, rstd


def _ln_bwd(dxhat, xhat, rstd):
    m1 = jnp.mean(dxhat, axis=-1, keepdims=True)
    m2 = jnp.mean(dxhat * xhat, axis=-1, keepdims=True)
    return rstd * (dxhat - m1 - xhat * m2)


def _sigmoid(x):
    return 0.5 * jnp.tanh(0.5 * x) + 0.5


def _colsum(a):
    return jnp.sum(a, axis=0, keepdims=True)


def _dot(a, b):
    return jnp.dot(a, b, preferred_element_type=F32)


def _dot_nt(a, b):
    return lax.dot_general(a, b, (((1,), (1,)), ((), ())), preferred_element_type=F32)


def _dot_tn(a, b):
    return lax.dot_general(a, b, (((0,), (0,)), ((), ())), preferred_element_type=F32)


def _load_cols(src_hbm, dst_vmem, sems):
    nblk, _, w = src_hbm.shape
    cps = [pltpu.make_async_copy(src_hbm.at[j], dst_vmem.at[:, pl.ds(j * w, w)], sems.at[j]) for j in range(nblk)]
    for cp in cps:
        cp.start()
    for cp in cps:
        cp.wait()


def _load_whole(pairs, sems):
    cps = [pltpu.make_async_copy(s, d, sems.at[k]) for k, (s, d) in enumerate(pairs)]
    for cp in cps:
        cp.start()
    for cp in cps:
        cp.wait()


def _mesh_pos():
    return lax.axis_index("x"), lax.axis_index("y"), lax.axis_index("c")


def _all_gather(arrs, name):
    n = len(arrs)

    def body(*refs):
        ins, outs = refs[:n], refs[n:2 * n]
        send_sems, recv_sems, local_sems = refs[2 * n:]
        x, y, c = _mesh_pos()
        me, sibling = (x, y, c), (x, y, 1 - c)
        chips = [(1 - x, y), (x, 1 - y), (1 - x, 1 - y)]

        def slot(a, px, py, pc):
            return outs[a].at[4 * px + 2 * py + pc]

        def copy(a, k, block, to, src=None):
            return pltpu.make_async_remote_copy(
                src_ref=slot(a, *block) if src is None else src, dst_ref=slot(a, *block),
                send_sem=send_sems.at[a, k], recv_sem=recv_sems.at[a, k], device_id=to, device_id_type=MESH)

        mine = [pltpu.make_async_copy(ins[a], slot(a, *me), local_sems.at[a]) for a in range(n)]
        for cp in mine:
            cp.start()
        first = []
        for a in range(n):
            first.append(copy(a, 0, me, sibling, src=ins[a]))
            first += [copy(a, 1 + j, me, (*chip, c), src=ins[a]) for j, chip in enumerate(chips)]
        for cp in first:
            cp.start()
        passed = []
        for a in range(n):
            for j, chip in enumerate(chips):
                copy(a, 1 + j, (*chip, c), me).wait_recv()
                fwd = copy(a, 4 + j, (*chip, c), sibling)
                fwd.start()
                passed.append(fwd)
        for a in range(n):
            copy(a, 0, sibling, me).wait_recv()
            for j, chip in enumerate(chips):
                copy(a, 4 + j, (*chip, 1 - c), me).wait_recv()
        for cp in first + passed:
            cp.wait_send()
        for cp in mine:
            cp.wait()

    outs = pl.pallas_call(
        body, name=name,
        out_shape=[jax.ShapeDtypeStruct((N_DEV,) + a.shape, a.dtype) for a in arrs],
        in_specs=[ANY] * n, out_specs=[ANY] * n,
        scratch_shapes=[pltpu.SemaphoreType.DMA((n, 7)), pltpu.SemaphoreType.DMA((n, 7)), pltpu.SemaphoreType.DMA((n,))],
    )(*arrs)
    return list(outs)


def _peer_copies(src_of, dsts, send_sems, recv_sems):
    x, y, c = _mesh_pos()
    me = 4 * x + 2 * y + c
    sends, arrivals = [], []
    for a in range(len(dsts)):
        for k in range(1, N_DEV):
            px, py, pc = (1 - x if k & 4 else x), (1 - y if k & 2 else y), (1 - c if k & 1 else c)
            p = 4 * px + 2 * py + pc
            common = dict(send_sem=send_sems.at[a, k - 1], recv_sem=recv_sems.at[a, k - 1],
                          device_id=(px, py, pc), device_id_type=MESH)
            sends.append(pltpu.make_async_remote_copy(src_ref=src_of(a, p), dst_ref=dsts[a].at[me], **common))
            arrivals.append(pltpu.make_async_remote_copy(src_ref=dsts[a].at[p], dst_ref=dsts[a].at[p], **common))
    return sends, arrivals


def _peer_sems(n):
    return [pltpu.SemaphoreType.DMA((n, N_DEV - 1)), pltpu.SemaphoreType.DMA((n, N_DEV - 1))]


def _shard_of(ref, kind, j):
    if kind == "col":
        w = ref.shape[1] // N_DEV
        return ref.at[:, pl.ds(j * w, w)]
    h = ref.shape[0] // N_DEV
    return ref.at[pl.ds(j * h, h), :]


def _shard_shape(shape, kind):
    return (shape[0], shape[1] // N_DEV) if kind == "col" else (shape[0] // N_DEV, shape[1])


def _exchange_cores(grads, kinds, name):
    n = len(grads)

    def body(*refs):
        ins, outs = refs[:n], refs[n:2 * n]
        send_sems, recv_sems = refs[2 * n:]
        x, y, c = _mesh_pos()
        sibling = (x, y, 1 - c)
        sends = []
        for a in range(n):
            for s in range(4):
                sends.append(pltpu.make_async_remote_copy(
                    src_ref=_shard_of(ins[a], kinds[a], 2 * s + (1 - c)), dst_ref=outs[a].at[s],
                    send_sem=send_sems.at[a, s], recv_sem=recv_sems.at[a, s], device_id=sibling, device_id_type=MESH))
        for cp in sends:
            cp.start()
        for a in range(n):
            for s in range(4):
                pltpu.make_async_remote_copy(
                    src_ref=outs[a].at[s], dst_ref=outs[a].at[s],
                    send_sem=send_sems.at[a, s], recv_sem=recv_sems.at[a, s], device_id=sibling, device_id_type=MESH).wait_recv()
        for cp in sends:
            cp.wait_send()

    outs = pl.pallas_call(
        body, name=name,
        out_shape=[jax.ShapeDtypeStruct((4,) + _shard_shape(g.shape, k), F32) for g, k in zip(grads, kinds)],
        in_specs=[ANY] * n, out_specs=[ANY] * n,
        scratch_shapes=[pltpu.SemaphoreType.DMA((n, 4)), pltpu.SemaphoreType.DMA((n, 4))],
    )(*grads)
    return list(outs)


HBM = pl.BlockSpec(memory_space=pltpu.HBM)
SEM = pl.BlockSpec(memory_space=pltpu.SEMAPHORE)
DATAFLOW = pltpu.SideEffectType.DATAFLOW_SIDE_EFFECTING


def _chip_copies(src_ref, land_ref, send_sems, recv_sems):
    x, y, c = _mesh_pos()
    my_slot = 2 * x + y
    sends, arrivals = [], []
    for j, (px, py) in enumerate([(1 - x, y), (x, 1 - y), (1 - x, 1 - y)]):
        common = dict(send_sem=send_sems[j], recv_sem=recv_sems[j], device_id=(px, py, c), device_id_type=MESH)
        sends.append(pltpu.make_async_remote_copy(src_ref=src_ref.at[2 * px + py], dst_ref=land_ref.at[my_slot], **common))
        arrivals.append(pltpu.make_async_remote_copy(src_ref=src_ref.at[2 * px + py], dst_ref=land_ref.at[2 * px + py], **common))
    return sends, arrivals


def _chips_start(part):
    def body(src_ref, land_ref, s0, s1, s2, r0, r1, r2, src_thru, land_thru, token):
        sends, _ = _chip_copies(src_ref, land_ref, (s0, s1, s2), (r0, r1, r2))
        for cp in sends:
            cp.start()
        token[...] = jnp.zeros_like(token)

    res = pl.pallas_call(
        body, name="reduce_chips_start",
        out_shape=(pltpu.SemaphoreType.DMA(()),) * 6 + (pltpu.HBM(part.shape, part.dtype), pltpu.HBM(part.shape, part.dtype),
                                                        jax.ShapeDtypeStruct((8, 128), F32)),
        in_specs=(HBM, HBM), out_specs=(SEM,) * 6 + (HBM, HBM, pl.BlockSpec(memory_space=pltpu.VMEM)),
        input_output_aliases={0: 6, 1: 7}, compiler_params=pltpu.CompilerParams(has_side_effects=DATAFLOW),
    )(pltpu.with_memory_space_constraint(part, pltpu.HBM),
      pltpu.with_memory_space_constraint(lax.empty(part.shape, part.dtype), pltpu.HBM))
    return res[:6], res[6], res[7], res[8]


def _chips_wait(sems, src_thru, land_thru, after):
    def body(src_ref, land_ref, s0, s1, s2, r0, r1, r2, after_ref, src_dead, got_ref):
        sends, arrivals = _chip_copies(src_ref, land_ref, (s0, s1, s2), (r0, r1, r2))
        for cp in sends:
            cp.wait_send()
        for cp in arrivals:
            cp.wait_recv()

    return pl.pallas_call(
        body, name="reduce_chips_wait",
        out_shape=(pltpu.HBM(src_thru.shape, src_thru.dtype), pltpu.HBM(land_thru.shape, land_thru.dtype)),
        in_specs=(HBM, HBM) + (SEM,) * 6 + (ANY,), out_specs=(HBM, HBM), input_output_aliases={0: 0, 1: 1},
        compiler_params=pltpu.CompilerParams(has_side_effects=DATAFLOW),
    )(src_thru, land_thru, *sems, after)[1]


def _row_tile(rows):
    for t in (256, 128, 64, 32, 16, 8):
        if rows % t == 0:
            return t
    return rows


def _wire_dtype(rows):
    return BF16 if rows % 16 == 0 else F32


def _add_own(core, g, recv, kind, name):
    ns, R, C = recv.shape
    tr = _row_tile(R)
    nr = R // tr
    if kind == "col":
        g_spec = pl.BlockSpec((tr, C), lambda s, r, c_ref: (r, 2 * s + c_ref[0]))
    else:
        g_spec = pl.BlockSpec((tr, C), lambda s, r, c_ref: ((2 * s + c_ref[0]) * nr + r, 0))
    slab = pl.BlockSpec((None, tr, C), lambda s, r, c_ref: (s, r, 0))

    def body(c_ref, g_ref, r_ref, o32_ref, o16_ref):
        t = g_ref[...] + r_ref[...]
        o32_ref[...] = t
        o16_ref[...] = t.astype(o16_ref.dtype)

    return pl.pallas_call(
        body, name=name,
        grid_spec=pltpu.PrefetchScalarGridSpec(num_scalar_prefetch=1, grid=(ns, nr), in_specs=[g_spec, slab], out_specs=[slab, slab]),
        out_shape=[jax.ShapeDtypeStruct((ns, R, C), F32), jax.ShapeDtypeStruct((ns, R, C), _wire_dtype(R))],
        compiler_params=_cparams(2),
    )(core, g, recv)


def _adamw_math(w, g, m, v):
    m2 = ADAM_B1 * m + (1.0 - ADAM_B1) * g
    v2 = ADAM_B2 * v + (1.0 - ADAM_B2) * (g * g)
    m_hat = m2 / (1.0 - ADAM_B1 ** ADAM_STEP)
    v_hat = v2 / (1.0 - ADAM_B2 ** ADAM_STEP)
    delta = -ADAM_LR * (m_hat / (jnp.sqrt(v_hat) + ADAM_EPS) + ADAM_WD * w)
    return delta, m2, v2


def _sum_adamw(parts, w, m, v, name):
    n, R, C = parts.shape
    tr = _row_tile(R)

    def body(p_ref, w_ref, m_ref, v_ref, g_ref, d_ref, m_out, v_out):
        g = p_ref[0]
        for k in range(1, n):
            g = g + p_ref[k]
        g_ref[...] = g
        d_ref[...], m_out[...], v_out[...] = _adamw_math(w_ref[...], g, m_ref[...], v_ref[...])

    blk = pl.BlockSpec((tr, C), lambda r: (r, 0))
    return pl.pallas_call(
        body, name=name, grid=(R // tr,),
        in_specs=[pl.BlockSpec((n, tr, C), lambda r: (0, r, 0)), blk, blk, blk],
        out_specs=[blk] * 4, out_shape=[jax.ShapeDtypeStruct((R, C), F32)] * 4, compiler_params=_cparams(1),
    )(parts, w, m, v)


def _sum_chips_adamw(chip, own, recv, w, m, v, name):
    n, R, C = recv.shape
    tr = _row_tile(R)

    def body(chip_ref, own_ref, r_ref, w_ref, m_ref, v_ref, g_ref, d_ref, m_out, v_out):
        g = None
        for k in range(n):
            term = jnp.where(chip_ref[0] == k, own_ref[...], r_ref[k].astype(F32))
            g = term if g is None else g + term
        g_ref[...] = g
        d_ref[...], m_out[...], v_out[...] = _adamw_math(w_ref[...], g, m_ref[...], v_ref[...])

    blk = pl.BlockSpec((tr, C), lambda r, chip_ref: (r, 0))
    return pl.pallas_call(
        body, name=name,
        grid_spec=pltpu.PrefetchScalarGridSpec(
            num_scalar_prefetch=1, grid=(R // tr,),
            in_specs=[pl.BlockSpec((None, tr, C), lambda r, chip_ref: (chip_ref[0], r, 0)),
                      pl.BlockSpec((n, tr, C), lambda r, chip_ref: (0, r, 0)), blk, blk, blk],
            out_specs=[blk] * 4),
        out_shape=[jax.ShapeDtypeStruct((R, C), F32)] * 4, compiler_params=_cparams(1),
    )(chip, own, recv, w, m, v)


def _pack_rows(pieces, name):
    arrs = []
    for a, _, _, _ in pieces:
        if not any(a is b for b in arrs):
            arrs.append(a)
    which = [next(i for i, b in enumerate(arrs) if b is a) for a, _, _, _ in pieces]
    total = sum(n for _, _, _, n in pieces)

    def body(*refs):
        o_ref = refs[len(arrs)]
        off = 0
        for (_, r, c0, n), i in zip(pieces, which):
            o_ref[:, off:off + n] = refs[i][r:r + 1, c0:c0 + n]
            off += n

    return pl.pallas_call(
        body, name=name, in_specs=[_full(a.shape) for a in arrs], out_specs=_full((1, total)),
        out_shape=jax.ShapeDtypeStruct((1, total), F32), grid=(1,), compiler_params=_cparams(1),
    )(*arrs)


def _adamw_vectors(parts, ws, ms, vs):
    nv = len(ws)
    widths = [a.shape[1] for a in ws]

    def body(*refs):
        p_ref = refs[0]
        w_refs, m_refs, v_refs = refs[1:1 + nv], refs[1 + nv:1 + 2 * nv], refs[1 + 2 * nv:1 + 3 * nv]
        outs = refs[1 + 3 * nv:]
        off = 0
        for i, n in enumerate(widths):
            g = p_ref[0:1, off:off + n]
            for k in range(1, p_ref.shape[0]):
                g = g + p_ref[k:k + 1, off:off + n]
            outs[i][...] = g
            outs[nv + i][...], outs[2 * nv + i][...], outs[3 * nv + i][...] = _adamw_math(w_refs[i][...], g, m_refs[i][...], v_refs[i][...])
            off += n

    vec_specs = [_full((1, n)) for n in widths]
    res = pl.pallas_call(
        body, name="adamw_vectors", grid=(1,),
        in_specs=[_full(parts.shape)] + vec_specs * 3, out_specs=vec_specs * 4,
        out_shape=[jax.ShapeDtypeStruct((1, n), F32) for n in widths] * 4, compiler_params=_cparams(1),
    )(parts, *ws, *ms, *vs)
    return [res[t * nv:(t + 1) * nv] for t in range(4)]


def _sum_peers_adamw(me, g, recv, kind, w, m, v, name, after):
    n, R, C = recv.shape
    tr = _row_tile(R)
    nr = R // tr
    if kind == "col":
        g_spec = pl.BlockSpec((tr, C), lambda r, me_ref: (r, me_ref[0]))
    else:
        g_spec = pl.BlockSpec((tr, C), lambda r, me_ref: (me_ref[0] * nr + r, 0))

    def body(me_ref, own_ref, r_ref, w_ref, m_ref, v_ref, after_ref, g_ref, d_ref, m_out, v_out):
        acc = None
        for k in range(n):
            term = jnp.where(me_ref[0] == k, own_ref[...], r_ref[k].astype(F32))
            acc = term if acc is None else acc + term
        g_ref[...] = acc
        d_ref[...], m_out[...], v_out[...] = _adamw_math(w_ref[...], acc, m_ref[...], v_ref[...])

    blk = pl.BlockSpec((tr, C), lambda r, me_ref: (r, 0))
    return pl.pallas_call(
        body, name=name,
        grid_spec=pltpu.PrefetchScalarGridSpec(
            num_scalar_prefetch=1, grid=(nr,),
            in_specs=[g_spec, pl.BlockSpec((n, tr, C), lambda r, me_ref: (0, r, 0)), blk, blk, blk, ANY], out_specs=[blk] * 4),
        out_shape=[jax.ShapeDtypeStruct((R, C), F32)] * 4, compiler_params=_cparams(1),
    )(me, g, recv, w, m, v, after)


def _ada_cols(me, c_all, w_ada16, b_ada):
    nd, D = c_all.shape
    W = w_ada16.shape[1]

    def body(me_ref, c_ref, w_ref, b_ref, o_ref):
        cv = c_ref[...]
        r = _dot((cv * _sigmoid(cv)).astype(BF16), w_ref[...]) + b_ref[...]
        for b in range(nd):
            o_ref[b] = r[b:b + 1, :]

    return pl.pallas_call(
        body, name="ada_cols",
        grid_spec=pltpu.PrefetchScalarGridSpec(
            num_scalar_prefetch=1, grid=(1,),
            in_specs=[_full((nd, D)), _full((D, W)), pl.BlockSpec((1, W), lambda i, me_ref: (0, me_ref[0]))],
            out_specs=_full((nd, 1, W))),
        out_shape=jax.ShapeDtypeStruct((nd, 1, W), F32), compiler_params=_cparams(1),
    )(me, c_all, w_ada16, b_ada)


def _rows_to_owners(part, name):
    def body(p_ref, o_ref, send_sems, recv_sems, local_sem):
        x, y, c = _mesh_pos()
        me = 4 * x + 2 * y + c
        mine = pltpu.make_async_copy(p_ref.at[me], o_ref.at[me], local_sem.at[0])
        sends, arrivals = _peer_copies(lambda a, p: p_ref.at[p], [o_ref], send_sems, recv_sems)
        mine.start()
        for cp in sends:
            cp.start()
        for cp in arrivals:
            cp.wait_recv()
        for cp in sends:
            cp.wait_send()
        mine.wait()

    return pl.pallas_call(
        body, name=name, in_specs=[ANY], out_specs=ANY, out_shape=jax.ShapeDtypeStruct(part.shape, part.dtype),
        scratch_shapes=_peer_sems(1) + [pltpu.SemaphoreType.DMA((1,))],
    )(part)


def _ada_mod(me, c, w_ada16, b_ada):
    (c_all,) = _all_gather([c], "gather_c")
    part = _ada_cols(me, c_all.reshape(N_DEV, c.shape[1]), w_ada16, b_ada)
    return _rows_to_owners(part, "scatter_mod").reshape(1, N_DEV * w_ada16.shape[1])


def _ada_bwd(me, small_g, D, W, c_block):
    def body(me_ref, c_ref, dm_ref, o_ref):
        cv = c_ref[...]
        ca = (cv * _sigmoid(cv)).astype(BF16).astype(F32)
        dm = dm_ref[...].astype(BF16).astype(F32)
        o_ref[...] = lax.dot_general(ca, dm, (((0,), (0,)), ((), ())), precision=lax.Precision.HIGHEST,
                                     preferred_element_type=F32)

    return pl.pallas_call(
        body, name="ada_bwd",
        grid_spec=pltpu.PrefetchScalarGridSpec(
            num_scalar_prefetch=1, grid=(1,),
            in_specs=[pl.BlockSpec((N_DEV, D), lambda i, me_ref: (0, c_block)),
                      pl.BlockSpec((N_DEV, W), lambda i, me_ref: (0, me_ref[0]))],
            out_specs=pl.BlockSpec((D, W), lambda i, me_ref: (0, 0))),
        out_shape=jax.ShapeDtypeStruct((D, W), F32), compiler_params=_cparams(1),
    )(me, small_g, small_g)


def _fwd_in(x, mod, w_in_g, b_in, shards):
    S, D = x.shape
    nb, _, wb = w_in_g.shape
    N = nb * wb
    TM = TOKEN_TILE
    n_tiles = S // TM
    ns = len(shards)

    def body(*refs):
        x_ref, mod_ref, w_hbm, b_ref = refs[:4]
        ins = refs[4:4 + ns]
        z_ref, h_ref = refs[4 + ns:6 + ns]
        outs = refs[6 + ns:6 + 2 * ns]
        w_v, sems, send_sems, recv_sems, local_sems = refs[6 + 2 * ns:]
        i = pl.program_id(0)
        x_, y_, c_ = _mesh_pos()
        me = 4 * x_ + 2 * y_ + c_
        mine = [pltpu.make_async_copy(ins[a], outs[a].at[me], local_sems.at[a]) for a in range(ns)]
        sends, arrivals = _peer_copies(lambda a, p: ins[a], outs, send_sems, recv_sems)

        @pl.when(i == 0)
        def _():
            for cp in sends + mine:
                cp.start()
            _load_cols(w_hbm, w_v, sems)

        xhat, _ = _ln(x_ref[...])
        hb = (xhat * (1.0 + mod_ref[:, D:2 * D]) + mod_ref[:, 0:D]).astype(BF16)
        h_ref[...] = hb
        z_ref[...] = _dot(hb, w_v[...]) + b_ref[...]

        @pl.when(i == n_tiles - 1)
        def _():
            for cp in arrivals:
                cp.wait_recv()
            for cp in sends:
                cp.wait_send()
            for cp in mine:
                cp.wait()

    res = pl.pallas_call(
        body, name="fwd_in", grid=(n_tiles,),
        in_specs=[pl.BlockSpec((TM, D), lambda i: (i, 0)), _full(mod.shape), ANY, _full((1, N))] + [ANY] * ns,
        out_specs=[pl.BlockSpec((TM, N), lambda i: (i, 0)), pl.BlockSpec((TM, D), lambda i: (i, 0))] + [ANY] * ns,
        out_shape=[jax.ShapeDtypeStruct((S, N), F32), jax.ShapeDtypeStruct((S, D), BF16)]
        + [jax.ShapeDtypeStruct((N_DEV,) + a.shape, a.dtype) for a in shards],
        scratch_shapes=[pltpu.VMEM((D, N), BF16), pltpu.SemaphoreType.DMA((nb,))] + _peer_sems(ns) + [pltpu.SemaphoreType.DMA((ns,))],
        compiler_params=_cparams(1),
    )(x, mod, w_in_g, b_in, *shards)
    return res[0], res[1], list(res[2:])


def _rows_from(win, o, R):
    if o % 8 == 0:
        return win[o:o + R, :]
    return pltpu.roll(win, win.shape[0] - o, axis=0)[0:R, :]


def _causal_taps(buf, w_ref, o_ref, K, halo, TM, D):
    R, C = CONV_ROWS, CONV_LANES
    base = halo - (K - 1)

    for r0 in range(0, TM, R):
        for c0 in range(0, D, C):
            win = buf[r0:r0 + R + halo, c0:c0 + C]
            acc = jnp.zeros((R, C), F32)
            for k in range(K):
                acc = acc + w_ref[k:k + 1, c0:c0 + C] * _rows_from(win, base + k, R)
            o_ref[r0:r0 + R, c0:c0 + C] = acc


def _fwd_mix(z, x, mod, wa, ba, lag, lab, w_a_out, b_a_out, wb, w_b_out, w_o, b_o):
    S, D = x.shape
    N = z.shape[1]
    TM = TOKEN_TILE
    KA, KB = wa.shape[0], wb.shape[0]

    def body(z_ref, x_ref, mod_ref, wa_ref, ba_ref, lag_ref, lab_ref, wao_hbm, bao_ref, wb_ref, wbo_hbm, wo_hbm, bo_ref,
             u1_ref, ya_ref, yb_ref, q_ref, out_ref, r1_ref, u2_ref, v_ref, mg_ref,
             ubuf, pbuf, qbuf, wao, wbo, wo, sems):
        i = pl.program_id(0)

        @pl.when(i == 0)
        def _():
            _load_whole([(wao_hbm, wao), (wbo_hbm, wbo), (wo_hbm, wo)], sems)
            ubuf[0:HALO_A, :] = jnp.zeros((HALO_A, D), F32)
            pbuf[0:HALO_B, :] = jnp.zeros((HALO_B, D), F32)

        ubuf[HALO_A:HALO_A + TM, :] = z_ref[:, 0:D] * _sigmoid(z_ref[:, D:2 * D])
        _causal_taps(ubuf, wa_ref, u1_ref, KA, HALO_A, TM, D)
        ubuf[0:HALO_A, :] = ubuf[TM:TM + HALO_A, :]
        u1 = u1_ref[...] + ba_ref[...]
        u1_ref[...] = u1
        xa, _ = _ln(u1)
        l = xa * lag_ref[...] + lab_ref[...]
        u2 = (l * _sigmoid(l)).astype(BF16)
        u2_ref[...] = u2
        ya = _dot(u2, wao[...]) + bao_ref[...]
        ya_ref[...] = ya.astype(BF16)
        pbuf[HALO_B:HALO_B + TM, :] = z_ref[:, 3 * D:4 * D] * z_ref[:, 4 * D:5 * D]
        _causal_taps(pbuf, wb_ref, qbuf, KB, HALO_B, TM, D)
        pbuf[0:HALO_B, :] = pbuf[TM:TM + HALO_B, :]
        q_ref[...] = qbuf[...].astype(BF16)
        v = (z_ref[:, 2 * D:3 * D] * qbuf[...]).astype(BF16)
        v_ref[...] = v
        yb = _dot(v, wbo[...])
        yb_ref[...] = yb.astype(BF16)
        mg = (_sigmoid(z_ref[:, 5 * D:6 * D]) * ya + _sigmoid(z_ref[:, 6 * D:7 * D]) * yb).astype(BF16)
        mg_ref[...] = mg
        out = _dot(mg, wo[...]) + bo_ref[...]
        out_ref[...] = out.astype(BF16)
        r1_ref[...] = ALPHA * x_ref[...] + (1.0 + mod_ref[:, 2 * D:3 * D]) * out

    tile = pl.BlockSpec((TM, D), lambda i: (i, 0))
    vec = _full((1, D))
    return pl.pallas_call(
        body, name="fwd_mix", grid=(S // TM,),
        in_specs=[pl.BlockSpec((TM, N), lambda i: (i, 0)), tile, _full(mod.shape), _full((KA, D)), vec, vec, vec, ANY, vec,
                  _full((KB, D)), ANY, ANY, vec],
        out_specs=[tile] * 9,
        out_shape=[jax.ShapeDtypeStruct((S, D), dt) for dt in (F32, BF16, BF16, BF16, BF16, F32, BF16, BF16, BF16)],
        scratch_shapes=[pltpu.VMEM((TM + HALO_A, D), F32), pltpu.VMEM((TM + HALO_B, D), F32), pltpu.VMEM((TM, D), F32),
                        pltpu.VMEM((D, D), BF16), pltpu.VMEM((D, D), BF16), pltpu.VMEM((D, D), BF16),
                        pltpu.SemaphoreType.DMA((3,))],
        compiler_params=_cparams(1),
    )(z, x, mod, wa, ba, lag, lab, w_a_out, b_a_out, wb, w_b_out, w_o, b_o)


FFN_ROWS = ("b_down", "ln2_g", "ln2_b", "gate2", "scale2", "shift2", "loss")


def _ffn(r1, tgt, mod, ln1_g, ln1_b, w_up_g, b_up, w_down, b_down, ln2_g, ln2_b):
    S, D = r1.shape
    nb, _, wb = w_up_g.shape
    FF = nb * wb
    TM = TOKEN_TILE
    CF = 1024
    n_tiles = S // TM

    def body(r1_ref, t_ref, mod_ref, g1_ref, b1_ref, wup_hbm, bup_ref, wdn_hbm, bdn_ref, g2_ref, b2_ref,
             dx1_ref, h2_ref, f_ref, dhu_ref, do2_ref, acc_ref, accup_ref,
             wup, wdn, relu_buf, sems, sem2):
        i = pl.program_id(0)

        @pl.when(i == 0)
        def _():
            _load_cols(wup_hbm, wup, sems)
            _load_whole([(wdn_hbm, wdn)], sem2)
            acc_ref[...] = jnp.zeros(acc_ref.shape, F32)
            accup_ref[...] = jnp.zeros(accup_ref.shape, F32)

        scale2, shift2, gate2 = mod_ref[:, 4 * D:5 * D], mod_ref[:, 3 * D:4 * D], mod_ref[:, 5 * D:6 * D]
        xhat1, _ = _ln(r1_ref[...])
        x1 = xhat1 * g1_ref[...] + b1_ref[...]
        xh0, rstd0 = _ln(x1)
        h2 = (xh0 * (1.0 + scale2) + shift2).astype(BF16)
        h2_ref[...] = h2
        out2 = jnp.zeros((TM, D), F32) + bdn_ref[...]
        for c0 in range(0, FF, CF):
            hu = _dot(h2, wup[:, c0:c0 + CF]) + bup_ref[:, c0:c0 + CF]
            rl = jnp.maximum(hu, 0.0)
            relu_buf[:, c0:c0 + CF] = rl
            fb = (rl * rl).astype(BF16)
            f_ref[:, c0:c0 + CF] = fb
            out2 = out2 + _dot(fb, wdn[c0:c0 + CF, :])
        r2 = ALPHA * x1 + (1.0 + gate2) * out2
        xh2, rstd2 = _ln(r2)
        e = xh2 * g2_ref[...] + b2_ref[...] - t_ref[...]
        acc_ref[6:7, :] += _colsum(e * e)
        dy = e * (1.0 / D)
        acc_ref[1:2, :] += _colsum(dy * xh2)
        acc_ref[2:3, :] += _colsum(dy)
        dr2 = _ln_bwd(dy * g2_ref[...], xh2, rstd2)
        acc_ref[3:4, :] += _colsum(dr2 * out2)
        do2 = (1.0 + gate2) * dr2
        acc_ref[0:1, :] += _colsum(do2)
        do2b = do2.astype(BF16)
        do2_ref[...] = do2b
        dh2 = jnp.zeros((TM, D), F32)
        for c0 in range(0, FF, CF):
            dhu = _dot_nt(do2b, wdn[c0:c0 + CF, :]) * (2.0 * relu_buf[:, c0:c0 + CF])
            accup_ref[:, c0:c0 + CF] += _colsum(dhu)
            dhub = dhu.astype(BF16)
            dhu_ref[:, c0:c0 + CF] = dhub
            dh2 = dh2 + _dot_nt(dhub, wup[:, c0:c0 + CF])
        acc_ref[4:5, :] += _colsum(dh2 * xh0)
        acc_ref[5:6, :] += _colsum(dh2)
        dx1_ref[...] = ALPHA * dr2 + _ln_bwd(dh2 * (1.0 + scale2), xh0, rstd0)

        @pl.when(i == n_tiles - 1)
        def _():
            tot = jnp.sum(acc_ref[6:7, :], axis=-1, keepdims=True) * (0.5 / D)
            acc_ref[6:7, :] = jnp.broadcast_to(tot, (1, D))

    tile = pl.BlockSpec((TM, D), lambda i: (i, 0))
    wide = pl.BlockSpec((TM, FF), lambda i: (i, 0))
    vec = _full((1, D))
    return pl.pallas_call(
        body, name="ffn", grid=(n_tiles,),
        in_specs=[tile, tile, _full(mod.shape), vec, vec, ANY, _full((1, FF)), ANY, vec, vec, vec],
        out_specs=[tile, tile, wide, wide, tile, _full((8, D)), _full((1, FF))],
        out_shape=[jax.ShapeDtypeStruct((S, D), F32), jax.ShapeDtypeStruct((S, D), BF16), jax.ShapeDtypeStruct((S, FF), BF16),
                   jax.ShapeDtypeStruct((S, FF), BF16), jax.ShapeDtypeStruct((S, D), BF16),
                   jax.ShapeDtypeStruct((8, D), F32), jax.ShapeDtypeStruct((1, FF), F32)],
        scratch_shapes=[pltpu.VMEM((D, FF), BF16), pltpu.VMEM((FF, D), BF16), pltpu.VMEM((TM, FF), F32),
                        pltpu.SemaphoreType.DMA((nb,)), pltpu.SemaphoreType.DMA((1,))],
        compiler_params=_cparams(1),
    )(r1, tgt, mod, ln1_g, ln1_b, w_up_g, b_up, w_down, b_down, ln2_g, ln2_b)


def _dw(a, b, name, msplit=1, nsplit=1, wire=False, exchange=None):
    S, M = a.shape
    N = b.shape[1]
    TK = min(DW_TILE, S)
    mb, nbk = M // msplit, N // nsplit
    nk = S // TK
    srcs, kinds = exchange if exchange else ((), ())
    ns = len(srcs)

    def body(*refs):
        a_ref, b_ref = refs[:2]
        ins = refs[2:2 + ns]
        o_ref = refs[2 + ns]
        o16_ref = refs[3 + ns] if wire else None
        rest = refs[3 + ns + (1 if wire else 0):]
        k = pl.program_id(2)
        if ns:
            outs, (send_sems, recv_sems) = rest[:ns], rest[ns:]
            sends, arrivals = _peer_copies(lambda q, p: _shard_of(ins[q], kinds[q], p), outs, send_sems, recv_sems)
            first = (pl.program_id(0) == 0) & (pl.program_id(1) == 0) & (k == 0)
            last = (pl.program_id(0) == msplit - 1) & (pl.program_id(1) == nsplit - 1) & (k == nk - 1)

            @pl.when(first)
            def _():
                for cp in sends:
                    cp.start()

        @pl.when(k == 0)
        def _():
            o_ref[...] = jnp.zeros(o_ref.shape, F32)

        o_ref[...] += _dot_tn(a_ref[...], b_ref[...])

        if wire:
            @pl.when(k == nk - 1)
            def _():
                o16_ref[...] = o_ref[...].astype(BF16)

        if ns:
            @pl.when(last)
            def _():
                for cp in arrivals:
                    cp.wait_recv()
                for cp in sends:
                    cp.wait_send()

    oblk = pl.BlockSpec((mb, nbk), lambda i, j, k: (i, j))
    res = pl.pallas_call(
        body, name=name, grid=(msplit, nsplit, nk),
        in_specs=[pl.BlockSpec((TK, mb), lambda i, j, k: (k, i)), pl.BlockSpec((TK, nbk), lambda i, j, k: (k, j))] + [ANY] * ns,
        out_specs=[oblk] + ([oblk] if wire else []) + [ANY] * ns,
        out_shape=[jax.ShapeDtypeStruct((M, N), F32)] + ([jax.ShapeDtypeStruct((M, N), BF16)] if wire else [])
        + [jax.ShapeDtypeStruct((N_DEV,) + _shard_shape(g.shape, kd), g.dtype) for g, kd in zip(srcs, kinds)],
        scratch_shapes=_peer_sems(ns) if ns else [],
        compiler_params=_cparams(3),
    )(a, b, *srcs)
    return res if (wire or ns) else res[0]


BWD1_ROWS = ("ln1_g", "ln1_b", "gate1", "b_o", "b_a_out", "ln_a_g", "ln_a_b", "conv_a_b", "b_in_gb", "b_in_ga", "b_in_gbb")


def _bwd_mix1(dx1, r1, out, ya, yb, q, u1, z, mod, ln1_g, lag, lab, w_o, w_a_out, w_b_out):
    S, D = dx1.shape
    TM = TOKEN_TILE

    def body(dx1_ref, r1_ref, out_ref, ya_ref, yb_ref, q_ref, u1_ref, zgb_ref, zga_ref, zgg_ref, mod_ref, g1_ref, lag_ref, lab_ref,
             wo_hbm, wao_hbm, wbo_hbm,
             dxp_ref, du1_ref, dq_ref, dzc_ref, dout_ref, dya_ref, dyb_ref, acc_ref,
             wo, wao, wbo, sems):
        @pl.when(pl.program_id(0) == 0)
        def _():
            _load_whole([(wo_hbm, wo), (wao_hbm, wao), (wbo_hbm, wbo)], sems)
            acc_ref[...] = jnp.zeros(acc_ref.shape, F32)

        dx1v = dx1_ref[...]
        xhat1, rstd1 = _ln(r1_ref[...])
        acc_ref[0:1, :] += _colsum(dx1v * xhat1)
        acc_ref[1:2, :] += _colsum(dx1v)
        dr1 = _ln_bwd(dx1v * g1_ref[...], xhat1, rstd1)
        dxp_ref[...] = (ALPHA * dr1).astype(BF16)
        acc_ref[2:3, :] += _colsum(dr1 * out_ref[...].astype(F32))
        dout = (1.0 + mod_ref[:, 2 * D:3 * D]) * dr1
        acc_ref[3:4, :] += _colsum(dout)
        doutb = dout.astype(BF16)
        dout_ref[...] = doutb
        dmg = _dot_nt(doutb, wo[...])
        sga, sgb = _sigmoid(zga_ref[...]), _sigmoid(zgg_ref[...])
        dga = dmg * ya_ref[...].astype(F32) * sga * (1.0 - sga)
        dgb = dmg * yb_ref[...].astype(F32) * sgb * (1.0 - sgb)
        acc_ref[9:10, :] += _colsum(dga)
        acc_ref[10:11, :] += _colsum(dgb)
        dzc_ref[:, D:2 * D] = dga.astype(BF16)
        dzc_ref[:, 2 * D:3 * D] = dgb.astype(BF16)
        dya = dmg * sga
        acc_ref[4:5, :] += _colsum(dya)
        dyab = dya.astype(BF16)
        dya_ref[...] = dyab
        dybb = (dmg * sgb).astype(BF16)
        dyb_ref[...] = dybb
        du2 = _dot_nt(dyab, wao[...])
        xa, rstda = _ln(u1_ref[...])
        l = xa * lag_ref[...] + lab_ref[...]
        sl = _sigmoid(l)
        dl = du2 * (sl * (1.0 + l * (1.0 - sl)))
        acc_ref[5:6, :] += _colsum(dl * xa)
        acc_ref[6:7, :] += _colsum(dl)
        du1 = _ln_bwd(dl * lag_ref[...], xa, rstda)
        acc_ref[7:8, :] += _colsum(du1)
        du1_ref[...] = du1.astype(BF16)
        dv = _dot_nt(dybb, wbo[...])
        dgbk = dv * q_ref[...].astype(F32)
        acc_ref[8:9, :] += _colsum(dgbk)
        dzc_ref[:, 0:D] = dgbk.astype(BF16)
        dq_ref[...] = (dv * zgb_ref[...]).astype(BF16)

    tile = pl.BlockSpec((TM, D), lambda i: (i, 0))
    vec = _full((1, D))
    zcol = lambda k: pl.BlockSpec((TM, D), lambda i: (i, k))
    return pl.pallas_call(
        body, name="bwd_mix1", grid=(S // TM,),
        in_specs=[tile] * 7 + [zcol(2), zcol(5), zcol(6), _full(mod.shape), vec, vec, vec, ANY, ANY, ANY],
        out_specs=[tile, tile, tile, pl.BlockSpec((TM, 3 * D), lambda i: (i, 0)), tile, tile, tile, _full((16, D))],
        out_shape=[jax.ShapeDtypeStruct((S, D), BF16)] * 3 + [jax.ShapeDtypeStruct((S, 3 * D), BF16)]
        + [jax.ShapeDtypeStruct((S, D), BF16)] * 3 + [jax.ShapeDtypeStruct((16, D), F32)],
        scratch_shapes=[pltpu.VMEM((D, D), BF16)] * 3 + [pltpu.SemaphoreType.DMA((3,))],
        compiler_params=_cparams(1),
    )(dx1, r1, out, ya, yb, q, u1, z, z, z, mod, ln1_g, lag, lab, w_o, w_a_out, w_b_out)


def _anticausal_taps(dbuf, u_ref, w_ref, o_ref, dwacc, K, halo, TM, D):
    R, C = CONV_ROWS, CONV_LANES

    for r0 in range(0, TM, R):
        for c0 in range(0, D, C):
            win = dbuf[r0:r0 + R + halo, c0:c0 + C]
            uc = u_ref[r0:r0 + R, c0:c0 + C]
            acc = jnp.zeros((R, C), F32)
            for j in range(K):
                sh = _rows_from(win, j, R)
                k = K - 1 - j
                acc = acc + w_ref[k:k + 1, c0:c0 + C] * sh
                pr = uc * sh
                part = pr[0:8, :]
                for s in range(8, R, 8):
                    part = part + pr[s:s + 8, :]
                dwacc[k, :, c0:c0 + C] += part
            o_ref[r0:r0 + R, c0:c0 + C] = acc


def _bwd_mix2(du1, dq, z, dzc, x, dxp, mod, wa, wb, w_in_g):
    S, D = x.shape
    N = z.shape[1]
    nb, _, wbk = w_in_g.shape
    TM = TOKEN_TILE
    KA, KB = wa.shape[0], wb.shape[0]
    n_tiles = S // TM

    def body(du1_ref, dq_ref, zav_ref, zc_ref, zx_ref, dzc_ref, x_ref, dxp_ref, mod_ref, wa_ref, wb_ref, win_hbm,
             dz_ref, gx_ref, dwa_ref, dwb_ref, acc_ref, dbin_ref,
             dbuf, qbuf, ubuf, pbuf, obuf, dwa_acc, dwb_acc, w_v, sems):
        i = pl.program_id(0)

        @pl.when(i == 0)
        def _():
            _load_cols(win_hbm, w_v, sems)
            dbuf[TM:TM + HALO_A, :] = jnp.zeros((HALO_A, D), F32)
            qbuf[TM:TM + HALO_B, :] = jnp.zeros((HALO_B, D), F32)
            dwa_acc[...] = jnp.zeros(dwa_acc.shape, F32)
            dwb_acc[...] = jnp.zeros(dwb_acc.shape, F32)
            acc_ref[...] = jnp.zeros(acc_ref.shape, F32)
            dbin_ref[...] = jnp.zeros(dbin_ref.shape, F32)

        a_val = zav_ref[:, 0:D]
        sa = _sigmoid(zav_ref[:, D:2 * D])
        ubuf[...] = a_val * sa
        dbuf[0:TM, :] = du1_ref[...].astype(F32)
        _anticausal_taps(dbuf, ubuf, wa_ref, obuf, dwa_acc, KA, HALO_A, TM, D)
        dbuf[TM:TM + HALO_A, :] = dbuf[0:HALO_A, :]
        du0 = obuf[...]
        dav = du0 * sa
        dag = du0 * a_val * sa * (1.0 - sa)
        dbin_ref[:, 0:D] += _colsum(dav)
        dbin_ref[:, D:2 * D] += _colsum(dag)
        dz_ref[:, 0:D] = dav.astype(BF16)
        dz_ref[:, D:2 * D] = dag.astype(BF16)
        pbuf[...] = zc_ref[...] * zx_ref[...]
        qbuf[0:TM, :] = dq_ref[...].astype(F32)
        _anticausal_taps(qbuf, pbuf, wb_ref, obuf, dwb_acc, KB, HALO_B, TM, D)
        qbuf[TM:TM + HALO_B, :] = qbuf[0:HALO_B, :]
        dp = obuf[...]
        dgc = dp * zx_ref[...]
        dgx = dp * zc_ref[...]
        dbin_ref[:, 3 * D:4 * D] += _colsum(dgc)
        dbin_ref[:, 4 * D:5 * D] += _colsum(dgx)
        dz_ref[:, 3 * D:4 * D] = dgc.astype(BF16)
        dz_ref[:, 4 * D:5 * D] = dgx.astype(BF16)
        dz_ref[:, 2 * D:3 * D] = dzc_ref[:, 0:D]
        dz_ref[:, 5 * D:7 * D] = dzc_ref[:, D:3 * D]
        dh = _dot_nt(dz_ref[...], w_v[...])
        xhat, rstd = _ln(x_ref[...])
        acc_ref[0:1, :] += _colsum(dh * xhat)
        acc_ref[1:2, :] += _colsum(dh)
        gx_ref[...] = dxp_ref[...].astype(F32) + _ln_bwd(dh * (1.0 + mod_ref[:, D:2 * D]), xhat, rstd)

        @pl.when(i == n_tiles - 1)
        def _():
            for k in range(KA):
                dwa_ref[k:k + 1, :] = jnp.sum(dwa_acc[k], axis=0, keepdims=True)
            for k in range(KB):
                dwb_ref[k:k + 1, :] = jnp.sum(dwb_acc[k], axis=0, keepdims=True)

    rev = lambda i: n_tiles - 1 - i
    tile = pl.BlockSpec((TM, D), lambda i: (rev(i), 0))
    zcol = lambda k: pl.BlockSpec((TM, D), lambda i: (rev(i), k))
    return pl.pallas_call(
        body, name="bwd_mix2", grid=(n_tiles,),
        in_specs=[tile, tile, pl.BlockSpec((TM, 2 * D), lambda i: (rev(i), 0)), zcol(3), zcol(4),
                  pl.BlockSpec((TM, 3 * D), lambda i: (rev(i), 0)), tile, tile, _full(mod.shape), _full((KA, D)), _full((KB, D)), ANY],
        out_specs=[pl.BlockSpec((TM, N), lambda i: (rev(i), 0)), tile, _full((KA, D)), _full((KB, D)), _full((8, D)), _full((1, N))],
        out_shape=[jax.ShapeDtypeStruct((S, N), BF16), jax.ShapeDtypeStruct((S, D), F32), jax.ShapeDtypeStruct((KA, D), F32),
                   jax.ShapeDtypeStruct((KB, D), F32), jax.ShapeDtypeStruct((8, D), F32), jax.ShapeDtypeStruct((1, N), F32)],
        scratch_shapes=[pltpu.VMEM((TM + HALO_A, D), F32), pltpu.VMEM((TM + HALO_B, D), F32), pltpu.VMEM((TM, D), F32),
                        pltpu.VMEM((TM, D), F32), pltpu.VMEM((TM, D), F32), pltpu.VMEM((KA, 8, D), F32), pltpu.VMEM((KB, 8, D), F32),
                        pltpu.VMEM((D, N), BF16), pltpu.SemaphoreType.DMA((nb,))],
        compiler_params=_cparams(1),
    )(du1, dq, z, z, z, dzc, x, dxp, mod, wa, wb, w_in_g)


PEER_REDUCED = ("w_up", "w_down", "w_o", "w_a_out", "w_b_out", "conv_a_w", "conv_b_w")


def _local_grads(me, x, c, tgt, w_ada16, w_in_g, shards, vecs):
    D = x.shape[1]
    rows = lambda a: a.reshape(a.shape[0] * a.shape[1], a.shape[2])
    cols = lambda a: jnp.transpose(a, (1, 0, 2)).reshape(a.shape[1], a.shape[0] * a.shape[2])
    mod = _ada_mod(me, c, w_ada16, vecs["b_ada"])
    z, h1, gathered = _fwd_in(x, mod, w_in_g, vecs["b_in"], shards)
    w_a_out, w_b_out, w_o, w_up_g, w_down = rows(gathered[0]), rows(gathered[1]), rows(gathered[2]), gathered[3], rows(gathered[4])
    wa, wb = cols(gathered[5]), cols(gathered[6])
    u1, ya, yb, q, out, r1, u2, v, mg = _fwd_mix(
        z, x, mod, wa, vecs["conv_a_b"], vecs["ln_a_g"], vecs["ln_a_b"], w_a_out, vecs["b_a_out"], wb, w_b_out, w_o, vecs["b_o"])
    dx1, h2, f, dhu, do2, acc_f, acc_up = _ffn(
        r1, tgt, mod, vecs["ln1_g"], vecs["ln1_b"], w_up_g, vecs["b_up"], w_down, vecs["b_down"], vecs["ln2_g"], vecs["ln2_b"])
    g_up, g_up16 = _dw(h2, dhu, "dw_up", nsplit=2, wire=True)
    g_down, g_down16 = _dw(f, do2, "dw_down", msplit=2, wire=True)
    dxp, du1, dq, dzc, doutb, dyab, dybb, acc_1 = _bwd_mix1(
        dx1, r1, out, ya, yb, q, u1, z, mod, vecs["ln1_g"], vecs["ln_a_g"], vecs["ln_a_b"], w_o, w_a_out, w_b_out)
    g_o, g_o16 = _dw(mg, doutb, "dw_o", wire=True)
    g_a_out, g_a_out16 = _dw(u2, dyab, "dw_a_out", wire=True)
    g_b_out, g_b_out16 = _dw(v, dybb, "dw_b_out", wire=True)
    dz, gx, g_wa, g_wb, acc_2, db_in = _bwd_mix2(du1, dq, z, dzc, x, dxp, mod, wa, wb, w_in_g)
    full = dict(zip(PEER_REDUCED, (g_up, g_down, g_o, g_a_out, g_b_out, g_wa, g_wb)))
    res = _dw(h1, dz, "dw_in", nsplit=4,
              exchange=((g_up16, g_down16, g_o16, g_a_out16, g_b_out16, g_wa, g_wb), [SHARD_KIND[n] for n in PEER_REDUCED]))
    g_in = res[0]
    peers = {n: (full[n], r) for n, r in zip(PEER_REDUCED, res[1:])}
    row = lambda acc, k: (acc, k, 0, D)
    small = _pack_rows(
        [row(acc_2, 1), row(acc_2, 0), row(acc_1, 2), row(acc_f, 5), row(acc_f, 4), row(acc_f, 3),
         (db_in, 0, 0, 2 * D), row(acc_1, 8), (db_in, 0, 3 * D, 2 * D), row(acc_1, 9), row(acc_1, 10),
         row(acc_1, 7), row(acc_1, 5), row(acc_1, 6), row(acc_1, 4), row(acc_1, 3), row(acc_1, 0), row(acc_1, 1),
         (acc_up, 0, 0, acc_up.shape[1]), row(acc_f, 0), row(acc_f, 1), row(acc_f, 2), (c, 0, 0, D)],
        "pack_vectors")
    return acc_f[6, 0], gx, g_in, peers, small


WEIGHTS = ("w_ada", "b_ada", "w_in", "b_in", "conv_a_w", "conv_a_b", "ln_a_g", "ln_a_b", "w_a_out", "b_a_out", "conv_b_w",
           "w_b_out", "w_o", "b_o", "ln1_g", "ln1_b", "w_up", "b_up", "w_down", "b_down", "ln2_g", "ln2_b")
VECTORS = ("b_ada", "b_in", "conv_a_b", "ln_a_g", "ln_a_b", "b_a_out", "b_o", "ln1_g", "ln1_b", "b_up", "b_down", "ln2_g", "ln2_b")
SHARD_KIND = {"w_in": "col", "w_a_out": "row", "w_b_out": "row", "w_o": "row", "w_up": "col", "w_down": "row",
              "conv_a_w": "col", "conv_b_w": "col"}


def kernel(x, c, w_ada, b_ada, w_in, b_in, conv_a_w, conv_a_b, ln_a_g, ln_a_b, w_a_out, b_a_out, conv_b_w, w_b_out, w_o, b_o, ln1_g, ln1_b, w_up, b_up, w_down, b_down, ln2_g, ln2_b, loss_target, m_w_ada, m_b_ada, m_w_in, m_b_in, m_conv_a_w, m_conv_a_b, m_ln_a_g, m_ln_a_b, m_w_a_out, m_b_a_out, m_conv_b_w, m_w_b_out, m_w_o, m_b_o, m_ln1_g, m_ln1_b, m_w_up, m_b_up, m_w_down, m_b_down, m_ln2_g, m_ln2_b, v_w_ada, v_b_ada, v_w_in, v_b_in, v_conv_a_w, v_conv_a_b, v_ln_a_g, v_ln_a_b, v_w_a_out, v_b_a_out, v_conv_b_w, v_w_b_out, v_w_o, v_b_o, v_ln1_g, v_ln1_b, v_w_up, v_b_up, v_w_down, v_b_down, v_ln2_g, v_ln2_b):
    args = dict(locals())
    w = {n: args[n][0] for n in WEIGHTS}
    m = {n: args["m_" + n][0] for n in WEIGHTS}
    v = {n: args["v_" + n][0] for n in WEIGHTS}
    w = {n: (a[None, :] if a.ndim == 1 else a) for n, a in w.items()}
    m = {n: (a[None, :] if a.ndim == 1 else a) for n, a in m.items()}
    v = {n: (a[None, :] if a.ndim == 1 else a) for n, a in v.items()}
    xs, tgt = x[0], loss_target[0]
    S, D = xs.shape
    me = (4 * lax.axis_index("x") + 2 * lax.axis_index("y") + lax.axis_index("c")).astype(jnp.int32).reshape(1)
    core = lax.axis_index("c").astype(jnp.int32).reshape(1)
    chip = (2 * lax.axis_index("x") + lax.axis_index("y")).astype(jnp.int32).reshape(1)

    bf = lambda n: w[n].astype(BF16)
    (w_in_g,) = _all_gather([bf("w_in")], "gather_weights")
    shards = [bf("w_a_out"), bf("w_b_out"), bf("w_o"), bf("w_up"), bf("w_down"), w["conv_a_w"], w["conv_b_w"]]
    vecs = {n: w[n] for n in VECTORS}

    loss, gx, g_in, peers, small = _local_grads(me, xs, c, tgt, bf("w_ada"), w_in_g, shards, vecs)

    (small_g,) = _all_gather([small], "gather_vectors")
    small_g = small_g.reshape(N_DEV, small.shape[1])
    (from_sibling,) = _exchange_cores([g_in], [SHARD_KIND["w_in"]], "reduce_cores")
    own_in, wire_in = _add_own(core, g_in, from_sibling, SHARD_KIND["w_in"], "add_w_in")
    sems, wire_thru, landing, after = _chips_start(wire_in)

    res = {}
    for n, (g_full, recv) in peers.items():
        res[n] = _sum_peers_adamw(me, g_full, recv, SHARD_KIND[n], w[n], m[n], v[n], "adamw_" + n, after)
        after = res[n][0]
    from_chips = _chips_wait(sems, wire_thru, landing, after)
    res["w_in"] = _sum_chips_adamw(chip, own_in, from_chips, w["w_in"], m["w_in"], v["w_in"], "adamw_w_in")
    W = w["w_ada"].shape[1]
    n_vec = small.shape[1] - D
    g_ada = _ada_bwd(me, small_g, D, W, n_vec // D)
    res["w_ada"] = _sum_adamw(g_ada[None], w["w_ada"], m["w_ada"], v["w_ada"], "adamw_w_ada")
    by_kind = _adamw_vectors(small_g, [w[n] for n in VECTORS], [m[n] for n in VECTORS], [v[n] for n in VECTORS])
    for i, n in enumerate(VECTORS):
        res[n] = tuple(by_kind[t][i] for t in range(4))

    loss = lax.psum(loss, ("x", "y", "c"))
    outs = [loss, gx[None]]
    for t in range(4):
        outs += [res[n][t].reshape(args[n].shape) for n in WEIGHTS]
    return tuple(outs)
```

```python
import functools

import jax
import jax.numpy as jnp
from jax import lax
from jax.experimental import pallas as pl
from jax.experimental.pallas import tpu as pltpu

F32 = jnp.float32
BF16 = jnp.bfloat16
MESH = pl.DeviceIdType.MESH

N_DEV = 8
LN_EPS = 1e-5
DEPTH = 1
ALPHA = (2.0 * DEPTH) ** 0.25
ADAM_LR, ADAM_B1, ADAM_B2, ADAM_EPS, ADAM_WD, ADAM_STEP = 0.001, 0.9, 0.999, 1e-08, 0.01, 10

VMEM_LIMIT = 60 * 1024 * 1024
TOKEN_TILE = 256
DW_TILE = 1024
HALO_A = 32
HALO_B = 8
CONV_ROWS, CONV_LANES = 32, 512


def _cparams(n_grid):
    return pltpu.CompilerParams(dimension_semantics=("arbitrary",) * n_grid, vmem_limit_bytes=VMEM_LIMIT)


def _full(shape):
    return pl.BlockSpec(shape, lambda *_: (0,) * len(shape))


ANY = pl.BlockSpec(memory_space=pl.ANY)


def _ln(x):
    mu = jnp.mean(x, axis=-1, keepdims=True)
    xc = x - mu
    var = jnp.mean(xc * xc, axis=-1, keepdims=True)
    rstd = lax.rsqrt(var + LN_EPS)
    return xc * rstd, rstd


def _ln_bwd(dxhat, xhat, rstd):
    m1 = jnp.mean(dxhat, axis=-1, keepdims=True)
    m2 = jnp.mean(dxhat * xhat, axis=-1, keepdims=True)
    return rstd * (dxhat - m1 - xhat * m2)


def _sigmoid(x):
    return 0.5 * jnp.tanh(0.5 * x) + 0.5


def _colsum(a):
    return jnp.sum(a, axis=0, keepdims=True)


def _dot(a, b):
    return jnp.dot(a, b, preferred_element_type=F32)


def _dot_nt(a, b):
    return lax.dot_general(a, b, (((1,), (1,)), ((), ())), preferred_element_type=F32)


def _dot_tn(a, b):
    return lax.dot_general(a, b, (((0,), (0,)), ((), ())), preferred_element_type=F32)


def _load_cols(src_hbm, dst_vmem, sems):
    nblk, _, w = src_hbm.shape
    cps = [pltpu.make_async_copy(src_hbm.at[j], dst_vmem.at[:, pl.ds(j * w, w)], sems.at[j]) for j in range(nblk)]
    for cp in cps:
        cp.start()
    for cp in cps:
        cp.wait()


def _load_whole(pairs, sems):
    cps = [pltpu.make_async_copy(s, d, sems.at[k]) for k, (s, d) in enumerate(pairs)]
    for cp in cps:
        cp.start()
    for cp in cps:
        cp.wait()


def _mesh_pos():
    return lax.axis_index("x"), lax.axis_index("y"), lax.axis_index("c")


def _all_gather(arrs, name):
    n = len(arrs)

    def body(*refs):
        ins, outs = refs[:n], refs[n:2 * n]
        send_sems, recv_sems, local_sems = refs[2 * n:]
        x, y, c = _mesh_pos()
        me, sibling = (x, y, c), (x, y, 1 - c)
        chips = [(1 - x, y), (x, 1 - y), (1 - x, 1 - y)]

        def slot(a, px, py, pc):
            return outs[a].at[4 * px + 2 * py + pc]

        def copy(a, k, block, to, src=None):
            return pltpu.make_async_remote_copy(
                src_ref=slot(a, *block) if src is None else src, dst_ref=slot(a, *block),
                send_sem=send_sems.at[a, k], recv_sem=recv_sems.at[a, k], device_id=to, device_id_type=MESH)

        mine = [pltpu.make_async_copy(ins[a], slot(a, *me), local_sems.at[a]) for a in range(n)]
        for cp in mine:
            cp.start()
        first = []
        for a in range(n):
            first.append(copy(a, 0, me, sibling, src=ins[a]))
            first += [copy(a, 1 + j, me, (*chip, c), src=ins[a]) for j, chip in enumerate(chips)]
        for cp in first:
            cp.start()
        passed = []
        for a in range(n):
            for j, chip in enumerate(chips):
                copy(a, 1 + j, (*chip, c), me).wait_recv()
                fwd = copy(a, 4 + j, (*chip, c), sibling)
                fwd.start()
                passed.append(fwd)
        for a in range(n):
            copy(a, 0, sibling, me).wait_recv()
            for j, chip in enumerate(chips):
                copy(a, 4 + j, (*chip, 1 - c), me).wait_recv()
        for cp in first + passed:
            cp.wait_send()
        for cp in mine:
            cp.wait()

    outs = pl.pallas_call(
        body, name=name,
        out_shape=[jax.ShapeDtypeStruct((N_DEV,) + a.shape, a.dtype) for a in arrs],
        in_specs=[ANY] * n, out_specs=[ANY] * n,
        scratch_shapes=[pltpu.SemaphoreType.DMA((n, 7)), pltpu.SemaphoreType.DMA((n, 7)), pltpu.SemaphoreType.DMA((n,))],
    )(*arrs)
    return list(outs)


def _peer_copies(src_of, dsts, send_sems, recv_sems):
    x, y, c = _mesh_pos()
    me = 4 * x + 2 * y + c
    sends, arrivals = [], []
    for a in range(len(dsts)):
        for k in range(1, N_DEV):
            px, py, pc = (1 - x if k & 4 else x), (1 - y if k & 2 else y), (1 - c if k & 1 else c)
            p = 4 * px + 2 * py + pc
            common = dict(send_sem=send_sems.at[a, k - 1], recv_sem=recv_sems.at[a, k - 1],
                          device_id=(px, py, pc), device_id_type=MESH)
            sends.append(pltpu.make_async_remote_copy(src_ref=src_of(a, p), dst_ref=dsts[a].at[me], **common))
            arrivals.append(pltpu.make_async_remote_copy(src_ref=dsts[a].at[p], dst_ref=dsts[a].at[p], **common))
    return sends, arrivals


def _peer_sems(n):
    return [pltpu.SemaphoreType.DMA((n, N_DEV - 1)), pltpu.SemaphoreType.DMA((n, N_DEV - 1))]


def _shard_of(ref, kind, j):
    if kind == "col":
        w = ref.shape[1] // N_DEV
        return ref.at[:, pl.ds(j * w, w)]
    h = ref.shape[0] // N_DEV
    return ref.at[pl.ds(j * h, h), :]


def _shard_shape(shape, kind):
    return (shape[0], shape[1] // N_DEV) if kind == "col" else (shape[0] // N_DEV, shape[1])


def _exchange_cores(grads, kinds, name):
    n = len(grads)

    def body(*refs):
        ins, outs = refs[:n], refs[n:2 * n]
        send_sems, recv_sems = refs[2 * n:]
        x, y, c = _mesh_pos()
        sibling = (x, y, 1 - c)
        sends = []
        for a in range(n):
            for s in range(4):
                sends.append(pltpu.make_async_remote_copy(
                    src_ref=_shard_of(ins[a], kinds[a], 2 * s + (1 - c)), dst_ref=outs[a].at[s],
                    send_sem=send_sems.at[a, s], recv_sem=recv_sems.at[a, s], device_id=sibling, device_id_type=MESH))
        for cp in sends:
            cp.start()
        for a in range(n):
            for s in range(4):
                pltpu.make_async_remote_copy(
                    src_ref=outs[a].at[s], dst_ref=outs[a].at[s],
                    send_sem=send_sems.at[a, s], recv_sem=recv_sems.at[a, s], device_id=sibling, device_id_type=MESH).wait_recv()
        for cp in sends:
            cp.wait_send()

    outs = pl.pallas_call(
        body, name=name,
        out_shape=[jax.ShapeDtypeStruct((4,) + _shard_shape(g.shape, k), F32) for g, k in zip(grads, kinds)],
        in_specs=[ANY] * n, out_specs=[ANY] * n,
        scratch_shapes=[pltpu.SemaphoreType.DMA((n, 4)), pltpu.SemaphoreType.DMA((n, 4))],
    )(*grads)
    return list(outs)


HBM = pl.BlockSpec(memory_space=pltpu.HBM)
SEM = pl.BlockSpec(memory_space=pltpu.SEMAPHORE)
DATAFLOW = pltpu.SideEffectType.DATAFLOW_SIDE_EFFECTING


def _chip_copies(src_ref, land_ref, send_sems, recv_sems):
    x, y, c = _mesh_pos()
    my_slot = 2 * x + y
    sends, arrivals = [], []
    for j, (px, py) in enumerate([(1 - x, y), (x, 1 - y), (1 - x, 1 - y)]):
        common = dict(send_sem=send_sems[j], recv_sem=recv_sems[j], device_id=(px, py, c), device_id_type=MESH)
        sends.append(pltpu.make_async_remote_copy(src_ref=src_ref.at[2 * px + py], dst_ref=land_ref.at[my_slot], **common))
        arrivals.append(pltpu.make_async_remote_copy(src_ref=src_ref.at[2 * px + py], dst_ref=land_ref.at[2 * px + py], **common))
    return sends, arrivals


def _chips_start(part):
    def body(src_ref, land_ref, s0, s1, s2, r0, r1, r2, src_thru, land_thru, token):
        sends, _ = _chip_copies(src_ref, land_ref, (s0, s1, s2), (r0, r1, r2))
        for cp in sends:
            cp.start()
        token[...] = jnp.zeros_like(token)

    res = pl.pallas_call(
        body, name="reduce_chips_start",
        out_shape=(pltpu.SemaphoreType.DMA(()),) * 6 + (pltpu.HBM(part.shape, part.dtype), pltpu.HBM(part.shape, part.dtype),
                                                        jax.ShapeDtypeStruct((8, 128), F32)),
        in_specs=(HBM, HBM), out_specs=(SEM,) * 6 + (HBM, HBM, pl.BlockSpec(memory_space=pltpu.VMEM)),
        input_output_aliases={0: 6, 1: 7}, compiler_params=pltpu.CompilerParams(has_side_effects=DATAFLOW),
    )(pltpu.with_memory_space_constraint(part, pltpu.HBM),
      pltpu.with_memory_space_constraint(lax.empty(part.shape, part.dtype), pltpu.HBM))
    return res[:6], res[6], res[7], res[8]


def _chips_wait(sems, src_thru, land_thru, after):
    def body(src_ref, land_ref, s0, s1, s2, r0, r1, r2, after_ref, src_dead, got_ref):
        sends, arrivals = _chip_copies(src_ref, land_ref, (s0, s1, s2), (r0, r1, r2))
        for cp in sends:
            cp.wait_send()
        for cp in arrivals:
            cp.wait_recv()

    return pl.pallas_call(
        body, name="reduce_chips_wait",
        out_shape=(pltpu.HBM(src_thru.shape, src_thru.dtype), pltpu.HBM(land_thru.shape, land_thru.dtype)),
        in_specs=(HBM, HBM) + (SEM,) * 6 + (ANY,), out_specs=(HBM, HBM), input_output_aliases={0: 0, 1: 1},
        compiler_params=pltpu.CompilerParams(has_side_effects=DATAFLOW),
    )(src_thru, land_thru, *sems, after)[1]


def _row_tile(rows):
    for t in (256, 128, 64, 32, 16, 8):
        if rows % t == 0:
            return t
    return rows


def _wire_dtype(rows):
    return BF16 if rows % 16 == 0 else F32


def _add_own(core, g, recv, kind, name):
    ns, R, C = recv.shape
    tr = _row_tile(R)
    nr = R // tr
    if kind == "col":
        g_spec = pl.BlockSpec((tr, C), lambda s, r, c_ref: (r, 2 * s + c_ref[0]))
    else:
        g_spec = pl.BlockSpec((tr, C), lambda s, r, c_ref: ((2 * s + c_ref[0]) * nr + r, 0))
    slab = pl.BlockSpec((None, tr, C), lambda s, r, c_ref: (s, r, 0))

    def body(c_ref, g_ref, r_ref, o32_ref, o16_ref):
        t = g_ref[...] + r_ref[...]
        o32_ref[...] = t
        o16_ref[...] = t.astype(o16_ref.dtype)

    return pl.pallas_call(
        body, name=name,
        grid_spec=pltpu.PrefetchScalarGridSpec(num_scalar_prefetch=1, grid=(ns, nr), in_specs=[g_spec, slab], out_specs=[slab, slab]),
        out_shape=[jax.ShapeDtypeStruct((ns, R, C), F32), jax.ShapeDtypeStruct((ns, R, C), _wire_dtype(R))],
        compiler_params=_cparams(2),
    )(core, g, recv)


def _adamw_math(w, g, m, v):
    m2 = ADAM_B1 * m + (1.0 - ADAM_B1) * g
    v2 = ADAM_B2 * v + (1.0 - ADAM_B2) * (g * g)
    m_hat = m2 / (1.0 - ADAM_B1 ** ADAM_STEP)
    v_hat = v2 / (1.0 - ADAM_B2 ** ADAM_STEP)
    delta = -ADAM_LR * (m_hat / (jnp.sqrt(v_hat) + ADAM_EPS) + ADAM_WD * w)
    return delta, m2, v2


def _sum_adamw(parts, w, m, v, name):
    n, R, C = parts.shape
    tr = _row_tile(R)

    def body(p_ref, w_ref, m_ref, v_ref, g_ref, d_ref, m_out, v_out):
        g = p_ref[0]
        for k in range(1, n):
            g = g + p_ref[k]
        g_ref[...] = g
        d_ref[...], m_out[...], v_out[...] = _adamw_math(w_ref[...], g, m_ref[...], v_ref[...])

    blk = pl.BlockSpec((tr, C), lambda r: (r, 0))
    return pl.pallas_call(
        body, name=name, grid=(R // tr,),
        in_specs=[pl.BlockSpec((n, tr, C), lambda r: (0, r, 0)), blk, blk, blk],
        out_specs=[blk] * 4, out_shape=[jax.ShapeDtypeStruct((R, C), F32)] * 4, compiler_params=_cparams(1),
    )(parts, w, m, v)


def _sum_chips_adamw(chip, own, recv, w, m, v, name):
    n, R, C = recv.shape
    tr = _row_tile(R)

    def body(chip_ref, own_ref, r_ref, w_ref, m_ref, v_ref, g_ref, d_ref, m_out, v_out):
        g = None
        for k in range(n):
            term = jnp.where(chip_ref[0] == k, own_ref[...], r_ref[k].astype(F32))
            g = term if g is None else g + term
        g_ref[...] = g
        d_ref[...], m_out[...], v_out[...] = _adamw_math(w_ref[...], g, m_ref[...], v_ref[...])

    blk = pl.BlockSpec((tr, C), lambda r, chip_ref: (r, 0))
    return pl.pallas_call(
        body, name=name,
        grid_spec=pltpu.PrefetchScalarGridSpec(
            num_scalar_prefetch=1, grid=(R // tr,),
            in_specs=[pl.BlockSpec((None, tr, C), lambda r, chip_ref: (chip_ref[0], r, 0)),
                      pl.BlockSpec((n, tr, C), lambda r, chip_ref: (0, r, 0)), blk, blk, blk],
            out_specs=[blk] * 4),
        out_shape=[jax.ShapeDtypeStruct((R, C), F32)] * 4, compiler_params=_cparams(1),
    )(chip, own, recv, w, m, v)


def _pack_rows(pieces, name, after):
    arrs = []
    for a, _, _, _ in pieces:
        if not any(a is b for b in arrs):
            arrs.append(a)
    which = [next(i for i, b in enumerate(arrs) if b is a) for a, _, _, _ in pieces]
    total = sum(n for _, _, _, n in pieces)

    def body(*refs):
        o_ref = refs[len(arrs) + 1]
        off = 0
        for (_, r, c0, n), i in zip(pieces, which):
            o_ref[:, off:off + n] = refs[i][r:r + 1, c0:c0 + n]
            off += n

    return pl.pallas_call(
        body, name=name, in_specs=[_full(a.shape) for a in arrs] + [ANY], out_specs=_full((1, total)),
        out_shape=jax.ShapeDtypeStruct((1, total), F32), grid=(1,), compiler_params=_cparams(1),
    )(*arrs, after)


def _adamw_vectors(parts, ws, ms, vs):
    nv = len(ws)
    widths = [a.shape[1] for a in ws]

    def body(*refs):
        p_ref = refs[0]
        w_refs, m_refs, v_refs = refs[1:1 + nv], refs[1 + nv:1 + 2 * nv], refs[1 + 2 * nv:1 + 3 * nv]
        outs = refs[1 + 3 * nv:]
        off = 0
        for i, n in enumerate(widths):
            g = p_ref[0:1, off:off + n]
            for k in range(1, p_ref.shape[0]):
                g = g + p_ref[k:k + 1, off:off + n]
            outs[i][...] = g
            outs[nv + i][...], outs[2 * nv + i][...], outs[3 * nv + i][...] = _adamw_math(w_refs[i][...], g, m_refs[i][...], v_refs[i][...])
            off += n

    vec_specs = [_full((1, n)) for n in widths]
    res = pl.pallas_call(
        body, name="adamw_vectors", grid=(1,),
        in_specs=[_full(parts.shape)] + vec_specs * 3, out_specs=vec_specs * 4,
        out_shape=[jax.ShapeDtypeStruct((1, n), F32) for n in widths] * 4, compiler_params=_cparams(1),
    )(parts, *ws, *ms, *vs)
    return [res[t * nv:(t + 1) * nv] for t in range(4)]


def _sum_peers_adamw(me, g, recv, kind, w, m, v, name, after):
    n, R, C = recv.shape
    tr = _row_tile(R)
    nr = R // tr
    if kind == "col":
        g_spec = pl.BlockSpec((tr, C), lambda r, me_ref: (r, me_ref[0]))
    else:
        g_spec = pl.BlockSpec((tr, C), lambda r, me_ref: (me_ref[0] * nr + r, 0))

    def body(me_ref, own_ref, r_ref, w_ref, m_ref, v_ref, after_ref, g_ref, d_ref, m_out, v_out):
        acc = None
        for k in range(n):
            term = jnp.where(me_ref[0] == k, own_ref[...], r_ref[k].astype(F32))
            acc = term if acc is None else acc + term
        g_ref[...] = acc
        d_ref[...], m_out[...], v_out[...] = _adamw_math(w_ref[...], acc, m_ref[...], v_ref[...])

    blk = pl.BlockSpec((tr, C), lambda r, me_ref: (r, 0))
    return pl.pallas_call(
        body, name=name,
        grid_spec=pltpu.PrefetchScalarGridSpec(
            num_scalar_prefetch=1, grid=(nr,),
            in_specs=[g_spec, pl.BlockSpec((n, tr, C), lambda r, me_ref: (0, r, 0)), blk, blk, blk, ANY], out_specs=[blk] * 4),
        out_shape=[jax.ShapeDtypeStruct((R, C), F32)] * 4, compiler_params=_cparams(1),
    )(me, g, recv, w, m, v, after)


def _ada_cols(me, c_all, w_ada16, b_ada):
    nd, D = c_all.shape
    W = w_ada16.shape[1]

    def body(me_ref, c_ref, w_ref, b_ref, o_ref):
        cv = c_ref[...]
        r = _dot((cv * _sigmoid(cv)).astype(BF16), w_ref[...]) + b_ref[...]
        for b in range(nd):
            o_ref[b] = r[b:b + 1, :]

    return pl.pallas_call(
        body, name="ada_cols",
        grid_spec=pltpu.PrefetchScalarGridSpec(
            num_scalar_prefetch=1, grid=(1,),
            in_specs=[_full((nd, D)), _full((D, W)), pl.BlockSpec((1, W), lambda i, me_ref: (0, me_ref[0]))],
            out_specs=_full((nd, 1, W))),
        out_shape=jax.ShapeDtypeStruct((nd, 1, W), F32), compiler_params=_cparams(1),
    )(me, c_all, w_ada16, b_ada)


def _rows_to_owners(part, name):
    def body(p_ref, o_ref, send_sems, recv_sems, local_sem):
        x, y, c = _mesh_pos()
        me = 4 * x + 2 * y + c
        mine = pltpu.make_async_copy(p_ref.at[me], o_ref.at[me], local_sem.at[0])
        sends, arrivals = _peer_copies(lambda a, p: p_ref.at[p], [o_ref], send_sems, recv_sems)
        mine.start()
        for cp in sends:
            cp.start()
        for cp in arrivals:
            cp.wait_recv()
        for cp in sends:
            cp.wait_send()
        mine.wait()

    return pl.pallas_call(
        body, name=name, in_specs=[ANY], out_specs=ANY, out_shape=jax.ShapeDtypeStruct(part.shape, part.dtype),
        scratch_shapes=_peer_sems(1) + [pltpu.SemaphoreType.DMA((1,))],
    )(part)


def _ada_mod(me, c_all, w_ada16, b_ada):
    part = _ada_cols(me, c_all, w_ada16, b_ada)
    return _rows_to_owners(part, "scatter_mod").reshape(1, N_DEV * w_ada16.shape[1])


def _ada_bwd(me, small_g, D, W, c_block):
    def body(me_ref, c_ref, dm_ref, o_ref):
        cv = c_ref[...]
        ca = (cv * _sigmoid(cv)).astype(BF16).astype(F32)
        dm = dm_ref[...].astype(BF16).astype(F32)
        o_ref[...] = lax.dot_general(ca, dm, (((0,), (0,)), ((), ())), precision=lax.Precision.HIGHEST,
                                     preferred_element_type=F32)

    return pl.pallas_call(
        body, name="ada_bwd",
        grid_spec=pltpu.PrefetchScalarGridSpec(
            num_scalar_prefetch=1, grid=(1,),
            in_specs=[pl.BlockSpec((N_DEV, D), lambda i, me_ref: (0, c_block)),
                      pl.BlockSpec((N_DEV, W), lambda i, me_ref: (0, me_ref[0]))],
            out_specs=pl.BlockSpec((D, W), lambda i, me_ref: (0, 0))),
        out_shape=jax.ShapeDtypeStruct((D, W), F32), compiler_params=_cparams(1),
    )(me, small_g, small_g)


def _fwd_in(x, mod, w_in_g, b_in, shards):
    S, D = x.shape
    nb, _, wb = w_in_g.shape
    N = nb * wb
    TM = TOKEN_TILE
    n_tiles = S // TM
    ns = len(shards)

    def body(*refs):
        x_ref, mod_ref, w_hbm, b_ref = refs[:4]
        ins = refs[4:4 + ns]
        z_ref, h_ref = refs[4 + ns:6 + ns]
        outs = refs[6 + ns:6 + 2 * ns]
        w_v, sems, send_sems, recv_sems, local_sems = refs[6 + 2 * ns:]
        i = pl.program_id(0)
        x_, y_, c_ = _mesh_pos()
        me = 4 * x_ + 2 * y_ + c_
        mine = [pltpu.make_async_copy(ins[a], outs[a].at[me], local_sems.at[a]) for a in range(ns)]
        sends, arrivals = _peer_copies(lambda a, p: ins[a], outs, send_sems, recv_sems)

        @pl.when(i == 0)
        def _():
            for cp in sends + mine:
                cp.start()
            _load_cols(w_hbm, w_v, sems)

        xhat, _ = _ln(x_ref[...])
        hb = (xhat * (1.0 + mod_ref[:, D:2 * D]) + mod_ref[:, 0:D]).astype(BF16)
        h_ref[...] = hb
        z_ref[...] = _dot(hb, w_v[...]) + b_ref[...]

        @pl.when(i == n_tiles - 1)
        def _():
            for cp in arrivals:
                cp.wait_recv()
            for cp in sends:
                cp.wait_send()
            for cp in mine:
                cp.wait()

    res = pl.pallas_call(
        body, name="fwd_in", grid=(n_tiles,),
        in_specs=[pl.BlockSpec((TM, D), lambda i: (i, 0)), _full(mod.shape), ANY, _full((1, N))] + [ANY] * ns,
        out_specs=[pl.BlockSpec((TM, N), lambda i: (i, 0)), pl.BlockSpec((TM, D), lambda i: (i, 0))] + [ANY] * ns,
        out_shape=[jax.ShapeDtypeStruct((S, N), F32), jax.ShapeDtypeStruct((S, D), BF16)]
        + [jax.ShapeDtypeStruct((N_DEV,) + a.shape, a.dtype) for a in shards],
        scratch_shapes=[pltpu.VMEM((D, N), BF16), pltpu.SemaphoreType.DMA((nb,))] + _peer_sems(ns) + [pltpu.SemaphoreType.DMA((ns,))],
        compiler_params=_cparams(1),
    )(x, mod, w_in_g, b_in, *shards)
    return res[0], res[1], list(res[2:])


def _rows_from(win, o, R):
    if o % 8 == 0:
        return win[o:o + R, :]
    return pltpu.roll(win, win.shape[0] - o, axis=0)[0:R, :]


def _causal_taps(buf, w_ref, o_ref, K, halo, TM, D):
    R, C = CONV_ROWS, CONV_LANES
    base = halo - (K - 1)

    for r0 in range(0, TM, R):
        for c0 in range(0, D, C):
            win = buf[r0:r0 + R + halo, c0:c0 + C]
            acc = jnp.zeros((R, C), F32)
            for k in range(K):
                acc = acc + w_ref[k:k + 1, c0:c0 + C] * _rows_from(win, base + k, R)
            o_ref[r0:r0 + R, c0:c0 + C] = acc


def _fwd_mix(z, x, mod, wa, ba, lag, lab, w_a_out, b_a_out, wb, w_b_out, w_o, b_o):
    S, D = x.shape
    N = z.shape[1]
    TM = TOKEN_TILE
    KA, KB = wa.shape[0], wb.shape[0]

    def body(z_ref, x_ref, mod_ref, wa_ref, ba_ref, lag_ref, lab_ref, wao_hbm, bao_ref, wb_ref, wbo_hbm, wo_hbm, bo_ref,
             u1_ref, ya_ref, yb_ref, q_ref, out_ref, r1_ref, u2_ref, v_ref, mg_ref,
             ubuf, pbuf, qbuf, wao, wbo, wo, sems):
        i = pl.program_id(0)

        @pl.when(i == 0)
        def _():
            _load_whole([(wao_hbm, wao), (wbo_hbm, wbo), (wo_hbm, wo)], sems)
            ubuf[0:HALO_A, :] = jnp.zeros((HALO_A, D), F32)
            pbuf[0:HALO_B, :] = jnp.zeros((HALO_B, D), F32)

        ubuf[HALO_A:HALO_A + TM, :] = z_ref[:, 0:D] * _sigmoid(z_ref[:, D:2 * D])
        _causal_taps(ubuf, wa_ref, u1_ref, KA, HALO_A, TM, D)
        ubuf[0:HALO_A, :] = ubuf[TM:TM + HALO_A, :]
        u1 = u1_ref[...] + ba_ref[...]
        u1_ref[...] = u1
        xa, _ = _ln(u1)
        l = xa * lag_ref[...] + lab_ref[...]
        u2 = (l * _sigmoid(l)).astype(BF16)
        u2_ref[...] = u2
        ya = _dot(u2, wao[...]) + bao_ref[...]
        ya_ref[...] = ya.astype(BF16)
        pbuf[HALO_B:HALO_B + TM, :] = z_ref[:, 3 * D:4 * D] * z_ref[:, 4 * D:5 * D]
        _causal_taps(pbuf, wb_ref, qbuf, KB, HALO_B, TM, D)
        pbuf[0:HALO_B, :] = pbuf[TM:TM + HALO_B, :]
        q_ref[...] = qbuf[...].astype(BF16)
        v = (z_ref[:, 2 * D:3 * D] * qbuf[...]).astype(BF16)
        v_ref[...] = v
        yb = _dot(v, wbo[...])
        yb_ref[...] = yb.astype(BF16)
        mg = (_sigmoid(z_ref[:, 5 * D:6 * D]) * ya + _sigmoid(z_ref[:, 6 * D:7 * D]) * yb).astype(BF16)
        mg_ref[...] = mg
        out = _dot(mg, wo[...]) + bo_ref[...]
        out_ref[...] = out.astype(BF16)
        r1_ref[...] = ALPHA * x_ref[...] + (1.0 + mod_ref[:, 2 * D:3 * D]) * out

    tile = pl.BlockSpec((TM, D), lambda i: (i, 0))
    vec = _full((1, D))
    return pl.pallas_call(
        body, name="fwd_mix", grid=(S // TM,),
        in_specs=[pl.BlockSpec((TM, N), lambda i: (i, 0)), tile, _full(mod.shape), _full((KA, D)), vec, vec, vec, ANY, vec,
                  _full((KB, D)), ANY, ANY, vec],
        out_specs=[tile] * 9,
        out_shape=[jax.ShapeDtypeStruct((S, D), dt) for dt in (F32, BF16, BF16, BF16, BF16, F32, BF16, BF16, BF16)],
        scratch_shapes=[pltpu.VMEM((TM + HALO_A, D), F32), pltpu.VMEM((TM + HALO_B, D), F32), pltpu.VMEM((TM, D), F32),
                        pltpu.VMEM((D, D), BF16), pltpu.VMEM((D, D), BF16), pltpu.VMEM((D, D), BF16),
                        pltpu.SemaphoreType.DMA((3,))],
        compiler_params=_cparams(1),
    )(z, x, mod, wa, ba, lag, lab, w_a_out, b_a_out, wb, w_b_out, w_o, b_o)


FFN_ROWS = ("b_down", "ln2_g", "ln2_b", "gate2", "scale2", "shift2", "loss")


def _ffn(r1, tgt, mod, ln1_g, ln1_b, w_up_g, b_up, w_down, b_down, ln2_g, ln2_b):
    S, D = r1.shape
    nb, _, wb = w_up_g.shape
    FF = nb * wb
    TM = TOKEN_TILE
    CF = 1024
    n_tiles = S // TM

    def body(r1_ref, t_ref, mod_ref, g1_ref, b1_ref, wup_hbm, bup_ref, wdn_hbm, bdn_ref, g2_ref, b2_ref,
             dx1_ref, h2_ref, f_ref, dhu_ref, do2_ref, acc_ref, accup_ref,
             wup, wdn, relu_buf, sems, sem2):
        i = pl.program_id(0)

        @pl.when(i == 0)
        def _():
            _load_cols(wup_hbm, wup, sems)
            _load_whole([(wdn_hbm, wdn)], sem2)
            acc_ref[...] = jnp.zeros(acc_ref.shape, F32)
            accup_ref[...] = jnp.zeros(accup_ref.shape, F32)

        scale2, shift2, gate2 = mod_ref[:, 4 * D:5 * D], mod_ref[:, 3 * D:4 * D], mod_ref[:, 5 * D:6 * D]
        xhat1, _ = _ln(r1_ref[...])
        x1 = xhat1 * g1_ref[...] + b1_ref[...]
        xh0, rstd0 = _ln(x1)
        h2 = (xh0 * (1.0 + scale2) + shift2).astype(BF16)
        h2_ref[...] = h2
        out2 = jnp.zeros((TM, D), F32) + bdn_ref[...]
        for c0 in range(0, FF, CF):
            hu = _dot(h2, wup[:, c0:c0 + CF]) + bup_ref[:, c0:c0 + CF]
            rl = jnp.maximum(hu, 0.0)
            relu_buf[:, c0:c0 + CF] = rl
            fb = (rl * rl).astype(BF16)
            f_ref[:, c0:c0 + CF] = fb
            out2 = out2 + _dot(fb, wdn[c0:c0 + CF, :])
        r2 = ALPHA * x1 + (1.0 + gate2) * out2
        xh2, rstd2 = _ln(r2)
        e = xh2 * g2_ref[...] + b2_ref[...] - t_ref[...]
        acc_ref[6:7, :] += _colsum(e * e)
        dy = e * (1.0 / D)
        acc_ref[1:2, :] += _colsum(dy * xh2)
        acc_ref[2:3, :] += _colsum(dy)
        dr2 = _ln_bwd(dy * g2_ref[...], xh2, rstd2)
        acc_ref[3:4, :] += _colsum(dr2 * out2)
        do2 = (1.0 + gate2) * dr2
        acc_ref[0:1, :] += _colsum(do2)
        do2b = do2.astype(BF16)
        do2_ref[...] = do2b
        dh2 = jnp.zeros((TM, D), F32)
        for c0 in range(0, FF, CF):
            dhu = _dot_nt(do2b, wdn[c0:c0 + CF, :]) * (2.0 * relu_buf[:, c0:c0 + CF])
            accup_ref[:, c0:c0 + CF] += _colsum(dhu)
            dhub = dhu.astype(BF16)
            dhu_ref[:, c0:c0 + CF] = dhub
            dh2 = dh2 + _dot_nt(dhub, wup[:, c0:c0 + CF])
        acc_ref[4:5, :] += _colsum(dh2 * xh0)
        acc_ref[5:6, :] += _colsum(dh2)
        dx1_ref[...] = ALPHA * dr2 + _ln_bwd(dh2 * (1.0 + scale2), xh0, rstd0)

        @pl.when(i == n_tiles - 1)
        def _():
            tot = jnp.sum(acc_ref[6:7, :], axis=-1, keepdims=True) * (0.5 / D)
            acc_ref[6:7, :] = jnp.broadcast_to(tot, (1, D))

    tile = pl.BlockSpec((TM, D), lambda i: (i, 0))
    wide = pl.BlockSpec((TM, FF), lambda i: (i, 0))
    vec = _full((1, D))
    return pl.pallas_call(
        body, name="ffn", grid=(n_tiles,),
        in_specs=[tile, tile, _full(mod.shape), vec, vec, ANY, _full((1, FF)), ANY, vec, vec, vec],
        out_specs=[tile, tile, wide, wide, tile, _full((8, D)), _full((1, FF))],
        out_shape=[jax.ShapeDtypeStruct((S, D), F32), jax.ShapeDtypeStruct((S, D), BF16), jax.ShapeDtypeStruct((S, FF), BF16),
                   jax.ShapeDtypeStruct((S, FF), BF16), jax.ShapeDtypeStruct((S, D), BF16),
                   jax.ShapeDtypeStruct((8, D), F32), jax.ShapeDtypeStruct((1, FF), F32)],
        scratch_shapes=[pltpu.VMEM((D, FF), BF16), pltpu.VMEM((FF, D), BF16), pltpu.VMEM((TM, FF), F32),
                        pltpu.SemaphoreType.DMA((nb,)), pltpu.SemaphoreType.DMA((1,))],
        compiler_params=_cparams(1),
    )(r1, tgt, mod, ln1_g, ln1_b, w_up_g, b_up, w_down, b_down, ln2_g, ln2_b)


def _dw(a, b, name, msplit=1, nsplit=1, wire=False, exchange=None):
    S, M = a.shape
    N = b.shape[1]
    TK = min(DW_TILE, S)
    mb, nbk = M // msplit, N // nsplit
    nk = S // TK
    srcs, kinds = exchange if exchange else ((), ())
    ns = len(srcs)

    def body(*refs):
        a_ref, b_ref = refs[:2]
        ins = refs[2:2 + ns]
        o_ref = refs[2 + ns]
        o16_ref = refs[3 + ns] if wire else None
        rest = refs[3 + ns + (1 if wire else 0):]
        k = pl.program_id(2)
        if ns:
            outs, (send_sems, recv_sems) = rest[:ns], rest[ns:]
            sends, arrivals = _peer_copies(lambda q, p: _shard_of(ins[q], kinds[q], p), outs, send_sems, recv_sems)
            first = (pl.program_id(0) == 0) & (pl.program_id(1) == 0) & (k == 0)
            last = (pl.program_id(0) == msplit - 1) & (pl.program_id(1) == nsplit - 1) & (k == nk - 1)

            @pl.when(first)
            def _():
                for cp in sends:
                    cp.start()

        @pl.when(k == 0)
        def _():
            o_ref[...] = jnp.zeros(o_ref.shape, F32)

        o_ref[...] += _dot_tn(a_ref[...], b_ref[...])

        if wire:
            @pl.when(k == nk - 1)
            def _():
                o16_ref[...] = o_ref[...].astype(BF16)

        if ns:
            @pl.when(last)
            def _():
                for cp in arrivals:
                    cp.wait_recv()
                for cp in sends:
                    cp.wait_send()

    oblk = pl.BlockSpec((mb, nbk), lambda i, j, k: (i, j))
    res = pl.pallas_call(
        body, name=name, grid=(msplit, nsplit, nk),
        in_specs=[pl.BlockSpec((TK, mb), lambda i, j, k: (k, i)), pl.BlockSpec((TK, nbk), lambda i, j, k: (k, j))] + [ANY] * ns,
        out_specs=[oblk] + ([oblk] if wire else []) + [ANY] * ns,
        out_shape=[jax.ShapeDtypeStruct((M, N), F32)] + ([jax.ShapeDtypeStruct((M, N), BF16)] if wire else [])
        + [jax.ShapeDtypeStruct((N_DEV,) + _shard_shape(g.shape, kd), g.dtype) for g, kd in zip(srcs, kinds)],
        scratch_shapes=_peer_sems(ns) if ns else [],
        compiler_params=_cparams(3),
    )(a, b, *srcs)
    return res if (wire or ns) else res[0]


BWD1_ROWS = ("ln1_g", "ln1_b", "gate1", "b_o", "b_a_out", "ln_a_g", "ln_a_b", "conv_a_b", "b_in_gb", "b_in_ga", "b_in_gbb")


def _bwd_mix1(dx1, r1, out, ya, yb, q, u1, z, mod, ln1_g, lag, lab, w_o, w_a_out, w_b_out):
    S, D = dx1.shape
    TM = TOKEN_TILE

    def body(dx1_ref, r1_ref, out_ref, ya_ref, yb_ref, q_ref, u1_ref, zgb_ref, zga_ref, zgg_ref, mod_ref, g1_ref, lag_ref, lab_ref,
             wo_hbm, wao_hbm, wbo_hbm,
             dxp_ref, du1_ref, dq_ref, dzc_ref, dout_ref, dya_ref, dyb_ref, acc_ref,
             wo, wao, wbo, sems):
        @pl.when(pl.program_id(0) == 0)
        def _():
            _load_whole([(wo_hbm, wo), (wao_hbm, wao), (wbo_hbm, wbo)], sems)
            acc_ref[...] = jnp.zeros(acc_ref.shape, F32)

        dx1v = dx1_ref[...]
        xhat1, rstd1 = _ln(r1_ref[...])
        acc_ref[0:1, :] += _colsum(dx1v * xhat1)
        acc_ref[1:2, :] += _colsum(dx1v)
        dr1 = _ln_bwd(dx1v * g1_ref[...], xhat1, rstd1)
        dxp_ref[...] = (ALPHA * dr1).astype(BF16)
        acc_ref[2:3, :] += _colsum(dr1 * out_ref[...].astype(F32))
        dout = (1.0 + mod_ref[:, 2 * D:3 * D]) * dr1
        acc_ref[3:4, :] += _colsum(dout)
        doutb = dout.astype(BF16)
        dout_ref[...] = doutb
        dmg = _dot_nt(doutb, wo[...])
        sga, sgb = _sigmoid(zga_ref[...]), _sigmoid(zgg_ref[...])
        dga = dmg * ya_ref[...].astype(F32) * sga * (1.0 - sga)
        dgb = dmg * yb_ref[...].astype(F32) * sgb * (1.0 - sgb)
        acc_ref[9:10, :] += _colsum(dga)
        acc_ref[10:11, :] += _colsum(dgb)
        dzc_ref[:, D:2 * D] = dga.astype(BF16)
        dzc_ref[:, 2 * D:3 * D] = dgb.astype(BF16)
        dya = dmg * sga
        acc_ref[4:5, :] += _colsum(dya)
        dyab = dya.astype(BF16)
        dya_ref[...] = dyab
        dybb = (dmg * sgb).astype(BF16)
        dyb_ref[...] = dybb
        du2 = _dot_nt(dyab, wao[...])
        xa, rstda = _ln(u1_ref[...])
        l = xa * lag_ref[...] + lab_ref[...]
        sl = _sigmoid(l)
        dl = du2 * (sl * (1.0 + l * (1.0 - sl)))
        acc_ref[5:6, :] += _colsum(dl * xa)
        acc_ref[6:7, :] += _colsum(dl)
        du1 = _ln_bwd(dl * lag_ref[...], xa, rstda)
        acc_ref[7:8, :] += _colsum(du1)
        du1_ref[...] = du1.astype(BF16)
        dv = _dot_nt(dybb, wbo[...])
        dgbk = dv * q_ref[...].astype(F32)
        acc_ref[8:9, :] += _colsum(dgbk)
        dzc_ref[:, 0:D] = dgbk.astype(BF16)
        dq_ref[...] = (dv * zgb_ref[...]).astype(BF16)

    tile = pl.BlockSpec((TM, D), lambda i: (i, 0))
    vec = _full((1, D))
    zcol = lambda k: pl.BlockSpec((TM, D), lambda i: (i, k))
    return pl.pallas_call(
        body, name="bwd_mix1", grid=(S // TM,),
        in_specs=[tile] * 7 + [zcol(2), zcol(5), zcol(6), _full(mod.shape), vec, vec, vec, ANY, ANY, ANY],
        out_specs=[tile, tile, tile, pl.BlockSpec((TM, 3 * D), lambda i: (i, 0)), tile, tile, tile, _full((16, D))],
        out_shape=[jax.ShapeDtypeStruct((S, D), BF16)] * 3 + [jax.ShapeDtypeStruct((S, 3 * D), BF16)]
        + [jax.ShapeDtypeStruct((S, D), BF16)] * 3 + [jax.ShapeDtypeStruct((16, D), F32)],
        scratch_shapes=[pltpu.VMEM((D, D), BF16)] * 3 + [pltpu.SemaphoreType.DMA((3,))],
        compiler_params=_cparams(1),
    )(dx1, r1, out, ya, yb, q, u1, z, z, z, mod, ln1_g, lag, lab, w_o, w_a_out, w_b_out)


def _anticausal_taps(dbuf, u_ref, w_ref, o_ref, dwacc, K, halo, TM, D):
    R, C = CONV_ROWS, CONV_LANES

    for r0 in range(0, TM, R):
        for c0 in range(0, D, C):
            win = dbuf[r0:r0 + R + halo, c0:c0 + C]
            uc = u_ref[r0:r0 + R, c0:c0 + C]
            acc = jnp.zeros((R, C), F32)
            for j in range(K):
                sh = _rows_from(win, j, R)
                k = K - 1 - j
                acc = acc + w_ref[k:k + 1, c0:c0 + C] * sh
                pr = uc * sh
                part = pr[0:8, :]
                for s in range(8, R, 8):
                    part = part + pr[s:s + 8, :]
                dwacc[k, :, c0:c0 + C] += part
            o_ref[r0:r0 + R, c0:c0 + C] = acc


def _bwd_mix2(du1, dq, z, dzc, x, dxp, mod, wa, wb, w_in_g):
    S, D = x.shape
    N = z.shape[1]
    nb, _, wbk = w_in_g.shape
    TM = TOKEN_TILE
    KA, KB = wa.shape[0], wb.shape[0]
    n_tiles = S // TM

    def body(du1_ref, dq_ref, zav_ref, zc_ref, zx_ref, dzc_ref, x_ref, dxp_ref, mod_ref, wa_ref, wb_ref, win_hbm,
             dz_ref, gx_ref, dwa_ref, dwb_ref, acc_ref, dbin_ref,
             dbuf, qbuf, ubuf, pbuf, obuf, dwa_acc, dwb_acc, w_v, sems):
        i = pl.program_id(0)

        @pl.when(i == 0)
        def _():
            _load_cols(win_hbm, w_v, sems)
            dbuf[TM:TM + HALO_A, :] = jnp.zeros((HALO_A, D), F32)
            qbuf[TM:TM + HALO_B, :] = jnp.zeros((HALO_B, D), F32)
            dwa_acc[...] = jnp.zeros(dwa_acc.shape, F32)
            dwb_acc[...] = jnp.zeros(dwb_acc.shape, F32)
            acc_ref[...] = jnp.zeros(acc_ref.shape, F32)
            dbin_ref[...] = jnp.zeros(dbin_ref.shape, F32)

        a_val = zav_ref[:, 0:D]
        sa = _sigmoid(zav_ref[:, D:2 * D])
        ubuf[...] = a_val * sa
        dbuf[0:TM, :] = du1_ref[...].astype(F32)
        _anticausal_taps(dbuf, ubuf, wa_ref, obuf, dwa_acc, KA, HALO_A, TM, D)
        dbuf[TM:TM + HALO_A, :] = dbuf[0:HALO_A, :]
        du0 = obuf[...]
        dav = du0 * sa
        dag = du0 * a_val * sa * (1.0 - sa)
        dbin_ref[:, 0:D] += _colsum(dav)
        dbin_ref[:, D:2 * D] += _colsum(dag)
        dz_ref[:, 0:D] = dav.astype(BF16)
        dz_ref[:, D:2 * D] = dag.astype(BF16)
        pbuf[...] = zc_ref[...] * zx_ref[...]
        qbuf[0:TM, :] = dq_ref[...].astype(F32)
        _anticausal_taps(qbuf, pbuf, wb_ref, obuf, dwb_acc, KB, HALO_B, TM, D)
        qbuf[TM:TM + HALO_B, :] = qbuf[0:HALO_B, :]
        dp = obuf[...]
        dgc = dp * zx_ref[...]
        dgx = dp * zc_ref[...]
        dbin_ref[:, 3 * D:4 * D] += _colsum(dgc)
        dbin_ref[:, 4 * D:5 * D] += _colsum(dgx)
        dz_ref[:, 3 * D:4 * D] = dgc.astype(BF16)
        dz_ref[:, 4 * D:5 * D] = dgx.astype(BF16)
        dz_ref[:, 2 * D:3 * D] = dzc_ref[:, 0:D]
        dz_ref[:, 5 * D:7 * D] = dzc_ref[:, D:3 * D]
        dh = _dot_nt(dz_ref[...], w_v[...])
        xhat, rstd = _ln(x_ref[...])
        acc_ref[0:1, :] += _colsum(dh * xhat)
        acc_ref[1:2, :] += _colsum(dh)
        gx_ref[...] = dxp_ref[...].astype(F32) + _ln_bwd(dh * (1.0 + mod_ref[:, D:2 * D]), xhat, rstd)

        @pl.when(i == n_tiles - 1)
        def _():
            for k in range(KA):
                dwa_ref[k:k + 1, :] = jnp.sum(dwa_acc[k], axis=0, keepdims=True)
            for k in range(KB):
                dwb_ref[k:k + 1, :] = jnp.sum(dwb_acc[k], axis=0, keepdims=True)

    rev = lambda i: n_tiles - 1 - i
    tile = pl.BlockSpec((TM, D), lambda i: (rev(i), 0))
    zcol = lambda k: pl.BlockSpec((TM, D), lambda i: (rev(i), k))
    return pl.pallas_call(
        body, name="bwd_mix2", grid=(n_tiles,),
        in_specs=[tile, tile, pl.BlockSpec((TM, 2 * D), lambda i: (rev(i), 0)), zcol(3), zcol(4),
                  pl.BlockSpec((TM, 3 * D), lambda i: (rev(i), 0)), tile, tile, _full(mod.shape), _full((KA, D)), _full((KB, D)), ANY],
        out_specs=[pl.BlockSpec((TM, N), lambda i: (rev(i), 0)), tile, _full((KA, D)), _full((KB, D)), _full((8, D)), _full((1, N))],
        out_shape=[jax.ShapeDtypeStruct((S, N), BF16), jax.ShapeDtypeStruct((S, D), F32), jax.ShapeDtypeStruct((KA, D), F32),
                   jax.ShapeDtypeStruct((KB, D), F32), jax.ShapeDtypeStruct((8, D), F32), jax.ShapeDtypeStruct((1, N), F32)],
        scratch_shapes=[pltpu.VMEM((TM + HALO_A, D), F32), pltpu.VMEM((TM + HALO_B, D), F32), pltpu.VMEM((TM, D), F32),
                        pltpu.VMEM((TM, D), F32), pltpu.VMEM((TM, D), F32), pltpu.VMEM((KA, 8, D), F32), pltpu.VMEM((KB, 8, D), F32),
                        pltpu.VMEM((D, N), BF16), pltpu.SemaphoreType.DMA((nb,))],
        compiler_params=_cparams(1),
    )(du1, dq, z, z, z, dzc, x, dxp, mod, wa, wb, w_in_g)


PEER_REDUCED = ("w_up", "w_down", "w_o", "w_a_out", "w_b_out", "conv_a_w", "conv_b_w")


def _local_grads(me, x, c, c_all, tgt, w_ada16, w_in_g, shards, vecs):
    D = x.shape[1]
    rows = lambda a: a.reshape(a.shape[0] * a.shape[1], a.shape[2])
    cols = lambda a: jnp.transpose(a, (1, 0, 2)).reshape(a.shape[1], a.shape[0] * a.shape[2])
    mod = _ada_mod(me, c_all, w_ada16, vecs["b_ada"])
    z, h1, gathered = _fwd_in(x, mod, w_in_g, vecs["b_in"], shards)
    w_a_out, w_b_out, w_o, w_up_g, w_down = rows(gathered[0]), rows(gathered[1]), rows(gathered[2]), gathered[3], rows(gathered[4])
    wa, wb = cols(gathered[5]), cols(gathered[6])
    u1, ya, yb, q, out, r1, u2, v, mg = _fwd_mix(
        z, x, mod, wa, vecs["conv_a_b"], vecs["ln_a_g"], vecs["ln_a_b"], w_a_out, vecs["b_a_out"], wb, w_b_out, w_o, vecs["b_o"])
    dx1, h2, f, dhu, do2, acc_f, acc_up = _ffn(
        r1, tgt, mod, vecs["ln1_g"], vecs["ln1_b"], w_up_g, vecs["b_up"], w_down, vecs["b_down"], vecs["ln2_g"], vecs["ln2_b"])
    g_up, g_up16 = _dw(h2, dhu, "dw_up", nsplit=2, wire=True)
    g_down, g_down16 = _dw(f, do2, "dw_down", msplit=2, wire=True)
    dxp, du1, dq, dzc, doutb, dyab, dybb, acc_1 = _bwd_mix1(
        dx1, r1, out, ya, yb, q, u1, z, mod, vecs["ln1_g"], vecs["ln_a_g"], vecs["ln_a_b"], w_o, w_a_out, w_b_out)
    g_o, g_o16 = _dw(mg, doutb, "dw_o", wire=True)
    g_a_out, g_a_out16 = _dw(u2, dyab, "dw_a_out", wire=True)
    g_b_out, g_b_out16 = _dw(v, dybb, "dw_b_out", wire=True)
    dz, gx, g_wa, g_wb, acc_2, db_in = _bwd_mix2(du1, dq, z, dzc, x, dxp, mod, wa, wb, w_in_g)
    full = dict(zip(PEER_REDUCED, (g_up, g_down, g_o, g_a_out, g_b_out, g_wa, g_wb)))
    res = _dw(h1, dz, "dw_in", nsplit=4,
              exchange=((g_up16, g_down16, g_o16, g_a_out16, g_b_out16, g_wa, g_wb), [SHARD_KIND[n] for n in PEER_REDUCED]))
    g_in = res[0]
    peers = {n: (full[n], r) for n, r in zip(PEER_REDUCED, res[1:])}
    row = lambda acc, k: (acc, k, 0, D)
    pieces = [
        row(acc_2, 1), row(acc_2, 0), row(acc_1, 2), row(acc_f, 5), row(acc_f, 4), row(acc_f, 3),
        (db_in, 0, 0, 2 * D), row(acc_1, 8), (db_in, 0, 3 * D, 2 * D), row(acc_1, 9), row(acc_1, 10),
        row(acc_1, 7), row(acc_1, 5), row(acc_1, 6), row(acc_1, 4), row(acc_1, 3), row(acc_1, 0), row(acc_1, 1),
        (acc_up, 0, 0, acc_up.shape[1]), row(acc_f, 0), row(acc_f, 1), row(acc_f, 2), (c, 0, 0, D)]
    return acc_f[6, 0], gx, g_in, peers, pieces


WEIGHTS = ("w_ada", "b_ada", "w_in", "b_in", "conv_a_w", "conv_a_b", "ln_a_g", "ln_a_b", "w_a_out", "b_a_out", "conv_b_w",
           "w_b_out", "w_o", "b_o", "ln1_g", "ln1_b", "w_up", "b_up", "w_down", "b_down", "ln2_g", "ln2_b")
VECTORS = ("b_ada", "b_in", "conv_a_b", "ln_a_g", "ln_a_b", "b_a_out", "b_o", "ln1_g", "ln1_b", "b_up", "b_down", "ln2_g", "ln2_b")
SHARD_KIND = {"w_in": "col", "w_a_out": "row", "w_b_out": "row", "w_o": "row", "w_up": "col", "w_down": "row",
              "conv_a_w": "col", "conv_b_w": "col"}


def kernel(x, c, w_ada, b_ada, w_in, b_in, conv_a_w, conv_a_b, ln_a_g, ln_a_b, w_a_out, b_a_out, conv_b_w, w_b_out, w_o, b_o, ln1_g, ln1_b, w_up, b_up, w_down, b_down, ln2_g, ln2_b, loss_target, m_w_ada, m_b_ada, m_w_in, m_b_in, m_conv_a_w, m_conv_a_b, m_ln_a_g, m_ln_a_b, m_w_a_out, m_b_a_out, m_conv_b_w, m_w_b_out, m_w_o, m_b_o, m_ln1_g, m_ln1_b, m_w_up, m_b_up, m_w_down, m_b_down, m_ln2_g, m_ln2_b, v_w_ada, v_b_ada, v_w_in, v_b_in, v_conv_a_w, v_conv_a_b, v_ln_a_g, v_ln_a_b, v_w_a_out, v_b_a_out, v_conv_b_w, v_w_b_out, v_w_o, v_b_o, v_ln1_g, v_ln1_b, v_w_up, v_b_up, v_w_down, v_b_down, v_ln2_g, v_ln2_b):
    args = dict(locals())
    w = {n: args[n][0] for n in WEIGHTS}
    m = {n: args["m_" + n][0] for n in WEIGHTS}
    v = {n: args["v_" + n][0] for n in WEIGHTS}
    w = {n: (a[None, :] if a.ndim == 1 else a) for n, a in w.items()}
    m = {n: (a[None, :] if a.ndim == 1 else a) for n, a in m.items()}
    v = {n: (a[None, :] if a.ndim == 1 else a) for n, a in v.items()}
    xs, tgt = x[0], loss_target[0]
    S, D = xs.shape
    me = (4 * lax.axis_index("x") + 2 * lax.axis_index("y") + lax.axis_index("c")).astype(jnp.int32).reshape(1)
    core = lax.axis_index("c").astype(jnp.int32).reshape(1)
    chip = (2 * lax.axis_index("x") + lax.axis_index("y")).astype(jnp.int32).reshape(1)

    bf = lambda n: w[n].astype(BF16)
    w_in_g, c_all = _all_gather([bf("w_in"), c], "gather_weights")
    shards = [bf("w_a_out"), bf("w_b_out"), bf("w_o"), bf("w_up"), bf("w_down"), w["conv_a_w"], w["conv_b_w"]]
    vecs = {n: w[n] for n in VECTORS}

    loss, gx, g_in, peers, pieces = _local_grads(me, xs, c, c_all.reshape(N_DEV, D), tgt, bf("w_ada"), w_in_g, shards, vecs)

    (from_sibling,) = _exchange_cores([g_in], [SHARD_KIND["w_in"]], "reduce_cores")
    own_in, wire_in = _add_own(core, g_in, from_sibling, SHARD_KIND["w_in"], "add_w_in")
    sems, wire_thru, landing, after = _chips_start(wire_in)

    res = {}
    for n, (g_full, recv) in peers.items():
        res[n] = _sum_peers_adamw(me, g_full, recv, SHARD_KIND[n], w[n], m[n], v[n], "adamw_" + n, after)
        after = res[n][0]
    small = _pack_rows(pieces, "pack_vectors", after)
    (small_g,) = _all_gather([small], "gather_vectors")
    small_g = small_g.reshape(N_DEV, small.shape[1])
    W = w["w_ada"].shape[1]
    n_vec = small.shape[1] - D
    g_ada = _ada_bwd(me, small_g, D, W, n_vec // D)
    res["w_ada"] = _sum_adamw(g_ada[None], w["w_ada"], m["w_ada"], v["w_ada"], "adamw_w_ada")
    by_kind = _adamw_vectors(small_g, [w[n] for n in VECTORS], [m[n] for n in VECTORS], [v[n] for n in VECTORS])
    for i, n in enumerate(VECTORS):
        res[n] = tuple(by_kind[t][i] for t in range(4))
    from_chips = _chips_wait(sems, wire_thru, landing, res["w_ada"][0])
    res["w_in"] = _sum_chips_adamw(chip, own_in, from_chips, w["w_in"], m["w_in"], v["w_in"], "adamw_w_in")

    loss = lax.psum(loss, ("x", "y", "c"))
    outs = [loss, gx[None]]
    for t in range(4):
        outs += [res[n][t].reshape(args[n].shape) for n in WEIGHTS]
    return tuple(outs)
```

```python
import functools

import jax
import jax.numpy as jnp
from jax import lax
from jax.experimental import pallas as pl
from jax.experimental.pallas import tpu as pltpu

F32 = jnp.float32
BF16 = jnp.bfloat16
MESH = pl.DeviceIdType.MESH

N_DEV = 8
LN_EPS = 1e-5
DEPTH = 1
ALPHA = (2.0 * DEPTH) ** 0.25
ADAM_LR, ADAM_B1, ADAM_B2, ADAM_EPS, ADAM_WD, ADAM_STEP = 0.001, 0.9, 0.999, 1e-08, 0.01, 10

VMEM_LIMIT = 60 * 1024 * 1024
TOKEN_TILE = 256
DW_TILE = 1024
HALO_A = 32
HALO_B = 8
CONV_ROWS, CONV_LANES = 32, 512
FFN_CHUNK = 1024


def _cparams(n_grid):
    return pltpu.CompilerParams(dimension_semantics=("arbitrary",) * n_grid, vmem_limit_bytes=VMEM_LIMIT)


def _full(shape):
    return pl.BlockSpec(shape, lambda *_: (0,) * len(shape))


ANY = pl.BlockSpec(memory_space=pl.ANY)


def _ln(x):
    mu = jnp.mean(x, axis=-1, keepdims=True)
    xc = x - mu
    var = jnp.mean(xc * xc, axis=-1, keepdims=True)
    rstd = lax.rsqrt(var + LN_EPS)
    return xc * rstd, rstd


def _ln_bwd(dxhat, xhat, rstd):
    m1 = jnp.mean(dxhat, axis=-1, keepdims=True)
    m2 = jnp.mean(dxhat * xhat, axis=-1, keepdims=True)
    return rstd * (dxhat - m1 - xhat * m2)


def _sigmoid(x):
    return 0.5 * jnp.tanh(0.5 * x) + 0.5


def _colsum(a):
    return jnp.sum(a, axis=0, keepdims=True)


def _dot(a, b):
    return jnp.dot(a, b, preferred_element_type=F32)


def _dot_nt(a, b):
    return lax.dot_general(a, b, (((1,), (1,)), ((), ())), preferred_element_type=F32)


def _dot_tn(a, b):
    return lax.dot_general(a, b, (((0,), (0,)), ((), ())), preferred_element_type=F32)


def _load_cols(src_hbm, dst_vmem, sems):
    nblk, _, w = src_hbm.shape
    cps = [pltpu.make_async_copy(src_hbm.at[j], dst_vmem.at[:, pl.ds(j * w, w)], sems.at[j]) for j in range(nblk)]
    for cp in cps:
        cp.start()
    for cp in cps:
        cp.wait()


def _load_whole(pairs, sems):
    cps = [pltpu.make_async_copy(s, d, sems.at[k]) for k, (s, d) in enumerate(pairs)]
    for cp in cps:
        cp.start()
    for cp in cps:
        cp.wait()


def _mesh_pos():
    return lax.axis_index("x"), lax.axis_index("y"), lax.axis_index("c")


def _all_gather(arrs, name):
    n = len(arrs)

    def body(*refs):
        ins, outs = refs[:n], refs[n:2 * n]
        send_sems, recv_sems, local_sems = refs[2 * n:]
        x, y, c = _mesh_pos()
        me, sibling = (x, y, c), (x, y, 1 - c)
        chips = [(1 - x, y), (x, 1 - y), (1 - x, 1 - y)]

        def slot(a, px, py, pc):
            return outs[a].at[4 * px + 2 * py + pc]

        def copy(a, k, block, to, src=None):
            return pltpu.make_async_remote_copy(
                src_ref=slot(a, *block) if src is None else src, dst_ref=slot(a, *block),
                send_sem=send_sems.at[a, k], recv_sem=recv_sems.at[a, k], device_id=to, device_id_type=MESH)

        mine = [pltpu.make_async_copy(ins[a], slot(a, *me), local_sems.at[a]) for a in range(n)]
        for cp in mine:
            cp.start()
        first = []
        for a in range(n):
            first.append(copy(a, 0, me, sibling, src=ins[a]))
            first += [copy(a, 1 + j, me, (*chip, c), src=ins[a]) for j, chip in enumerate(chips)]
        for cp in first:
            cp.start()
        passed = []
        for a in range(n):
            for j, chip in enumerate(chips):
                copy(a, 1 + j, (*chip, c), me).wait_recv()
                fwd = copy(a, 4 + j, (*chip, c), sibling)
                fwd.start()
                passed.append(fwd)
        for a in range(n):
            copy(a, 0, sibling, me).wait_recv()
            for j, chip in enumerate(chips):
                copy(a, 4 + j, (*chip, 1 - c), me).wait_recv()
        for cp in first + passed:
            cp.wait_send()
        for cp in mine:
            cp.wait()

    outs = pl.pallas_call(
        body, name=name,
        out_shape=[jax.ShapeDtypeStruct((N_DEV,) + a.shape, a.dtype) for a in arrs],
        in_specs=[ANY] * n, out_specs=[ANY] * n,
        scratch_shapes=[pltpu.SemaphoreType.DMA((n, 7)), pltpu.SemaphoreType.DMA((n, 7)), pltpu.SemaphoreType.DMA((n,))],
    )(*arrs)
    return list(outs)


def _peer_copies(src_of, dsts, send_sems, recv_sems):
    x, y, c = _mesh_pos()
    me = 4 * x + 2 * y + c
    sends, arrivals = [], []
    for a in range(len(dsts)):
        for k in range(1, N_DEV):
            px, py, pc = (1 - x if k & 4 else x), (1 - y if k & 2 else y), (1 - c if k & 1 else c)
            p = 4 * px + 2 * py + pc
            common = dict(send_sem=send_sems.at[a, k - 1], recv_sem=recv_sems.at[a, k - 1],
                          device_id=(px, py, pc), device_id_type=MESH)
            sends.append(pltpu.make_async_remote_copy(src_ref=src_of(a, p), dst_ref=dsts[a].at[me], **common))
            arrivals.append(pltpu.make_async_remote_copy(src_ref=dsts[a].at[p], dst_ref=dsts[a].at[p], **common))
    return sends, arrivals


def _peer_sems(n):
    return [pltpu.SemaphoreType.DMA((n, N_DEV - 1)), pltpu.SemaphoreType.DMA((n, N_DEV - 1))]


def _shard_of(ref, kind, j):
    if kind == "col":
        w = ref.shape[1] // N_DEV
        return ref.at[:, pl.ds(j * w, w)]
    h = ref.shape[0] // N_DEV
    return ref.at[pl.ds(j * h, h), :]


def _shard_shape(shape, kind):
    return (shape[0], shape[1] // N_DEV) if kind == "col" else (shape[0] // N_DEV, shape[1])


def _exchange_cores(grads, kinds, name):
    n = len(grads)

    def body(*refs):
        ins, outs = refs[:n], refs[n:2 * n]
        send_sems, recv_sems = refs[2 * n:]
        x, y, c = _mesh_pos()
        sibling = (x, y, 1 - c)
        sends = []
        for a in range(n):
            for s in range(4):
                sends.append(pltpu.make_async_remote_copy(
                    src_ref=_shard_of(ins[a], kinds[a], 2 * s + (1 - c)), dst_ref=outs[a].at[s],
                    send_sem=send_sems.at[a, s], recv_sem=recv_sems.at[a, s], device_id=sibling, device_id_type=MESH))
        for cp in sends:
            cp.start()
        for a in range(n):
            for s in range(4):
                pltpu.make_async_remote_copy(
                    src_ref=outs[a].at[s], dst_ref=outs[a].at[s],
                    send_sem=send_sems.at[a, s], recv_sem=recv_sems.at[a, s], device_id=sibling, device_id_type=MESH).wait_recv()
        for cp in sends:
            cp.wait_send()

    outs = pl.pallas_call(
        body, name=name,
        out_shape=[jax.ShapeDtypeStruct((4,) + _shard_shape(g.shape, k), F32) for g, k in zip(grads, kinds)],
        in_specs=[ANY] * n, out_specs=[ANY] * n,
        scratch_shapes=[pltpu.SemaphoreType.DMA((n, 4)), pltpu.SemaphoreType.DMA((n, 4))],
    )(*grads)
    return list(outs)


HBM = pl.BlockSpec(memory_space=pltpu.HBM)
SEM = pl.BlockSpec(memory_space=pltpu.SEMAPHORE)
DATAFLOW = pltpu.SideEffectType.DATAFLOW_SIDE_EFFECTING


def _chip_copies(src_ref, land_ref, send_sems, recv_sems):
    x, y, c = _mesh_pos()
    my_slot = 2 * x + y
    sends, arrivals = [], []
    for j, (px, py) in enumerate([(1 - x, y), (x, 1 - y), (1 - x, 1 - y)]):
        common = dict(send_sem=send_sems[j], recv_sem=recv_sems[j], device_id=(px, py, c), device_id_type=MESH)
        sends.append(pltpu.make_async_remote_copy(src_ref=src_ref.at[2 * px + py], dst_ref=land_ref.at[my_slot], **common))
        arrivals.append(pltpu.make_async_remote_copy(src_ref=src_ref.at[2 * px + py], dst_ref=land_ref.at[2 * px + py], **common))
    return sends, arrivals


def _chips_start(part):
    def body(src_ref, land_ref, s0, s1, s2, r0, r1, r2, src_thru, land_thru, token):
        sends, _ = _chip_copies(src_ref, land_ref, (s0, s1, s2), (r0, r1, r2))
        for cp in sends:
            cp.start()
        token[...] = jnp.zeros_like(token)

    res = pl.pallas_call(
        body, name="reduce_chips_start",
        out_shape=(pltpu.SemaphoreType.DMA(()),) * 6 + (pltpu.HBM(part.shape, part.dtype), pltpu.HBM(part.shape, part.dtype),
                                                        jax.ShapeDtypeStruct((8, 128), F32)),
        in_specs=(HBM, HBM), out_specs=(SEM,) * 6 + (HBM, HBM, pl.BlockSpec(memory_space=pltpu.VMEM)),
        input_output_aliases={0: 6, 1: 7}, compiler_params=pltpu.CompilerParams(has_side_effects=DATAFLOW),
    )(pltpu.with_memory_space_constraint(part, pltpu.HBM),
      pltpu.with_memory_space_constraint(lax.empty(part.shape, part.dtype), pltpu.HBM))
    return res[:6], res[6], res[7], res[8]


def _chips_wait(sems, src_thru, land_thru, after):
    def body(src_ref, land_ref, s0, s1, s2, r0, r1, r2, after_ref, src_dead, got_ref):
        sends, arrivals = _chip_copies(src_ref, land_ref, (s0, s1, s2), (r0, r1, r2))
        for cp in sends:
            cp.wait_send()
        for cp in arrivals:
            cp.wait_recv()

    return pl.pallas_call(
        body, name="reduce_chips_wait",
        out_shape=(pltpu.HBM(src_thru.shape, src_thru.dtype), pltpu.HBM(land_thru.shape, land_thru.dtype)),
        in_specs=(HBM, HBM) + (SEM,) * 6 + (ANY,), out_specs=(HBM, HBM), input_output_aliases={0: 0, 1: 1},
        compiler_params=pltpu.CompilerParams(has_side_effects=DATAFLOW),
    )(src_thru, land_thru, *sems, after)[1]


def _row_tile(rows):
    for t in (256, 128, 64, 32, 16, 8):
        if rows % t == 0:
            return t
    return rows


def _wire_dtype(rows):
    return BF16 if rows % 16 == 0 else F32


def _add_own(core, g, recv, kind, name):
    ns, R, C = recv.shape
    tr = _row_tile(R)
    nr = R // tr
    if kind == "col":
        g_spec = pl.BlockSpec((tr, C), lambda s, r, c_ref: (r, 2 * s + c_ref[0]))
    else:
        g_spec = pl.BlockSpec((tr, C), lambda s, r, c_ref: ((2 * s + c_ref[0]) * nr + r, 0))
    slab = pl.BlockSpec((None, tr, C), lambda s, r, c_ref: (s, r, 0))

    def body(c_ref, g_ref, r_ref, o32_ref, o16_ref):
        t = g_ref[...] + r_ref[...]
        o32_ref[...] = t
        o16_ref[...] = t.astype(o16_ref.dtype)

    return pl.pallas_call(
        body, name=name,
        grid_spec=pltpu.PrefetchScalarGridSpec(num_scalar_prefetch=1, grid=(ns, nr), in_specs=[g_spec, slab], out_specs=[slab, slab]),
        out_shape=[jax.ShapeDtypeStruct((ns, R, C), F32), jax.ShapeDtypeStruct((ns, R, C), _wire_dtype(R))],
        compiler_params=_cparams(2),
    )(core, g, recv)


def _adamw_math(w, g, m, v):
    m2 = ADAM_B1 * m + (1.0 - ADAM_B1) * g
    v2 = ADAM_B2 * v + (1.0 - ADAM_B2) * (g * g)
    m_hat = m2 / (1.0 - ADAM_B1 ** ADAM_STEP)
    v_hat = v2 / (1.0 - ADAM_B2 ** ADAM_STEP)
    delta = -ADAM_LR * (m_hat / (jnp.sqrt(v_hat) + ADAM_EPS) + ADAM_WD * w)
    return delta, m2, v2


def _sum_adamw(parts, w, m, v, name):
    n, R, C = parts.shape
    tr = _row_tile(R)

    def body(p_ref, w_ref, m_ref, v_ref, g_ref, d_ref, m_out, v_out):
        g = p_ref[0]
        for k in range(1, n):
            g = g + p_ref[k]
        g_ref[...] = g
        d_ref[...], m_out[...], v_out[...] = _adamw_math(w_ref[...], g, m_ref[...], v_ref[...])

    blk = pl.BlockSpec((tr, C), lambda r: (r, 0))
    return pl.pallas_call(
        body, name=name, grid=(R // tr,),
        in_specs=[pl.BlockSpec((n, tr, C), lambda r: (0, r, 0)), blk, blk, blk],
        out_specs=[blk] * 4, out_shape=[jax.ShapeDtypeStruct((R, C), F32)] * 4, compiler_params=_cparams(1),
    )(parts, w, m, v)


def _sum_chips_adamw(chip, own, recv, w, m, v, name):
    n, R, C = recv.shape
    tr = _row_tile(R)

    def body(chip_ref, own_ref, r_ref, w_ref, m_ref, v_ref, g_ref, d_ref, m_out, v_out):
        g = None
        for k in range(n):
            term = jnp.where(chip_ref[0] == k, own_ref[...], r_ref[k].astype(F32))
            g = term if g is None else g + term
        g_ref[...] = g
        d_ref[...], m_out[...], v_out[...] = _adamw_math(w_ref[...], g, m_ref[...], v_ref[...])

    blk = pl.BlockSpec((tr, C), lambda r, chip_ref: (r, 0))
    return pl.pallas_call(
        body, name=name,
        grid_spec=pltpu.PrefetchScalarGridSpec(
            num_scalar_prefetch=1, grid=(R // tr,),
            in_specs=[pl.BlockSpec((None, tr, C), lambda r, chip_ref: (chip_ref[0], r, 0)),
                      pl.BlockSpec((n, tr, C), lambda r, chip_ref: (0, r, 0)), blk, blk, blk],
            out_specs=[blk] * 4),
        out_shape=[jax.ShapeDtypeStruct((R, C), F32)] * 4, compiler_params=_cparams(1),
    )(chip, own, recv, w, m, v)


def _pack_rows(pieces, name, after):
    arrs = []
    for a, _, _, _ in pieces:
        if not any(a is b for b in arrs):
            arrs.append(a)
    which = [next(i for i, b in enumerate(arrs) if b is a) for a, _, _, _ in pieces]
    total = sum(n for _, _, _, n in pieces)

    def body(*refs):
        o_ref = refs[len(arrs) + 1]
        off = 0
        for (_, r, c0, n), i in zip(pieces, which):
            o_ref[:, off:off + n] = refs[i][r:r + 1, c0:c0 + n]
            off += n

    return pl.pallas_call(
        body, name=name, in_specs=[_full(a.shape) for a in arrs] + [ANY], out_specs=_full((1, total)),
        out_shape=jax.ShapeDtypeStruct((1, total), F32), grid=(1,), compiler_params=_cparams(1),
    )(*arrs, after)


def _adamw_vectors(parts, ws, ms, vs):
    nv = len(ws)
    widths = [a.shape[1] for a in ws]

    def body(*refs):
        p_ref = refs[0]
        w_refs, m_refs, v_refs = refs[1:1 + nv], refs[1 + nv:1 + 2 * nv], refs[1 + 2 * nv:1 + 3 * nv]
        outs = refs[1 + 3 * nv:]
        off = 0
        for i, n in enumerate(widths):
            g = p_ref[0:1, off:off + n]
            for k in range(1, p_ref.shape[0]):
                g = g + p_ref[k:k + 1, off:off + n]
            outs[i][...] = g
            outs[nv + i][...], outs[2 * nv + i][...], outs[3 * nv + i][...] = _adamw_math(w_refs[i][...], g, m_refs[i][...], v_refs[i][...])
            off += n

    vec_specs = [_full((1, n)) for n in widths]
    res = pl.pallas_call(
        body, name="adamw_vectors", grid=(1,),
        in_specs=[_full(parts.shape)] + vec_specs * 3, out_specs=vec_specs * 4,
        out_shape=[jax.ShapeDtypeStruct((1, n), F32) for n in widths] * 4, compiler_params=_cparams(1),
    )(parts, *ws, *ms, *vs)
    return [res[t * nv:(t + 1) * nv] for t in range(4)]


def _sum_peers_adamw(me, g, recv, kind, w, m, v, name, after):
    n, R, C = recv.shape
    tr = _row_tile(R)
    nr = R // tr
    if kind == "col":
        g_spec = pl.BlockSpec((tr, C), lambda r, me_ref: (r, me_ref[0]))
    else:
        g_spec = pl.BlockSpec((tr, C), lambda r, me_ref: (me_ref[0] * nr + r, 0))

    def body(me_ref, own_ref, r_ref, w_ref, m_ref, v_ref, after_ref, g_ref, d_ref, m_out, v_out):
        acc = None
        for k in range(n):
            term = jnp.where(me_ref[0] == k, own_ref[...], r_ref[k].astype(F32))
            acc = term if acc is None else acc + term
        g_ref[...] = acc
        d_ref[...], m_out[...], v_out[...] = _adamw_math(w_ref[...], acc, m_ref[...], v_ref[...])

    blk = pl.BlockSpec((tr, C), lambda r, me_ref: (r, 0))
    return pl.pallas_call(
        body, name=name,
        grid_spec=pltpu.PrefetchScalarGridSpec(
            num_scalar_prefetch=1, grid=(nr,),
            in_specs=[g_spec, pl.BlockSpec((n, tr, C), lambda r, me_ref: (0, r, 0)), blk, blk, blk, ANY], out_specs=[blk] * 4),
        out_shape=[jax.ShapeDtypeStruct((R, C), F32)] * 4, compiler_params=_cparams(1),
    )(me, g, recv, w, m, v, after)


def _ada_cols(me, c_all, w_ada16, b_ada):
    nd, D = c_all.shape
    W = w_ada16.shape[1]

    def body(me_ref, c_ref, w_ref, b_ref, o_ref):
        cv = c_ref[...]
        r = _dot((cv * _sigmoid(cv)).astype(BF16), w_ref[...]) + b_ref[...]
        for b in range(nd):
            o_ref[b] = r[b:b + 1, :]

    return pl.pallas_call(
        body, name="ada_cols",
        grid_spec=pltpu.PrefetchScalarGridSpec(
            num_scalar_prefetch=1, grid=(1,),
            in_specs=[_full((nd, D)), _full((D, W)), pl.BlockSpec((1, W), lambda i, me_ref: (0, me_ref[0]))],
            out_specs=_full((nd, 1, W))),
        out_shape=jax.ShapeDtypeStruct((nd, 1, W), F32), compiler_params=_cparams(1),
    )(me, c_all, w_ada16, b_ada)


def _rows_to_owners(part, name):
    def body(p_ref, o_ref, send_sems, recv_sems, local_sem):
        x, y, c = _mesh_pos()
        me = 4 * x + 2 * y + c
        mine = pltpu.make_async_copy(p_ref.at[me], o_ref.at[me], local_sem.at[0])
        sends, arrivals = _peer_copies(lambda a, p: p_ref.at[p], [o_ref], send_sems, recv_sems)
        mine.start()
        for cp in sends:
            cp.start()
        for cp in arrivals:
            cp.wait_recv()
        for cp in sends:
            cp.wait_send()
        mine.wait()

    return pl.pallas_call(
        body, name=name, in_specs=[ANY], out_specs=ANY, out_shape=jax.ShapeDtypeStruct(part.shape, part.dtype),
        scratch_shapes=_peer_sems(1) + [pltpu.SemaphoreType.DMA((1,))],
    )(part)


def _ada_mod(me, c_all, w_ada16, b_ada):
    part = _ada_cols(me, c_all, w_ada16, b_ada)
    return _rows_to_owners(part, "scatter_mod").reshape(1, N_DEV * w_ada16.shape[1])


def _ada_bwd(me, small_g, D, W, c_block):
    def body(me_ref, c_ref, dm_ref, o_ref):
        cv = c_ref[...]
        ca = (cv * _sigmoid(cv)).astype(BF16).astype(F32)
        dm = dm_ref[...].astype(BF16).astype(F32)
        o_ref[...] = lax.dot_general(ca, dm, (((0,), (0,)), ((), ())), precision=lax.Precision.HIGHEST,
                                     preferred_element_type=F32)

    return pl.pallas_call(
        body, name="ada_bwd",
        grid_spec=pltpu.PrefetchScalarGridSpec(
            num_scalar_prefetch=1, grid=(1,),
            in_specs=[pl.BlockSpec((N_DEV, D), lambda i, me_ref: (0, c_block)),
                      pl.BlockSpec((N_DEV, W), lambda i, me_ref: (0, me_ref[0]))],
            out_specs=pl.BlockSpec((D, W), lambda i, me_ref: (0, 0))),
        out_shape=jax.ShapeDtypeStruct((D, W), F32), compiler_params=_cparams(1),
    )(me, small_g, small_g)


def _fwd_in(x, mod, w_in_g, b_in, shards):
    S, D = x.shape
    nb, _, wb = w_in_g.shape
    N = nb * wb
    TM = TOKEN_TILE
    n_tiles = S // TM
    ns = len(shards)

    def body(*refs):
        x_ref, mod_ref, w_hbm, b_ref = refs[:4]
        ins = refs[4:4 + ns]
        z_ref, h_ref, zg_ref = refs[4 + ns:7 + ns]
        outs = refs[7 + ns:7 + 2 * ns]
        w_v, sems, send_sems, recv_sems, local_sems = refs[7 + 2 * ns:]
        i = pl.program_id(0)
        x_, y_, c_ = _mesh_pos()
        me = 4 * x_ + 2 * y_ + c_
        mine = [pltpu.make_async_copy(ins[a], outs[a].at[me], local_sems.at[a]) for a in range(ns)]
        sends, arrivals = _peer_copies(lambda a, p: ins[a], outs, send_sems, recv_sems)

        @pl.when(i == 0)
        def _():
            for cp in sends + mine:
                cp.start()
            _load_cols(w_hbm, w_v, sems)

        xhat, _ = _ln(x_ref[...])
        hb = (xhat * (1.0 + mod_ref[:, D:2 * D]) + mod_ref[:, 0:D]).astype(BF16)
        h_ref[...] = hb
        z_ref[...] = _dot(hb, w_v[...]) + b_ref[...]
        zg_ref[:, 0:D] = z_ref[:, 2 * D:3 * D].astype(BF16)
        zg_ref[:, D:3 * D] = z_ref[:, 5 * D:7 * D].astype(BF16)

        @pl.when(i == n_tiles - 1)
        def _():
            for cp in arrivals:
                cp.wait_recv()
            for cp in sends:
                cp.wait_send()
            for cp in mine:
                cp.wait()

    res = pl.pallas_call(
        body, name="fwd_in", grid=(n_tiles,),
        in_specs=[pl.BlockSpec((TM, D), lambda i: (i, 0)), _full(mod.shape), ANY, _full((1, N))] + [ANY] * ns,
        out_specs=[pl.BlockSpec((TM, N), lambda i: (i, 0)), pl.BlockSpec((TM, D), lambda i: (i, 0)),
                   pl.BlockSpec((TM, 3 * D), lambda i: (i, 0))] + [ANY] * ns,
        out_shape=[jax.ShapeDtypeStruct((S, N), F32), jax.ShapeDtypeStruct((S, D), BF16), jax.ShapeDtypeStruct((S, 3 * D), BF16)]
        + [jax.ShapeDtypeStruct((N_DEV,) + a.shape, a.dtype) for a in shards],
        scratch_shapes=[pltpu.VMEM((D, N), BF16), pltpu.SemaphoreType.DMA((nb,))] + _peer_sems(ns) + [pltpu.SemaphoreType.DMA((ns,))],
        compiler_params=_cparams(1),
    )(x, mod, w_in_g, b_in, *shards)
    return res[0], res[1], res[2], list(res[3:])


def _rows_from(win, o, R):
    if o % 8 == 0:
        return win[o:o + R, :]
    return pltpu.roll(win, win.shape[0] - o, axis=0)[0:R, :]


def _causal_taps(buf, w_ref, o_ref, K, halo, TM, D):
    R, C = CONV_ROWS, CONV_LANES
    base = halo - (K - 1)

    for r0 in range(0, TM, R):
        for c0 in range(0, D, C):
            win = buf[r0:r0 + R + halo, c0:c0 + C]
            acc = jnp.zeros((R, C), F32)
            for k in range(K):
                acc = acc + w_ref[k:k + 1, c0:c0 + C] * _rows_from(win, base + k, R)
            o_ref[r0:r0 + R, c0:c0 + C] = acc


def _fwd_mix(z, x, mod, wa, ba, lag, lab, w_a_out, b_a_out, wb, w_b_out, w_o, b_o):
    S, D = x.shape
    N = z.shape[1]
    TM = TOKEN_TILE
    KA, KB = wa.shape[0], wb.shape[0]

    def body(z_ref, x_ref, mod_ref, wa_ref, ba_ref, lag_ref, lab_ref, wao_hbm, bao_ref, wb_ref, wbo_hbm, wo_hbm, bo_ref,
             u1_ref, ya_ref, yb_ref, q_ref, out_ref, r1_ref, u2_ref, v_ref, mg_ref,
             ubuf, pbuf, qbuf, wao, wbo, wo, sems):
        i = pl.program_id(0)

        @pl.when(i == 0)
        def _():
            _load_whole([(wao_hbm, wao), (wbo_hbm, wbo), (wo_hbm, wo)], sems)
            ubuf[0:HALO_A, :] = jnp.zeros((HALO_A, D), F32)
            pbuf[0:HALO_B, :] = jnp.zeros((HALO_B, D), F32)

        ubuf[HALO_A:HALO_A + TM, :] = z_ref[:, 0:D] * _sigmoid(z_ref[:, D:2 * D])
        _causal_taps(ubuf, wa_ref, u1_ref, KA, HALO_A, TM, D)
        ubuf[0:HALO_A, :] = ubuf[TM:TM + HALO_A, :]
        u1 = u1_ref[...] + ba_ref[...]
        u1_ref[...] = u1
        xa, _ = _ln(u1)
        l = xa * lag_ref[...] + lab_ref[...]
        u2 = (l * _sigmoid(l)).astype(BF16)
        u2_ref[...] = u2
        ya = _dot(u2, wao[...]) + bao_ref[...]
        ya_ref[...] = ya.astype(BF16)
        pbuf[HALO_B:HALO_B + TM, :] = z_ref[:, 3 * D:4 * D] * z_ref[:, 4 * D:5 * D]
        _causal_taps(pbuf, wb_ref, qbuf, KB, HALO_B, TM, D)
        pbuf[0:HALO_B, :] = pbuf[TM:TM + HALO_B, :]
        q_ref[...] = qbuf[...].astype(BF16)
        v = (z_ref[:, 2 * D:3 * D] * qbuf[...]).astype(BF16)
        v_ref[...] = v
        yb = _dot(v, wbo[...])
        yb_ref[...] = yb.astype(BF16)
        mg = (_sigmoid(z_ref[:, 5 * D:6 * D]) * ya + _sigmoid(z_ref[:, 6 * D:7 * D]) * yb).astype(BF16)
        mg_ref[...] = mg
        out = _dot(mg, wo[...]) + bo_ref[...]
        out_ref[...] = out.astype(BF16)
        r1_ref[...] = ALPHA * x_ref[...] + (1.0 + mod_ref[:, 2 * D:3 * D]) * out

    tile = pl.BlockSpec((TM, D), lambda i: (i, 0))
    vec = _full((1, D))
    return pl.pallas_call(
        body, name="fwd_mix", grid=(S // TM,),
        in_specs=[pl.BlockSpec((TM, N), lambda i: (i, 0)), tile, _full(mod.shape), _full((KA, D)), vec, vec, vec, ANY, vec,
                  _full((KB, D)), ANY, ANY, vec],
        out_specs=[tile] * 9,
        out_shape=[jax.ShapeDtypeStruct((S, D), dt) for dt in (F32, BF16, BF16, BF16, BF16, F32, BF16, BF16, BF16)],
        scratch_shapes=[pltpu.VMEM((TM + HALO_A, D), F32), pltpu.VMEM((TM + HALO_B, D), F32), pltpu.VMEM((TM, D), F32),
                        pltpu.VMEM((D, D), BF16), pltpu.VMEM((D, D), BF16), pltpu.VMEM((D, D), BF16),
                        pltpu.SemaphoreType.DMA((3,))],
        compiler_params=_cparams(1),
    )(z, x, mod, wa, ba, lag, lab, w_a_out, b_a_out, wb, w_b_out, w_o, b_o)


def _ffn(r1, tgt, mod, ln1_g, ln1_b, w_up_g, b_up, w_down, b_down, ln2_g, ln2_b):
    S, D = r1.shape
    nb, _, wb = w_up_g.shape
    FF = nb * wb
    TM = TOKEN_TILE
    CF = FFN_CHUNK
    n_tiles = S // TM

    def body(r1_ref, t_ref, mod_ref, g1_ref, b1_ref, wup_hbm, bup_ref, wdn_hbm, bdn_ref, g2_ref, b2_ref,
             dx1_ref, h2_ref, f_ref, dhu_ref, do2_ref, acc_ref, accup_ref,
             wup, wdn, relu_buf, sems, sem2):
        i = pl.program_id(0)

        @pl.when(i == 0)
        def _():
            _load_cols(wup_hbm, wup, sems)
            _load_whole([(wdn_hbm, wdn)], sem2)
            acc_ref[...] = jnp.zeros(acc_ref.shape, F32)
            accup_ref[...] = jnp.zeros(accup_ref.shape, F32)

        scale2, shift2, gate2 = mod_ref[:, 4 * D:5 * D], mod_ref[:, 3 * D:4 * D], mod_ref[:, 5 * D:6 * D]
        xhat1, _ = _ln(r1_ref[...])
        x1 = xhat1 * g1_ref[...] + b1_ref[...]
        xh0, rstd0 = _ln(x1)
        h2 = (xh0 * (1.0 + scale2) + shift2).astype(BF16)
        h2_ref[...] = h2
        out2 = jnp.zeros((TM, D), F32) + bdn_ref[...]
        for c0 in range(0, FF, CF):
            hu = _dot(h2, wup[:, c0:c0 + CF]) + bup_ref[:, c0:c0 + CF]
            rl = jnp.maximum(hu, 0.0)
            relu_buf[:, c0:c0 + CF] = rl
            fb = (rl * rl).astype(BF16)
            f_ref[:, c0:c0 + CF] = fb
            out2 = out2 + _dot(fb, wdn[c0:c0 + CF, :])
        r2 = ALPHA * x1 + (1.0 + gate2) * out2
        xh2, rstd2 = _ln(r2)
        e = xh2 * g2_ref[...] + b2_ref[...] - t_ref[...]
        acc_ref[6:7, :] += _colsum(e * e)
        dy = e * (1.0 / D)
        acc_ref[1:2, :] += _colsum(dy * xh2)
        acc_ref[2:3, :] += _colsum(dy)
        dr2 = _ln_bwd(dy * g2_ref[...], xh2, rstd2)
        acc_ref[3:4, :] += _colsum(dr2 * out2)
        do2 = (1.0 + gate2) * dr2
        acc_ref[0:1, :] += _colsum(do2)
        do2b = do2.astype(BF16)
        do2_ref[...] = do2b
        dh2 = jnp.zeros((TM, D), F32)
        for c0 in range(0, FF, CF):
            dhu = _dot_nt(do2b, wdn[c0:c0 + CF, :]) * (2.0 * relu_buf[:, c0:c0 + CF])
            accup_ref[:, c0:c0 + CF] += _colsum(dhu)
            dhub = dhu.astype(BF16)
            dhu_ref[:, c0:c0 + CF] = dhub
            dh2 = dh2 + _dot_nt(dhub, wup[:, c0:c0 + CF])
        acc_ref[4:5, :] += _colsum(dh2 * xh0)
        acc_ref[5:6, :] += _colsum(dh2)
        dx1_ref[...] = ALPHA * dr2 + _ln_bwd(dh2 * (1.0 + scale2), xh0, rstd0)

        @pl.when(i == n_tiles - 1)
        def _():
            tot = jnp.sum(acc_ref[6:7, :], axis=-1, keepdims=True) * (0.5 / D)
            acc_ref[6:7, :] = jnp.broadcast_to(tot, (1, D))

    tile = pl.BlockSpec((TM, D), lambda i: (i, 0))
    wide = pl.BlockSpec((TM, FF), lambda i: (i, 0))
    vec = _full((1, D))
    return pl.pallas_call(
        body, name="ffn", grid=(n_tiles,),
        in_specs=[tile, tile, _full(mod.shape), vec, vec, ANY, _full((1, FF)), ANY, vec, vec, vec],
        out_specs=[tile, tile, wide, wide, tile, _full((8, D)), _full((1, FF))],
        out_shape=[jax.ShapeDtypeStruct((S, D), F32), jax.ShapeDtypeStruct((S, D), BF16), jax.ShapeDtypeStruct((S, FF), BF16),
                   jax.ShapeDtypeStruct((S, FF), BF16), jax.ShapeDtypeStruct((S, D), BF16),
                   jax.ShapeDtypeStruct((8, D), F32), jax.ShapeDtypeStruct((1, FF), F32)],
        scratch_shapes=[pltpu.VMEM((D, FF), BF16), pltpu.VMEM((FF, D), BF16), pltpu.VMEM((TM, FF), F32),
                        pltpu.SemaphoreType.DMA((nb,)), pltpu.SemaphoreType.DMA((1,))],
        compiler_params=_cparams(1),
    )(r1, tgt, mod, ln1_g, ln1_b, w_up_g, b_up, w_down, b_down, ln2_g, ln2_b)


def _dw(a, b, name, msplit=1, nsplit=1, wire=False, exchange=None):
    S, M = a.shape
    N = b.shape[1]
    TK = min(DW_TILE, S)
    mb, nbk = M // msplit, N // nsplit
    nk = S // TK
    srcs, kinds = exchange if exchange else ((), ())
    ns = len(srcs)

    def body(*refs):
        a_ref, b_ref = refs[:2]
        ins = refs[2:2 + ns]
        o_ref = refs[2 + ns]
        o16_ref = refs[3 + ns] if wire else None
        rest = refs[3 + ns + (1 if wire else 0):]
        k = pl.program_id(2)
        if ns:
            outs, (send_sems, recv_sems) = rest[:ns], rest[ns:]
            sends, arrivals = _peer_copies(lambda q, p: _shard_of(ins[q], kinds[q], p), outs, send_sems, recv_sems)
            first = (pl.program_id(0) == 0) & (pl.program_id(1) == 0) & (k == 0)
            last = (pl.program_id(0) == msplit - 1) & (pl.program_id(1) == nsplit - 1) & (k == nk - 1)

            @pl.when(first)
            def _():
                for cp in sends:
                    cp.start()

        @pl.when(k == 0)
        def _():
            o_ref[...] = jnp.zeros(o_ref.shape, F32)

        o_ref[...] += _dot_tn(a_ref[...], b_ref[...])

        if wire:
            @pl.when(k == nk - 1)
            def _():
                o16_ref[...] = o_ref[...].astype(BF16)

        if ns:
            @pl.when(last)
            def _():
                for cp in arrivals:
                    cp.wait_recv()
                for cp in sends:
                    cp.wait_send()

    oblk = pl.BlockSpec((mb, nbk), lambda i, j, k: (i, j))
    res = pl.pallas_call(
        body, name=name, grid=(msplit, nsplit, nk),
        in_specs=[pl.BlockSpec((TK, mb), lambda i, j, k: (k, i)), pl.BlockSpec((TK, nbk), lambda i, j, k: (k, j))] + [ANY] * ns,
        out_specs=[oblk] + ([oblk] if wire else []) + [ANY] * ns,
        out_shape=[jax.ShapeDtypeStruct((M, N), F32)] + ([jax.ShapeDtypeStruct((M, N), BF16)] if wire else [])
        + [jax.ShapeDtypeStruct((N_DEV,) + _shard_shape(g.shape, kd), g.dtype) for g, kd in zip(srcs, kinds)],
        scratch_shapes=_peer_sems(ns) if ns else [],
        compiler_params=_cparams(3),
    )(a, b, *srcs)
    return res if (wire or ns) else res[0]


def _bwd_mix1(dx1, r1, out, ya, yb, q, u1, zg, mod, ln1_g, lag, lab, w_o, w_a_out, w_b_out):
    S, D = dx1.shape
    TM = TOKEN_TILE

    def body(dx1_ref, r1_ref, out_ref, ya_ref, yb_ref, q_ref, u1_ref, zg_ref, mod_ref, g1_ref, lag_ref, lab_ref,
             wo_hbm, wao_hbm, wbo_hbm,
             dxp_ref, du1_ref, dq_ref, dzc_ref, dout_ref, dya_ref, dyb_ref, acc_ref,
             wo, wao, wbo, sems):
        @pl.when(pl.program_id(0) == 0)
        def _():
            _load_whole([(wo_hbm, wo), (wao_hbm, wao), (wbo_hbm, wbo)], sems)
            acc_ref[...] = jnp.zeros(acc_ref.shape, F32)

        dx1v = dx1_ref[...]
        xhat1, rstd1 = _ln(r1_ref[...])
        acc_ref[0:1, :] += _colsum(dx1v * xhat1)
        acc_ref[1:2, :] += _colsum(dx1v)
        dr1 = _ln_bwd(dx1v * g1_ref[...], xhat1, rstd1)
        dxp_ref[...] = (ALPHA * dr1).astype(BF16)
        acc_ref[2:3, :] += _colsum(dr1 * out_ref[...].astype(F32))
        dout = (1.0 + mod_ref[:, 2 * D:3 * D]) * dr1
        acc_ref[3:4, :] += _colsum(dout)
        doutb = dout.astype(BF16)
        dout_ref[...] = doutb
        dmg = _dot_nt(doutb, wo[...])
        sga, sgb = _sigmoid(zg_ref[:, D:2 * D].astype(F32)), _sigmoid(zg_ref[:, 2 * D:3 * D].astype(F32))
        dga = dmg * ya_ref[...].astype(F32) * sga * (1.0 - sga)
        dgb = dmg * yb_ref[...].astype(F32) * sgb * (1.0 - sgb)
        acc_ref[9:10, :] += _colsum(dga)
        acc_ref[10:11, :] += _colsum(dgb)
        dzc_ref[:, D:2 * D] = dga.astype(BF16)
        dzc_ref[:, 2 * D:3 * D] = dgb.astype(BF16)
        dya = dmg * sga
        acc_ref[4:5, :] += _colsum(dya)
        dyab = dya.astype(BF16)
        dya_ref[...] = dyab
        dybb = (dmg * sgb).astype(BF16)
        dyb_ref[...] = dybb
        du2 = _dot_nt(dyab, wao[...])
        xa, rstda = _ln(u1_ref[...])
        l = xa * lag_ref[...] + lab_ref[...]
        sl = _sigmoid(l)
        dl = du2 * (sl * (1.0 + l * (1.0 - sl)))
        acc_ref[5:6, :] += _colsum(dl * xa)
        acc_ref[6:7, :] += _colsum(dl)
        du1 = _ln_bwd(dl * lag_ref[...], xa, rstda)
        acc_ref[7:8, :] += _colsum(du1)
        du1_ref[...] = du1.astype(BF16)
        dv = _dot_nt(dybb, wbo[...])
        dgbk = dv * q_ref[...].astype(F32)
        acc_ref[8:9, :] += _colsum(dgbk)
        dzc_ref[:, 0:D] = dgbk.astype(BF16)
        dq_ref[...] = (dv * zg_ref[:, 0:D].astype(F32)).astype(BF16)

    tile = pl.BlockSpec((TM, D), lambda i: (i, 0))
    vec = _full((1, D))
    return pl.pallas_call(
        body, name="bwd_mix1", grid=(S // TM,),
        in_specs=[tile] * 7 + [pl.BlockSpec((TM, 3 * D), lambda i: (i, 0)), _full(mod.shape), vec, vec, vec, ANY, ANY, ANY],
        out_specs=[tile, tile, tile, pl.BlockSpec((TM, 3 * D), lambda i: (i, 0)), tile, tile, tile, _full((16, D))],
        out_shape=[jax.ShapeDtypeStruct((S, D), BF16)] * 3 + [jax.ShapeDtypeStruct((S, 3 * D), BF16)]
        + [jax.ShapeDtypeStruct((S, D), BF16)] * 3 + [jax.ShapeDtypeStruct((16, D), F32)],
        scratch_shapes=[pltpu.VMEM((D, D), BF16)] * 3 + [pltpu.SemaphoreType.DMA((3,))],
        compiler_params=_cparams(1),
    )(dx1, r1, out, ya, yb, q, u1, zg, mod, ln1_g, lag, lab, w_o, w_a_out, w_b_out)


def _anticausal_taps(dbuf, u_ref, w_ref, o_ref, dwacc, K, halo, TM, D):
    R, C = CONV_ROWS, CONV_LANES

    for r0 in range(0, TM, R):
        for c0 in range(0, D, C):
            win = dbuf[r0:r0 + R + halo, c0:c0 + C]
            uc = u_ref[r0:r0 + R, c0:c0 + C]
            acc = jnp.zeros((R, C), F32)
            for j in range(K):
                sh = _rows_from(win, j, R)
                k = K - 1 - j
                acc = acc + w_ref[k:k + 1, c0:c0 + C] * sh
                pr = uc * sh
                part = pr[0:8, :]
                for s in range(8, R, 8):
                    part = part + pr[s:s + 8, :]
                dwacc[k, :, c0:c0 + C] += part
            o_ref[r0:r0 + R, c0:c0 + C] = acc


def _bwd_mix2(du1, dq, z, dzc, x, dxp, mod, wa, wb, w_in_g):
    S, D = x.shape
    N = z.shape[1]
    nb, _, wbk = w_in_g.shape
    TM = TOKEN_TILE
    KA, KB = wa.shape[0], wb.shape[0]
    n_tiles = S // TM

    def body(du1_ref, dq_ref, zav_ref, zc_ref, zx_ref, dzc_ref, x_ref, dxp_ref, mod_ref, wa_ref, wb_ref, win_hbm,
             dz_ref, gx_ref, dwa_ref, dwb_ref, acc_ref, dbin_ref,
             dbuf, qbuf, ubuf, pbuf, obuf, dwa_acc, dwb_acc, w_v, sems):
        i = pl.program_id(0)

        @pl.when(i == 0)
        def _():
            _load_cols(win_hbm, w_v, sems)
            dbuf[TM:TM + HALO_A, :] = jnp.zeros((HALO_A, D), F32)
            qbuf[TM:TM + HALO_B, :] = jnp.zeros((HALO_B, D), F32)
            dwa_acc[...] = jnp.zeros(dwa_acc.shape, F32)
            dwb_acc[...] = jnp.zeros(dwb_acc.shape, F32)
            acc_ref[...] = jnp.zeros(acc_ref.shape, F32)
            dbin_ref[...] = jnp.zeros(dbin_ref.shape, F32)

        a_val = zav_ref[:, 0:D]
        sa = _sigmoid(zav_ref[:, D:2 * D])
        ubuf[...] = a_val * sa
        dbuf[0:TM, :] = du1_ref[...].astype(F32)
        _anticausal_taps(dbuf, ubuf, wa_ref, obuf, dwa_acc, KA, HALO_A, TM, D)
        dbuf[TM:TM + HALO_A, :] = dbuf[0:HALO_A, :]
        du0 = obuf[...]
        dav = du0 * sa
        dag = du0 * a_val * sa * (1.0 - sa)
        dbin_ref[:, 0:D] += _colsum(dav)
        dbin_ref[:, D:2 * D] += _colsum(dag)
        dz_ref[:, 0:D] = dav.astype(BF16)
        dz_ref[:, D:2 * D] = dag.astype(BF16)
        pbuf[...] = zc_ref[...] * zx_ref[...]
        qbuf[0:TM, :] = dq_ref[...].astype(F32)
        _anticausal_taps(qbuf, pbuf, wb_ref, obuf, dwb_acc, KB, HALO_B, TM, D)
        qbuf[TM:TM + HALO_B, :] = qbuf[0:HALO_B, :]
        dp = obuf[...]
        dgc = dp * zx_ref[...]
        dgx = dp * zc_ref[...]
        dbin_ref[:, 3 * D:4 * D] += _colsum(dgc)
        dbin_ref[:, 4 * D:5 * D] += _colsum(dgx)
        dz_ref[:, 3 * D:4 * D] = dgc.astype(BF16)
        dz_ref[:, 4 * D:5 * D] = dgx.astype(BF16)
        dz_ref[:, 2 * D:3 * D] = dzc_ref[:, 0:D]
        dz_ref[:, 5 * D:7 * D] = dzc_ref[:, D:3 * D]
        dh = _dot_nt(dz_ref[...], w_v[...])
        xhat, rstd = _ln(x_ref[...])
        acc_ref[0:1, :] += _colsum(dh * xhat)
        acc_ref[1:2, :] += _colsum(dh)
        gx_ref[...] = dxp_ref[...].astype(F32) + _ln_bwd(dh * (1.0 + mod_ref[:, D:2 * D]), xhat, rstd)

        @pl.when(i == n_tiles - 1)
        def _():
            for k in range(KA):
                dwa_ref[k:k + 1, :] = jnp.sum(dwa_acc[k], axis=0, keepdims=True)
            for k in range(KB):
                dwb_ref[k:k + 1, :] = jnp.sum(dwb_acc[k], axis=0, keepdims=True)

    rev = lambda i: n_tiles - 1 - i
    tile = pl.BlockSpec((TM, D), lambda i: (rev(i), 0))
    zcol = lambda k: pl.BlockSpec((TM, D), lambda i: (rev(i), k))
    return pl.pallas_call(
        body, name="bwd_mix2", grid=(n_tiles,),
        in_specs=[tile, tile, pl.BlockSpec((TM, 2 * D), lambda i: (rev(i), 0)), zcol(3), zcol(4),
                  pl.BlockSpec((TM, 3 * D), lambda i: (rev(i), 0)), tile, tile, _full(mod.shape), _full((KA, D)), _full((KB, D)), ANY],
        out_specs=[pl.BlockSpec((TM, N), lambda i: (rev(i), 0)), tile, _full((KA, D)), _full((KB, D)), _full((8, D)), _full((1, N))],
        out_shape=[jax.ShapeDtypeStruct((S, N), BF16), jax.ShapeDtypeStruct((S, D), F32), jax.ShapeDtypeStruct((KA, D), F32),
                   jax.ShapeDtypeStruct((KB, D), F32), jax.ShapeDtypeStruct((8, D), F32), jax.ShapeDtypeStruct((1, N), F32)],
        scratch_shapes=[pltpu.VMEM((TM + HALO_A, D), F32), pltpu.VMEM((TM + HALO_B, D), F32), pltpu.VMEM((TM, D), F32),
                        pltpu.VMEM((TM, D), F32), pltpu.VMEM((TM, D), F32), pltpu.VMEM((KA, 8, D), F32), pltpu.VMEM((KB, 8, D), F32),
                        pltpu.VMEM((D, N), BF16), pltpu.SemaphoreType.DMA((nb,))],
        compiler_params=_cparams(1),
    )(du1, dq, z, z, z, dzc, x, dxp, mod, wa, wb, w_in_g)


PEER_REDUCED = ("w_up", "w_down", "w_o", "w_a_out", "w_b_out", "conv_a_w", "conv_b_w")


def _local_grads(me, x, c, c_all, tgt, w_ada16, w_in_g, shards, vecs):
    D = x.shape[1]
    rows = lambda a: a.reshape(a.shape[0] * a.shape[1], a.shape[2])
    cols = lambda a: jnp.transpose(a, (1, 0, 2)).reshape(a.shape[1], a.shape[0] * a.shape[2])
    mod = _ada_mod(me, c_all, w_ada16, vecs["b_ada"])
    z, h1, zg, gathered = _fwd_in(x, mod, w_in_g, vecs["b_in"], shards)
    w_a_out, w_b_out, w_o, w_up_g, w_down = rows(gathered[0]), rows(gathered[1]), rows(gathered[2]), gathered[3], rows(gathered[4])
    wa, wb = cols(gathered[5]), cols(gathered[6])
    u1, ya, yb, q, out, r1, u2, v, mg = _fwd_mix(
        z, x, mod, wa, vecs["conv_a_b"], vecs["ln_a_g"], vecs["ln_a_b"], w_a_out, vecs["b_a_out"], wb, w_b_out, w_o, vecs["b_o"])
    dx1, h2, f, dhu, do2, acc_f, acc_up = _ffn(
        r1, tgt, mod, vecs["ln1_g"], vecs["ln1_b"], w_up_g, vecs["b_up"], w_down, vecs["b_down"], vecs["ln2_g"], vecs["ln2_b"])
    g_up, g_up16 = _dw(h2, dhu, "dw_up", nsplit=2, wire=True)
    g_down, g_down16 = _dw(f, do2, "dw_down", msplit=2, wire=True)
    dxp, du1, dq, dzc, doutb, dyab, dybb, acc_1 = _bwd_mix1(
        dx1, r1, out, ya, yb, q, u1, zg, mod, vecs["ln1_g"], vecs["ln_a_g"], vecs["ln_a_b"], w_o, w_a_out, w_b_out)
    g_o, g_o16 = _dw(mg, doutb, "dw_o", wire=True)
    g_a_out, g_a_out16 = _dw(u2, dyab, "dw_a_out", wire=True)
    g_b_out, g_b_out16 = _dw(v, dybb, "dw_b_out", wire=True)
    dz, gx, g_wa, g_wb, acc_2, db_in = _bwd_mix2(du1, dq, z, dzc, x, dxp, mod, wa, wb, w_in_g)
    full = dict(zip(PEER_REDUCED, (g_up, g_down, g_o, g_a_out, g_b_out, g_wa, g_wb)))
    res = _dw(h1, dz, "dw_in", nsplit=4,
              exchange=((g_up16, g_down16, g_o16, g_a_out16, g_b_out16, g_wa, g_wb), [SHARD_KIND[n] for n in PEER_REDUCED]))
    g_in = res[0]
    peers = {n: (full[n], r) for n, r in zip(PEER_REDUCED, res[1:])}
    row = lambda acc, k: (acc, k, 0, D)
    pieces = [
        row(acc_2, 1), row(acc_2, 0), row(acc_1, 2), row(acc_f, 5), row(acc_f, 4), row(acc_f, 3),
        (db_in, 0, 0, 2 * D), row(acc_1, 8), (db_in, 0, 3 * D, 2 * D), row(acc_1, 9), row(acc_1, 10),
        row(acc_1, 7), row(acc_1, 5), row(acc_1, 6), row(acc_1, 4), row(acc_1, 3), row(acc_1, 0), row(acc_1, 1),
        (acc_up, 0, 0, acc_up.shape[1]), row(acc_f, 0), row(acc_f, 1), row(acc_f, 2), (c, 0, 0, D)]
    return acc_f[6, 0], gx, g_in, peers, pieces


WEIGHTS = ("w_ada", "b_ada", "w_in", "b_in", "conv_a_w", "conv_a_b", "ln_a_g", "ln_a_b", "w_a_out", "b_a_out", "conv_b_w",
           "w_b_out", "w_o", "b_o", "ln1_g", "ln1_b", "w_up", "b_up", "w_down", "b_down", "ln2_g", "ln2_b")
VECTORS = ("b_ada", "b_in", "conv_a_b", "ln_a_g", "ln_a_b", "b_a_out", "b_o", "ln1_g", "ln1_b", "b_up", "b_down", "ln2_g", "ln2_b")
SHARD_KIND = {"w_in": "col", "w_a_out": "row", "w_b_out": "row", "w_o": "row", "w_up": "col", "w_down": "row",
              "conv_a_w": "col", "conv_b_w": "col"}


def kernel(x, c, w_ada, b_ada, w_in, b_in, conv_a_w, conv_a_b, ln_a_g, ln_a_b, w_a_out, b_a_out, conv_b_w, w_b_out, w_o, b_o, ln1_g, ln1_b, w_up, b_up, w_down, b_down, ln2_g, ln2_b, loss_target, m_w_ada, m_b_ada, m_w_in, m_b_in, m_conv_a_w, m_conv_a_b, m_ln_a_g, m_ln_a_b, m_w_a_out, m_b_a_out, m_conv_b_w, m_w_b_out, m_w_o, m_b_o, m_ln1_g, m_ln1_b, m_w_up, m_b_up, m_w_down, m_b_down, m_ln2_g, m_ln2_b, v_w_ada, v_b_ada, v_w_in, v_b_in, v_conv_a_w, v_conv_a_b, v_ln_a_g, v_ln_a_b, v_w_a_out, v_b_a_out, v_conv_b_w, v_w_b_out, v_w_o, v_b_o, v_ln1_g, v_ln1_b, v_w_up, v_b_up, v_w_down, v_b_down, v_ln2_g, v_ln2_b):
    args = dict(locals())
    w = {n: args[n][0] for n in WEIGHTS}
    m = {n: args["m_" + n][0] for n in WEIGHTS}
    v = {n: args["v_" + n][0] for n in WEIGHTS}
    w = {n: (a[None, :] if a.ndim == 1 else a) for n, a in w.items()}
    m = {n: (a[None, :] if a.ndim == 1 else a) for n, a in m.items()}
    v = {n: (a[None, :] if a.ndim == 1 else a) for n, a in v.items()}
    xs, tgt = x[0], loss_target[0]
    S, D = xs.shape
    me = (4 * lax.axis_index("x") + 2 * lax.axis_index("y") + lax.axis_index("c")).astype(jnp.int32).reshape(1)
    core = lax.axis_index("c").astype(jnp.int32).reshape(1)
    chip = (2 * lax.axis_index("x") + lax.axis_index("y")).astype(jnp.int32).reshape(1)

    bf = lambda n: w[n].astype(BF16)
    w_in_g, c_all = _all_gather([bf("w_in"), c], "gather_weights")
    shards = [bf("w_a_out"), bf("w_b_out"), bf("w_o"), bf("w_up"), bf("w_down"), w["conv_a_w"], w["conv_b_w"]]
    vecs = {n: w[n] for n in VECTORS}

    loss, gx, g_in, peers, pieces = _local_grads(me, xs, c, c_all.reshape(N_DEV, D), tgt, bf("w_ada"), w_in_g, shards, vecs)

    (from_sibling,) = _exchange_cores([g_in], [SHARD_KIND["w_in"]], "reduce_cores")
    own_in, wire_in = _add_own(core, g_in, from_sibling, SHARD_KIND["w_in"], "add_w_in")
    sems, wire_thru, landing, after = _chips_start(wire_in)

    res = {}
    for n, (g_full, recv) in peers.items():
        res[n] = _sum_peers_adamw(me, g_full, recv, SHARD_KIND[n], w[n], m[n], v[n], "adamw_" + n, after)
        after = res[n][0]
    small = _pack_rows(pieces, "pack_vectors", after)
    (small_g,) = _all_gather([small], "gather_vectors")
    small_g = small_g.reshape(N_DEV, small.shape[1])
    W = w["w_ada"].shape[1]
    n_vec = small.shape[1] - D
    g_ada = _ada_bwd(me, small_g, D, W, n_vec // D)
    res["w_ada"] = _sum_adamw(g_ada[None], w["w_ada"], m["w_ada"], v["w_ada"], "adamw_w_ada")
    by_kind = _adamw_vectors(small_g, [w[n] for n in VECTORS], [m[n] for n in VECTORS], [v[n] for n in VECTORS])
    for i, n in enumerate(VECTORS):
        res[n] = tuple(by_kind[t][i] for t in range(4))
    from_chips = _chips_wait(sems, wire_thru, landing, res["w_ada"][0])
    res["w_in"] = _sum_chips_adamw(chip, own_in, from_chips, w["w_in"], m["w_in"], v["w_in"], "adamw_w_in")

    loss = lax.psum(loss, ("x", "y", "c"))
    outs = [loss, gx[None]]
    for t in range(4):
        outs += [res[n][t].reshape(args[n].shape) for n in WEIGHTS]
    return tuple(outs)
```

```python
import functools

import jax
import jax.numpy as jnp
from jax import lax
from jax.experimental import pallas as pl
from jax.experimental.pallas import tpu as pltpu

F32 = jnp.float32
BF16 = jnp.bfloat16
MESH = pl.DeviceIdType.MESH

N_DEV = 8
LN_EPS = 1e-5
DEPTH = 1
ALPHA = (2.0 * DEPTH) ** 0.25
ADAM_LR, ADAM_B1, ADAM_B2, ADAM_EPS, ADAM_WD, ADAM_STEP = 0.001, 0.9, 0.999, 1e-08, 0.01, 10

VMEM_LIMIT = 60 * 1024 * 1024
TOKEN_TILE = 256
DW_TILE = 1024
HALO_A = 32
HALO_B = 8
CONV_ROWS, CONV_LANES = 256, 256
FFN_CHUNK = 1024


def _cparams(n_grid):
    return pltpu.CompilerParams(dimension_semantics=("arbitrary",) * n_grid, vmem_limit_bytes=VMEM_LIMIT)


def _full(shape):
    return pl.BlockSpec(shape, lambda *_: (0,) * len(shape))


ANY = pl.BlockSpec(memory_space=pl.ANY)


def _ln(x):
    mu = jnp.mean(x, axis=-1, keepdims=True)
    xc = x - mu
    var = jnp.mean(xc * xc, axis=-1, keepdims=True)
    rstd = lax.rsqrt(var + LN_EPS)
    return xc * rstd, rstd


def _ln_bwd(dxhat, xhat, rstd):
    m1 = jnp.mean(dxhat, axis=-1, keepdims=True)
    m2 = jnp.mean(dxhat * xhat, axis=-1, keepdims=True)
    return rstd * (dxhat - m1 - xhat * m2)


def _sigmoid(x):
    return 0.5 * jnp.tanh(0.5 * x) + 0.5


def _colsum(a):
    return jnp.sum(a, axis=0, keepdims=True)


def _dot(a, b):
    return jnp.dot(a, b, preferred_element_type=F32)


def _dot_nt(a, b):
    return lax.dot_general(a, b, (((1,), (1,)), ((), ())), preferred_element_type=F32)


def _dot_tn(a, b):
    return lax.dot_general(a, b, (((0,), (0,)), ((), ())), preferred_element_type=F32)


def _load_cols(src_hbm, dst_vmem, sems):
    nblk, _, w = src_hbm.shape
    cps = [pltpu.make_async_copy(src_hbm.at[j], dst_vmem.at[:, pl.ds(j * w, w)], sems.at[j]) for j in range(nblk)]
    for cp in cps:
        cp.start()
    for cp in cps:
        cp.wait()


def _load_whole(pairs, sems):
    cps = [pltpu.make_async_copy(s, d, sems.at[k]) for k, (s, d) in enumerate(pairs)]
    for cp in cps:
        cp.start()
    for cp in cps:
        cp.wait()


def _mesh_pos():
    return lax.axis_index("x"), lax.axis_index("y"), lax.axis_index("c")


def _all_gather(arrs, name):
    n = len(arrs)

    def body(*refs):
        ins, outs = refs[:n], refs[n:2 * n]
        send_sems, recv_sems, local_sems = refs[2 * n:]
        x, y, c = _mesh_pos()
        me, sibling = (x, y, c), (x, y, 1 - c)
        chips = [(1 - x, y), (x, 1 - y), (1 - x, 1 - y)]

        def slot(a, px, py, pc):
            return outs[a].at[4 * px + 2 * py + pc]

        def copy(a, k, block, to, src=None):
            return pltpu.make_async_remote_copy(
                src_ref=slot(a, *block) if src is None else src, dst_ref=slot(a, *block),
                send_sem=send_sems.at[a, k], recv_sem=recv_sems.at[a, k], device_id=to, device_id_type=MESH)

        mine = [pltpu.make_async_copy(ins[a], slot(a, *me), local_sems.at[a]) for a in range(n)]
        for cp in mine:
            cp.start()
        first = []
        for a in range(n):
            first.append(copy(a, 0, me, sibling, src=ins[a]))
            first += [copy(a, 1 + j, me, (*chip, c), src=ins[a]) for j, chip in enumerate(chips)]
        for cp in first:
            cp.start()
        passed = []
        for a in range(n):
            for j, chip in enumerate(chips):
                copy(a, 1 + j, (*chip, c), me).wait_recv()
                fwd = copy(a, 4 + j, (*chip, c), sibling)
                fwd.start()
                passed.append(fwd)
        for a in range(n):
            copy(a, 0, sibling, me).wait_recv()
            for j, chip in enumerate(chips):
                copy(a, 4 + j, (*chip, 1 - c), me).wait_recv()
        for cp in first + passed:
            cp.wait_send()
        for cp in mine:
            cp.wait()

    outs = pl.pallas_call(
        body, name=name,
        out_shape=[jax.ShapeDtypeStruct((N_DEV,) + a.shape, a.dtype) for a in arrs],
        in_specs=[ANY] * n, out_specs=[ANY] * n,
        scratch_shapes=[pltpu.SemaphoreType.DMA((n, 7)), pltpu.SemaphoreType.DMA((n, 7)), pltpu.SemaphoreType.DMA((n,))],
    )(*arrs)
    return list(outs)


def _peer_copies(src_of, dsts, send_sems, recv_sems):
    x, y, c = _mesh_pos()
    me = 4 * x + 2 * y + c
    sends, arrivals = [], []
    for a in range(len(dsts)):
        for k in range(1, N_DEV):
            px, py, pc = (1 - x if k & 4 else x), (1 - y if k & 2 else y), (1 - c if k & 1 else c)
            p = 4 * px + 2 * py + pc
            common = dict(send_sem=send_sems.at[a, k - 1], recv_sem=recv_sems.at[a, k - 1],
                          device_id=(px, py, pc), device_id_type=MESH)
            sends.append(pltpu.make_async_remote_copy(src_ref=src_of(a, p), dst_ref=dsts[a].at[me], **common))
            arrivals.append(pltpu.make_async_remote_copy(src_ref=dsts[a].at[p], dst_ref=dsts[a].at[p], **common))
    return sends, arrivals


def _peer_sems(n):
    return [pltpu.SemaphoreType.DMA((n, N_DEV - 1)), pltpu.SemaphoreType.DMA((n, N_DEV - 1))]


def _shard_of(ref, kind, j):
    if kind == "col":
        w = ref.shape[1] // N_DEV
        return ref.at[:, pl.ds(j * w, w)]
    h = ref.shape[0] // N_DEV
    return ref.at[pl.ds(j * h, h), :]


def _shard_shape(shape, kind):
    return (shape[0], shape[1] // N_DEV) if kind == "col" else (shape[0] // N_DEV, shape[1])


def _exchange_cores(grads, kinds, name):
    n = len(grads)

    def body(*refs):
        ins, outs = refs[:n], refs[n:2 * n]
        send_sems, recv_sems = refs[2 * n:]
        x, y, c = _mesh_pos()
        sibling = (x, y, 1 - c)
        sends = []
        for a in range(n):
            for s in range(4):
                sends.append(pltpu.make_async_remote_copy(
                    src_ref=_shard_of(ins[a], kinds[a], 2 * s + (1 - c)), dst_ref=outs[a].at[s],
                    send_sem=send_sems.at[a, s], recv_sem=recv_sems.at[a, s], device_id=sibling, device_id_type=MESH))
        for cp in sends:
            cp.start()
        for a in range(n):
            for s in range(4):
                pltpu.make_async_remote_copy(
                    src_ref=outs[a].at[s], dst_ref=outs[a].at[s],
                    send_sem=send_sems.at[a, s], recv_sem=recv_sems.at[a, s], device_id=sibling, device_id_type=MESH).wait_recv()
        for cp in sends:
            cp.wait_send()

    outs = pl.pallas_call(
        body, name=name,
        out_shape=[jax.ShapeDtypeStruct((4,) + _shard_shape(g.shape, k), F32) for g, k in zip(grads, kinds)],
        in_specs=[ANY] * n, out_specs=[ANY] * n,
        scratch_shapes=[pltpu.SemaphoreType.DMA((n, 4)), pltpu.SemaphoreType.DMA((n, 4))],
    )(*grads)
    return list(outs)


HBM = pl.BlockSpec(memory_space=pltpu.HBM)
SEM = pl.BlockSpec(memory_space=pltpu.SEMAPHORE)
DATAFLOW = pltpu.SideEffectType.DATAFLOW_SIDE_EFFECTING


def _chip_copies(src_ref, land_ref, send_sems, recv_sems):
    x, y, c = _mesh_pos()
    my_slot = 2 * x + y
    sends, arrivals = [], []
    for j, (px, py) in enumerate([(1 - x, y), (x, 1 - y), (1 - x, 1 - y)]):
        common = dict(send_sem=send_sems[j], recv_sem=recv_sems[j], device_id=(px, py, c), device_id_type=MESH)
        sends.append(pltpu.make_async_remote_copy(src_ref=src_ref.at[2 * px + py], dst_ref=land_ref.at[my_slot], **common))
        arrivals.append(pltpu.make_async_remote_copy(src_ref=src_ref.at[2 * px + py], dst_ref=land_ref.at[2 * px + py], **common))
    return sends, arrivals


def _chips_start(part):
    def body(src_ref, land_ref, s0, s1, s2, r0, r1, r2, src_thru, land_thru, token):
        sends, _ = _chip_copies(src_ref, land_ref, (s0, s1, s2), (r0, r1, r2))
        for cp in sends:
            cp.start()
        token[...] = jnp.zeros_like(token)

    res = pl.pallas_call(
        body, name="reduce_chips_start",
        out_shape=(pltpu.SemaphoreType.DMA(()),) * 6 + (pltpu.HBM(part.shape, part.dtype), pltpu.HBM(part.shape, part.dtype),
                                                        jax.ShapeDtypeStruct((8, 128), F32)),
        in_specs=(HBM, HBM), out_specs=(SEM,) * 6 + (HBM, HBM, pl.BlockSpec(memory_space=pltpu.VMEM)),
        input_output_aliases={0: 6, 1: 7}, compiler_params=pltpu.CompilerParams(has_side_effects=DATAFLOW),
    )(pltpu.with_memory_space_constraint(part, pltpu.HBM),
      pltpu.with_memory_space_constraint(lax.empty(part.shape, part.dtype), pltpu.HBM))
    return res[:6], res[6], res[7], res[8]


def _chips_wait(sems, src_thru, land_thru, after):
    def body(src_ref, land_ref, s0, s1, s2, r0, r1, r2, after_ref, src_dead, got_ref):
        sends, arrivals = _chip_copies(src_ref, land_ref, (s0, s1, s2), (r0, r1, r2))
        for cp in sends:
            cp.wait_send()
        for cp in arrivals:
            cp.wait_recv()

    return pl.pallas_call(
        body, name="reduce_chips_wait",
        out_shape=(pltpu.HBM(src_thru.shape, src_thru.dtype), pltpu.HBM(land_thru.shape, land_thru.dtype)),
        in_specs=(HBM, HBM) + (SEM,) * 6 + (ANY,), out_specs=(HBM, HBM), input_output_aliases={0: 0, 1: 1},
        compiler_params=pltpu.CompilerParams(has_side_effects=DATAFLOW),
    )(src_thru, land_thru, *sems, after)[1]


def _row_tile(rows):
    for t in (256, 128, 64, 32, 16, 8):
        if rows % t == 0:
            return t
    return rows


def _wire_dtype(rows):
    return BF16 if rows % 16 == 0 else F32


def _add_own(core, g, recv, kind, name):
    ns, R, C = recv.shape
    tr = _row_tile(R)
    nr = R // tr
    if kind == "col":
        g_spec = pl.BlockSpec((tr, C), lambda s, r, c_ref: (r, 2 * s + c_ref[0]))
    else:
        g_spec = pl.BlockSpec((tr, C), lambda s, r, c_ref: ((2 * s + c_ref[0]) * nr + r, 0))
    slab = pl.BlockSpec((None, tr, C), lambda s, r, c_ref: (s, r, 0))

    def body(c_ref, g_ref, r_ref, o32_ref, o16_ref):
        t = g_ref[...] + r_ref[...]
        o32_ref[...] = t
        o16_ref[...] = t.astype(o16_ref.dtype)

    return pl.pallas_call(
        body, name=name,
        grid_spec=pltpu.PrefetchScalarGridSpec(num_scalar_prefetch=1, grid=(ns, nr), in_specs=[g_spec, slab], out_specs=[slab, slab]),
        out_shape=[jax.ShapeDtypeStruct((ns, R, C), F32), jax.ShapeDtypeStruct((ns, R, C), _wire_dtype(R))],
        compiler_params=_cparams(2),
    )(core, g, recv)


def _adamw_math(w, g, m, v):
    m2 = ADAM_B1 * m + (1.0 - ADAM_B1) * g
    v2 = ADAM_B2 * v + (1.0 - ADAM_B2) * (g * g)
    m_hat = m2 / (1.0 - ADAM_B1 ** ADAM_STEP)
    v_hat = v2 / (1.0 - ADAM_B2 ** ADAM_STEP)
    delta = -ADAM_LR * (m_hat / (jnp.sqrt(v_hat) + ADAM_EPS) + ADAM_WD * w)
    return delta, m2, v2


def _sum_adamw(parts, w, m, v, name):
    n, R, C = parts.shape
    tr = _row_tile(R)

    def body(p_ref, w_ref, m_ref, v_ref, g_ref, d_ref, m_out, v_out):
        g = p_ref[0]
        for k in range(1, n):
            g = g + p_ref[k]
        g_ref[...] = g
        d_ref[...], m_out[...], v_out[...] = _adamw_math(w_ref[...], g, m_ref[...], v_ref[...])

    blk = pl.BlockSpec((tr, C), lambda r: (r, 0))
    return pl.pallas_call(
        body, name=name, grid=(R // tr,),
        in_specs=[pl.BlockSpec((n, tr, C), lambda r: (0, r, 0)), blk, blk, blk],
        out_specs=[blk] * 4, out_shape=[jax.ShapeDtypeStruct((R, C), F32)] * 4, compiler_params=_cparams(1),
    )(parts, w, m, v)


def _sum_chips_adamw(chip, own, recv, w, m, v, name):
    n, R, C = recv.shape
    tr = _row_tile(R)

    def body(chip_ref, own_ref, r_ref, w_ref, m_ref, v_ref, g_ref, d_ref, m_out, v_out):
        g = None
        for k in range(n):
            term = jnp.where(chip_ref[0] == k, own_ref[...], r_ref[k].astype(F32))
            g = term if g is None else g + term
        g_ref[...] = g
        d_ref[...], m_out[...], v_out[...] = _adamw_math(w_ref[...], g, m_ref[...], v_ref[...])

    blk = pl.BlockSpec((tr, C), lambda r, chip_ref: (r, 0))
    return pl.pallas_call(
        body, name=name,
        grid_spec=pltpu.PrefetchScalarGridSpec(
            num_scalar_prefetch=1, grid=(R // tr,),
            in_specs=[pl.BlockSpec((None, tr, C), lambda r, chip_ref: (chip_ref[0], r, 0)),
                      pl.BlockSpec((n, tr, C), lambda r, chip_ref: (0, r, 0)), blk, blk, blk],
            out_specs=[blk] * 4),
        out_shape=[jax.ShapeDtypeStruct((R, C), F32)] * 4, compiler_params=_cparams(1),
    )(chip, own, recv, w, m, v)


def _pack_rows(pieces, name, after):
    arrs = []
    for a, _, _, _ in pieces:
        if not any(a is b for b in arrs):
            arrs.append(a)
    which = [next(i for i, b in enumerate(arrs) if b is a) for a, _, _, _ in pieces]
    total = sum(n for _, _, _, n in pieces)

    def body(*refs):
        o_ref = refs[len(arrs) + 1]
        off = 0
        for (_, r, c0, n), i in zip(pieces, which):
            o_ref[:, off:off + n] = refs[i][r:r + 1, c0:c0 + n]
            off += n

    return pl.pallas_call(
        body, name=name, in_specs=[_full(a.shape) for a in arrs] + [ANY], out_specs=_full((1, total)),
        out_shape=jax.ShapeDtypeStruct((1, total), F32), grid=(1,), compiler_params=_cparams(1),
    )(*arrs, after)


def _adamw_vectors(parts, ws, ms, vs):
    nv = len(ws)
    widths = [a.shape[1] for a in ws]

    def body(*refs):
        p_ref = refs[0]
        w_refs, m_refs, v_refs = refs[1:1 + nv], refs[1 + nv:1 + 2 * nv], refs[1 + 2 * nv:1 + 3 * nv]
        outs = refs[1 + 3 * nv:]
        off = 0
        for i, n in enumerate(widths):
            g = p_ref[0:1, off:off + n]
            for k in range(1, p_ref.shape[0]):
                g = g + p_ref[k:k + 1, off:off + n]
            outs[i][...] = g
            outs[nv + i][...], outs[2 * nv + i][...], outs[3 * nv + i][...] = _adamw_math(w_refs[i][...], g, m_refs[i][...], v_refs[i][...])
            off += n

    vec_specs = [_full((1, n)) for n in widths]
    res = pl.pallas_call(
        body, name="adamw_vectors", grid=(1,),
        in_specs=[_full(parts.shape)] + vec_specs * 3, out_specs=vec_specs * 4,
        out_shape=[jax.ShapeDtypeStruct((1, n), F32) for n in widths] * 4, compiler_params=_cparams(1),
    )(parts, *ws, *ms, *vs)
    return [res[t * nv:(t + 1) * nv] for t in range(4)]


def _sum_peers_adamw(me, g, recv, kind, w, m, v, name, after):
    n, R, C = recv.shape
    tr = _row_tile(R)
    nr = R // tr
    if kind == "col":
        g_spec = pl.BlockSpec((tr, C), lambda r, me_ref: (r, me_ref[0]))
    else:
        g_spec = pl.BlockSpec((tr, C), lambda r, me_ref: (me_ref[0] * nr + r, 0))

    def body(me_ref, own_ref, r_ref, w_ref, m_ref, v_ref, after_ref, g_ref, d_ref, m_out, v_out):
        acc = None
        for k in range(n):
            term = jnp.where(me_ref[0] == k, own_ref[...], r_ref[k].astype(F32))
            acc = term if acc is None else acc + term
        g_ref[...] = acc
        d_ref[...], m_out[...], v_out[...] = _adamw_math(w_ref[...], acc, m_ref[...], v_ref[...])

    blk = pl.BlockSpec((tr, C), lambda r, me_ref: (r, 0))
    return pl.pallas_call(
        body, name=name,
        grid_spec=pltpu.PrefetchScalarGridSpec(
            num_scalar_prefetch=1, grid=(nr,),
            in_specs=[g_spec, pl.BlockSpec((n, tr, C), lambda r, me_ref: (0, r, 0)), blk, blk, blk, ANY], out_specs=[blk] * 4),
        out_shape=[jax.ShapeDtypeStruct((R, C), F32)] * 4, compiler_params=_cparams(1),
    )(me, g, recv, w, m, v, after)


def _ada_cols(me, c_all, w_ada16, b_ada):
    nd, D = c_all.shape
    W = w_ada16.shape[1]

    def body(me_ref, c_ref, w_ref, b_ref, o_ref):
        cv = c_ref[...]
        r = _dot((cv * _sigmoid(cv)).astype(BF16), w_ref[...]) + b_ref[...]
        for b in range(nd):
            o_ref[b] = r[b:b + 1, :]

    return pl.pallas_call(
        body, name="ada_cols",
        grid_spec=pltpu.PrefetchScalarGridSpec(
            num_scalar_prefetch=1, grid=(1,),
            in_specs=[_full((nd, D)), _full((D, W)), pl.BlockSpec((1, W), lambda i, me_ref: (0, me_ref[0]))],
            out_specs=_full((nd, 1, W))),
        out_shape=jax.ShapeDtypeStruct((nd, 1, W), F32), compiler_params=_cparams(1),
    )(me, c_all, w_ada16, b_ada)


def _rows_to_owners(part, name):
    def body(p_ref, o_ref, send_sems, recv_sems, local_sem):
        x, y, c = _mesh_pos()
        me = 4 * x + 2 * y + c
        mine = pltpu.make_async_copy(p_ref.at[me], o_ref.at[me], local_sem.at[0])
        sends, arrivals = _peer_copies(lambda a, p: p_ref.at[p], [o_ref], send_sems, recv_sems)
        mine.start()
        for cp in sends:
            cp.start()
        for cp in arrivals:
            cp.wait_recv()
        for cp in sends:
            cp.wait_send()
        mine.wait()

    return pl.pallas_call(
        body, name=name, in_specs=[ANY], out_specs=ANY, out_shape=jax.ShapeDtypeStruct(part.shape, part.dtype),
        scratch_shapes=_peer_sems(1) + [pltpu.SemaphoreType.DMA((1,))],
    )(part)


def _ada_mod(me, c_all, w_ada16, b_ada):
    part = _ada_cols(me, c_all, w_ada16, b_ada)
    return _rows_to_owners(part, "scatter_mod").reshape(1, N_DEV * w_ada16.shape[1])


def _ada_bwd(me, small_g, D, W, c_block):
    def body(me_ref, c_ref, dm_ref, o_ref):
        cv = c_ref[...]
        ca = (cv * _sigmoid(cv)).astype(BF16).astype(F32)
        dm = dm_ref[...].astype(BF16).astype(F32)
        o_ref[...] = lax.dot_general(ca, dm, (((0,), (0,)), ((), ())), precision=lax.Precision.HIGHEST,
                                     preferred_element_type=F32)

    return pl.pallas_call(
        body, name="ada_bwd",
        grid_spec=pltpu.PrefetchScalarGridSpec(
            num_scalar_prefetch=1, grid=(1,),
            in_specs=[pl.BlockSpec((N_DEV, D), lambda i, me_ref: (0, c_block)),
                      pl.BlockSpec((N_DEV, W), lambda i, me_ref: (0, me_ref[0]))],
            out_specs=pl.BlockSpec((D, W), lambda i, me_ref: (0, 0))),
        out_shape=jax.ShapeDtypeStruct((D, W), F32), compiler_params=_cparams(1),
    )(me, small_g, small_g)


def _fwd_in(x, mod, w_in_g, b_in, shards):
    S, D = x.shape
    nb, _, wb = w_in_g.shape
    N = nb * wb
    TM = TOKEN_TILE
    n_tiles = S // TM
    ns = len(shards)

    def body(*refs):
        x_ref, mod_ref, w_hbm, b_ref = refs[:4]
        ins = refs[4:4 + ns]
        z_ref, h_ref, zg_ref = refs[4 + ns:7 + ns]
        outs = refs[7 + ns:7 + 2 * ns]
        w_v, sems, send_sems, recv_sems, local_sems = refs[7 + 2 * ns:]
        i = pl.program_id(0)
        x_, y_, c_ = _mesh_pos()
        me = 4 * x_ + 2 * y_ + c_
        mine = [pltpu.make_async_copy(ins[a], outs[a].at[me], local_sems.at[a]) for a in range(ns)]
        sends, arrivals = _peer_copies(lambda a, p: ins[a], outs, send_sems, recv_sems)

        @pl.when(i == 0)
        def _():
            for cp in sends + mine:
                cp.start()
            _load_cols(w_hbm, w_v, sems)

        xhat, _ = _ln(x_ref[...])
        hb = (xhat * (1.0 + mod_ref[:, D:2 * D]) + mod_ref[:, 0:D]).astype(BF16)
        h_ref[...] = hb
        z_ref[...] = _dot(hb, w_v[...]) + b_ref[...]
        zg_ref[:, 0:D] = z_ref[:, 2 * D:3 * D].astype(BF16)
        zg_ref[:, D:3 * D] = z_ref[:, 5 * D:7 * D].astype(BF16)

        @pl.when(i == n_tiles - 1)
        def _():
            for cp in arrivals:
                cp.wait_recv()
            for cp in sends:
                cp.wait_send()
            for cp in mine:
                cp.wait()

    res = pl.pallas_call(
        body, name="fwd_in", grid=(n_tiles,),
        in_specs=[pl.BlockSpec((TM, D), lambda i: (i, 0)), _full(mod.shape), ANY, _full((1, N))] + [ANY] * ns,
        out_specs=[pl.BlockSpec((TM, N), lambda i: (i, 0)), pl.BlockSpec((TM, D), lambda i: (i, 0)),
                   pl.BlockSpec((TM, 3 * D), lambda i: (i, 0))] + [ANY] * ns,
        out_shape=[jax.ShapeDtypeStruct((S, N), F32), jax.ShapeDtypeStruct((S, D), BF16), jax.ShapeDtypeStruct((S, 3 * D), BF16)]
        + [jax.ShapeDtypeStruct((N_DEV,) + a.shape, a.dtype) for a in shards],
        scratch_shapes=[pltpu.VMEM((D, N), BF16), pltpu.SemaphoreType.DMA((nb,))] + _peer_sems(ns) + [pltpu.SemaphoreType.DMA((ns,))],
        compiler_params=_cparams(1),
    )(x, mod, w_in_g, b_in, *shards)
    return res[0], res[1], res[2], list(res[3:])


def _rows_from(win, o, R):
    if o % 8 == 0:
        return win[o:o + R, :]
    return pltpu.roll(win, win.shape[0] - o, axis=0)[0:R, :]


def _causal_taps(buf, w_ref, o_ref, K, halo, TM, D):
    R, C = CONV_ROWS, CONV_LANES
    base = halo - (K - 1)

    for r0 in range(0, TM, R):
        for c0 in range(0, D, C):
            win = buf[r0:r0 + R + halo, c0:c0 + C]
            acc = jnp.zeros((R, C), F32)
            for k in range(K):
                acc = acc + w_ref[k:k + 1, c0:c0 + C] * _rows_from(win, base + k, R)
            o_ref[r0:r0 + R, c0:c0 + C] = acc


def _fwd_mix(z, x, mod, wa, ba, lag, lab, w_a_out, b_a_out, wb, w_b_out, w_o, b_o):
    S, D = x.shape
    N = z.shape[1]
    TM = TOKEN_TILE
    KA, KB = wa.shape[0], wb.shape[0]

    def body(z_ref, x_ref, mod_ref, wa_ref, ba_ref, lag_ref, lab_ref, wao_hbm, bao_ref, wb_ref, wbo_hbm, wo_hbm, bo_ref,
             u1_ref, ya_ref, yb_ref, q_ref, out_ref, r1_ref, u2_ref, v_ref, mg_ref,
             ubuf, pbuf, qbuf, wao, wbo, wo, sems):
        i = pl.program_id(0)

        @pl.when(i == 0)
        def _():
            _load_whole([(wao_hbm, wao), (wbo_hbm, wbo), (wo_hbm, wo)], sems)
            ubuf[0:HALO_A, :] = jnp.zeros((HALO_A, D), F32)
            pbuf[0:HALO_B, :] = jnp.zeros((HALO_B, D), F32)

        ubuf[HALO_A:HALO_A + TM, :] = z_ref[:, 0:D] * _sigmoid(z_ref[:, D:2 * D])
        _causal_taps(ubuf, wa_ref, u1_ref, KA, HALO_A, TM, D)
        ubuf[0:HALO_A, :] = ubuf[TM:TM + HALO_A, :]
        u1 = u1_ref[...] + ba_ref[...]
        u1_ref[...] = u1
        xa, _ = _ln(u1)
        l = xa * lag_ref[...] + lab_ref[...]
        u2 = (l * _sigmoid(l)).astype(BF16)
        u2_ref[...] = u2
        ya = _dot(u2, wao[...]) + bao_ref[...]
        ya_ref[...] = ya.astype(BF16)
        pbuf[HALO_B:HALO_B + TM, :] = z_ref[:, 3 * D:4 * D] * z_ref[:, 4 * D:5 * D]
        _causal_taps(pbuf, wb_ref, qbuf, KB, HALO_B, TM, D)
        pbuf[0:HALO_B, :] = pbuf[TM:TM + HALO_B, :]
        q_ref[...] = qbuf[...].astype(BF16)
        v = (z_ref[:, 2 * D:3 * D] * qbuf[...]).astype(BF16)
        v_ref[...] = v
        yb = _dot(v, wbo[...])
        yb_ref[...] = yb.astype(BF16)
        mg = (_sigmoid(z_ref[:, 5 * D:6 * D]) * ya + _sigmoid(z_ref[:, 6 * D:7 * D]) * yb).astype(BF16)
        mg_ref[...] = mg
        out = _dot(mg, wo[...]) + bo_ref[...]
        out_ref[...] = out.astype(BF16)
        r1_ref[...] = ALPHA * x_ref[...] + (1.0 + mod_ref[:, 2 * D:3 * D]) * out

    tile = pl.BlockSpec((TM, D), lambda i: (i, 0))
    vec = _full((1, D))
    return pl.pallas_call(
        body, name="fwd_mix", grid=(S // TM,),
        in_specs=[pl.BlockSpec((TM, N), lambda i: (i, 0)), tile, _full(mod.shape), _full((KA, D)), vec, vec, vec, ANY, vec,
                  _full((KB, D)), ANY, ANY, vec],
        out_specs=[tile] * 9,
        out_shape=[jax.ShapeDtypeStruct((S, D), dt) for dt in (F32, BF16, BF16, BF16, BF16, F32, BF16, BF16, BF16)],
        scratch_shapes=[pltpu.VMEM((TM + HALO_A, D), F32), pltpu.VMEM((TM + HALO_B, D), F32), pltpu.VMEM((TM, D), F32),
                        pltpu.VMEM((D, D), BF16), pltpu.VMEM((D, D), BF16), pltpu.VMEM((D, D), BF16),
                        pltpu.SemaphoreType.DMA((3,))],
        compiler_params=_cparams(1),
    )(z, x, mod, wa, ba, lag, lab, w_a_out, b_a_out, wb, w_b_out, w_o, b_o)


def _ffn(r1, tgt, mod, ln1_g, ln1_b, w_up_g, b_up, w_down, b_down, ln2_g, ln2_b):
    S, D = r1.shape
    nb, _, wb = w_up_g.shape
    FF = nb * wb
    TM = TOKEN_TILE
    CF = FFN_CHUNK
    n_tiles = S // TM

    def body(r1_ref, t_ref, mod_ref, g1_ref, b1_ref, wup_hbm, bup_ref, wdn_hbm, bdn_ref, g2_ref, b2_ref,
             dx1_ref, h2_ref, f_ref, dhu_ref, do2_ref, acc_ref, accup_ref,
             wup, wdn, relu_buf, sems, sem2):
        i = pl.program_id(0)

        @pl.when(i == 0)
        def _():
            _load_cols(wup_hbm, wup, sems)
            _load_whole([(wdn_hbm, wdn)], sem2)
            acc_ref[...] = jnp.zeros(acc_ref.shape, F32)
            accup_ref[...] = jnp.zeros(accup_ref.shape, F32)

        scale2, shift2, gate2 = mod_ref[:, 4 * D:5 * D], mod_ref[:, 3 * D:4 * D], mod_ref[:, 5 * D:6 * D]
        xhat1, _ = _ln(r1_ref[...])
        x1 = xhat1 * g1_ref[...] + b1_ref[...]
        xh0, rstd0 = _ln(x1)
        h2 = (xh0 * (1.0 + scale2) + shift2).astype(BF16)
        h2_ref[...] = h2
        out2 = jnp.zeros((TM, D), F32) + bdn_ref[...]
        for c0 in range(0, FF, CF):
            hu = _dot(h2, wup[:, c0:c0 + CF]) + bup_ref[:, c0:c0 + CF]
            rl = jnp.maximum(hu, 0.0)
            relu_buf[:, c0:c0 + CF] = rl
            fb = (rl * rl).astype(BF16)
            f_ref[:, c0:c0 + CF] = fb
            out2 = out2 + _dot(fb, wdn[c0:c0 + CF, :])
        r2 = ALPHA * x1 + (1.0 + gate2) * out2
        xh2, rstd2 = _ln(r2)
        e = xh2 * g2_ref[...] + b2_ref[...] - t_ref[...]
        acc_ref[6:7, :] += _colsum(e * e)
        dy = e * (1.0 / D)
        acc_ref[1:2, :] += _colsum(dy * xh2)
        acc_ref[2:3, :] += _colsum(dy)
        dr2 = _ln_bwd(dy * g2_ref[...], xh2, rstd2)
        acc_ref[3:4, :] += _colsum(dr2 * out2)
        do2 = (1.0 + gate2) * dr2
        acc_ref[0:1, :] += _colsum(do2)
        do2b = do2.astype(BF16)
        do2_ref[...] = do2b
        dh2 = jnp.zeros((TM, D), F32)
        for c0 in range(0, FF, CF):
            dhu = _dot_nt(do2b, wdn[c0:c0 + CF, :]) * (2.0 * relu_buf[:, c0:c0 + CF])
            accup_ref[:, c0:c0 + CF] += _colsum(dhu)
            dhub = dhu.astype(BF16)
            dhu_ref[:, c0:c0 + CF] = dhub
            dh2 = dh2 + _dot_nt(dhub, wup[:, c0:c0 + CF])
        acc_ref[4:5, :] += _colsum(dh2 * xh0)
        acc_ref[5:6, :] += _colsum(dh2)
        dx1_ref[...] = ALPHA * dr2 + _ln_bwd(dh2 * (1.0 + scale2), xh0, rstd0)

        @pl.when(i == n_tiles - 1)
        def _():
            tot = jnp.sum(acc_ref[6:7, :], axis=-1, keepdims=True) * (0.5 / D)
            acc_ref[6:7, :] = jnp.broadcast_to(tot, (1, D))

    tile = pl.BlockSpec((TM, D), lambda i: (i, 0))
    wide = pl.BlockSpec((TM, FF), lambda i: (i, 0))
    vec = _full((1, D))
    return pl.pallas_call(
        body, name="ffn", grid=(n_tiles,),
        in_specs=[tile, tile, _full(mod.shape), vec, vec, ANY, _full((1, FF)), ANY, vec, vec, vec],
        out_specs=[tile, tile, wide, wide, tile, _full((8, D)), _full((1, FF))],
        out_shape=[jax.ShapeDtypeStruct((S, D), F32), jax.ShapeDtypeStruct((S, D), BF16), jax.ShapeDtypeStruct((S, FF), BF16),
                   jax.ShapeDtypeStruct((S, FF), BF16), jax.ShapeDtypeStruct((S, D), BF16),
                   jax.ShapeDtypeStruct((8, D), F32), jax.ShapeDtypeStruct((1, FF), F32)],
        scratch_shapes=[pltpu.VMEM((D, FF), BF16), pltpu.VMEM((FF, D), BF16), pltpu.VMEM((TM, FF), F32),
                        pltpu.SemaphoreType.DMA((nb,)), pltpu.SemaphoreType.DMA((1,))],
        compiler_params=_cparams(1),
    )(r1, tgt, mod, ln1_g, ln1_b, w_up_g, b_up, w_down, b_down, ln2_g, ln2_b)


def _dw(a, b, name, msplit=1, nsplit=1, wire=False, exchange=None):
    S, M = a.shape
    N = b.shape[1]
    TK = min(DW_TILE, S)
    mb, nbk = M // msplit, N // nsplit
    nk = S // TK
    srcs, kinds = exchange if exchange else ((), ())
    ns = len(srcs)

    def body(*refs):
        a_ref, b_ref = refs[:2]
        ins = refs[2:2 + ns]
        o_ref = refs[2 + ns]
        o16_ref = refs[3 + ns] if wire else None
        rest = refs[3 + ns + (1 if wire else 0):]
        k = pl.program_id(2)
        if ns:
            outs, (send_sems, recv_sems) = rest[:ns], rest[ns:]
            sends, arrivals = _peer_copies(lambda q, p: _shard_of(ins[q], kinds[q], p), outs, send_sems, recv_sems)
            first = (pl.program_id(0) == 0) & (pl.program_id(1) == 0) & (k == 0)
            last = (pl.program_id(0) == msplit - 1) & (pl.program_id(1) == nsplit - 1) & (k == nk - 1)

            @pl.when(first)
            def _():
                for cp in sends:
                    cp.start()

        @pl.when(k == 0)
        def _():
            o_ref[...] = jnp.zeros(o_ref.shape, F32)

        o_ref[...] += _dot_tn(a_ref[...], b_ref[...])

        if wire:
            @pl.when(k == nk - 1)
            def _():
                o16_ref[...] = o_ref[...].astype(BF16)

        if ns:
            @pl.when(last)
            def _():
                for cp in arrivals:
                    cp.wait_recv()
                for cp in sends:
                    cp.wait_send()

    oblk = pl.BlockSpec((mb, nbk), lambda i, j, k: (i, j))
    res = pl.pallas_call(
        body, name=name, grid=(msplit, nsplit, nk),
        in_specs=[pl.BlockSpec((TK, mb), lambda i, j, k: (k, i)), pl.BlockSpec((TK, nbk), lambda i, j, k: (k, j))] + [ANY] * ns,
        out_specs=[oblk] + ([oblk] if wire else []) + [ANY] * ns,
        out_shape=[jax.ShapeDtypeStruct((M, N), F32)] + ([jax.ShapeDtypeStruct((M, N), BF16)] if wire else [])
        + [jax.ShapeDtypeStruct((N_DEV,) + _shard_shape(g.shape, kd), g.dtype) for g, kd in zip(srcs, kinds)],
        scratch_shapes=_peer_sems(ns) if ns else [],
        compiler_params=_cparams(3),
    )(a, b, *srcs)
    return res if (wire or ns) else res[0]


def _bwd_mix1(dx1, r1, out, ya, yb, q, u1, zg, mod, ln1_g, lag, lab, w_o, w_a_out, w_b_out):
    S, D = dx1.shape
    TM = TOKEN_TILE

    def body(dx1_ref, r1_ref, out_ref, ya_ref, yb_ref, q_ref, u1_ref, zg_ref, mod_ref, g1_ref, lag_ref, lab_ref,
             wo_hbm, wao_hbm, wbo_hbm,
             dxp_ref, du1_ref, dq_ref, dzc_ref, dout_ref, dya_ref, dyb_ref, acc_ref,
             wo, wao, wbo, sems):
        @pl.when(pl.program_id(0) == 0)
        def _():
            _load_whole([(wo_hbm, wo), (wao_hbm, wao), (wbo_hbm, wbo)], sems)
            acc_ref[...] = jnp.zeros(acc_ref.shape, F32)

        dx1v = dx1_ref[...]
        xhat1, rstd1 = _ln(r1_ref[...])
        acc_ref[0:1, :] += _colsum(dx1v * xhat1)
        acc_ref[1:2, :] += _colsum(dx1v)
        dr1 = _ln_bwd(dx1v * g1_ref[...], xhat1, rstd1)
        dxp_ref[...] = (ALPHA * dr1).astype(BF16)
        acc_ref[2:3, :] += _colsum(dr1 * out_ref[...].astype(F32))
        dout = (1.0 + mod_ref[:, 2 * D:3 * D]) * dr1
        acc_ref[3:4, :] += _colsum(dout)
        doutb = dout.astype(BF16)
        dout_ref[...] = doutb
        dmg = _dot_nt(doutb, wo[...])
        sga, sgb = _sigmoid(zg_ref[:, D:2 * D].astype(F32)), _sigmoid(zg_ref[:, 2 * D:3 * D].astype(F32))
        dga = dmg * ya_ref[...].astype(F32) * sga * (1.0 - sga)
        dgb = dmg * yb_ref[...].astype(F32) * sgb * (1.0 - sgb)
        acc_ref[9:10, :] += _colsum(dga)
        acc_ref[10:11, :] += _colsum(dgb)
        dzc_ref[:, D:2 * D] = dga.astype(BF16)
        dzc_ref[:, 2 * D:3 * D] = dgb.astype(BF16)
        dya = dmg * sga
        acc_ref[4:5, :] += _colsum(dya)
        dyab = dya.astype(BF16)
        dya_ref[...] = dyab
        dybb = (dmg * sgb).astype(BF16)
        dyb_ref[...] = dybb
        du2 = _dot_nt(dyab, wao[...])
        xa, rstda = _ln(u1_ref[...])
        l = xa * lag_ref[...] + lab_ref[...]
        sl = _sigmoid(l)
        dl = du2 * (sl * (1.0 + l * (1.0 - sl)))
        acc_ref[5:6, :] += _colsum(dl * xa)
        acc_ref[6:7, :] += _colsum(dl)
        du1 = _ln_bwd(dl * lag_ref[...], xa, rstda)
        acc_ref[7:8, :] += _colsum(du1)
        du1_ref[...] = du1.astype(BF16)
        dv = _dot_nt(dybb, wbo[...])
        dgbk = dv * q_ref[...].astype(F32)
        acc_ref[8:9, :] += _colsum(dgbk)
        dzc_ref[:, 0:D] = dgbk.astype(BF16)
        dq_ref[...] = (dv * zg_ref[:, 0:D].astype(F32)).astype(BF16)

    tile = pl.BlockSpec((TM, D), lambda i: (i, 0))
    vec = _full((1, D))
    return pl.pallas_call(
        body, name="bwd_mix1", grid=(S // TM,),
        in_specs=[tile] * 7 + [pl.BlockSpec((TM, 3 * D), lambda i: (i, 0)), _full(mod.shape), vec, vec, vec, ANY, ANY, ANY],
        out_specs=[tile, tile, tile, pl.BlockSpec((TM, 3 * D), lambda i: (i, 0)), tile, tile, tile, _full((16, D))],
        out_shape=[jax.ShapeDtypeStruct((S, D), BF16)] * 3 + [jax.ShapeDtypeStruct((S, 3 * D), BF16)]
        + [jax.ShapeDtypeStruct((S, D), BF16)] * 3 + [jax.ShapeDtypeStruct((16, D), F32)],
        scratch_shapes=[pltpu.VMEM((D, D), BF16)] * 3 + [pltpu.SemaphoreType.DMA((3,))],
        compiler_params=_cparams(1),
    )(dx1, r1, out, ya, yb, q, u1, zg, mod, ln1_g, lag, lab, w_o, w_a_out, w_b_out)


def _anticausal_taps(dbuf, u_ref, w_ref, o_ref, dwacc, K, halo, TM, D):
    R, C = CONV_ROWS, CONV_LANES

    for r0 in range(0, TM, R):
        for c0 in range(0, D, C):
            win = dbuf[r0:r0 + R + halo, c0:c0 + C]
            uc = u_ref[r0:r0 + R, c0:c0 + C]
            acc = jnp.zeros((R, C), F32)
            for j in range(K):
                sh = _rows_from(win, j, R)
                k = K - 1 - j
                acc = acc + w_ref[k:k + 1, c0:c0 + C] * sh
                pr = uc * sh
                part = pr[0:8, :]
                for s in range(8, R, 8):
                    part = part + pr[s:s + 8, :]
                dwacc[k, :, c0:c0 + C] += part
            o_ref[r0:r0 + R, c0:c0 + C] = acc


def _bwd_mix2(du1, dq, z, dzc, x, dxp, mod, wa, wb, w_in_g):
    S, D = x.shape
    N = z.shape[1]
    nb, _, wbk = w_in_g.shape
    TM = TOKEN_TILE
    KA, KB = wa.shape[0], wb.shape[0]
    n_tiles = S // TM

    def body(du1_ref, dq_ref, zav_ref, zc_ref, zx_ref, dzc_ref, x_ref, dxp_ref, mod_ref, wa_ref, wb_ref, win_hbm,
             dz_ref, gx_ref, dwa_ref, dwb_ref, acc_ref, dbin_ref,
             dbuf, qbuf, ubuf, pbuf, obuf, dwa_acc, dwb_acc, w_v, sems):
        i = pl.program_id(0)

        @pl.when(i == 0)
        def _():
            _load_cols(win_hbm, w_v, sems)
            dbuf[TM:TM + HALO_A, :] = jnp.zeros((HALO_A, D), F32)
            qbuf[TM:TM + HALO_B, :] = jnp.zeros((HALO_B, D), F32)
            dwa_acc[...] = jnp.zeros(dwa_acc.shape, F32)
            dwb_acc[...] = jnp.zeros(dwb_acc.shape, F32)
            acc_ref[...] = jnp.zeros(acc_ref.shape, F32)
            dbin_ref[...] = jnp.zeros(dbin_ref.shape, F32)

        a_val = zav_ref[:, 0:D]
        sa = _sigmoid(zav_ref[:, D:2 * D])
        ubuf[...] = a_val * sa
        dbuf[0:TM, :] = du1_ref[...].astype(F32)
        _anticausal_taps(dbuf, ubuf, wa_ref, obuf, dwa_acc, KA, HALO_A, TM, D)
        dbuf[TM:TM + HALO_A, :] = dbuf[0:HALO_A, :]
        du0 = obuf[...]
        dav = du0 * sa
        dag = du0 * a_val * sa * (1.0 - sa)
        dbin_ref[:, 0:D] += _colsum(dav)
        dbin_ref[:, D:2 * D] += _colsum(dag)
        dz_ref[:, 0:D] = dav.astype(BF16)
        dz_ref[:, D:2 * D] = dag.astype(BF16)
        pbuf[...] = zc_ref[...] * zx_ref[...]
        qbuf[0:TM, :] = dq_ref[...].astype(F32)
        _anticausal_taps(qbuf, pbuf, wb_ref, obuf, dwb_acc, KB, HALO_B, TM, D)
        qbuf[TM:TM + HALO_B, :] = qbuf[0:HALO_B, :]
        dp = obuf[...]
        dgc = dp * zx_ref[...]
        dgx = dp * zc_ref[...]
        dbin_ref[:, 3 * D:4 * D] += _colsum(dgc)
        dbin_ref[:, 4 * D:5 * D] += _colsum(dgx)
        dz_ref[:, 3 * D:4 * D] = dgc.astype(BF16)
        dz_ref[:, 4 * D:5 * D] = dgx.astype(BF16)
        dz_ref[:, 2 * D:3 * D] = dzc_ref[:, 0:D]
        dz_ref[:, 5 * D:7 * D] = dzc_ref[:, D:3 * D]
        dh = _dot_nt(dz_ref[...], w_v[...])
        xhat, rstd = _ln(x_ref[...])
        acc_ref[0:1, :] += _colsum(dh * xhat)
        acc_ref[1:2, :] += _colsum(dh)
        gx_ref[...] = dxp_ref[...].astype(F32) + _ln_bwd(dh * (1.0 + mod_ref[:, D:2 * D]), xhat, rstd)

        @pl.when(i == n_tiles - 1)
        def _():
            for k in range(KA):
                dwa_ref[k:k + 1, :] = jnp.sum(dwa_acc[k], axis=0, keepdims=True)
            for k in range(KB):
                dwb_ref[k:k + 1, :] = jnp.sum(dwb_acc[k], axis=0, keepdims=True)

    rev = lambda i: n_tiles - 1 - i
    tile = pl.BlockSpec((TM, D), lambda i: (rev(i), 0))
    zcol = lambda k: pl.BlockSpec((TM, D), lambda i: (rev(i), k))
    return pl.pallas_call(
        body, name="bwd_mix2", grid=(n_tiles,),
        in_specs=[tile, tile, pl.BlockSpec((TM, 2 * D), lambda i: (rev(i), 0)), zcol(3), zcol(4),
                  pl.BlockSpec((TM, 3 * D), lambda i: (rev(i), 0)), tile, tile, _full(mod.shape), _full((KA, D)), _full((KB, D)), ANY],
        out_specs=[pl.BlockSpec((TM, N), lambda i: (rev(i), 0)), tile, _full((KA, D)), _full((KB, D)), _full((8, D)), _full((1, N))],
        out_shape=[jax.ShapeDtypeStruct((S, N), BF16), jax.ShapeDtypeStruct((S, D), F32), jax.ShapeDtypeStruct((KA, D), F32),
                   jax.ShapeDtypeStruct((KB, D), F32), jax.ShapeDtypeStruct((8, D), F32), jax.ShapeDtypeStruct((1, N), F32)],
        scratch_shapes=[pltpu.VMEM((TM + HALO_A, D), F32), pltpu.VMEM((TM + HALO_B, D), F32), pltpu.VMEM((TM, D), F32),
                        pltpu.VMEM((TM, D), F32), pltpu.VMEM((TM, D), F32), pltpu.VMEM((KA, 8, D), F32), pltpu.VMEM((KB, 8, D), F32),
                        pltpu.VMEM((D, N), BF16), pltpu.SemaphoreType.DMA((nb,))],
        compiler_params=_cparams(1),
    )(du1, dq, z, z, z, dzc, x, dxp, mod, wa, wb, w_in_g)


PEER_REDUCED = ("w_up", "w_down", "w_o", "w_a_out", "w_b_out", "conv_a_w", "conv_b_w")


def _local_grads(me, x, c, c_all, tgt, w_ada16, w_in_g, shards, vecs):
    D = x.shape[1]
    rows = lambda a: a.reshape(a.shape[0] * a.shape[1], a.shape[2])
    cols = lambda a: jnp.transpose(a, (1, 0, 2)).reshape(a.shape[1], a.shape[0] * a.shape[2])
    mod = _ada_mod(me, c_all, w_ada16, vecs["b_ada"])
    z, h1, zg, gathered = _fwd_in(x, mod, w_in_g, vecs["b_in"], shards)
    w_a_out, w_b_out, w_o, w_up_g, w_down = rows(gathered[0]), rows(gathered[1]), rows(gathered[2]), gathered[3], rows(gathered[4])
    wa, wb = cols(gathered[5]), cols(gathered[6])
    u1, ya, yb, q, out, r1, u2, v, mg = _fwd_mix(
        z, x, mod, wa, vecs["conv_a_b"], vecs["ln_a_g"], vecs["ln_a_b"], w_a_out, vecs["b_a_out"], wb, w_b_out, w_o, vecs["b_o"])
    dx1, h2, f, dhu, do2, acc_f, acc_up = _ffn(
        r1, tgt, mod, vecs["ln1_g"], vecs["ln1_b"], w_up_g, vecs["b_up"], w_down, vecs["b_down"], vecs["ln2_g"], vecs["ln2_b"])
    g_up, g_up16 = _dw(h2, dhu, "dw_up", nsplit=2, wire=True)
    g_down, g_down16 = _dw(f, do2, "dw_down", msplit=2, wire=True)
    dxp, du1, dq, dzc, doutb, dyab, dybb, acc_1 = _bwd_mix1(
        dx1, r1, out, ya, yb, q, u1, zg, mod, vecs["ln1_g"], vecs["ln_a_g"], vecs["ln_a_b"], w_o, w_a_out, w_b_out)
    g_o, g_o16 = _dw(mg, doutb, "dw_o", wire=True)
    g_a_out, g_a_out16 = _dw(u2, dyab, "dw_a_out", wire=True)
    g_b_out, g_b_out16 = _dw(v, dybb, "dw_b_out", wire=True)
    dz, gx, g_wa, g_wb, acc_2, db_in = _bwd_mix2(du1, dq, z, dzc, x, dxp, mod, wa, wb, w_in_g)
    full = dict(zip(PEER_REDUCED, (g_up, g_down, g_o, g_a_out, g_b_out, g_wa, g_wb)))
    res = _dw(h1, dz, "dw_in", nsplit=4,
              exchange=((g_up16, g_down16, g_o16, g_a_out16, g_b_out16, g_wa, g_wb), [SHARD_KIND[n] for n in PEER_REDUCED]))
    g_in = res[0]
    peers = {n: (full[n], r) for n, r in zip(PEER_REDUCED, res[1:])}
    row = lambda acc, k: (acc, k, 0, D)
    pieces = [
        row(acc_2, 1), row(acc_2, 0), row(acc_1, 2), row(acc_f, 5), row(acc_f, 4), row(acc_f, 3),
        (db_in, 0, 0, 2 * D), row(acc_1, 8), (db_in, 0, 3 * D, 2 * D), row(acc_1, 9), row(acc_1, 10),
        row(acc_1, 7), row(acc_1, 5), row(acc_1, 6), row(acc_1, 4), row(acc_1, 3), row(acc_1, 0), row(acc_1, 1),
        (acc_up, 0, 0, acc_up.shape[1]), row(acc_f, 0), row(acc_f, 1), row(acc_f, 2), (c, 0, 0, D)]
    return acc_f[6, 0], gx, g_in, peers, pieces


WEIGHTS = ("w_ada", "b_ada", "w_in", "b_in", "conv_a_w", "conv_a_b", "ln_a_g", "ln_a_b", "w_a_out", "b_a_out", "conv_b_w",
           "w_b_out", "w_o", "b_o", "ln1_g", "ln1_b", "w_up", "b_up", "w_down", "b_down", "ln2_g", "ln2_b")
VECTORS = ("b_ada", "b_in", "conv_a_b", "ln_a_g", "ln_a_b", "b_a_out", "b_o", "ln1_g", "ln1_b", "b_up", "b_down", "ln2_g", "ln2_b")
SHARD_KIND = {"w_in": "col", "w_a_out": "row", "w_b_out": "row", "w_o": "row", "w_up": "col", "w_down": "row",
              "conv_a_w": "col", "conv_b_w": "col"}


def kernel(x, c, w_ada, b_ada, w_in, b_in, conv_a_w, conv_a_b, ln_a_g, ln_a_b, w_a_out, b_a_out, conv_b_w, w_b_out, w_o, b_o, ln1_g, ln1_b, w_up, b_up, w_down, b_down, ln2_g, ln2_b, loss_target, m_w_ada, m_b_ada, m_w_in, m_b_in, m_conv_a_w, m_conv_a_b, m_ln_a_g, m_ln_a_b, m_w_a_out, m_b_a_out, m_conv_b_w, m_w_b_out, m_w_o, m_b_o, m_ln1_g, m_ln1_b, m_w_up, m_b_up, m_w_down, m_b_down, m_ln2_g, m_ln2_b, v_w_ada, v_b_ada, v_w_in, v_b_in, v_conv_a_w, v_conv_a_b, v_ln_a_g, v_ln_a_b, v_w_a_out, v_b_a_out, v_conv_b_w, v_w_b_out, v_w_o, v_b_o, v_ln1_g, v_ln1_b, v_w_up, v_b_up, v_w_down, v_b_down, v_ln2_g, v_ln2_b):
    args = dict(locals())
    w = {n: args[n][0] for n in WEIGHTS}
    m = {n: args["m_" + n][0] for n in WEIGHTS}
    v = {n: args["v_" + n][0] for n in WEIGHTS}
    w = {n: (a[None, :] if a.ndim == 1 else a) for n, a in w.items()}
    m = {n: (a[None, :] if a.ndim == 1 else a) for n, a in m.items()}
    v = {n: (a[None, :] if a.ndim == 1 else a) for n, a in v.items()}
    xs, tgt = x[0], loss_target[0]
    S, D = xs.shape
    me = (4 * lax.axis_index("x") + 2 * lax.axis_index("y") + lax.axis_index("c")).astype(jnp.int32).reshape(1)
    core = lax.axis_index("c").astype(jnp.int32).reshape(1)
    chip = (2 * lax.axis_index("x") + lax.axis_index("y")).astype(jnp.int32).reshape(1)

    bf = lambda n: w[n].astype(BF16)
    w_in_g, c_all = _all_gather([bf("w_in"), c], "gather_weights")
    shards = [bf("w_a_out"), bf("w_b_out"), bf("w_o"), bf("w_up"), bf("w_down"), w["conv_a_w"], w["conv_b_w"]]
    vecs = {n: w[n] for n in VECTORS}

    loss, gx, g_in, peers, pieces = _local_grads(me, xs, c, c_all.reshape(N_DEV, D), tgt, bf("w_ada"), w_in_g, shards, vecs)

    (from_sibling,) = _exchange_cores([g_in], [SHARD_KIND["w_in"]], "reduce_cores")
    own_in, wire_in = _add_own(core, g_in, from_sibling, SHARD_KIND["w_in"], "add_w_in")
    sems, wire_thru, landing, after = _chips_start(wire_in)

    res = {}
    for n, (g_full, recv) in peers.items():
        res[n] = _sum_peers_adamw(me, g_full, recv, SHARD_KIND[n], w[n], m[n], v[n], "adamw_" + n, after)
        after = res[n][0]
    small = _pack_rows(pieces, "pack_vectors", after)
    (small_g,) = _all_gather([small], "gather_vectors")
    small_g = small_g.reshape(N_DEV, small.shape[1])
    W = w["w_ada"].shape[1]
    n_vec = small.shape[1] - D
    g_ada = _ada_bwd(me, small_g, D, W, n_vec // D)
    res["w_ada"] = _sum_adamw(g_ada[None], w["w_ada"], m["w_ada"], v["w_ada"], "adamw_w_ada")
    by_kind = _adamw_vectors(small_g, [w[n] for n in VECTORS], [m[n] for n in VECTORS], [v[n] for n in VECTORS])
    for i, n in enumerate(VECTORS):
        res[n] = tuple(by_kind[t][i] for t in range(4))
    from_chips = _chips_wait(sems, wire_thru, landing, res["w_ada"][0])
    res["w_in"] = _sum_chips_adamw(chip, own_in, from_chips, w["w_in"], m["w_in"], v["w_in"], "adamw_w_in")

    loss = lax.psum(loss, ("x", "y", "c"))
    outs = [loss, gx[None]]
    for t in range(4):
        outs += [res[n][t].reshape(args[n].shape) for n in WEIGHTS]
    return tuple(outs)
```

```python
import functools

import jax
import jax.numpy as jnp
from jax import lax
from jax.experimental import pallas as pl
from jax.experimental.pallas import tpu as pltpu

F32 = jnp.float32
BF16 = jnp.bfloat16
MESH = pl.DeviceIdType.MESH

N_DEV = 8
LN_EPS = 1e-5
DEPTH = 1
ALPHA = (2.0 * DEPTH) ** 0.25
ADAM_LR, ADAM_B1, ADAM_B2, ADAM_EPS, ADAM_WD, ADAM_STEP = 0.001, 0.9, 0.999, 1e-08, 0.01, 10

VMEM_LIMIT = 60 * 1024 * 1024
TOKEN_TILE = 256
DW_TILE = 2048
HALO_A = 32
HALO_B = 8
CONV_ROWS, CONV_LANES = 256, 256
FFN_CHUNK = 1024


def _cparams(n_grid):
    return pltpu.CompilerParams(dimension_semantics=("arbitrary",) * n_grid, vmem_limit_bytes=VMEM_LIMIT)


def _full(shape):
    return pl.BlockSpec(shape, lambda *_: (0,) * len(shape))


ANY = pl.BlockSpec(memory_space=pl.ANY)


def _ln(x):
    mu = jnp.mean(x, axis=-1, keepdims=True)
    xc = x - mu
    var = jnp.mean(xc * xc, axis=-1, keepdims=True)
    rstd = lax.rsqrt(var + LN_EPS)
    return xc * rstd, rstd


def _ln_bwd(dxhat, xhat, rstd):
    m1 = jnp.mean(dxhat, axis=-1, keepdims=True)
    m2 = jnp.mean(dxhat * xhat, axis=-1, keepdims=True)
    return rstd * (dxhat - m1 - xhat * m2)


def _sigmoid(x):
    return 0.5 * jnp.tanh(0.5 * x) + 0.5


def _colsum(a):
    return jnp.sum(a, axis=0, keepdims=True)


def _dot(a, b):
    return jnp.dot(a, b, preferred_element_type=F32)


def _dot_nt(a, b):
    return lax.dot_general(a, b, (((1,), (1,)), ((), ())), preferred_element_type=F32)


def _dot_tn(a, b):
    return lax.dot_general(a, b, (((0,), (0,)), ((), ())), preferred_element_type=F32)


def _load_cols(src_hbm, dst_vmem, sems):
    nblk, _, w = src_hbm.shape
    cps = [pltpu.make_async_copy(src_hbm.at[j], dst_vmem.at[:, pl.ds(j * w, w)], sems.at[j]) for j in range(nblk)]
    for cp in cps:
        cp.start()
    for cp in cps:
        cp.wait()


def _load_whole(pairs, sems):
    cps = [pltpu.make_async_copy(s, d, sems.at[k]) for k, (s, d) in enumerate(pairs)]
    for cp in cps:
        cp.start()
    for cp in cps:
        cp.wait()


def _mesh_pos():
    return lax.axis_index("x"), lax.axis_index("y"), lax.axis_index("c")


def _all_gather(arrs, name):
    n = len(arrs)

    def body(*refs):
        ins, outs = refs[:n], refs[n:2 * n]
        send_sems, recv_sems, local_sems = refs[2 * n:]
        x, y, c = _mesh_pos()
        me, sibling = (x, y, c), (x, y, 1 - c)
        chips = [(1 - x, y), (x, 1 - y), (1 - x, 1 - y)]

        def slot(a, px, py, pc):
            return outs[a].at[4 * px + 2 * py + pc]

        def copy(a, k, block, to, src=None):
            return pltpu.make_async_remote_copy(
                src_ref=slot(a, *block) if src is None else src, dst_ref=slot(a, *block),
                send_sem=send_sems.at[a, k], recv_sem=recv_sems.at[a, k], device_id=to, device_id_type=MESH)

        mine = [pltpu.make_async_copy(ins[a], slot(a, *me), local_sems.at[a]) for a in range(n)]
        for cp in mine:
            cp.start()
        first = []
        for a in range(n):
            first.append(copy(a, 0, me, sibling, src=ins[a]))
            first += [copy(a, 1 + j, me, (*chip, c), src=ins[a]) for j, chip in enumerate(chips)]
        for cp in first:
            cp.start()
        passed = []
        for a in range(n):
            for j, chip in enumerate(chips):
                copy(a, 1 + j, (*chip, c), me).wait_recv()
                fwd = copy(a, 4 + j, (*chip, c), sibling)
                fwd.start()
                passed.append(fwd)
        for a in range(n):
            copy(a, 0, sibling, me).wait_recv()
            for j, chip in enumerate(chips):
                copy(a, 4 + j, (*chip, 1 - c), me).wait_recv()
        for cp in first + passed:
            cp.wait_send()
        for cp in mine:
            cp.wait()

    outs = pl.pallas_call(
        body, name=name,
        out_shape=[jax.ShapeDtypeStruct((N_DEV,) + a.shape, a.dtype) for a in arrs],
        in_specs=[ANY] * n, out_specs=[ANY] * n,
        scratch_shapes=[pltpu.SemaphoreType.DMA((n, 7)), pltpu.SemaphoreType.DMA((n, 7)), pltpu.SemaphoreType.DMA((n,))],
    )(*arrs)
    return list(outs)


def _peer_copies(src_of, dsts, send_sems, recv_sems):
    x, y, c = _mesh_pos()
    me = 4 * x + 2 * y + c
    sends, arrivals = [], []
    for a in range(len(dsts)):
        for k in range(1, N_DEV):
            px, py, pc = (1 - x if k & 4 else x), (1 - y if k & 2 else y), (1 - c if k & 1 else c)
            p = 4 * px + 2 * py + pc
            common = dict(send_sem=send_sems.at[a, k - 1], recv_sem=recv_sems.at[a, k - 1],
                          device_id=(px, py, pc), device_id_type=MESH)
            sends.append(pltpu.make_async_remote_copy(src_ref=src_of(a, p), dst_ref=dsts[a].at[me], **common))
            arrivals.append(pltpu.make_async_remote_copy(src_ref=dsts[a].at[p], dst_ref=dsts[a].at[p], **common))
    return sends, arrivals


def _peer_sems(n):
    return [pltpu.SemaphoreType.DMA((n, N_DEV - 1)), pltpu.SemaphoreType.DMA((n, N_DEV - 1))]


def _shard_of(ref, kind, j):
    if kind == "col":
        w = ref.shape[1] // N_DEV
        return ref.at[:, pl.ds(j * w, w)]
    h = ref.shape[0] // N_DEV
    return ref.at[pl.ds(j * h, h), :]


def _shard_shape(shape, kind):
    return (shape[0], shape[1] // N_DEV) if kind == "col" else (shape[0] // N_DEV, shape[1])


def _exchange_cores(grads, kinds, name):
    n = len(grads)

    def body(*refs):
        ins, outs = refs[:n], refs[n:2 * n]
        send_sems, recv_sems = refs[2 * n:]
        x, y, c = _mesh_pos()
        sibling = (x, y, 1 - c)
        sends = []
        for a in range(n):
            for s in range(4):
                sends.append(pltpu.make_async_remote_copy(
                    src_ref=_shard_of(ins[a], kinds[a], 2 * s + (1 - c)), dst_ref=outs[a].at[s],
                    send_sem=send_sems.at[a, s], recv_sem=recv_sems.at[a, s], device_id=sibling, device_id_type=MESH))
        for cp in sends:
            cp.start()
        for a in range(n):
            for s in range(4):
                pltpu.make_async_remote_copy(
                    src_ref=outs[a].at[s], dst_ref=outs[a].at[s],
                    send_sem=send_sems.at[a, s], recv_sem=recv_sems.at[a, s], device_id=sibling, device_id_type=MESH).wait_recv()
        for cp in sends:
            cp.wait_send()

    outs = pl.pallas_call(
        body, name=name,
        out_shape=[jax.ShapeDtypeStruct((4,) + _shard_shape(g.shape, k), F32) for g, k in zip(grads, kinds)],
        in_specs=[ANY] * n, out_specs=[ANY] * n,
        scratch_shapes=[pltpu.SemaphoreType.DMA((n, 4)), pltpu.SemaphoreType.DMA((n, 4))],
    )(*grads)
    return list(outs)


HBM = pl.BlockSpec(memory_space=pltpu.HBM)
SEM = pl.BlockSpec(memory_space=pltpu.SEMAPHORE)
DATAFLOW = pltpu.SideEffectType.DATAFLOW_SIDE_EFFECTING


def _chip_copies(src_ref, land_ref, send_sems, recv_sems):
    x, y, c = _mesh_pos()
    my_slot = 2 * x + y
    sends, arrivals = [], []
    for j, (px, py) in enumerate([(1 - x, y), (x, 1 - y), (1 - x, 1 - y)]):
        common = dict(send_sem=send_sems[j], recv_sem=recv_sems[j], device_id=(px, py, c), device_id_type=MESH)
        sends.append(pltpu.make_async_remote_copy(src_ref=src_ref.at[2 * px + py], dst_ref=land_ref.at[my_slot], **common))
        arrivals.append(pltpu.make_async_remote_copy(src_ref=src_ref.at[2 * px + py], dst_ref=land_ref.at[2 * px + py], **common))
    return sends, arrivals


def _chips_start(part):
    def body(src_ref, land_ref, s0, s1, s2, r0, r1, r2, src_thru, land_thru, token):
        sends, _ = _chip_copies(src_ref, land_ref, (s0, s1, s2), (r0, r1, r2))
        for cp in sends:
            cp.start()
        token[...] = jnp.zeros_like(token)

    res = pl.pallas_call(
        body, name="reduce_chips_start",
        out_shape=(pltpu.SemaphoreType.DMA(()),) * 6 + (pltpu.HBM(part.shape, part.dtype), pltpu.HBM(part.shape, part.dtype),
                                                        jax.ShapeDtypeStruct((8, 128), F32)),
        in_specs=(HBM, HBM), out_specs=(SEM,) * 6 + (HBM, HBM, pl.BlockSpec(memory_space=pltpu.VMEM)),
        input_output_aliases={0: 6, 1: 7}, compiler_params=pltpu.CompilerParams(has_side_effects=DATAFLOW),
    )(pltpu.with_memory_space_constraint(part, pltpu.HBM),
      pltpu.with_memory_space_constraint(lax.empty(part.shape, part.dtype), pltpu.HBM))
    return res[:6], res[6], res[7], res[8]


def _chips_wait(sems, src_thru, land_thru, after):
    def body(src_ref, land_ref, s0, s1, s2, r0, r1, r2, after_ref, src_dead, got_ref):
        sends, arrivals = _chip_copies(src_ref, land_ref, (s0, s1, s2), (r0, r1, r2))
        for cp in sends:
            cp.wait_send()
        for cp in arrivals:
            cp.wait_recv()

    return pl.pallas_call(
        body, name="reduce_chips_wait",
        out_shape=(pltpu.HBM(src_thru.shape, src_thru.dtype), pltpu.HBM(land_thru.shape, land_thru.dtype)),
        in_specs=(HBM, HBM) + (SEM,) * 6 + (ANY,), out_specs=(HBM, HBM), input_output_aliases={0: 0, 1: 1},
        compiler_params=pltpu.CompilerParams(has_side_effects=DATAFLOW),
    )(src_thru, land_thru, *sems, after)[1]


def _row_tile(rows):
    for t in (256, 128, 64, 32, 16, 8):
        if rows % t == 0:
            return t
    return rows


def _wire_dtype(rows):
    return BF16 if rows % 16 == 0 else F32


def _add_own(core, g, recv, kind, name):
    ns, R, C = recv.shape
    tr = _row_tile(R)
    nr = R // tr
    if kind == "col":
        g_spec = pl.BlockSpec((tr, C), lambda s, r, c_ref: (r, 2 * s + c_ref[0]))
    else:
        g_spec = pl.BlockSpec((tr, C), lambda s, r, c_ref: ((2 * s + c_ref[0]) * nr + r, 0))
    slab = pl.BlockSpec((None, tr, C), lambda s, r, c_ref: (s, r, 0))

    def body(c_ref, g_ref, r_ref, o32_ref, o16_ref):
        t = g_ref[...] + r_ref[...]
        o32_ref[...] = t
        o16_ref[...] = t.astype(o16_ref.dtype)

    return pl.pallas_call(
        body, name=name,
        grid_spec=pltpu.PrefetchScalarGridSpec(num_scalar_prefetch=1, grid=(ns, nr), in_specs=[g_spec, slab], out_specs=[slab, slab]),
        out_shape=[jax.ShapeDtypeStruct((ns, R, C), F32), jax.ShapeDtypeStruct((ns, R, C), _wire_dtype(R))],
        compiler_params=_cparams(2),
    )(core, g, recv)


def _adamw_math(w, g, m, v):
    m2 = ADAM_B1 * m + (1.0 - ADAM_B1) * g
    v2 = ADAM_B2 * v + (1.0 - ADAM_B2) * (g * g)
    m_hat = m2 / (1.0 - ADAM_B1 ** ADAM_STEP)
    v_hat = v2 / (1.0 - ADAM_B2 ** ADAM_STEP)
    delta = -ADAM_LR * (m_hat / (jnp.sqrt(v_hat) + ADAM_EPS) + ADAM_WD * w)
    return delta, m2, v2


def _sum_adamw(parts, w, m, v, name):
    n, R, C = parts.shape
    tr = _row_tile(R)

    def body(p_ref, w_ref, m_ref, v_ref, g_ref, d_ref, m_out, v_out):
        g = p_ref[0]
        for k in range(1, n):
            g = g + p_ref[k]
        g_ref[...] = g
        d_ref[...], m_out[...], v_out[...] = _adamw_math(w_ref[...], g, m_ref[...], v_ref[...])

    blk = pl.BlockSpec((tr, C), lambda r: (r, 0))
    return pl.pallas_call(
        body, name=name, grid=(R // tr,),
        in_specs=[pl.BlockSpec((n, tr, C), lambda r: (0, r, 0)), blk, blk, blk],
        out_specs=[blk] * 4, out_shape=[jax.ShapeDtypeStruct((R, C), F32)] * 4, compiler_params=_cparams(1),
    )(parts, w, m, v)


def _sum_chips_adamw(chip, own, recv, w, m, v, name):
    n, R, C = recv.shape
    tr = _row_tile(R)

    def body(chip_ref, own_ref, r_ref, w_ref, m_ref, v_ref, g_ref, d_ref, m_out, v_out):
        g = None
        for k in range(n):
            term = jnp.where(chip_ref[0] == k, own_ref[...], r_ref[k].astype(F32))
            g = term if g is None else g + term
        g_ref[...] = g
        d_ref[...], m_out[...], v_out[...] = _adamw_math(w_ref[...], g, m_ref[...], v_ref[...])

    blk = pl.BlockSpec((tr, C), lambda r, chip_ref: (r, 0))
    return pl.pallas_call(
        body, name=name,
        grid_spec=pltpu.PrefetchScalarGridSpec(
            num_scalar_prefetch=1, grid=(R // tr,),
            in_specs=[pl.BlockSpec((None, tr, C), lambda r, chip_ref: (chip_ref[0], r, 0)),
                      pl.BlockSpec((n, tr, C), lambda r, chip_ref: (0, r, 0)), blk, blk, blk],
            out_specs=[blk] * 4),
        out_shape=[jax.ShapeDtypeStruct((R, C), F32)] * 4, compiler_params=_cparams(1),
    )(chip, own, recv, w, m, v)


def _pack_rows(pieces, name, after):
    arrs = []
    for a, _, _, _ in pieces:
        if not any(a is b for b in arrs):
            arrs.append(a)
    which = [next(i for i, b in enumerate(arrs) if b is a) for a, _, _, _ in pieces]
    total = sum(n for _, _, _, n in pieces)

    def body(*refs):
        o_ref = refs[len(arrs) + 1]
        off = 0
        for (_, r, c0, n), i in zip(pieces, which):
            o_ref[:, off:off + n] = refs[i][r:r + 1, c0:c0 + n]
            off += n

    return pl.pallas_call(
        body, name=name, in_specs=[_full(a.shape) for a in arrs] + [ANY], out_specs=_full((1, total)),
        out_shape=jax.ShapeDtypeStruct((1, total), F32), grid=(1,), compiler_params=_cparams(1),
    )(*arrs, after)


def _adamw_vectors(parts, ws, ms, vs):
    nv = len(ws)
    widths = [a.shape[1] for a in ws]

    def body(*refs):
        p_ref = refs[0]
        w_refs, m_refs, v_refs = refs[1:1 + nv], refs[1 + nv:1 + 2 * nv], refs[1 + 2 * nv:1 + 3 * nv]
        outs = refs[1 + 3 * nv:]
        off = 0
        for i, n in enumerate(widths):
            g = p_ref[0:1, off:off + n]
            for k in range(1, p_ref.shape[0]):
                g = g + p_ref[k:k + 1, off:off + n]
            outs[i][...] = g
            outs[nv + i][...], outs[2 * nv + i][...], outs[3 * nv + i][...] = _adamw_math(w_refs[i][...], g, m_refs[i][...], v_refs[i][...])
            off += n

    vec_specs = [_full((1, n)) for n in widths]
    res = pl.pallas_call(
        body, name="adamw_vectors", grid=(1,),
        in_specs=[_full(parts.shape)] + vec_specs * 3, out_specs=vec_specs * 4,
        out_shape=[jax.ShapeDtypeStruct((1, n), F32) for n in widths] * 4, compiler_params=_cparams(1),
    )(parts, *ws, *ms, *vs)
    return [res[t * nv:(t + 1) * nv] for t in range(4)]


def _sum_peers_adamw(me, g, recv, kind, w, m, v, name, after):
    n, R, C = recv.shape
    tr = _row_tile(R)
    nr = R // tr
    if kind == "col":
        g_spec = pl.BlockSpec((tr, C), lambda r, me_ref: (r, me_ref[0]))
    else:
        g_spec = pl.BlockSpec((tr, C), lambda r, me_ref: (me_ref[0] * nr + r, 0))

    def body(me_ref, own_ref, r_ref, w_ref, m_ref, v_ref, after_ref, g_ref, d_ref, m_out, v_out):
        acc = None
        for k in range(n):
            term = jnp.where(me_ref[0] == k, own_ref[...], r_ref[k].astype(F32))
            acc = term if acc is None else acc + term
        g_ref[...] = acc
        d_ref[...], m_out[...], v_out[...] = _adamw_math(w_ref[...], acc, m_ref[...], v_ref[...])

    blk = pl.BlockSpec((tr, C), lambda r, me_ref: (r, 0))
    return pl.pallas_call(
        body, name=name,
        grid_spec=pltpu.PrefetchScalarGridSpec(
            num_scalar_prefetch=1, grid=(nr,),
            in_specs=[g_spec, pl.BlockSpec((n, tr, C), lambda r, me_ref: (0, r, 0)), blk, blk, blk, ANY], out_specs=[blk] * 4),
        out_shape=[jax.ShapeDtypeStruct((R, C), F32)] * 4, compiler_params=_cparams(1),
    )(me, g, recv, w, m, v, after)


def _ada_cols(me, c_all, w_ada16, b_ada):
    nd, D = c_all.shape
    W = w_ada16.shape[1]

    def body(me_ref, c_ref, w_ref, b_ref, o_ref):
        cv = c_ref[...]
        r = _dot((cv * _sigmoid(cv)).astype(BF16), w_ref[...]) + b_ref[...]
        for b in range(nd):
            o_ref[b] = r[b:b + 1, :]

    return pl.pallas_call(
        body, name="ada_cols",
        grid_spec=pltpu.PrefetchScalarGridSpec(
            num_scalar_prefetch=1, grid=(1,),
            in_specs=[_full((nd, D)), _full((D, W)), pl.BlockSpec((1, W), lambda i, me_ref: (0, me_ref[0]))],
            out_specs=_full((nd, 1, W))),
        out_shape=jax.ShapeDtypeStruct((nd, 1, W), F32), compiler_params=_cparams(1),
    )(me, c_all, w_ada16, b_ada)


def _rows_to_owners(part, name):
    def body(p_ref, o_ref, send_sems, recv_sems, local_sem):
        x, y, c = _mesh_pos()
        me = 4 * x + 2 * y + c
        mine = pltpu.make_async_copy(p_ref.at[me], o_ref.at[me], local_sem.at[0])
        sends, arrivals = _peer_copies(lambda a, p: p_ref.at[p], [o_ref], send_sems, recv_sems)
        mine.start()
        for cp in sends:
            cp.start()
        for cp in arrivals:
            cp.wait_recv()
        for cp in sends:
            cp.wait_send()
        mine.wait()

    return pl.pallas_call(
        body, name=name, in_specs=[ANY], out_specs=ANY, out_shape=jax.ShapeDtypeStruct(part.shape, part.dtype),
        scratch_shapes=_peer_sems(1) + [pltpu.SemaphoreType.DMA((1,))],
    )(part)


def _ada_mod(me, c_all, w_ada16, b_ada):
    part = _ada_cols(me, c_all, w_ada16, b_ada)
    return _rows_to_owners(part, "scatter_mod").reshape(1, N_DEV * w_ada16.shape[1])


def _ada_bwd(me, small_g, D, W, c_block):
    def body(me_ref, c_ref, dm_ref, o_ref):
        cv = c_ref[...]
        ca = (cv * _sigmoid(cv)).astype(BF16).astype(F32)
        dm = dm_ref[...].astype(BF16).astype(F32)
        o_ref[...] = lax.dot_general(ca, dm, (((0,), (0,)), ((), ())), precision=lax.Precision.HIGHEST,
                                     preferred_element_type=F32)

    return pl.pallas_call(
        body, name="ada_bwd",
        grid_spec=pltpu.PrefetchScalarGridSpec(
            num_scalar_prefetch=1, grid=(1,),
            in_specs=[pl.BlockSpec((N_DEV, D), lambda i, me_ref: (0, c_block)),
                      pl.BlockSpec((N_DEV, W), lambda i, me_ref: (0, me_ref[0]))],
            out_specs=pl.BlockSpec((D, W), lambda i, me_ref: (0, 0))),
        out_shape=jax.ShapeDtypeStruct((D, W), F32), compiler_params=_cparams(1),
    )(me, small_g, small_g)


def _fwd_in(x, mod, w_in_g, b_in, shards):
    S, D = x.shape
    nb, _, wb = w_in_g.shape
    N = nb * wb
    TM = TOKEN_TILE
    n_tiles = S // TM
    ns = len(shards)

    def body(*refs):
        x_ref, mod_ref, w_hbm, b_ref = refs[:4]
        ins = refs[4:4 + ns]
        z_ref, h_ref, zg_ref = refs[4 + ns:7 + ns]
        outs = refs[7 + ns:7 + 2 * ns]
        w_v, sems, send_sems, recv_sems, local_sems = refs[7 + 2 * ns:]
        i = pl.program_id(0)
        x_, y_, c_ = _mesh_pos()
        me = 4 * x_ + 2 * y_ + c_
        mine = [pltpu.make_async_copy(ins[a], outs[a].at[me], local_sems.at[a]) for a in range(ns)]
        sends, arrivals = _peer_copies(lambda a, p: ins[a], outs, send_sems, recv_sems)

        @pl.when(i == 0)
        def _():
            for cp in sends + mine:
                cp.start()
            _load_cols(w_hbm, w_v, sems)

        xhat, _ = _ln(x_ref[...])
        hb = (xhat * (1.0 + mod_ref[:, D:2 * D]) + mod_ref[:, 0:D]).astype(BF16)
        h_ref[...] = hb
        z_ref[...] = _dot(hb, w_v[...]) + b_ref[...]
        zg_ref[:, 0:D] = z_ref[:, 2 * D:3 * D].astype(BF16)
        zg_ref[:, D:3 * D] = z_ref[:, 5 * D:7 * D].astype(BF16)

        @pl.when(i == n_tiles - 1)
        def _():
            for cp in arrivals:
                cp.wait_recv()
            for cp in sends:
                cp.wait_send()
            for cp in mine:
                cp.wait()

    res = pl.pallas_call(
        body, name="fwd_in", grid=(n_tiles,),
        in_specs=[pl.BlockSpec((TM, D), lambda i: (i, 0)), _full(mod.shape), ANY, _full((1, N))] + [ANY] * ns,
        out_specs=[pl.BlockSpec((TM, N), lambda i: (i, 0)), pl.BlockSpec((TM, D), lambda i: (i, 0)),
                   pl.BlockSpec((TM, 3 * D), lambda i: (i, 0))] + [ANY] * ns,
        out_shape=[jax.ShapeDtypeStruct((S, N), F32), jax.ShapeDtypeStruct((S, D), BF16), jax.ShapeDtypeStruct((S, 3 * D), BF16)]
        + [jax.ShapeDtypeStruct((N_DEV,) + a.shape, a.dtype) for a in shards],
        scratch_shapes=[pltpu.VMEM((D, N), BF16), pltpu.SemaphoreType.DMA((nb,))] + _peer_sems(ns) + [pltpu.SemaphoreType.DMA((ns,))],
        compiler_params=_cparams(1),
    )(x, mod, w_in_g, b_in, *shards)
    return res[0], res[1], res[2], list(res[3:])


def _rows_from(win, o, R):
    if o % 8 == 0:
        return win[o:o + R, :]
    return pltpu.roll(win, win.shape[0] - o, axis=0)[0:R, :]


def _causal_taps(buf, w_ref, o_ref, K, halo, TM, D):
    R, C = CONV_ROWS, CONV_LANES
    base = halo - (K - 1)

    for r0 in range(0, TM, R):
        for c0 in range(0, D, C):
            win = buf[r0:r0 + R + halo, c0:c0 + C]
            acc = jnp.zeros((R, C), F32)
            for k in range(K):
                acc = acc + w_ref[k:k + 1, c0:c0 + C] * _rows_from(win, base + k, R)
            o_ref[r0:r0 + R, c0:c0 + C] = acc


def _fwd_mix(z, x, mod, wa, ba, lag, lab, w_a_out, b_a_out, wb, w_b_out, w_o, b_o):
    S, D = x.shape
    N = z.shape[1]
    TM = TOKEN_TILE
    KA, KB = wa.shape[0], wb.shape[0]

    def body(z_ref, x_ref, mod_ref, wa_ref, ba_ref, lag_ref, lab_ref, wao_hbm, bao_ref, wb_ref, wbo_hbm, wo_hbm, bo_ref,
             u1_ref, ya_ref, yb_ref, q_ref, out_ref, r1_ref, u2_ref, v_ref, mg_ref,
             ubuf, pbuf, qbuf, wao, wbo, wo, sems):
        i = pl.program_id(0)

        @pl.when(i == 0)
        def _():
            _load_whole([(wao_hbm, wao), (wbo_hbm, wbo), (wo_hbm, wo)], sems)
            ubuf[0:HALO_A, :] = jnp.zeros((HALO_A, D), F32)
            pbuf[0:HALO_B, :] = jnp.zeros((HALO_B, D), F32)

        ubuf[HALO_A:HALO_A + TM, :] = z_ref[:, 0:D] * _sigmoid(z_ref[:, D:2 * D])
        _causal_taps(ubuf, wa_ref, u1_ref, KA, HALO_A, TM, D)
        ubuf[0:HALO_A, :] = ubuf[TM:TM + HALO_A, :]
        u1 = u1_ref[...] + ba_ref[...]
        u1_ref[...] = u1
        xa, _ = _ln(u1)
        l = xa * lag_ref[...] + lab_ref[...]
        u2 = (l * _sigmoid(l)).astype(BF16)
        u2_ref[...] = u2
        ya = _dot(u2, wao[...]) + bao_ref[...]
        ya_ref[...] = ya.astype(BF16)
        pbuf[HALO_B:HALO_B + TM, :] = z_ref[:, 3 * D:4 * D] * z_ref[:, 4 * D:5 * D]
        _causal_taps(pbuf, wb_ref, qbuf, KB, HALO_B, TM, D)
        pbuf[0:HALO_B, :] = pbuf[TM:TM + HALO_B, :]
        q_ref[...] = qbuf[...].astype(BF16)
        v = (z_ref[:, 2 * D:3 * D] * qbuf[...]).astype(BF16)
        v_ref[...] = v
        yb = _dot(v, wbo[...])
        yb_ref[...] = yb.astype(BF16)
        mg = (_sigmoid(z_ref[:, 5 * D:6 * D]) * ya + _sigmoid(z_ref[:, 6 * D:7 * D]) * yb).astype(BF16)
        mg_ref[...] = mg
        out = _dot(mg, wo[...]) + bo_ref[...]
        out_ref[...] = out.astype(BF16)
        r1_ref[...] = ALPHA * x_ref[...] + (1.0 + mod_ref[:, 2 * D:3 * D]) * out

    tile = pl.BlockSpec((TM, D), lambda i: (i, 0))
    vec = _full((1, D))
    return pl.pallas_call(
        body, name="fwd_mix", grid=(S // TM,),
        in_specs=[pl.BlockSpec((TM, N), lambda i: (i, 0)), tile, _full(mod.shape), _full((KA, D)), vec, vec, vec, ANY, vec,
                  _full((KB, D)), ANY, ANY, vec],
        out_specs=[tile] * 9,
        out_shape=[jax.ShapeDtypeStruct((S, D), dt) for dt in (F32, BF16, BF16, BF16, BF16, F32, BF16, BF16, BF16)],
        scratch_shapes=[pltpu.VMEM((TM + HALO_A, D), F32), pltpu.VMEM((TM + HALO_B, D), F32), pltpu.VMEM((TM, D), F32),
                        pltpu.VMEM((D, D), BF16), pltpu.VMEM((D, D), BF16), pltpu.VMEM((D, D), BF16),
                        pltpu.SemaphoreType.DMA((3,))],
        compiler_params=_cparams(1),
    )(z, x, mod, wa, ba, lag, lab, w_a_out, b_a_out, wb, w_b_out, w_o, b_o)


def _ffn(r1, tgt, mod, ln1_g, ln1_b, w_up_g, b_up, w_down, b_down, ln2_g, ln2_b):
    S, D = r1.shape
    nb, _, wb = w_up_g.shape
    FF = nb * wb
    TM = TOKEN_TILE
    CF = FFN_CHUNK
    n_tiles = S // TM

    def body(r1_ref, t_ref, mod_ref, g1_ref, b1_ref, wup_hbm, bup_ref, wdn_hbm, bdn_ref, g2_ref, b2_ref,
             dx1_ref, h2_ref, f_ref, dhu_ref, do2_ref, acc_ref, accup_ref,
             wup, wdn, relu_buf, sems, sem2):
        i = pl.program_id(0)

        @pl.when(i == 0)
        def _():
            _load_cols(wup_hbm, wup, sems)
            _load_whole([(wdn_hbm, wdn)], sem2)
            acc_ref[...] = jnp.zeros(acc_ref.shape, F32)
            accup_ref[...] = jnp.zeros(accup_ref.shape, F32)

        scale2, shift2, gate2 = mod_ref[:, 4 * D:5 * D], mod_ref[:, 3 * D:4 * D], mod_ref[:, 5 * D:6 * D]
        xhat1, _ = _ln(r1_ref[...])
        x1 = xhat1 * g1_ref[...] + b1_ref[...]
        xh0, rstd0 = _ln(x1)
        h2 = (xh0 * (1.0 + scale2) + shift2).astype(BF16)
        h2_ref[...] = h2
        out2 = jnp.zeros((TM, D), F32) + bdn_ref[...]
        for c0 in range(0, FF, CF):
            hu = _dot(h2, wup[:, c0:c0 + CF]) + bup_ref[:, c0:c0 + CF]
            rl = jnp.maximum(hu, 0.0)
            relu_buf[:, c0:c0 + CF] = rl
            fb = (rl * rl).astype(BF16)
            f_ref[:, c0:c0 + CF] = fb
            out2 = out2 + _dot(fb, wdn[c0:c0 + CF, :])
        r2 = ALPHA * x1 + (1.0 + gate2) * out2
        xh2, rstd2 = _ln(r2)
        e = xh2 * g2_ref[...] + b2_ref[...] - t_ref[...]
        acc_ref[6:7, :] += _colsum(e * e)
        dy = e * (1.0 / D)
        acc_ref[1:2, :] += _colsum(dy * xh2)
        acc_ref[2:3, :] += _colsum(dy)
        dr2 = _ln_bwd(dy * g2_ref[...], xh2, rstd2)
        acc_ref[3:4, :] += _colsum(dr2 * out2)
        do2 = (1.0 + gate2) * dr2
        acc_ref[0:1, :] += _colsum(do2)
        do2b = do2.astype(BF16)
        do2_ref[...] = do2b
        dh2 = jnp.zeros((TM, D), F32)
        for c0 in range(0, FF, CF):
            dhu = _dot_nt(do2b, wdn[c0:c0 + CF, :]) * (2.0 * relu_buf[:, c0:c0 + CF])
            accup_ref[:, c0:c0 + CF] += _colsum(dhu)
            dhub = dhu.astype(BF16)
            dhu_ref[:, c0:c0 + CF] = dhub
            dh2 = dh2 + _dot_nt(dhub, wup[:, c0:c0 + CF])
        acc_ref[4:5, :] += _colsum(dh2 * xh0)
        acc_ref[5:6, :] += _colsum(dh2)
        dx1_ref[...] = ALPHA * dr2 + _ln_bwd(dh2 * (1.0 + scale2), xh0, rstd0)

        @pl.when(i == n_tiles - 1)
        def _():
            tot = jnp.sum(acc_ref[6:7, :], axis=-1, keepdims=True) * (0.5 / D)
            acc_ref[6:7, :] = jnp.broadcast_to(tot, (1, D))

    tile = pl.BlockSpec((TM, D), lambda i: (i, 0))
    wide = pl.BlockSpec((TM, FF), lambda i: (i, 0))
    vec = _full((1, D))
    return pl.pallas_call(
        body, name="ffn", grid=(n_tiles,),
        in_specs=[tile, tile, _full(mod.shape), vec, vec, ANY, _full((1, FF)), ANY, vec, vec, vec],
        out_specs=[tile, tile, wide, wide, tile, _full((8, D)), _full((1, FF))],
        out_shape=[jax.ShapeDtypeStruct((S, D), F32), jax.ShapeDtypeStruct((S, D), BF16), jax.ShapeDtypeStruct((S, FF), BF16),
                   jax.ShapeDtypeStruct((S, FF), BF16), jax.ShapeDtypeStruct((S, D), BF16),
                   jax.ShapeDtypeStruct((8, D), F32), jax.ShapeDtypeStruct((1, FF), F32)],
        scratch_shapes=[pltpu.VMEM((D, FF), BF16), pltpu.VMEM((FF, D), BF16), pltpu.VMEM((TM, FF), F32),
                        pltpu.SemaphoreType.DMA((nb,)), pltpu.SemaphoreType.DMA((1,))],
        compiler_params=_cparams(1),
    )(r1, tgt, mod, ln1_g, ln1_b, w_up_g, b_up, w_down, b_down, ln2_g, ln2_b)


def _dw(a, b, name, msplit=1, nsplit=1, wire=False, exchange=None):
    S, M = a.shape
    N = b.shape[1]
    TK = min(DW_TILE, S)
    mb, nbk = M // msplit, N // nsplit
    nk = S // TK
    srcs, kinds = exchange if exchange else ((), ())
    ns = len(srcs)

    def body(*refs):
        a_ref, b_ref = refs[:2]
        ins = refs[2:2 + ns]
        o_ref = refs[2 + ns]
        o16_ref = refs[3 + ns] if wire else None
        rest = refs[3 + ns + (1 if wire else 0):]
        k = pl.program_id(2)
        if ns:
            outs, (send_sems, recv_sems) = rest[:ns], rest[ns:]
            sends, arrivals = _peer_copies(lambda q, p: _shard_of(ins[q], kinds[q], p), outs, send_sems, recv_sems)
            first = (pl.program_id(0) == 0) & (pl.program_id(1) == 0) & (k == 0)
            last = (pl.program_id(0) == msplit - 1) & (pl.program_id(1) == nsplit - 1) & (k == nk - 1)

            @pl.when(first)
            def _():
                for cp in sends:
                    cp.start()

        @pl.when(k == 0)
        def _():
            o_ref[...] = jnp.zeros(o_ref.shape, F32)

        o_ref[...] += _dot_tn(a_ref[...], b_ref[...])

        if wire:
            @pl.when(k == nk - 1)
            def _():
                o16_ref[...] = o_ref[...].astype(BF16)

        if ns:
            @pl.when(last)
            def _():
                for cp in arrivals:
                    cp.wait_recv()
                for cp in sends:
                    cp.wait_send()

    oblk = pl.BlockSpec((mb, nbk), lambda i, j, k: (i, j))
    res = pl.pallas_call(
        body, name=name, grid=(msplit, nsplit, nk),
        in_specs=[pl.BlockSpec((TK, mb), lambda i, j, k: (k, i)), pl.BlockSpec((TK, nbk), lambda i, j, k: (k, j))] + [ANY] * ns,
        out_specs=[oblk] + ([oblk] if wire else []) + [ANY] * ns,
        out_shape=[jax.ShapeDtypeStruct((M, N), F32)] + ([jax.ShapeDtypeStruct((M, N), BF16)] if wire else [])
        + [jax.ShapeDtypeStruct((N_DEV,) + _shard_shape(g.shape, kd), g.dtype) for g, kd in zip(srcs, kinds)],
        scratch_shapes=_peer_sems(ns) if ns else [],
        compiler_params=_cparams(3),
    )(a, b, *srcs)
    return res if (wire or ns) else res[0]


def _bwd_mix1(dx1, r1, out, ya, yb, q, u1, zg, mod, ln1_g, lag, lab, w_o, w_a_out, w_b_out):
    S, D = dx1.shape
    TM = TOKEN_TILE

    def body(dx1_ref, r1_ref, out_ref, ya_ref, yb_ref, q_ref, u1_ref, zg_ref, mod_ref, g1_ref, lag_ref, lab_ref,
             wo_hbm, wao_hbm, wbo_hbm,
             dxp_ref, du1_ref, dq_ref, dzc_ref, dout_ref, dya_ref, dyb_ref, acc_ref,
             wo, wao, wbo, sems):
        @pl.when(pl.program_id(0) == 0)
        def _():
            _load_whole([(wo_hbm, wo), (wao_hbm, wao), (wbo_hbm, wbo)], sems)
            acc_ref[...] = jnp.zeros(acc_ref.shape, F32)

        dx1v = dx1_ref[...]
        xhat1, rstd1 = _ln(r1_ref[...])
        acc_ref[0:1, :] += _colsum(dx1v * xhat1)
        acc_ref[1:2, :] += _colsum(dx1v)
        dr1 = _ln_bwd(dx1v * g1_ref[...], xhat1, rstd1)
        dxp_ref[...] = (ALPHA * dr1).astype(BF16)
        acc_ref[2:3, :] += _colsum(dr1 * out_ref[...].astype(F32))
        dout = (1.0 + mod_ref[:, 2 * D:3 * D]) * dr1
        acc_ref[3:4, :] += _colsum(dout)
        doutb = dout.astype(BF16)
        dout_ref[...] = doutb
        dmg = _dot_nt(doutb, wo[...])
        sga, sgb = _sigmoid(zg_ref[:, D:2 * D].astype(F32)), _sigmoid(zg_ref[:, 2 * D:3 * D].astype(F32))
        dga = dmg * ya_ref[...].astype(F32) * sga * (1.0 - sga)
        dgb = dmg * yb_ref[...].astype(F32) * sgb * (1.0 - sgb)
        acc_ref[9:10, :] += _colsum(dga)
        acc_ref[10:11, :] += _colsum(dgb)
        dzc_ref[:, D:2 * D] = dga.astype(BF16)
        dzc_ref[:, 2 * D:3 * D] = dgb.astype(BF16)
        dya = dmg * sga
        acc_ref[4:5, :] += _colsum(dya)
        dyab = dya.astype(BF16)
        dya_ref[...] = dyab
        dybb = (dmg * sgb).astype(BF16)
        dyb_ref[...] = dybb
        du2 = _dot_nt(dyab, wao[...])
        xa, rstda = _ln(u1_ref[...])
        l = xa * lag_ref[...] + lab_ref[...]
        sl = _sigmoid(l)
        dl = du2 * (sl * (1.0 + l * (1.0 - sl)))
        acc_ref[5:6, :] += _colsum(dl * xa)
        acc_ref[6:7, :] += _colsum(dl)
        du1 = _ln_bwd(dl * lag_ref[...], xa, rstda)
        acc_ref[7:8, :] += _colsum(du1)
        du1_ref[...] = du1.astype(BF16)
        dv = _dot_nt(dybb, wbo[...])
        dgbk = dv * q_ref[...].astype(F32)
        acc_ref[8:9, :] += _colsum(dgbk)
        dzc_ref[:, 0:D] = dgbk.astype(BF16)
        dq_ref[...] = (dv * zg_ref[:, 0:D].astype(F32)).astype(BF16)

    tile = pl.BlockSpec((TM, D), lambda i: (i, 0))
    vec = _full((1, D))
    return pl.pallas_call(
        body, name="bwd_mix1", grid=(S // TM,),
        in_specs=[tile] * 7 + [pl.BlockSpec((TM, 3 * D), lambda i: (i, 0)), _full(mod.shape), vec, vec, vec, ANY, ANY, ANY],
        out_specs=[tile, tile, tile, pl.BlockSpec((TM, 3 * D), lambda i: (i, 0)), tile, tile, tile, _full((16, D))],
        out_shape=[jax.ShapeDtypeStruct((S, D), BF16)] * 3 + [jax.ShapeDtypeStruct((S, 3 * D), BF16)]
        + [jax.ShapeDtypeStruct((S, D), BF16)] * 3 + [jax.ShapeDtypeStruct((16, D), F32)],
        scratch_shapes=[pltpu.VMEM((D, D), BF16)] * 3 + [pltpu.SemaphoreType.DMA((3,))],
        compiler_params=_cparams(1),
    )(dx1, r1, out, ya, yb, q, u1, zg, mod, ln1_g, lag, lab, w_o, w_a_out, w_b_out)


def _anticausal_taps(dbuf, u_ref, w_ref, o_ref, dwacc, K, halo, TM, D):
    R, C = CONV_ROWS, CONV_LANES

    for r0 in range(0, TM, R):
        for c0 in range(0, D, C):
            win = dbuf[r0:r0 + R + halo, c0:c0 + C]
            uc = u_ref[r0:r0 + R, c0:c0 + C]
            acc = jnp.zeros((R, C), F32)
            for j in range(K):
                sh = _rows_from(win, j, R)
                k = K - 1 - j
                acc = acc + w_ref[k:k + 1, c0:c0 + C] * sh
                pr = uc * sh
                part = pr[0:8, :]
                for s in range(8, R, 8):
                    part = part + pr[s:s + 8, :]
                dwacc[k, :, c0:c0 + C] += part
            o_ref[r0:r0 + R, c0:c0 + C] = acc


def _bwd_mix2(du1, dq, z, dzc, x, dxp, mod, wa, wb, w_in_g):
    S, D = x.shape
    N = z.shape[1]
    nb, _, wbk = w_in_g.shape
    TM = TOKEN_TILE
    KA, KB = wa.shape[0], wb.shape[0]
    n_tiles = S // TM

    def body(du1_ref, dq_ref, zav_ref, zc_ref, zx_ref, dzc_ref, x_ref, dxp_ref, mod_ref, wa_ref, wb_ref, win_hbm,
             dz_ref, gx_ref, dwa_ref, dwb_ref, acc_ref, dbin_ref,
             dbuf, qbuf, ubuf, pbuf, obuf, dwa_acc, dwb_acc, w_v, sems):
        i = pl.program_id(0)

        @pl.when(i == 0)
        def _():
            _load_cols(win_hbm, w_v, sems)
            dbuf[TM:TM + HALO_A, :] = jnp.zeros((HALO_A, D), F32)
            qbuf[TM:TM + HALO_B, :] = jnp.zeros((HALO_B, D), F32)
            dwa_acc[...] = jnp.zeros(dwa_acc.shape, F32)
            dwb_acc[...] = jnp.zeros(dwb_acc.shape, F32)
            acc_ref[...] = jnp.zeros(acc_ref.shape, F32)
            dbin_ref[...] = jnp.zeros(dbin_ref.shape, F32)

        a_val = zav_ref[:, 0:D]
        sa = _sigmoid(zav_ref[:, D:2 * D])
        ubuf[...] = a_val * sa
        dbuf[0:TM, :] = du1_ref[...].astype(F32)
        _anticausal_taps(dbuf, ubuf, wa_ref, obuf, dwa_acc, KA, HALO_A, TM, D)
        dbuf[TM:TM + HALO_A, :] = dbuf[0:HALO_A, :]
        du0 = obuf[...]
        dav = du0 * sa
        dag = du0 * a_val * sa * (1.0 - sa)
        dbin_ref[:, 0:D] += _colsum(dav)
        dbin_ref[:, D:2 * D] += _colsum(dag)
        dz_ref[:, 0:D] = dav.astype(BF16)
        dz_ref[:, D:2 * D] = dag.astype(BF16)
        pbuf[...] = zc_ref[...] * zx_ref[...]
        qbuf[0:TM, :] = dq_ref[...].astype(F32)
        _anticausal_taps(qbuf, pbuf, wb_ref, obuf, dwb_acc, KB, HALO_B, TM, D)
        qbuf[TM:TM + HALO_B, :] = qbuf[0:HALO_B, :]
        dp = obuf[...]
        dgc = dp * zx_ref[...]
        dgx = dp * zc_ref[...]
        dbin_ref[:, 3 * D:4 * D] += _colsum(dgc)
        dbin_ref[:, 4 * D:5 * D] += _colsum(dgx)
        dz_ref[:, 3 * D:4 * D] = dgc.astype(BF16)
        dz_ref[:, 4 * D:5 * D] = dgx.astype(BF16)
        dz_ref[:, 2 * D:3 * D] = dzc_ref[:, 0:D]
        dz_ref[:, 5 * D:7 * D] = dzc_ref[:, D:3 * D]
        dh = _dot_nt(dz_ref[...], w_v[...])
        xhat, rstd = _ln(x_ref[...])
        acc_ref[0:1, :] += _colsum(dh * xhat)
        acc_ref[1:2, :] += _colsum(dh)
        gx_ref[...] = dxp_ref[...].astype(F32) + _ln_bwd(dh * (1.0 + mod_ref[:, D:2 * D]), xhat, rstd)

        @pl.when(i == n_tiles - 1)
        def _():
            for k in range(KA):
                dwa_ref[k:k + 1, :] = jnp.sum(dwa_acc[k], axis=0, keepdims=True)
            for k in range(KB):
                dwb_ref[k:k + 1, :] = jnp.sum(dwb_acc[k], axis=0, keepdims=True)

    rev = lambda i: n_tiles - 1 - i
    tile = pl.BlockSpec((TM, D), lambda i: (rev(i), 0))
    zcol = lambda k: pl.BlockSpec((TM, D), lambda i: (rev(i), k))
    return pl.pallas_call(
        body, name="bwd_mix2", grid=(n_tiles,),
        in_specs=[tile, tile, pl.BlockSpec((TM, 2 * D), lambda i: (rev(i), 0)), zcol(3), zcol(4),
                  pl.BlockSpec((TM, 3 * D), lambda i: (rev(i), 0)), tile, tile, _full(mod.shape), _full((KA, D)), _full((KB, D)), ANY],
        out_specs=[pl.BlockSpec((TM, N), lambda i: (rev(i), 0)), tile, _full((KA, D)), _full((KB, D)), _full((8, D)), _full((1, N))],
        out_shape=[jax.ShapeDtypeStruct((S, N), BF16), jax.ShapeDtypeStruct((S, D), F32), jax.ShapeDtypeStruct((KA, D), F32),
                   jax.ShapeDtypeStruct((KB, D), F32), jax.ShapeDtypeStruct((8, D), F32), jax.ShapeDtypeStruct((1, N), F32)],
        scratch_shapes=[pltpu.VMEM((TM + HALO_A, D), F32), pltpu.VMEM((TM + HALO_B, D), F32), pltpu.VMEM((TM, D), F32),
                        pltpu.VMEM((TM, D), F32), pltpu.VMEM((TM, D), F32), pltpu.VMEM((KA, 8, D), F32), pltpu.VMEM((KB, 8, D), F32),
                        pltpu.VMEM((D, N), BF16), pltpu.SemaphoreType.DMA((nb,))],
        compiler_params=_cparams(1),
    )(du1, dq, z, z, z, dzc, x, dxp, mod, wa, wb, w_in_g)


PEER_REDUCED = ("w_up", "w_down", "w_o", "w_a_out", "w_b_out", "conv_a_w", "conv_b_w")


def _local_grads(me, x, c, c_all, tgt, w_ada16, w_in_g, shards, vecs):
    D = x.shape[1]
    rows = lambda a: a.reshape(a.shape[0] * a.shape[1], a.shape[2])
    cols = lambda a: jnp.transpose(a, (1, 0, 2)).reshape(a.shape[1], a.shape[0] * a.shape[2])
    mod = _ada_mod(me, c_all, w_ada16, vecs["b_ada"])
    z, h1, zg, gathered = _fwd_in(x, mod, w_in_g, vecs["b_in"], shards)
    w_a_out, w_b_out, w_o, w_up_g, w_down = rows(gathered[0]), rows(gathered[1]), rows(gathered[2]), gathered[3], rows(gathered[4])
    wa, wb = cols(gathered[5]), cols(gathered[6])
    u1, ya, yb, q, out, r1, u2, v, mg = _fwd_mix(
        z, x, mod, wa, vecs["conv_a_b"], vecs["ln_a_g"], vecs["ln_a_b"], w_a_out, vecs["b_a_out"], wb, w_b_out, w_o, vecs["b_o"])
    dx1, h2, f, dhu, do2, acc_f, acc_up = _ffn(
        r1, tgt, mod, vecs["ln1_g"], vecs["ln1_b"], w_up_g, vecs["b_up"], w_down, vecs["b_down"], vecs["ln2_g"], vecs["ln2_b"])
    g_up, g_up16 = _dw(h2, dhu, "dw_up", nsplit=2, wire=True)
    g_down, g_down16 = _dw(f, do2, "dw_down", msplit=2, wire=True)
    dxp, du1, dq, dzc, doutb, dyab, dybb, acc_1 = _bwd_mix1(
        dx1, r1, out, ya, yb, q, u1, zg, mod, vecs["ln1_g"], vecs["ln_a_g"], vecs["ln_a_b"], w_o, w_a_out, w_b_out)
    g_o, g_o16 = _dw(mg, doutb, "dw_o", wire=True)
    g_a_out, g_a_out16 = _dw(u2, dyab, "dw_a_out", wire=True)
    g_b_out, g_b_out16 = _dw(v, dybb, "dw_b_out", wire=True)
    dz, gx, g_wa, g_wb, acc_2, db_in = _bwd_mix2(du1, dq, z, dzc, x, dxp, mod, wa, wb, w_in_g)
    full = dict(zip(PEER_REDUCED, (g_up, g_down, g_o, g_a_out, g_b_out, g_wa, g_wb)))
    res = _dw(h1, dz, "dw_in", nsplit=4,
              exchange=((g_up16, g_down16, g_o16, g_a_out16, g_b_out16, g_wa, g_wb), [SHARD_KIND[n] for n in PEER_REDUCED]))
    g_in = res[0]
    peers = {n: (full[n], r) for n, r in zip(PEER_REDUCED, res[1:])}
    row = lambda acc, k: (acc, k, 0, D)
    pieces = [
        row(acc_2, 1), row(acc_2, 0), row(acc_1, 2), row(acc_f, 5), row(acc_f, 4), row(acc_f, 3),
        (db_in, 0, 0, 2 * D), row(acc_1, 8), (db_in, 0, 3 * D, 2 * D), row(acc_1, 9), row(acc_1, 10),
        row(acc_1, 7), row(acc_1, 5), row(acc_1, 6), row(acc_1, 4), row(acc_1, 3), row(acc_1, 0), row(acc_1, 1),
        (acc_up, 0, 0, acc_up.shape[1]), row(acc_f, 0), row(acc_f, 1), row(acc_f, 2), (c, 0, 0, D)]
    return acc_f[6, 0], gx, g_in, peers, pieces


WEIGHTS = ("w_ada", "b_ada", "w_in", "b_in", "conv_a_w", "conv_a_b", "ln_a_g", "ln_a_b", "w_a_out", "b_a_out", "conv_b_w",
           "w_b_out", "w_o", "b_o", "ln1_g", "ln1_b", "w_up", "b_up", "w_down", "b_down", "ln2_g", "ln2_b")
VECTORS = ("b_ada", "b_in", "conv_a_b", "ln_a_g", "ln_a_b", "b_a_out", "b_o", "ln1_g", "ln1_b", "b_up", "b_down", "ln2_g", "ln2_b")
SHARD_KIND = {"w_in": "col", "w_a_out": "row", "w_b_out": "row", "w_o": "row", "w_up": "col", "w_down": "row",
              "conv_a_w": "col", "conv_b_w": "col"}


def kernel(x, c, w_ada, b_ada, w_in, b_in, conv_a_w, conv_a_b, ln_a_g, ln_a_b, w_a_out, b_a_out, conv_b_w, w_b_out, w_o, b_o, ln1_g, ln1_b, w_up, b_up, w_down, b_down, ln2_g, ln2_b, loss_target, m_w_ada, m_b_ada, m_w_in, m_b_in, m_conv_a_w, m_conv_a_b, m_ln_a_g, m_ln_a_b, m_w_a_out, m_b_a_out, m_conv_b_w, m_w_b_out, m_w_o, m_b_o, m_ln1_g, m_ln1_b, m_w_up, m_b_up, m_w_down, m_b_down, m_ln2_g, m_ln2_b, v_w_ada, v_b_ada, v_w_in, v_b_in, v_conv_a_w, v_conv_a_b, v_ln_a_g, v_ln_a_b, v_w_a_out, v_b_a_out, v_conv_b_w, v_w_b_out, v_w_o, v_b_o, v_ln1_g, v_ln1_b, v_w_up, v_b_up, v_w_down, v_b_down, v_ln2_g, v_ln2_b):
    args = dict(locals())
    w = {n: args[n][0] for n in WEIGHTS}
    m = {n: args["m_" + n][0] for n in WEIGHTS}
    v = {n: args["v_" + n][0] for n in WEIGHTS}
    w = {n: (a[None, :] if a.ndim == 1 else a) for n, a in w.items()}
    m = {n: (a[None, :] if a.ndim == 1 else a) for n, a in m.items()}
    v = {n: (a[None, :] if a.ndim == 1 else a) for n, a in v.items()}
    xs, tgt = x[0], loss_target[0]
    S, D = xs.shape
    me = (4 * lax.axis_index("x") + 2 * lax.axis_index("y") + lax.axis_index("c")).astype(jnp.int32).reshape(1)
    core = lax.axis_index("c").astype(jnp.int32).reshape(1)
    chip = (2 * lax.axis_index("x") + lax.axis_index("y")).astype(jnp.int32).reshape(1)

    bf = lambda n: w[n].astype(BF16)
    w_in_g, c_all = _all_gather([bf("w_in"), c], "gather_weights")
    shards = [bf("w_a_out"), bf("w_b_out"), bf("w_o"), bf("w_up"), bf("w_down"), w["conv_a_w"], w["conv_b_w"]]
    vecs = {n: w[n] for n in VECTORS}

    loss, gx, g_in, peers, pieces = _local_grads(me, xs, c, c_all.reshape(N_DEV, D), tgt, bf("w_ada"), w_in_g, shards, vecs)

    (from_sibling,) = _exchange_cores([g_in], [SHARD_KIND["w_in"]], "reduce_cores")
    own_in, wire_in = _add_own(core, g_in, from_sibling, SHARD_KIND["w_in"], "add_w_in")
    sems, wire_thru, landing, after = _chips_start(wire_in)

    res = {}
    for n, (g_full, recv) in peers.items():
        res[n] = _sum_peers_adamw(me, g_full, recv, SHARD_KIND[n], w[n], m[n], v[n], "adamw_" + n, after)
        after = res[n][0]
    small = _pack_rows(pieces, "pack_vectors", after)
    (small_g,) = _all_gather([small], "gather_vectors")
    small_g = small_g.reshape(N_DEV, small.shape[1])
    W = w["w_ada"].shape[1]
    n_vec = small.shape[1] - D
    g_ada = _ada_bwd(me, small_g, D, W, n_vec // D)
    res["w_ada"] = _sum_adamw(g_ada[None], w["w_ada"], m["w_ada"], v["w_ada"], "adamw_w_ada")
    by_kind = _adamw_vectors(small_g, [w[n] for n in VECTORS], [m[n] for n in VECTORS], [v[n] for n in VECTORS])
    for i, n in enumerate(VECTORS):
        res[n] = tuple(by_kind[t][i] for t in range(4))
    from_chips = _chips_wait(sems, wire_thru, landing, res["w_ada"][0])
    res["w_in"] = _sum_chips_adamw(chip, own_in, from_chips, w["w_in"], m["w_in"], v["w_in"], "adamw_w_in")

    loss = lax.psum(loss, ("x", "y", "c"))
    outs = [loss, gx[None]]
    for t in range(4):
        outs += [res[n][t].reshape(args[n].shape) for n in WEIGHTS]
    return tuple(outs)
```

```python
import functools

import jax
import jax.numpy as jnp
from jax import lax
from jax.experimental import pallas as pl
from jax.experimental.pallas import tpu as pltpu

F32 = jnp.float32
BF16 = jnp.bfloat16
MESH = pl.DeviceIdType.MESH

N_DEV = 8
LN_EPS = 1e-5
DEPTH = 1
ALPHA = (2.0 * DEPTH) ** 0.25
ADAM_LR, ADAM_B1, ADAM_B2, ADAM_EPS, ADAM_WD, ADAM_STEP = 0.001, 0.9, 0.999, 1e-08, 0.01, 10

VMEM_LIMIT = 60 * 1024 * 1024
TOKEN_TILE = 256
DW_TILE = 2048
HALO_A = 32
HALO_B = 8
CONV_ROWS, CONV_LANES = 256, 256
FFN_CHUNK = 1024


def _cparams(n_grid):
    return pltpu.CompilerParams(dimension_semantics=("arbitrary",) * n_grid, vmem_limit_bytes=VMEM_LIMIT)


def _full(shape):
    return pl.BlockSpec(shape, lambda *_: (0,) * len(shape))


ANY = pl.BlockSpec(memory_space=pl.ANY)


def _ln(x):
    mu = jnp.mean(x, axis=-1, keepdims=True)
    xc = x - mu
    var = jnp.mean(xc * xc, axis=-1, keepdims=True)
    rstd = lax.rsqrt(var + LN_EPS)
    return xc * rstd, rstd


def _ln_bwd(dxhat, xhat, rstd):
    m1 = jnp.mean(dxhat, axis=-1, keepdims=True)
    m2 = jnp.mean(dxhat * xhat, axis=-1, keepdims=True)
    return rstd * (dxhat - m1 - xhat * m2)


def _sigmoid(x):
    return 0.5 * jnp.tanh(0.5 * x) + 0.5


def _colsum(a):
    return jnp.sum(a, axis=0, keepdims=True)


def _dot(a, b):
    return jnp.dot(a, b, preferred_element_type=F32)


def _dot_nt(a, b):
    return lax.dot_general(a, b, (((1,), (1,)), ((), ())), preferred_element_type=F32)


def _dot_tn(a, b):
    return lax.dot_general(a, b, (((0,), (0,)), ((), ())), preferred_element_type=F32)


def _load_cols(src_hbm, dst_vmem, sems):
    nblk, _, w = src_hbm.shape
    cps = [pltpu.make_async_copy(src_hbm.at[j], dst_vmem.at[:, pl.ds(j * w, w)], sems.at[j]) for j in range(nblk)]
    for cp in cps:
        cp.start()
    for cp in cps:
        cp.wait()


def _load_whole(pairs, sems):
    cps = [pltpu.make_async_copy(s, d, sems.at[k]) for k, (s, d) in enumerate(pairs)]
    for cp in cps:
        cp.start()
    for cp in cps:
        cp.wait()


def _mesh_pos():
    return lax.axis_index("x"), lax.axis_index("y"), lax.axis_index("c")


def _all_gather(arrs, name):
    n = len(arrs)

    def body(*refs):
        ins, outs = refs[:n], refs[n:2 * n]
        send_sems, recv_sems, local_sems = refs[2 * n:]
        x, y, c = _mesh_pos()
        me, sibling = (x, y, c), (x, y, 1 - c)
        chips = [(1 - x, y), (x, 1 - y), (1 - x, 1 - y)]

        def slot(a, px, py, pc):
            return outs[a].at[4 * px + 2 * py + pc]

        def copy(a, k, block, to, src=None):
            return pltpu.make_async_remote_copy(
                src_ref=slot(a, *block) if src is None else src, dst_ref=slot(a, *block),
                send_sem=send_sems.at[a, k], recv_sem=recv_sems.at[a, k], device_id=to, device_id_type=MESH)

        mine = [pltpu.make_async_copy(ins[a], slot(a, *me), local_sems.at[a]) for a in range(n)]
        for cp in mine:
            cp.start()
        first = []
        for a in range(n):
            first.append(copy(a, 0, me, sibling, src=ins[a]))
            first += [copy(a, 1 + j, me, (*chip, c), src=ins[a]) for j, chip in enumerate(chips)]
        for cp in first:
            cp.start()
        passed = []
        for a in range(n):
            for j, chip in enumerate(chips):
                copy(a, 1 + j, (*chip, c), me).wait_recv()
                fwd = copy(a, 4 + j, (*chip, c), sibling)
                fwd.start()
                passed.append(fwd)
        for a in range(n):
            copy(a, 0, sibling, me).wait_recv()
            for j, chip in enumerate(chips):
                copy(a, 4 + j, (*chip, 1 - c), me).wait_recv()
        for cp in first + passed:
            cp.wait_send()
        for cp in mine:
            cp.wait()

    outs = pl.pallas_call(
        body, name=name,
        out_shape=[jax.ShapeDtypeStruct((N_DEV,) + a.shape, a.dtype) for a in arrs],
        in_specs=[ANY] * n, out_specs=[ANY] * n,
        scratch_shapes=[pltpu.SemaphoreType.DMA((n, 7)), pltpu.SemaphoreType.DMA((n, 7)), pltpu.SemaphoreType.DMA((n,))],
    )(*arrs)
    return list(outs)


def _peer_copies(src_of, dsts, send_sems, recv_sems):
    x, y, c = _mesh_pos()
    me = 4 * x + 2 * y + c
    sends, arrivals = [], []
    for a in range(len(dsts)):
        for k in range(1, N_DEV):
            px, py, pc = (1 - x if k & 4 else x), (1 - y if k & 2 else y), (1 - c if k & 1 else c)
            p = 4 * px + 2 * py + pc
            common = dict(send_sem=send_sems.at[a, k - 1], recv_sem=recv_sems.at[a, k - 1],
                          device_id=(px, py, pc), device_id_type=MESH)
            sends.append(pltpu.make_async_remote_copy(src_ref=src_of(a, p), dst_ref=dsts[a].at[me], **common))
            arrivals.append(pltpu.make_async_remote_copy(src_ref=dsts[a].at[p], dst_ref=dsts[a].at[p], **common))
    return sends, arrivals


def _peer_sems(n):
    return [pltpu.SemaphoreType.DMA((n, N_DEV - 1)), pltpu.SemaphoreType.DMA((n, N_DEV - 1))]


def _shard_of(ref, kind, j):
    if kind == "col":
        w = ref.shape[1] // N_DEV
        return ref.at[:, pl.ds(j * w, w)]
    h = ref.shape[0] // N_DEV
    return ref.at[pl.ds(j * h, h), :]


def _shard_shape(shape, kind):
    return (shape[0], shape[1] // N_DEV) if kind == "col" else (shape[0] // N_DEV, shape[1])


def _exchange_cores(grads, kinds, name):
    n = len(grads)

    def body(*refs):
        ins, outs = refs[:n], refs[n:2 * n]
        send_sems, recv_sems = refs[2 * n:]
        x, y, c = _mesh_pos()
        sibling = (x, y, 1 - c)
        sends = []
        for a in range(n):
            for s in range(4):
                sends.append(pltpu.make_async_remote_copy(
                    src_ref=_shard_of(ins[a], kinds[a], 2 * s + (1 - c)), dst_ref=outs[a].at[s],
                    send_sem=send_sems.at[a, s], recv_sem=recv_sems.at[a, s], device_id=sibling, device_id_type=MESH))
        for cp in sends:
            cp.start()
        for a in range(n):
            for s in range(4):
                pltpu.make_async_remote_copy(
                    src_ref=outs[a].at[s], dst_ref=outs[a].at[s],
                    send_sem=send_sems.at[a, s], recv_sem=recv_sems.at[a, s], device_id=sibling, device_id_type=MESH).wait_recv()
        for cp in sends:
            cp.wait_send()

    outs = pl.pallas_call(
        body, name=name,
        out_shape=[jax.ShapeDtypeStruct((4,) + _shard_shape(g.shape, k), F32) for g, k in zip(grads, kinds)],
        in_specs=[ANY] * n, out_specs=[ANY] * n,
        scratch_shapes=[pltpu.SemaphoreType.DMA((n, 4)), pltpu.SemaphoreType.DMA((n, 4))],
    )(*grads)
    return list(outs)


HBM = pl.BlockSpec(memory_space=pltpu.HBM)
SEM = pl.BlockSpec(memory_space=pltpu.SEMAPHORE)
DATAFLOW = pltpu.SideEffectType.DATAFLOW_SIDE_EFFECTING


def _chip_copies(src_ref, land_ref, send_sems, recv_sems):
    x, y, c = _mesh_pos()
    my_slot = 2 * x + y
    sends, arrivals = [], []
    for j, (px, py) in enumerate([(1 - x, y), (x, 1 - y), (1 - x, 1 - y)]):
        common = dict(send_sem=send_sems[j], recv_sem=recv_sems[j], device_id=(px, py, c), device_id_type=MESH)
        sends.append(pltpu.make_async_remote_copy(src_ref=src_ref.at[2 * px + py], dst_ref=land_ref.at[my_slot], **common))
        arrivals.append(pltpu.make_async_remote_copy(src_ref=src_ref.at[2 * px + py], dst_ref=land_ref.at[2 * px + py], **common))
    return sends, arrivals


def _chips_start(part):
    def body(src_ref, land_ref, s0, s1, s2, r0, r1, r2, src_thru, land_thru, token):
        sends, _ = _chip_copies(src_ref, land_ref, (s0, s1, s2), (r0, r1, r2))
        for cp in sends:
            cp.start()
        token[...] = jnp.zeros_like(token)

    res = pl.pallas_call(
        body, name="reduce_chips_start",
        out_shape=(pltpu.SemaphoreType.DMA(()),) * 6 + (pltpu.HBM(part.shape, part.dtype), pltpu.HBM(part.shape, part.dtype),
                                                        jax.ShapeDtypeStruct((8, 128), F32)),
        in_specs=(HBM, HBM), out_specs=(SEM,) * 6 + (HBM, HBM, pl.BlockSpec(memory_space=pltpu.VMEM)),
        input_output_aliases={0: 6, 1: 7}, compiler_params=pltpu.CompilerParams(has_side_effects=DATAFLOW),
    )(pltpu.with_memory_space_constraint(part, pltpu.HBM),
      pltpu.with_memory_space_constraint(lax.empty(part.shape, part.dtype), pltpu.HBM))
    return res[:6], res[6], res[7], res[8]


def _chips_wait(sems, src_thru, land_thru, after):
    def body(src_ref, land_ref, s0, s1, s2, r0, r1, r2, after_ref, src_dead, got_ref):
        sends, arrivals = _chip_copies(src_ref, land_ref, (s0, s1, s2), (r0, r1, r2))
        for cp in sends:
            cp.wait_send()
        for cp in arrivals:
            cp.wait_recv()

    return pl.pallas_call(
        body, name="reduce_chips_wait",
        out_shape=(pltpu.HBM(src_thru.shape, src_thru.dtype), pltpu.HBM(land_thru.shape, land_thru.dtype)),
        in_specs=(HBM, HBM) + (SEM,) * 6 + (ANY,), out_specs=(HBM, HBM), input_output_aliases={0: 0, 1: 1},
        compiler_params=pltpu.CompilerParams(has_side_effects=DATAFLOW),
    )(src_thru, land_thru, *sems, after)[1]


def _row_tile(rows):
    for t in (256, 128, 64, 32, 16, 8):
        if rows % t == 0:
            return t
    return rows


def _wire_dtype(rows):
    return BF16 if rows % 16 == 0 else F32


def _add_own(core, g, recv, kind, name):
    ns, R, C = recv.shape
    tr = _row_tile(R)
    nr = R // tr
    if kind == "col":
        g_spec = pl.BlockSpec((tr, C), lambda s, r, c_ref: (r, 2 * s + c_ref[0]))
    else:
        g_spec = pl.BlockSpec((tr, C), lambda s, r, c_ref: ((2 * s + c_ref[0]) * nr + r, 0))
    slab = pl.BlockSpec((None, tr, C), lambda s, r, c_ref: (s, r, 0))

    def body(c_ref, g_ref, r_ref, o32_ref, o16_ref):
        t = g_ref[...] + r_ref[...]
        o32_ref[...] = t
        o16_ref[...] = t.astype(o16_ref.dtype)

    return pl.pallas_call(
        body, name=name,
        grid_spec=pltpu.PrefetchScalarGridSpec(num_scalar_prefetch=1, grid=(ns, nr), in_specs=[g_spec, slab], out_specs=[slab, slab]),
        out_shape=[jax.ShapeDtypeStruct((ns, R, C), F32), jax.ShapeDtypeStruct((ns, R, C), _wire_dtype(R))],
        compiler_params=_cparams(2),
    )(core, g, recv)


def _adamw_math(w, g, m, v):
    m2 = ADAM_B1 * m + (1.0 - ADAM_B1) * g
    v2 = ADAM_B2 * v + (1.0 - ADAM_B2) * (g * g)
    m_hat = m2 / (1.0 - ADAM_B1 ** ADAM_STEP)
    v_hat = v2 / (1.0 - ADAM_B2 ** ADAM_STEP)
    delta = -ADAM_LR * (m_hat / (jnp.sqrt(v_hat) + ADAM_EPS) + ADAM_WD * w)
    return delta, m2, v2


def _sum_adamw(parts, w, m, v, name):
    n, R, C = parts.shape
    tr = _row_tile(R)

    def body(p_ref, w_ref, m_ref, v_ref, g_ref, d_ref, m_out, v_out):
        g = p_ref[0]
        for k in range(1, n):
            g = g + p_ref[k]
        g_ref[...] = g
        d_ref[...], m_out[...], v_out[...] = _adamw_math(w_ref[...], g, m_ref[...], v_ref[...])

    blk = pl.BlockSpec((tr, C), lambda r: (r, 0))
    return pl.pallas_call(
        body, name=name, grid=(R // tr,),
        in_specs=[pl.BlockSpec((n, tr, C), lambda r: (0, r, 0)), blk, blk, blk],
        out_specs=[blk] * 4, out_shape=[jax.ShapeDtypeStruct((R, C), F32)] * 4, compiler_params=_cparams(1),
    )(parts, w, m, v)


def _sum_chips_adamw(chip, own, recv, w, m, v, name):
    n, R, C = recv.shape
    tr = _row_tile(R)

    def body(chip_ref, own_ref, r_ref, w_ref, m_ref, v_ref, g_ref, d_ref, m_out, v_out):
        g = None
        for k in range(n):
            term = jnp.where(chip_ref[0] == k, own_ref[...], r_ref[k].astype(F32))
            g = term if g is None else g + term
        g_ref[...] = g
        d_ref[...], m_out[...], v_out[...] = _adamw_math(w_ref[...], g, m_ref[...], v_ref[...])

    blk = pl.BlockSpec((tr, C), lambda r, chip_ref: (r, 0))
    return pl.pallas_call(
        body, name=name,
        grid_spec=pltpu.PrefetchScalarGridSpec(
            num_scalar_prefetch=1, grid=(R // tr,),
            in_specs=[pl.BlockSpec((None, tr, C), lambda r, chip_ref: (chip_ref[0], r, 0)),
                      pl.BlockSpec((n, tr, C), lambda r, chip_ref: (0, r, 0)), blk, blk, blk],
            out_specs=[blk] * 4),
        out_shape=[jax.ShapeDtypeStruct((R, C), F32)] * 4, compiler_params=_cparams(1),
    )(chip, own, recv, w, m, v)


def _pack_rows(pieces, name, after):
    arrs = []
    for a, _, _, _ in pieces:
        if not any(a is b for b in arrs):
            arrs.append(a)
    which = [next(i for i, b in enumerate(arrs) if b is a) for a, _, _, _ in pieces]
    total = sum(n for _, _, _, n in pieces)

    def body(*refs):
        o_ref = refs[len(arrs) + 1]
        off = 0
        for (_, r, c0, n), i in zip(pieces, which):
            o_ref[:, off:off + n] = refs[i][r:r + 1, c0:c0 + n]
            off += n

    return pl.pallas_call(
        body, name=name, in_specs=[_full(a.shape) for a in arrs] + [ANY], out_specs=_full((1, total)),
        out_shape=jax.ShapeDtypeStruct((1, total), F32), grid=(1,), compiler_params=_cparams(1),
    )(*arrs, after)


def _adamw_vectors(parts, ws, ms, vs):
    nv = len(ws)
    widths = [a.shape[1] for a in ws]

    def body(*refs):
        p_ref = refs[0]
        w_refs, m_refs, v_refs = refs[1:1 + nv], refs[1 + nv:1 + 2 * nv], refs[1 + 2 * nv:1 + 3 * nv]
        outs = refs[1 + 3 * nv:]
        off = 0
        for i, n in enumerate(widths):
            g = p_ref[0:1, off:off + n]
            for k in range(1, p_ref.shape[0]):
                g = g + p_ref[k:k + 1, off:off + n]
            outs[i][...] = g
            outs[nv + i][...], outs[2 * nv + i][...], outs[3 * nv + i][...] = _adamw_math(w_refs[i][...], g, m_refs[i][...], v_refs[i][...])
            off += n

    vec_specs = [_full((1, n)) for n in widths]
    res = pl.pallas_call(
        body, name="adamw_vectors", grid=(1,),
        in_specs=[_full(parts.shape)] + vec_specs * 3, out_specs=vec_specs * 4,
        out_shape=[jax.ShapeDtypeStruct((1, n), F32) for n in widths] * 4, compiler_params=_cparams(1),
    )(parts, *ws, *ms, *vs)
    return [res[t * nv:(t + 1) * nv] for t in range(4)]


def _sum_peers_adamw(me, g, recv, kind, w, m, v, name, after):
    n, R, C = recv.shape
    tr = _row_tile(R)
    nr = R // tr
    if kind == "col":
        g_spec = pl.BlockSpec((tr, C), lambda r, me_ref: (r, me_ref[0]))
    else:
        g_spec = pl.BlockSpec((tr, C), lambda r, me_ref: (me_ref[0] * nr + r, 0))

    def body(me_ref, own_ref, r_ref, w_ref, m_ref, v_ref, after_ref, g_ref, d_ref, m_out, v_out):
        acc = None
        for k in range(n):
            term = jnp.where(me_ref[0] == k, own_ref[...], r_ref[k].astype(F32))
            acc = term if acc is None else acc + term
        g_ref[...] = acc
        d_ref[...], m_out[...], v_out[...] = _adamw_math(w_ref[...], acc, m_ref[...], v_ref[...])

    blk = pl.BlockSpec((tr, C), lambda r, me_ref: (r, 0))
    return pl.pallas_call(
        body, name=name,
        grid_spec=pltpu.PrefetchScalarGridSpec(
            num_scalar_prefetch=1, grid=(nr,),
            in_specs=[g_spec, pl.BlockSpec((n, tr, C), lambda r, me_ref: (0, r, 0)), blk, blk, blk, ANY], out_specs=[blk] * 4),
        out_shape=[jax.ShapeDtypeStruct((R, C), F32)] * 4, compiler_params=_cparams(1),
    )(me, g, recv, w, m, v, after)


def _ada_cols(me, c_all, w_ada16, b_ada):
    nd, D = c_all.shape
    W = w_ada16.shape[1]

    def body(me_ref, c_ref, w_ref, b_ref, o_ref):
        cv = c_ref[...]
        r = _dot((cv * _sigmoid(cv)).astype(BF16), w_ref[...]) + b_ref[...]
        for b in range(nd):
            o_ref[b] = r[b:b + 1, :]

    return pl.pallas_call(
        body, name="ada_cols",
        grid_spec=pltpu.PrefetchScalarGridSpec(
            num_scalar_prefetch=1, grid=(1,),
            in_specs=[_full((nd, D)), _full((D, W)), pl.BlockSpec((1, W), lambda i, me_ref: (0, me_ref[0]))],
            out_specs=_full((nd, 1, W))),
        out_shape=jax.ShapeDtypeStruct((nd, 1, W), F32), compiler_params=_cparams(1),
    )(me, c_all, w_ada16, b_ada)


def _rows_to_owners(part, name):
    def body(p_ref, o_ref, send_sems, recv_sems, local_sem):
        x, y, c = _mesh_pos()
        me = 4 * x + 2 * y + c
        mine = pltpu.make_async_copy(p_ref.at[me], o_ref.at[me], local_sem.at[0])
        sends, arrivals = _peer_copies(lambda a, p: p_ref.at[p], [o_ref], send_sems, recv_sems)
        mine.start()
        for cp in sends:
            cp.start()
        for cp in arrivals:
            cp.wait_recv()
        for cp in sends:
            cp.wait_send()
        mine.wait()

    return pl.pallas_call(
        body, name=name, in_specs=[ANY], out_specs=ANY, out_shape=jax.ShapeDtypeStruct(part.shape, part.dtype),
        scratch_shapes=_peer_sems(1) + [pltpu.SemaphoreType.DMA((1,))],
    )(part)


def _ada_mod(me, c_all, w_ada16, b_ada):
    part = _ada_cols(me, c_all, w_ada16, b_ada)
    return _rows_to_owners(part, "scatter_mod").reshape(1, N_DEV * w_ada16.shape[1])


def _ada_bwd(me, small_g, D, W, c_block):
    def body(me_ref, c_ref, dm_ref, o_ref):
        cv = c_ref[...]
        ca = (cv * _sigmoid(cv)).astype(BF16).astype(F32)
        dm = dm_ref[...].astype(BF16).astype(F32)
        o_ref[...] = lax.dot_general(ca, dm, (((0,), (0,)), ((), ())), precision=lax.Precision.HIGHEST,
                                     preferred_element_type=F32)

    return pl.pallas_call(
        body, name="ada_bwd",
        grid_spec=pltpu.PrefetchScalarGridSpec(
            num_scalar_prefetch=1, grid=(1,),
            in_specs=[pl.BlockSpec((N_DEV, D), lambda i, me_ref: (0, c_block)),
                      pl.BlockSpec((N_DEV, W), lambda i, me_ref: (0, me_ref[0]))],
            out_specs=pl.BlockSpec((D, W), lambda i, me_ref: (0, 0))),
        out_shape=jax.ShapeDtypeStruct((D, W), F32), compiler_params=_cparams(1),
    )(me, small_g, small_g)


def _fwd_in(x, mod, w_in_g, b_in, shards):
    S, D = x.shape
    nb, _, wb = w_in_g.shape
    N = nb * wb
    TM = TOKEN_TILE
    n_tiles = S // TM
    ns = len(shards)

    def body(*refs):
        x_ref, mod_ref, w_hbm, b_ref = refs[:4]
        ins = refs[4:4 + ns]
        z_ref, h_ref, zg_ref = refs[4 + ns:7 + ns]
        outs = refs[7 + ns:7 + 2 * ns]
        w_v, sems, send_sems, recv_sems, local_sems = refs[7 + 2 * ns:]
        i = pl.program_id(0)
        x_, y_, c_ = _mesh_pos()
        me = 4 * x_ + 2 * y_ + c_
        mine = [pltpu.make_async_copy(ins[a], outs[a].at[me], local_sems.at[a]) for a in range(ns)]
        sends, arrivals = _peer_copies(lambda a, p: ins[a], outs, send_sems, recv_sems)

        @pl.when(i == 0)
        def _():
            for cp in sends + mine:
                cp.start()
            _load_cols(w_hbm, w_v, sems)

        xhat, _ = _ln(x_ref[...])
        hb = (xhat * (1.0 + mod_ref[:, D:2 * D]) + mod_ref[:, 0:D]).astype(BF16)
        h_ref[...] = hb
        z_ref[...] = _dot(hb, w_v[...]) + b_ref[...]
        zg_ref[:, 0:D] = z_ref[:, 2 * D:3 * D].astype(BF16)
        zg_ref[:, D:3 * D] = z_ref[:, 5 * D:7 * D].astype(BF16)

        @pl.when(i == n_tiles - 1)
        def _():
            for cp in arrivals:
                cp.wait_recv()
            for cp in sends:
                cp.wait_send()
            for cp in mine:
                cp.wait()

    res = pl.pallas_call(
        body, name="fwd_in", grid=(n_tiles,),
        in_specs=[pl.BlockSpec((TM, D), lambda i: (i, 0)), _full(mod.shape), ANY, _full((1, N))] + [ANY] * ns,
        out_specs=[pl.BlockSpec((TM, N), lambda i: (i, 0)), pl.BlockSpec((TM, D), lambda i: (i, 0)),
                   pl.BlockSpec((TM, 3 * D), lambda i: (i, 0))] + [ANY] * ns,
        out_shape=[jax.ShapeDtypeStruct((S, N), F32), jax.ShapeDtypeStruct((S, D), BF16), jax.ShapeDtypeStruct((S, 3 * D), BF16)]
        + [jax.ShapeDtypeStruct((N_DEV,) + a.shape, a.dtype) for a in shards],
        scratch_shapes=[pltpu.VMEM((D, N), BF16), pltpu.SemaphoreType.DMA((nb,))] + _peer_sems(ns) + [pltpu.SemaphoreType.DMA((ns,))],
        compiler_params=_cparams(1),
    )(x, mod, w_in_g, b_in, *shards)
    return res[0], res[1], res[2], list(res[3:])


def _rows_from(win, o, R):
    if o % 8 == 0:
        return win[o:o + R, :]
    return pltpu.roll(win, win.shape[0] - o, axis=0)[0:R, :]


def _causal_taps(buf, w_ref, o_ref, K, halo, TM, D):
    R, C = CONV_ROWS, CONV_LANES
    base = halo - (K - 1)

    for r0 in range(0, TM, R):
        for c0 in range(0, D, C):
            win = buf[r0:r0 + R + halo, c0:c0 + C]
            acc = jnp.zeros((R, C), F32)
            for k in range(K):
                acc = acc + w_ref[k:k + 1, c0:c0 + C] * _rows_from(win, base + k, R)
            o_ref[r0:r0 + R, c0:c0 + C] = acc


def _fwd_mix(z, x, mod, wa, ba, lag, lab, w_a_out, b_a_out, wb, w_b_out, w_o, b_o):
    S, D = x.shape
    N = z.shape[1]
    TM = TOKEN_TILE
    KA, KB = wa.shape[0], wb.shape[0]

    def body(z_ref, x_ref, mod_ref, wa_ref, ba_ref, lag_ref, lab_ref, wao_hbm, bao_ref, wb_ref, wbo_hbm, wo_hbm, bo_ref,
             u1_ref, ya_ref, yb_ref, q_ref, out_ref, r1_ref, u2_ref, v_ref, mg_ref,
             ubuf, pbuf, qbuf, wao, wbo, wo, sems):
        i = pl.program_id(0)

        @pl.when(i == 0)
        def _():
            _load_whole([(wao_hbm, wao), (wbo_hbm, wbo), (wo_hbm, wo)], sems)
            ubuf[0:HALO_A, :] = jnp.zeros((HALO_A, D), F32)
            pbuf[0:HALO_B, :] = jnp.zeros((HALO_B, D), F32)

        ubuf[HALO_A:HALO_A + TM, :] = z_ref[:, 0:D] * _sigmoid(z_ref[:, D:2 * D])
        _causal_taps(ubuf, wa_ref, u1_ref, KA, HALO_A, TM, D)
        ubuf[0:HALO_A, :] = ubuf[TM:TM + HALO_A, :]
        u1 = u1_ref[...] + ba_ref[...]
        u1_ref[...] = u1
        xa, _ = _ln(u1)
        l = xa * lag_ref[...] + lab_ref[...]
        u2 = (l * _sigmoid(l)).astype(BF16)
        u2_ref[...] = u2
        ya = _dot(u2, wao[...]) + bao_ref[...]
        ya_ref[...] = ya.astype(BF16)
        pbuf[HALO_B:HALO_B + TM, :] = z_ref[:, 3 * D:4 * D] * z_ref[:, 4 * D:5 * D]
        _causal_taps(pbuf, wb_ref, qbuf, KB, HALO_B, TM, D)
        pbuf[0:HALO_B, :] = pbuf[TM:TM + HALO_B, :]
        q_ref[...] = qbuf[...].astype(BF16)
        v = (z_ref[:, 2 * D:3 * D] * qbuf[...]).astype(BF16)
        v_ref[...] = v
        yb = _dot(v, wbo[...])
        yb_ref[...] = yb.astype(BF16)
        mg = (_sigmoid(z_ref[:, 5 * D:6 * D]) * ya + _sigmoid(z_ref[:, 6 * D:7 * D]) * yb).astype(BF16)
        mg_ref[...] = mg
        out = _dot(mg, wo[...]) + bo_ref[...]
        out_ref[...] = out.astype(BF16)
        r1_ref[...] = ALPHA * x_ref[...] + (1.0 + mod_ref[:, 2 * D:3 * D]) * out

    tile = pl.BlockSpec((TM, D), lambda i: (i, 0))
    vec = _full((1, D))
    return pl.pallas_call(
        body, name="fwd_mix", grid=(S // TM,),
        in_specs=[pl.BlockSpec((TM, N), lambda i: (i, 0)), tile, _full(mod.shape), _full((KA, D)), vec, vec, vec, ANY, vec,
                  _full((KB, D)), ANY, ANY, vec],
        out_specs=[tile] * 9,
        out_shape=[jax.ShapeDtypeStruct((S, D), dt) for dt in (F32, BF16, BF16, BF16, BF16, F32, BF16, BF16, BF16)],
        scratch_shapes=[pltpu.VMEM((TM + HALO_A, D), F32), pltpu.VMEM((TM + HALO_B, D), F32), pltpu.VMEM((TM, D), F32),
                        pltpu.VMEM((D, D), BF16), pltpu.VMEM((D, D), BF16), pltpu.VMEM((D, D), BF16),
                        pltpu.SemaphoreType.DMA((3,))],
        compiler_params=_cparams(1),
    )(z, x, mod, wa, ba, lag, lab, w_a_out, b_a_out, wb, w_b_out, w_o, b_o)


def _ffn(r1, tgt, mod, ln1_g, ln1_b, w_up_g, b_up, w_down, b_down, ln2_g, ln2_b):
    S, D = r1.shape
    nb, _, wb = w_up_g.shape
    FF = nb * wb
    TM = TOKEN_TILE
    CF = FFN_CHUNK
    n_tiles = S // TM

    def body(r1_ref, t_ref, mod_ref, g1_ref, b1_ref, wup_hbm, bup_ref, wdn_hbm, bdn_ref, g2_ref, b2_ref,
             dx1_ref, h2_ref, f_ref, dhu_ref, do2_ref, acc_ref, accup_ref,
             wup, wdn, relu_buf, sems, sem2):
        i = pl.program_id(0)

        @pl.when(i == 0)
        def _():
            _load_cols(wup_hbm, wup, sems)
            _load_whole([(wdn_hbm, wdn)], sem2)
            acc_ref[...] = jnp.zeros(acc_ref.shape, F32)
            accup_ref[...] = jnp.zeros(accup_ref.shape, F32)

        scale2, shift2, gate2 = mod_ref[:, 4 * D:5 * D], mod_ref[:, 3 * D:4 * D], mod_ref[:, 5 * D:6 * D]
        xhat1, _ = _ln(r1_ref[...])
        x1 = xhat1 * g1_ref[...] + b1_ref[...]
        xh0, rstd0 = _ln(x1)
        h2 = (xh0 * (1.0 + scale2) + shift2).astype(BF16)
        h2_ref[...] = h2
        out2 = jnp.zeros((TM, D), F32) + bdn_ref[...]
        for c0 in range(0, FF, CF):
            hu = _dot(h2, wup[:, c0:c0 + CF]) + bup_ref[:, c0:c0 + CF]
            rl = jnp.maximum(hu, 0.0)
            relu_buf[:, c0:c0 + CF] = rl
            fb = (rl * rl).astype(BF16)
            f_ref[:, c0:c0 + CF] = fb
            out2 = out2 + _dot(fb, wdn[c0:c0 + CF, :])
        r2 = ALPHA * x1 + (1.0 + gate2) * out2
        xh2, rstd2 = _ln(r2)
        e = xh2 * g2_ref[...] + b2_ref[...] - t_ref[...]
        acc_ref[6:7, :] += _colsum(e * e)
        dy = e * (1.0 / D)
        acc_ref[1:2, :] += _colsum(dy * xh2)
        acc_ref[2:3, :] += _colsum(dy)
        dr2 = _ln_bwd(dy * g2_ref[...], xh2, rstd2)
        acc_ref[3:4, :] += _colsum(dr2 * out2)
        do2 = (1.0 + gate2) * dr2
        acc_ref[0:1, :] += _colsum(do2)
        do2b = do2.astype(BF16)
        do2_ref[...] = do2b
        dh2 = jnp.zeros((TM, D), F32)
        for c0 in range(0, FF, CF):
            dhu = _dot_nt(do2b, wdn[c0:c0 + CF, :]) * (2.0 * relu_buf[:, c0:c0 + CF])
            accup_ref[:, c0:c0 + CF] += _colsum(dhu)
            dhub = dhu.astype(BF16)
            dhu_ref[:, c0:c0 + CF] = dhub
            dh2 = dh2 + _dot_nt(dhub, wup[:, c0:c0 + CF])
        acc_ref[4:5, :] += _colsum(dh2 * xh0)
        acc_ref[5:6, :] += _colsum(dh2)
        dx1_ref[...] = (ALPHA * dr2 + _ln_bwd(dh2 * (1.0 + scale2), xh0, rstd0)).astype(BF16)

        @pl.when(i == n_tiles - 1)
        def _():
            tot = jnp.sum(acc_ref[6:7, :], axis=-1, keepdims=True) * (0.5 / D)
            acc_ref[6:7, :] = jnp.broadcast_to(tot, (1, D))

    tile = pl.BlockSpec((TM, D), lambda i: (i, 0))
    wide = pl.BlockSpec((TM, FF), lambda i: (i, 0))
    vec = _full((1, D))
    return pl.pallas_call(
        body, name="ffn", grid=(n_tiles,),
        in_specs=[tile, tile, _full(mod.shape), vec, vec, ANY, _full((1, FF)), ANY, vec, vec, vec],
        out_specs=[tile, tile, wide, wide, tile, _full((8, D)), _full((1, FF))],
        out_shape=[jax.ShapeDtypeStruct((S, D), BF16), jax.ShapeDtypeStruct((S, D), BF16), jax.ShapeDtypeStruct((S, FF), BF16),
                   jax.ShapeDtypeStruct((S, FF), BF16), jax.ShapeDtypeStruct((S, D), BF16),
                   jax.ShapeDtypeStruct((8, D), F32), jax.ShapeDtypeStruct((1, FF), F32)],
        scratch_shapes=[pltpu.VMEM((D, FF), BF16), pltpu.VMEM((FF, D), BF16), pltpu.VMEM((TM, FF), F32),
                        pltpu.SemaphoreType.DMA((nb,)), pltpu.SemaphoreType.DMA((1,))],
        compiler_params=_cparams(1),
    )(r1, tgt, mod, ln1_g, ln1_b, w_up_g, b_up, w_down, b_down, ln2_g, ln2_b)


def _dw(a, b, name, msplit=1, nsplit=1, wire=False, exchange=None):
    S, M = a.shape
    N = b.shape[1]
    TK = min(DW_TILE, S)
    mb, nbk = M // msplit, N // nsplit
    nk = S // TK
    srcs, kinds = exchange if exchange else ((), ())
    ns = len(srcs)

    def body(*refs):
        a_ref, b_ref = refs[:2]
        ins = refs[2:2 + ns]
        o_ref = refs[2 + ns]
        o16_ref = refs[3 + ns] if wire else None
        rest = refs[3 + ns + (1 if wire else 0):]
        k = pl.program_id(2)
        if ns:
            outs, (send_sems, recv_sems) = rest[:ns], rest[ns:]
            sends, arrivals = _peer_copies(lambda q, p: _shard_of(ins[q], kinds[q], p), outs, send_sems, recv_sems)
            first = (pl.program_id(0) == 0) & (pl.program_id(1) == 0) & (k == 0)
            last = (pl.program_id(0) == msplit - 1) & (pl.program_id(1) == nsplit - 1) & (k == nk - 1)

            @pl.when(first)
            def _():
                for cp in sends:
                    cp.start()

        @pl.when(k == 0)
        def _():
            o_ref[...] = jnp.zeros(o_ref.shape, F32)

        o_ref[...] += _dot_tn(a_ref[...], b_ref[...])

        if wire:
            @pl.when(k == nk - 1)
            def _():
                o16_ref[...] = o_ref[...].astype(BF16)

        if ns:
            @pl.when(last)
            def _():
                for cp in arrivals:
                    cp.wait_recv()
                for cp in sends:
                    cp.wait_send()

    oblk = pl.BlockSpec((mb, nbk), lambda i, j, k: (i, j))
    res = pl.pallas_call(
        body, name=name, grid=(msplit, nsplit, nk),
        in_specs=[pl.BlockSpec((TK, mb), lambda i, j, k: (k, i)), pl.BlockSpec((TK, nbk), lambda i, j, k: (k, j))] + [ANY] * ns,
        out_specs=[oblk] + ([oblk] if wire else []) + [ANY] * ns,
        out_shape=[jax.ShapeDtypeStruct((M, N), F32)] + ([jax.ShapeDtypeStruct((M, N), BF16)] if wire else [])
        + [jax.ShapeDtypeStruct((N_DEV,) + _shard_shape(g.shape, kd), g.dtype) for g, kd in zip(srcs, kinds)],
        scratch_shapes=_peer_sems(ns) if ns else [],
        compiler_params=_cparams(3),
    )(a, b, *srcs)
    return res if (wire or ns) else res[0]


def _bwd_mix1(dx1, r1, out, ya, yb, q, u1, zg, mod, ln1_g, lag, lab, w_o, w_a_out, w_b_out):
    S, D = dx1.shape
    TM = TOKEN_TILE

    def body(dx1_ref, r1_ref, out_ref, ya_ref, yb_ref, q_ref, u1_ref, zg_ref, mod_ref, g1_ref, lag_ref, lab_ref,
             wo_hbm, wao_hbm, wbo_hbm,
             dxp_ref, du1_ref, dq_ref, dzc_ref, dout_ref, dya_ref, dyb_ref, acc_ref,
             wo, wao, wbo, sems):
        @pl.when(pl.program_id(0) == 0)
        def _():
            _load_whole([(wo_hbm, wo), (wao_hbm, wao), (wbo_hbm, wbo)], sems)
            acc_ref[...] = jnp.zeros(acc_ref.shape, F32)

        dx1v = dx1_ref[...].astype(F32)
        xhat1, rstd1 = _ln(r1_ref[...])
        acc_ref[0:1, :] += _colsum(dx1v * xhat1)
        acc_ref[1:2, :] += _colsum(dx1v)
        dr1 = _ln_bwd(dx1v * g1_ref[...], xhat1, rstd1)
        dxp_ref[...] = (ALPHA * dr1).astype(BF16)
        acc_ref[2:3, :] += _colsum(dr1 * out_ref[...].astype(F32))
        dout = (1.0 + mod_ref[:, 2 * D:3 * D]) * dr1
        acc_ref[3:4, :] += _colsum(dout)
        doutb = dout.astype(BF16)
        dout_ref[...] = doutb
        dmg = _dot_nt(doutb, wo[...])
        sga, sgb = _sigmoid(zg_ref[:, D:2 * D].astype(F32)), _sigmoid(zg_ref[:, 2 * D:3 * D].astype(F32))
        dga = dmg * ya_ref[...].astype(F32) * sga * (1.0 - sga)
        dgb = dmg * yb_ref[...].astype(F32) * sgb * (1.0 - sgb)
        acc_ref[9:10, :] += _colsum(dga)
        acc_ref[10:11, :] += _colsum(dgb)
        dzc_ref[:, D:2 * D] = dga.astype(BF16)
        dzc_ref[:, 2 * D:3 * D] = dgb.astype(BF16)
        dya = dmg * sga
        acc_ref[4:5, :] += _colsum(dya)
        dyab = dya.astype(BF16)
        dya_ref[...] = dyab
        dybb = (dmg * sgb).astype(BF16)
        dyb_ref[...] = dybb
        du2 = _dot_nt(dyab, wao[...])
        xa, rstda = _ln(u1_ref[...])
        l = xa * lag_ref[...] + lab_ref[...]
        sl = _sigmoid(l)
        dl = du2 * (sl * (1.0 + l * (1.0 - sl)))
        acc_ref[5:6, :] += _colsum(dl * xa)
        acc_ref[6:7, :] += _colsum(dl)
        du1 = _ln_bwd(dl * lag_ref[...], xa, rstda)
        acc_ref[7:8, :] += _colsum(du1)
        du1_ref[...] = du1.astype(BF16)
        dv = _dot_nt(dybb, wbo[...])
        dgbk = dv * q_ref[...].astype(F32)
        acc_ref[8:9, :] += _colsum(dgbk)
        dzc_ref[:, 0:D] = dgbk.astype(BF16)
        dq_ref[...] = (dv * zg_ref[:, 0:D].astype(F32)).astype(BF16)

    tile = pl.BlockSpec((TM, D), lambda i: (i, 0))
    vec = _full((1, D))
    return pl.pallas_call(
        body, name="bwd_mix1", grid=(S // TM,),
        in_specs=[tile] * 7 + [pl.BlockSpec((TM, 3 * D), lambda i: (i, 0)), _full(mod.shape), vec, vec, vec, ANY, ANY, ANY],
        out_specs=[tile, tile, tile, pl.BlockSpec((TM, 3 * D), lambda i: (i, 0)), tile, tile, tile, _full((16, D))],
        out_shape=[jax.ShapeDtypeStruct((S, D), BF16)] * 3 + [jax.ShapeDtypeStruct((S, 3 * D), BF16)]
        + [jax.ShapeDtypeStruct((S, D), BF16)] * 3 + [jax.ShapeDtypeStruct((16, D), F32)],
        scratch_shapes=[pltpu.VMEM((D, D), BF16)] * 3 + [pltpu.SemaphoreType.DMA((3,))],
        compiler_params=_cparams(1),
    )(dx1, r1, out, ya, yb, q, u1, zg, mod, ln1_g, lag, lab, w_o, w_a_out, w_b_out)


def _anticausal_taps(dbuf, u_ref, w_ref, o_ref, dwacc, K, halo, TM, D):
    R, C = CONV_ROWS, CONV_LANES

    for r0 in range(0, TM, R):
        for c0 in range(0, D, C):
            win = dbuf[r0:r0 + R + halo, c0:c0 + C]
            uc = u_ref[r0:r0 + R, c0:c0 + C]
            acc = jnp.zeros((R, C), F32)
            for j in range(K):
                sh = _rows_from(win, j, R)
                k = K - 1 - j
                acc = acc + w_ref[k:k + 1, c0:c0 + C] * sh
                pr = uc * sh
                part = pr[0:8, :]
                for s in range(8, R, 8):
                    part = part + pr[s:s + 8, :]
                dwacc[k, :, c0:c0 + C] += part
            o_ref[r0:r0 + R, c0:c0 + C] = acc


def _bwd_mix2(du1, dq, z, dzc, x, dxp, mod, wa, wb, w_in_g):
    S, D = x.shape
    N = z.shape[1]
    nb, _, wbk = w_in_g.shape
    TM = TOKEN_TILE
    KA, KB = wa.shape[0], wb.shape[0]
    n_tiles = S // TM

    def body(du1_ref, dq_ref, zav_ref, zc_ref, zx_ref, dzc_ref, x_ref, dxp_ref, mod_ref, wa_ref, wb_ref, win_hbm,
             dz_ref, gx_ref, dwa_ref, dwb_ref, acc_ref, dbin_ref,
             dbuf, qbuf, ubuf, pbuf, obuf, dwa_acc, dwb_acc, w_v, sems):
        i = pl.program_id(0)

        @pl.when(i == 0)
        def _():
            _load_cols(win_hbm, w_v, sems)
            dbuf[TM:TM + HALO_A, :] = jnp.zeros((HALO_A, D), F32)
            qbuf[TM:TM + HALO_B, :] = jnp.zeros((HALO_B, D), F32)
            dwa_acc[...] = jnp.zeros(dwa_acc.shape, F32)
            dwb_acc[...] = jnp.zeros(dwb_acc.shape, F32)
            acc_ref[...] = jnp.zeros(acc_ref.shape, F32)
            dbin_ref[...] = jnp.zeros(dbin_ref.shape, F32)

        a_val = zav_ref[:, 0:D]
        sa = _sigmoid(zav_ref[:, D:2 * D])
        ubuf[...] = a_val * sa
        dbuf[0:TM, :] = du1_ref[...].astype(F32)
        _anticausal_taps(dbuf, ubuf, wa_ref, obuf, dwa_acc, KA, HALO_A, TM, D)
        dbuf[TM:TM + HALO_A, :] = dbuf[0:HALO_A, :]
        du0 = obuf[...]
        dav = du0 * sa
        dag = du0 * a_val * sa * (1.0 - sa)
        dbin_ref[:, 0:D] += _colsum(dav)
        dbin_ref[:, D:2 * D] += _colsum(dag)
        dz_ref[:, 0:D] = dav.astype(BF16)
        dz_ref[:, D:2 * D] = dag.astype(BF16)
        pbuf[...] = zc_ref[...] * zx_ref[...]
        qbuf[0:TM, :] = dq_ref[...].astype(F32)
        _anticausal_taps(qbuf, pbuf, wb_ref, obuf, dwb_acc, KB, HALO_B, TM, D)
        qbuf[TM:TM + HALO_B, :] = qbuf[0:HALO_B, :]
        dp = obuf[...]
        dgc = dp * zx_ref[...]
        dgx = dp * zc_ref[...]
        dbin_ref[:, 3 * D:4 * D] += _colsum(dgc)
        dbin_ref[:, 4 * D:5 * D] += _colsum(dgx)
        dz_ref[:, 3 * D:4 * D] = dgc.astype(BF16)
        dz_ref[:, 4 * D:5 * D] = dgx.astype(BF16)
        dz_ref[:, 2 * D:3 * D] = dzc_ref[:, 0:D]
        dz_ref[:, 5 * D:7 * D] = dzc_ref[:, D:3 * D]
        dh = _dot_nt(dz_ref[...], w_v[...])
        xhat, rstd = _ln(x_ref[...])
        acc_ref[0:1, :] += _colsum(dh * xhat)
        acc_ref[1:2, :] += _colsum(dh)
        gx_ref[...] = dxp_ref[...].astype(F32) + _ln_bwd(dh * (1.0 + mod_ref[:, D:2 * D]), xhat, rstd)

        @pl.when(i == n_tiles - 1)
        def _():
            for k in range(KA):
                dwa_ref[k:k + 1, :] = jnp.sum(dwa_acc[k], axis=0, keepdims=True)
            for k in range(KB):
                dwb_ref[k:k + 1, :] = jnp.sum(dwb_acc[k], axis=0, keepdims=True)

    rev = lambda i: n_tiles - 1 - i
    tile = pl.BlockSpec((TM, D), lambda i: (rev(i), 0))
    zcol = lambda k: pl.BlockSpec((TM, D), lambda i: (rev(i), k))
    return pl.pallas_call(
        body, name="bwd_mix2", grid=(n_tiles,),
        in_specs=[tile, tile, pl.BlockSpec((TM, 2 * D), lambda i: (rev(i), 0)), zcol(3), zcol(4),
                  pl.BlockSpec((TM, 3 * D), lambda i: (rev(i), 0)), tile, tile, _full(mod.shape), _full((KA, D)), _full((KB, D)), ANY],
        out_specs=[pl.BlockSpec((TM, N), lambda i: (rev(i), 0)), tile, _full((KA, D)), _full((KB, D)), _full((8, D)), _full((1, N))],
        out_shape=[jax.ShapeDtypeStruct((S, N), BF16), jax.ShapeDtypeStruct((S, D), F32), jax.ShapeDtypeStruct((KA, D), F32),
                   jax.ShapeDtypeStruct((KB, D), F32), jax.ShapeDtypeStruct((8, D), F32), jax.ShapeDtypeStruct((1, N), F32)],
        scratch_shapes=[pltpu.VMEM((TM + HALO_A, D), F32), pltpu.VMEM((TM + HALO_B, D), F32), pltpu.VMEM((TM, D), F32),
                        pltpu.VMEM((TM, D), F32), pltpu.VMEM((TM, D), F32), pltpu.VMEM((KA, 8, D), F32), pltpu.VMEM((KB, 8, D), F32),
                        pltpu.VMEM((D, N), BF16), pltpu.SemaphoreType.DMA((nb,))],
        compiler_params=_cparams(1),
    )(du1, dq, z, z, z, dzc, x, dxp, mod, wa, wb, w_in_g)


PEER_REDUCED = ("w_up", "w_down", "w_o", "w_a_out", "w_b_out", "conv_a_w", "conv_b_w")


def _local_grads(me, x, c, c_all, tgt, w_ada16, w_in_g, shards, vecs):
    D = x.shape[1]
    rows = lambda a: a.reshape(a.shape[0] * a.shape[1], a.shape[2])
    cols = lambda a: jnp.transpose(a, (1, 0, 2)).reshape(a.shape[1], a.shape[0] * a.shape[2])
    mod = _ada_mod(me, c_all, w_ada16, vecs["b_ada"])
    z, h1, zg, gathered = _fwd_in(x, mod, w_in_g, vecs["b_in"], shards)
    w_a_out, w_b_out, w_o, w_up_g, w_down = rows(gathered[0]), rows(gathered[1]), rows(gathered[2]), gathered[3], rows(gathered[4])
    wa, wb = cols(gathered[5]), cols(gathered[6])
    u1, ya, yb, q, out, r1, u2, v, mg = _fwd_mix(
        z, x, mod, wa, vecs["conv_a_b"], vecs["ln_a_g"], vecs["ln_a_b"], w_a_out, vecs["b_a_out"], wb, w_b_out, w_o, vecs["b_o"])
    dx1, h2, f, dhu, do2, acc_f, acc_up = _ffn(
        r1, tgt, mod, vecs["ln1_g"], vecs["ln1_b"], w_up_g, vecs["b_up"], w_down, vecs["b_down"], vecs["ln2_g"], vecs["ln2_b"])
    g_up, g_up16 = _dw(h2, dhu, "dw_up", nsplit=2, wire=True)
    g_down, g_down16 = _dw(f, do2, "dw_down", msplit=2, wire=True)
    dxp, du1, dq, dzc, doutb, dyab, dybb, acc_1 = _bwd_mix1(
        dx1, r1, out, ya, yb, q, u1, zg, mod, vecs["ln1_g"], vecs["ln_a_g"], vecs["ln_a_b"], w_o, w_a_out, w_b_out)
    g_o, g_o16 = _dw(mg, doutb, "dw_o", wire=True)
    g_a_out, g_a_out16 = _dw(u2, dyab, "dw_a_out", wire=True)
    g_b_out, g_b_out16 = _dw(v, dybb, "dw_b_out", wire=True)
    dz, gx, g_wa, g_wb, acc_2, db_in = _bwd_mix2(du1, dq, z, dzc, x, dxp, mod, wa, wb, w_in_g)
    full = dict(zip(PEER_REDUCED, (g_up, g_down, g_o, g_a_out, g_b_out, g_wa, g_wb)))
    res = _dw(h1, dz, "dw_in", nsplit=4,
              exchange=((g_up16, g_down16, g_o16, g_a_out16, g_b_out16, g_wa, g_wb), [SHARD_KIND[n] for n in PEER_REDUCED]))
    g_in = res[0]
    peers = {n: (full[n], r) for n, r in zip(PEER_REDUCED, res[1:])}
    row = lambda acc, k: (acc, k, 0, D)
    pieces = [
        row(acc_2, 1), row(acc_2, 0), row(acc_1, 2), row(acc_f, 5), row(acc_f, 4), row(acc_f, 3),
        (db_in, 0, 0, 2 * D), row(acc_1, 8), (db_in, 0, 3 * D, 2 * D), row(acc_1, 9), row(acc_1, 10),
        row(acc_1, 7), row(acc_1, 5), row(acc_1, 6), row(acc_1, 4), row(acc_1, 3), row(acc_1, 0), row(acc_1, 1),
        (acc_up, 0, 0, acc_up.shape[1]), row(acc_f, 0), row(acc_f, 1), row(acc_f, 2), (c, 0, 0, D)]
    return acc_f[6, 0], gx, g_in, peers, pieces


WEIGHTS = ("w_ada", "b_ada", "w_in", "b_in", "conv_a_w", "conv_a_b", "ln_a_g", "ln_a_b", "w_a_out", "b_a_out", "conv_b_w",
           "w_b_out", "w_o", "b_o", "ln1_g", "ln1_b", "w_up", "b_up", "w_down", "b_down", "ln2_g", "ln2_b")
VECTORS = ("b_ada", "b_in", "conv_a_b", "ln_a_g", "ln_a_b", "b_a_out", "b_o", "ln1_g", "ln1_b", "b_up", "b_down", "ln2_g", "ln2_b")
SHARD_KIND = {"w_in": "col", "w_a_out": "row", "w_b_out": "row", "w_o": "row", "w_up": "col", "w_down": "row",
              "conv_a_w": "col", "conv_b_w": "col"}


def kernel(x, c, w_ada, b_ada, w_in, b_in, conv_a_w, conv_a_b, ln_a_g, ln_a_b, w_a_out, b_a_out, conv_b_w, w_b_out, w_o, b_o, ln1_g, ln1_b, w_up, b_up, w_down, b_down, ln2_g, ln2_b, loss_target, m_w_ada, m_b_ada, m_w_in, m_b_in, m_conv_a_w, m_conv_a_b, m_ln_a_g, m_ln_a_b, m_w_a_out, m_b_a_out, m_conv_b_w, m_w_b_out, m_w_o, m_b_o, m_ln1_g, m_ln1_b, m_w_up, m_b_up, m_w_down, m_b_down, m_ln2_g, m_ln2_b, v_w_ada, v_b_ada, v_w_in, v_b_in, v_conv_a_w, v_conv_a_b, v_ln_a_g, v_ln_a_b, v_w_a_out, v_b_a_out, v_conv_b_w, v_w_b_out, v_w_o, v_b_o, v_ln1_g, v_ln1_b, v_w_up, v_b_up, v_w_down, v_b_down, v_ln2_g, v_ln2_b):
    args = dict(locals())
    w = {n: args[n][0] for n in WEIGHTS}
    m = {n: args["m_" + n][0] for n in WEIGHTS}
    v = {n: args["v_" + n][0] for n in WEIGHTS}
    w = {n: (a[None, :] if a.ndim == 1 else a) for n, a in w.items()}
    m = {n: (a[None, :] if a.ndim == 1 else a) for n, a in m.items()}
    v = {n: (a[None, :] if a.ndim == 1 else a) for n, a in v.items()}
    xs, tgt = x[0], loss_target[0]
    S, D = xs.shape
    me = (4 * lax.axis_index("x") + 2 * lax.axis_index("y") + lax.axis_index("c")).astype(jnp.int32).reshape(1)
    core = lax.axis_index("c").astype(jnp.int32).reshape(1)
    chip = (2 * lax.axis_index("x") + lax.axis_index("y")).astype(jnp.int32).reshape(1)

    bf = lambda n: w[n].astype(BF16)
    w_in_g, c_all = _all_gather([bf("w_in"), c], "gather_weights")
    shards = [bf("w_a_out"), bf("w_b_out"), bf("w_o"), bf("w_up"), bf("w_down"), w["conv_a_w"], w["conv_b_w"]]
    vecs = {n: w[n] for n in VECTORS}

    loss, gx, g_in, peers, pieces = _local_grads(me, xs, c, c_all.reshape(N_DEV, D), tgt, bf("w_ada"), w_in_g, shards, vecs)

    (from_sibling,) = _exchange_cores([g_in], [SHARD_KIND["w_in"]], "reduce_cores")
    own_in, wire_in = _add_own(core, g_in, from_sibling, SHARD_KIND["w_in"], "add_w_in")
    sems, wire_thru, landing, after = _chips_start(wire_in)

    res = {}
    for n, (g_full, recv) in peers.items():
        res[n] = _sum_peers_adamw(me, g_full, recv, SHARD_KIND[n], w[n], m[n], v[n], "adamw_" + n, after)
        after = res[n][0]
    small = _pack_rows(pieces, "pack_vectors", after)
    (small_g,) = _all_gather([small], "gather_vectors")
    small_g = small_g.reshape(N_DEV, small.shape[1])
    W = w["w_ada"].shape[1]
    n_vec = small.shape[1] - D
    g_ada = _ada_bwd(me, small_g, D, W, n_vec // D)
    res["w_ada"] = _sum_adamw(g_ada[None], w["w_ada"], m["w_ada"], v["w_ada"], "adamw_w_ada")
    by_kind = _adamw_vectors(small_g, [w[n] for n in VECTORS], [m[n] for n in VECTORS], [v[n] for n in VECTORS])
    for i, n in enumerate(VECTORS):
        res[n] = tuple(by_kind[t][i] for t in range(4))
    from_chips = _chips_wait(sems, wire_thru, landing, res["w_ada"][0])
    res["w_in"] = _sum_chips_adamw(chip, own_in, from_chips, w["w_in"], m["w_in"], v["w_in"], "adamw_w_in")

    loss = lax.psum(loss, ("x", "y", "c"))
    outs = [loss, gx[None]]
    for t in range(4):
        outs += [res[n][t].reshape(args[n].shape) for n in WEIGHTS]
    return tuple(outs)
```

```python
import functools

import jax
import jax.numpy as jnp
from jax import lax
from jax.experimental import pallas as pl
from jax.experimental.pallas import tpu as pltpu

F32 = jnp.float32
BF16 = jnp.bfloat16
MESH = pl.DeviceIdType.MESH

N_DEV = 8
LN_EPS = 1e-5
DEPTH = 1
ALPHA = (2.0 * DEPTH) ** 0.25
ADAM_LR, ADAM_B1, ADAM_B2, ADAM_EPS, ADAM_WD, ADAM_STEP = 0.001, 0.9, 0.999, 1e-08, 0.01, 10

VMEM_LIMIT = 60 * 1024 * 1024
TOKEN_TILE = 256
DW_TILE = 2048
HALO_A = 32
HALO_B = 8
CONV_ROWS, CONV_LANES = 256, 256
FFN_CHUNK = 1024
LOSS_LANES = 128


def _cparams(n_grid):
    return pltpu.CompilerParams(dimension_semantics=("arbitrary",) * n_grid, vmem_limit_bytes=VMEM_LIMIT)


def _full(shape):
    return pl.BlockSpec(shape, lambda *_: (0,) * len(shape))


ANY = pl.BlockSpec(memory_space=pl.ANY)


def _ln(x):
    mu = jnp.mean(x, axis=-1, keepdims=True)
    xc = x - mu
    var = jnp.mean(xc * xc, axis=-1, keepdims=True)
    rstd = lax.rsqrt(var + LN_EPS)
    return xc * rstd, rstd


def _ln_bwd(dxhat, xhat, rstd):
    m1 = jnp.mean(dxhat, axis=-1, keepdims=True)
    m2 = jnp.mean(dxhat * xhat, axis=-1, keepdims=True)
    return rstd * (dxhat - m1 - xhat * m2)


def _sigmoid(x):
    return 0.5 * jnp.tanh(0.5 * x) + 0.5


def _colsum(a):
    return jnp.sum(a, axis=0, keepdims=True)


def _dot(a, b):
    return jnp.dot(a, b, preferred_element_type=F32)


def _dot_nt(a, b):
    return lax.dot_general(a, b, (((1,), (1,)), ((), ())), preferred_element_type=F32)


def _dot_tn(a, b):
    return lax.dot_general(a, b, (((0,), (0,)), ((), ())), preferred_element_type=F32)


def _load_cols(src_hbm, dst_vmem, sems):
    nblk, _, w = src_hbm.shape
    cps = [pltpu.make_async_copy(src_hbm.at[j], dst_vmem.at[:, pl.ds(j * w, w)], sems.at[j]) for j in range(nblk)]
    for cp in cps:
        cp.start()
    for cp in cps:
        cp.wait()


def _load_whole(pairs, sems):
    cps = [pltpu.make_async_copy(s, d, sems.at[k]) for k, (s, d) in enumerate(pairs)]
    for cp in cps:
        cp.start()
    for cp in cps:
        cp.wait()


def _mesh_pos():
    return lax.axis_index("x"), lax.axis_index("y"), lax.axis_index("c")


def _all_gather(arrs, name):
    n = len(arrs)

    def body(*refs):
        ins, outs = refs[:n], refs[n:2 * n]
        send_sems, recv_sems, local_sems = refs[2 * n:]
        x, y, c = _mesh_pos()
        me, sibling = (x, y, c), (x, y, 1 - c)
        chips = [(1 - x, y), (x, 1 - y), (1 - x, 1 - y)]

        def slot(a, px, py, pc):
            return outs[a].at[4 * px + 2 * py + pc]

        def copy(a, k, block, to, src=None):
            return pltpu.make_async_remote_copy(
                src_ref=slot(a, *block) if src is None else src, dst_ref=slot(a, *block),
                send_sem=send_sems.at[a, k], recv_sem=recv_sems.at[a, k], device_id=to, device_id_type=MESH)

        mine = [pltpu.make_async_copy(ins[a], slot(a, *me), local_sems.at[a]) for a in range(n)]
        for cp in mine:
            cp.start()
        first = []
        for a in range(n):
            first.append(copy(a, 0, me, sibling, src=ins[a]))
            first += [copy(a, 1 + j, me, (*chip, c), src=ins[a]) for j, chip in enumerate(chips)]
        for cp in first:
            cp.start()
        passed = []
        for a in range(n):
            for j, chip in enumerate(chips):
                copy(a, 1 + j, (*chip, c), me).wait_recv()
                fwd = copy(a, 4 + j, (*chip, c), sibling)
                fwd.start()
                passed.append(fwd)
        for a in range(n):
            copy(a, 0, sibling, me).wait_recv()
            for j, chip in enumerate(chips):
                copy(a, 4 + j, (*chip, 1 - c), me).wait_recv()
        for cp in first + passed:
            cp.wait_send()
        for cp in mine:
            cp.wait()

    outs = pl.pallas_call(
        body, name=name,
        out_shape=[jax.ShapeDtypeStruct((N_DEV,) + a.shape, a.dtype) for a in arrs],
        in_specs=[ANY] * n, out_specs=[ANY] * n,
        scratch_shapes=[pltpu.SemaphoreType.DMA((n, 7)), pltpu.SemaphoreType.DMA((n, 7)), pltpu.SemaphoreType.DMA((n,))],
    )(*arrs)
    return list(outs)


def _peer_copies(src_of, dsts, send_sems, recv_sems):
    x, y, c = _mesh_pos()
    me = 4 * x + 2 * y + c
    sends, arrivals = [], []
    for a in range(len(dsts)):
        for k in range(1, N_DEV):
            px, py, pc = (1 - x if k & 4 else x), (1 - y if k & 2 else y), (1 - c if k & 1 else c)
            p = 4 * px + 2 * py + pc
            common = dict(send_sem=send_sems.at[a, k - 1], recv_sem=recv_sems.at[a, k - 1],
                          device_id=(px, py, pc), device_id_type=MESH)
            sends.append(pltpu.make_async_remote_copy(src_ref=src_of(a, p), dst_ref=dsts[a].at[me], **common))
            arrivals.append(pltpu.make_async_remote_copy(src_ref=dsts[a].at[p], dst_ref=dsts[a].at[p], **common))
    return sends, arrivals


def _peer_sems(n):
    return [pltpu.SemaphoreType.DMA((n, N_DEV - 1)), pltpu.SemaphoreType.DMA((n, N_DEV - 1))]


def _shard_of(ref, kind, j):
    if kind == "col":
        w = ref.shape[1] // N_DEV
        return ref.at[:, pl.ds(j * w, w)]
    h = ref.shape[0] // N_DEV
    return ref.at[pl.ds(j * h, h), :]


def _shard_shape(shape, kind):
    return (shape[0], shape[1] // N_DEV) if kind == "col" else (shape[0] // N_DEV, shape[1])


def _exchange_cores(grads, kinds, name):
    n = len(grads)

    def body(*refs):
        ins, outs = refs[:n], refs[n:2 * n]
        send_sems, recv_sems = refs[2 * n:]
        x, y, c = _mesh_pos()
        sibling = (x, y, 1 - c)
        sends = []
        for a in range(n):
            for s in range(4):
                sends.append(pltpu.make_async_remote_copy(
                    src_ref=_shard_of(ins[a], kinds[a], 2 * s + (1 - c)), dst_ref=outs[a].at[s],
                    send_sem=send_sems.at[a, s], recv_sem=recv_sems.at[a, s], device_id=sibling, device_id_type=MESH))
        for cp in sends:
            cp.start()
        for a in range(n):
            for s in range(4):
                pltpu.make_async_remote_copy(
                    src_ref=outs[a].at[s], dst_ref=outs[a].at[s],
                    send_sem=send_sems.at[a, s], recv_sem=recv_sems.at[a, s], device_id=sibling, device_id_type=MESH).wait_recv()
        for cp in sends:
            cp.wait_send()

    outs = pl.pallas_call(
        body, name=name,
        out_shape=[jax.ShapeDtypeStruct((4,) + _shard_shape(g.shape, k), F32) for g, k in zip(grads, kinds)],
        in_specs=[ANY] * n, out_specs=[ANY] * n,
        scratch_shapes=[pltpu.SemaphoreType.DMA((n, 4)), pltpu.SemaphoreType.DMA((n, 4))],
    )(*grads)
    return list(outs)


HBM = pl.BlockSpec(memory_space=pltpu.HBM)
SEM = pl.BlockSpec(memory_space=pltpu.SEMAPHORE)
DATAFLOW = pltpu.SideEffectType.DATAFLOW_SIDE_EFFECTING


def _chip_copies(src_ref, land_ref, send_sems, recv_sems):
    x, y, c = _mesh_pos()
    my_slot = 2 * x + y
    sends, arrivals = [], []
    for j, (px, py) in enumerate([(1 - x, y), (x, 1 - y), (1 - x, 1 - y)]):
        common = dict(send_sem=send_sems[j], recv_sem=recv_sems[j], device_id=(px, py, c), device_id_type=MESH)
        sends.append(pltpu.make_async_remote_copy(src_ref=src_ref.at[2 * px + py], dst_ref=land_ref.at[my_slot], **common))
        arrivals.append(pltpu.make_async_remote_copy(src_ref=src_ref.at[2 * px + py], dst_ref=land_ref.at[2 * px + py], **common))
    return sends, arrivals


def _chips_start(part):
    def body(src_ref, land_ref, s0, s1, s2, r0, r1, r2, src_thru, land_thru, token):
        sends, _ = _chip_copies(src_ref, land_ref, (s0, s1, s2), (r0, r1, r2))
        for cp in sends:
            cp.start()
        token[...] = jnp.zeros_like(token)

    res = pl.pallas_call(
        body, name="reduce_chips_start",
        out_shape=(pltpu.SemaphoreType.DMA(()),) * 6 + (pltpu.HBM(part.shape, part.dtype), pltpu.HBM(part.shape, part.dtype),
                                                        jax.ShapeDtypeStruct((8, 128), F32)),
        in_specs=(HBM, HBM), out_specs=(SEM,) * 6 + (HBM, HBM, pl.BlockSpec(memory_space=pltpu.VMEM)),
        input_output_aliases={0: 6, 1: 7}, compiler_params=pltpu.CompilerParams(has_side_effects=DATAFLOW),
    )(pltpu.with_memory_space_constraint(part, pltpu.HBM),
      pltpu.with_memory_space_constraint(lax.empty(part.shape, part.dtype), pltpu.HBM))
    return res[:6], res[6], res[7], res[8]


def _chips_wait(sems, src_thru, land_thru, after):
    def body(src_ref, land_ref, s0, s1, s2, r0, r1, r2, after_ref, src_dead, got_ref):
        sends, arrivals = _chip_copies(src_ref, land_ref, (s0, s1, s2), (r0, r1, r2))
        for cp in sends:
            cp.wait_send()
        for cp in arrivals:
            cp.wait_recv()

    return pl.pallas_call(
        body, name="reduce_chips_wait",
        out_shape=(pltpu.HBM(src_thru.shape, src_thru.dtype), pltpu.HBM(land_thru.shape, land_thru.dtype)),
        in_specs=(HBM, HBM) + (SEM,) * 6 + (ANY,), out_specs=(HBM, HBM), input_output_aliases={0: 0, 1: 1},
        compiler_params=pltpu.CompilerParams(has_side_effects=DATAFLOW),
    )(src_thru, land_thru, *sems, after)[1]


def _row_tile(rows):
    for t in (256, 128, 64, 32, 16, 8):
        if rows % t == 0:
            return t
    return rows


def _wire_dtype(rows):
    return BF16 if rows % 16 == 0 else F32


def _add_own(core, g, recv, kind, name):
    ns, R, C = recv.shape
    tr = _row_tile(R)
    nr = R // tr
    if kind == "col":
        g_spec = pl.BlockSpec((tr, C), lambda s, r, c_ref: (r, 2 * s + c_ref[0]))
    else:
        g_spec = pl.BlockSpec((tr, C), lambda s, r, c_ref: ((2 * s + c_ref[0]) * nr + r, 0))
    slab = pl.BlockSpec((None, tr, C), lambda s, r, c_ref: (s, r, 0))

    def body(c_ref, g_ref, r_ref, o32_ref, o16_ref):
        t = g_ref[...] + r_ref[...]
        o32_ref[...] = t
        o16_ref[...] = t.astype(o16_ref.dtype)

    return pl.pallas_call(
        body, name=name,
        grid_spec=pltpu.PrefetchScalarGridSpec(num_scalar_prefetch=1, grid=(ns, nr), in_specs=[g_spec, slab], out_specs=[slab, slab]),
        out_shape=[jax.ShapeDtypeStruct((ns, R, C), F32), jax.ShapeDtypeStruct((ns, R, C), _wire_dtype(R))],
        compiler_params=_cparams(2),
    )(core, g, recv)


def _adamw_math(w, g, m, v):
    m2 = ADAM_B1 * m + (1.0 - ADAM_B1) * g
    v2 = ADAM_B2 * v + (1.0 - ADAM_B2) * (g * g)
    m_hat = m2 / (1.0 - ADAM_B1 ** ADAM_STEP)
    v_hat = v2 / (1.0 - ADAM_B2 ** ADAM_STEP)
    delta = -ADAM_LR * (m_hat / (jnp.sqrt(v_hat) + ADAM_EPS) + ADAM_WD * w)
    return delta, m2, v2


def _sum_adamw(parts, w, m, v, name):
    n, R, C = parts.shape
    tr = _row_tile(R)

    def body(p_ref, w_ref, m_ref, v_ref, g_ref, d_ref, m_out, v_out):
        g = p_ref[0]
        for k in range(1, n):
            g = g + p_ref[k]
        g_ref[...] = g
        d_ref[...], m_out[...], v_out[...] = _adamw_math(w_ref[...], g, m_ref[...], v_ref[...])

    blk = pl.BlockSpec((tr, C), lambda r: (r, 0))
    return pl.pallas_call(
        body, name=name, grid=(R // tr,),
        in_specs=[pl.BlockSpec((n, tr, C), lambda r: (0, r, 0)), blk, blk, blk],
        out_specs=[blk] * 4, out_shape=[jax.ShapeDtypeStruct((R, C), F32)] * 4, compiler_params=_cparams(1),
    )(parts, w, m, v)


def _sum_chips_adamw(chip, own, recv, w, m, v, name):
    n, R, C = recv.shape
    tr = _row_tile(R)

    def body(chip_ref, own_ref, r_ref, w_ref, m_ref, v_ref, g_ref, d_ref, m_out, v_out):
        g = None
        for k in range(n):
            term = jnp.where(chip_ref[0] == k, own_ref[...], r_ref[k].astype(F32))
            g = term if g is None else g + term
        g_ref[...] = g
        d_ref[...], m_out[...], v_out[...] = _adamw_math(w_ref[...], g, m_ref[...], v_ref[...])

    blk = pl.BlockSpec((tr, C), lambda r, chip_ref: (r, 0))
    return pl.pallas_call(
        body, name=name,
        grid_spec=pltpu.PrefetchScalarGridSpec(
            num_scalar_prefetch=1, grid=(R // tr,),
            in_specs=[pl.BlockSpec((None, tr, C), lambda r, chip_ref: (chip_ref[0], r, 0)),
                      pl.BlockSpec((n, tr, C), lambda r, chip_ref: (0, r, 0)), blk, blk, blk],
            out_specs=[blk] * 4),
        out_shape=[jax.ShapeDtypeStruct((R, C), F32)] * 4, compiler_params=_cparams(1),
    )(chip, own, recv, w, m, v)


def _pack_rows(pieces, name, after):
    arrs = []
    for a, _, _, _ in pieces:
        if not any(a is b for b in arrs):
            arrs.append(a)
    which = [next(i for i, b in enumerate(arrs) if b is a) for a, _, _, _ in pieces]
    total = sum(n for _, _, _, n in pieces)

    def body(*refs):
        o_ref = refs[len(arrs) + 1]
        off = 0
        for (_, r, c0, n), i in zip(pieces, which):
            o_ref[:, off:off + n] = refs[i][r:r + 1, c0:c0 + n]
            off += n

    return pl.pallas_call(
        body, name=name, in_specs=[_full(a.shape) for a in arrs] + [ANY], out_specs=_full((1, total)),
        out_shape=jax.ShapeDtypeStruct((1, total), F32), grid=(1,), compiler_params=_cparams(1),
    )(*arrs, after)


def _adamw_vectors(parts, ws, ms, vs, loss_at):
    nv = len(ws)
    widths = [a.shape[1] for a in ws]

    def body(*refs):
        p_ref = refs[0]
        w_refs, m_refs, v_refs = refs[1:1 + nv], refs[1 + nv:1 + 2 * nv], refs[1 + 2 * nv:1 + 3 * nv]
        outs = refs[1 + 3 * nv:]

        def total(off, n):
            g = p_ref[0:1, off:off + n]
            for k in range(1, p_ref.shape[0]):
                g = g + p_ref[k:k + 1, off:off + n]
            return g

        off = 0
        for i, n in enumerate(widths):
            g = total(off, n)
            outs[i][...] = g
            outs[nv + i][...], outs[2 * nv + i][...], outs[3 * nv + i][...] = _adamw_math(w_refs[i][...], g, m_refs[i][...], v_refs[i][...])
            off += n
        outs[4 * nv][...] = total(loss_at, LOSS_LANES)

    vec_specs = [_full((1, n)) for n in widths]
    res = pl.pallas_call(
        body, name="adamw_vectors", grid=(1,),
        in_specs=[_full(parts.shape)] + vec_specs * 3, out_specs=vec_specs * 4 + [_full((1, LOSS_LANES))],
        out_shape=[jax.ShapeDtypeStruct((1, n), F32) for n in widths] * 4 + [jax.ShapeDtypeStruct((1, LOSS_LANES), F32)],
        compiler_params=_cparams(1),
    )(parts, *ws, *ms, *vs)
    return [res[t * nv:(t + 1) * nv] for t in range(4)], res[4 * nv][0, 0]


def _sum_peers_adamw(me, g, recv, kind, w, m, v, name, after):
    n, R, C = recv.shape
    tr = _row_tile(R)
    nr = R // tr
    if kind == "col":
        g_spec = pl.BlockSpec((tr, C), lambda r, me_ref: (r, me_ref[0]))
    else:
        g_spec = pl.BlockSpec((tr, C), lambda r, me_ref: (me_ref[0] * nr + r, 0))

    def body(me_ref, own_ref, r_ref, w_ref, m_ref, v_ref, after_ref, g_ref, d_ref, m_out, v_out):
        acc = None
        for k in range(n):
            term = jnp.where(me_ref[0] == k, own_ref[...], r_ref[k].astype(F32))
            acc = term if acc is None else acc + term
        g_ref[...] = acc
        d_ref[...], m_out[...], v_out[...] = _adamw_math(w_ref[...], acc, m_ref[...], v_ref[...])

    blk = pl.BlockSpec((tr, C), lambda r, me_ref: (r, 0))
    return pl.pallas_call(
        body, name=name,
        grid_spec=pltpu.PrefetchScalarGridSpec(
            num_scalar_prefetch=1, grid=(nr,),
            in_specs=[g_spec, pl.BlockSpec((n, tr, C), lambda r, me_ref: (0, r, 0)), blk, blk, blk, ANY], out_specs=[blk] * 4),
        out_shape=[jax.ShapeDtypeStruct((R, C), F32)] * 4, compiler_params=_cparams(1),
    )(me, g, recv, w, m, v, after)


def _ada_cols(me, c_all, w_ada16, b_ada):
    nd, D = c_all.shape
    W = w_ada16.shape[1]

    def body(me_ref, c_ref, w_ref, b_ref, o_ref):
        cv = c_ref[...]
        r = _dot((cv * _sigmoid(cv)).astype(BF16), w_ref[...]) + b_ref[...]
        for b in range(nd):
            o_ref[b] = r[b:b + 1, :]

    return pl.pallas_call(
        body, name="ada_cols",
        grid_spec=pltpu.PrefetchScalarGridSpec(
            num_scalar_prefetch=1, grid=(1,),
            in_specs=[_full((nd, D)), _full((D, W)), pl.BlockSpec((1, W), lambda i, me_ref: (0, me_ref[0]))],
            out_specs=_full((nd, 1, W))),
        out_shape=jax.ShapeDtypeStruct((nd, 1, W), F32), compiler_params=_cparams(1),
    )(me, c_all, w_ada16, b_ada)


def _rows_to_owners(part, name):
    def body(p_ref, o_ref, send_sems, recv_sems, local_sem):
        x, y, c = _mesh_pos()
        me = 4 * x + 2 * y + c
        mine = pltpu.make_async_copy(p_ref.at[me], o_ref.at[me], local_sem.at[0])
        sends, arrivals = _peer_copies(lambda a, p: p_ref.at[p], [o_ref], send_sems, recv_sems)
        mine.start()
        for cp in sends:
            cp.start()
        for cp in arrivals:
            cp.wait_recv()
        for cp in sends:
            cp.wait_send()
        mine.wait()

    return pl.pallas_call(
        body, name=name, in_specs=[ANY], out_specs=ANY, out_shape=jax.ShapeDtypeStruct(part.shape, part.dtype),
        scratch_shapes=_peer_sems(1) + [pltpu.SemaphoreType.DMA((1,))],
    )(part)


def _ada_mod(me, c_all, w_ada16, b_ada):
    part = _ada_cols(me, c_all, w_ada16, b_ada)
    return _rows_to_owners(part, "scatter_mod").reshape(1, N_DEV * w_ada16.shape[1])


def _ada_bwd(me, small_g, D, W, c_block):
    def body(me_ref, c_ref, dm_ref, o_ref):
        cv = c_ref[...]
        ca = (cv * _sigmoid(cv)).astype(BF16).astype(F32)
        dm = dm_ref[...].astype(BF16).astype(F32)
        o_ref[...] = lax.dot_general(ca, dm, (((0,), (0,)), ((), ())), precision=lax.Precision.HIGHEST,
                                     preferred_element_type=F32)

    return pl.pallas_call(
        body, name="ada_bwd",
        grid_spec=pltpu.PrefetchScalarGridSpec(
            num_scalar_prefetch=1, grid=(1,),
            in_specs=[pl.BlockSpec((N_DEV, D), lambda i, me_ref: (0, c_block)),
                      pl.BlockSpec((N_DEV, W), lambda i, me_ref: (0, me_ref[0]))],
            out_specs=pl.BlockSpec((D, W), lambda i, me_ref: (0, 0))),
        out_shape=jax.ShapeDtypeStruct((D, W), F32), compiler_params=_cparams(1),
    )(me, small_g, small_g)


def _fwd_in(x, mod, w_in_g, b_in, shards):
    S, D = x.shape
    nb, _, wb = w_in_g.shape
    N = nb * wb
    TM = TOKEN_TILE
    n_tiles = S // TM
    ns = len(shards)

    def body(*refs):
        x_ref, mod_ref, w_hbm, b_ref = refs[:4]
        ins = refs[4:4 + ns]
        z_ref, h_ref, zg_ref = refs[4 + ns:7 + ns]
        outs = refs[7 + ns:7 + 2 * ns]
        w_v, sems, send_sems, recv_sems, local_sems = refs[7 + 2 * ns:]
        i = pl.program_id(0)
        x_, y_, c_ = _mesh_pos()
        me = 4 * x_ + 2 * y_ + c_
        mine = [pltpu.make_async_copy(ins[a], outs[a].at[me], local_sems.at[a]) for a in range(ns)]
        sends, arrivals = _peer_copies(lambda a, p: ins[a], outs, send_sems, recv_sems)

        @pl.when(i == 0)
        def _():
            for cp in sends + mine:
                cp.start()
            _load_cols(w_hbm, w_v, sems)

        xhat, _ = _ln(x_ref[...])
        hb = (xhat * (1.0 + mod_ref[:, D:2 * D]) + mod_ref[:, 0:D]).astype(BF16)
        h_ref[...] = hb
        z_ref[...] = _dot(hb, w_v[...]) + b_ref[...]
        zg_ref[:, 0:D] = z_ref[:, 2 * D:3 * D].astype(BF16)
        zg_ref[:, D:3 * D] = z_ref[:, 5 * D:7 * D].astype(BF16)

        @pl.when(i == n_tiles - 1)
        def _():
            for cp in arrivals:
                cp.wait_recv()
            for cp in sends:
                cp.wait_send()
            for cp in mine:
                cp.wait()

    res = pl.pallas_call(
        body, name="fwd_in", grid=(n_tiles,),
        in_specs=[pl.BlockSpec((TM, D), lambda i: (i, 0)), _full(mod.shape), ANY, _full((1, N))] + [ANY] * ns,
        out_specs=[pl.BlockSpec((TM, N), lambda i: (i, 0)), pl.BlockSpec((TM, D), lambda i: (i, 0)),
                   pl.BlockSpec((TM, 3 * D), lambda i: (i, 0))] + [ANY] * ns,
        out_shape=[jax.ShapeDtypeStruct((S, N), F32), jax.ShapeDtypeStruct((S, D), BF16), jax.ShapeDtypeStruct((S, 3 * D), BF16)]
        + [jax.ShapeDtypeStruct((N_DEV,) + a.shape, a.dtype) for a in shards],
        scratch_shapes=[pltpu.VMEM((D, N), BF16), pltpu.SemaphoreType.DMA((nb,))] + _peer_sems(ns) + [pltpu.SemaphoreType.DMA((ns,))],
        compiler_params=_cparams(1),
    )(x, mod, w_in_g, b_in, *shards)
    return res[0], res[1], res[2], list(res[3:])


def _rows_from(win, o, R):
    if o % 8 == 0:
        return win[o:o + R, :]
    return pltpu.roll(win, win.shape[0] - o, axis=0)[0:R, :]


def _causal_taps(buf, w_ref, o_ref, K, halo, TM, D):
    R, C = CONV_ROWS, CONV_LANES
    base = halo - (K - 1)

    for r0 in range(0, TM, R):
        for c0 in range(0, D, C):
            win = buf[r0:r0 + R + halo, c0:c0 + C]
            acc = jnp.zeros((R, C), F32)
            for k in range(K):
                acc = acc + w_ref[k:k + 1, c0:c0 + C] * _rows_from(win, base + k, R)
            o_ref[r0:r0 + R, c0:c0 + C] = acc


def _fwd_mix(z, x, mod, wa, ba, lag, lab, w_a_out, b_a_out, wb, w_b_out, w_o, b_o):
    S, D = x.shape
    N = z.shape[1]
    TM = TOKEN_TILE
    KA, KB = wa.shape[0], wb.shape[0]

    def body(z_ref, x_ref, mod_ref, wa_ref, ba_ref, lag_ref, lab_ref, wao_hbm, bao_ref, wb_ref, wbo_hbm, wo_hbm, bo_ref,
             u1_ref, ya_ref, yb_ref, q_ref, out_ref, r1_ref, u2_ref, v_ref, mg_ref,
             ubuf, pbuf, qbuf, wao, wbo, wo, sems):
        i = pl.program_id(0)

        @pl.when(i == 0)
        def _():
            _load_whole([(wao_hbm, wao), (wbo_hbm, wbo), (wo_hbm, wo)], sems)
            ubuf[0:HALO_A, :] = jnp.zeros((HALO_A, D), F32)
            pbuf[0:HALO_B, :] = jnp.zeros((HALO_B, D), F32)

        ubuf[HALO_A:HALO_A + TM, :] = z_ref[:, 0:D] * _sigmoid(z_ref[:, D:2 * D])
        _causal_taps(ubuf, wa_ref, u1_ref, KA, HALO_A, TM, D)
        ubuf[0:HALO_A, :] = ubuf[TM:TM + HALO_A, :]
        u1 = u1_ref[...] + ba_ref[...]
        u1_ref[...] = u1
        xa, _ = _ln(u1)
        l = xa * lag_ref[...] + lab_ref[...]
        u2 = (l * _sigmoid(l)).astype(BF16)
        u2_ref[...] = u2
        ya = _dot(u2, wao[...]) + bao_ref[...]
        ya_ref[...] = ya.astype(BF16)
        pbuf[HALO_B:HALO_B + TM, :] = z_ref[:, 3 * D:4 * D] * z_ref[:, 4 * D:5 * D]
        _causal_taps(pbuf, wb_ref, qbuf, KB, HALO_B, TM, D)
        pbuf[0:HALO_B, :] = pbuf[TM:TM + HALO_B, :]
        q_ref[...] = qbuf[...].astype(BF16)
        v = (z_ref[:, 2 * D:3 * D] * qbuf[...]).astype(BF16)
        v_ref[...] = v
        yb = _dot(v, wbo[...])
        yb_ref[...] = yb.astype(BF16)
        mg = (_sigmoid(z_ref[:, 5 * D:6 * D]) * ya + _sigmoid(z_ref[:, 6 * D:7 * D]) * yb).astype(BF16)
        mg_ref[...] = mg
        out = _dot(mg, wo[...]) + bo_ref[...]
        out_ref[...] = out.astype(BF16)
        r1_ref[...] = ALPHA * x_ref[...] + (1.0 + mod_ref[:, 2 * D:3 * D]) * out

    tile = pl.BlockSpec((TM, D), lambda i: (i, 0))
    vec = _full((1, D))
    return pl.pallas_call(
        body, name="fwd_mix", grid=(S // TM,),
        in_specs=[pl.BlockSpec((TM, N), lambda i: (i, 0)), tile, _full(mod.shape), _full((KA, D)), vec, vec, vec, ANY, vec,
                  _full((KB, D)), ANY, ANY, vec],
        out_specs=[tile] * 9,
        out_shape=[jax.ShapeDtypeStruct((S, D), dt) for dt in (F32, BF16, BF16, BF16, BF16, F32, BF16, BF16, BF16)],
        scratch_shapes=[pltpu.VMEM((TM + HALO_A, D), F32), pltpu.VMEM((TM + HALO_B, D), F32), pltpu.VMEM((TM, D), F32),
                        pltpu.VMEM((D, D), BF16), pltpu.VMEM((D, D), BF16), pltpu.VMEM((D, D), BF16),
                        pltpu.SemaphoreType.DMA((3,))],
        compiler_params=_cparams(1),
    )(z, x, mod, wa, ba, lag, lab, w_a_out, b_a_out, wb, w_b_out, w_o, b_o)


def _ffn(r1, tgt, mod, ln1_g, ln1_b, w_up_g, b_up, w_down, b_down, ln2_g, ln2_b):
    S, D = r1.shape
    nb, _, wb = w_up_g.shape
    FF = nb * wb
    TM = TOKEN_TILE
    CF = FFN_CHUNK
    n_tiles = S // TM

    def body(r1_ref, t_ref, mod_ref, g1_ref, b1_ref, wup_hbm, bup_ref, wdn_hbm, bdn_ref, g2_ref, b2_ref,
             dx1_ref, h2_ref, f_ref, dhu_ref, do2_ref, acc_ref, accup_ref,
             wup, wdn, relu_buf, sems, sem2):
        i = pl.program_id(0)

        @pl.when(i == 0)
        def _():
            _load_cols(wup_hbm, wup, sems)
            _load_whole([(wdn_hbm, wdn)], sem2)
            acc_ref[...] = jnp.zeros(acc_ref.shape, F32)
            accup_ref[...] = jnp.zeros(accup_ref.shape, F32)

        scale2, shift2, gate2 = mod_ref[:, 4 * D:5 * D], mod_ref[:, 3 * D:4 * D], mod_ref[:, 5 * D:6 * D]
        xhat1, _ = _ln(r1_ref[...])
        x1 = xhat1 * g1_ref[...] + b1_ref[...]
        xh0, rstd0 = _ln(x1)
        h2 = (xh0 * (1.0 + scale2) + shift2).astype(BF16)
        h2_ref[...] = h2
        out2 = jnp.zeros((TM, D), F32) + bdn_ref[...]
        for c0 in range(0, FF, CF):
            hu = _dot(h2, wup[:, c0:c0 + CF]) + bup_ref[:, c0:c0 + CF]
            rl = jnp.maximum(hu, 0.0)
            relu_buf[:, c0:c0 + CF] = rl
            fb = (rl * rl).astype(BF16)
            f_ref[:, c0:c0 + CF] = fb
            out2 = out2 + _dot(fb, wdn[c0:c0 + CF, :])
        r2 = ALPHA * x1 + (1.0 + gate2) * out2
        xh2, rstd2 = _ln(r2)
        e = xh2 * g2_ref[...] + b2_ref[...] - t_ref[...]
        acc_ref[6:7, :] += _colsum(e * e)
        dy = e * (1.0 / D)
        acc_ref[1:2, :] += _colsum(dy * xh2)
        acc_ref[2:3, :] += _colsum(dy)
        dr2 = _ln_bwd(dy * g2_ref[...], xh2, rstd2)
        acc_ref[3:4, :] += _colsum(dr2 * out2)
        do2 = (1.0 + gate2) * dr2
        acc_ref[0:1, :] += _colsum(do2)
        do2b = do2.astype(BF16)
        do2_ref[...] = do2b
        dh2 = jnp.zeros((TM, D), F32)
        for c0 in range(0, FF, CF):
            dhu = _dot_nt(do2b, wdn[c0:c0 + CF, :]) * (2.0 * relu_buf[:, c0:c0 + CF])
            accup_ref[:, c0:c0 + CF] += _colsum(dhu)
            dhub = dhu.astype(BF16)
            dhu_ref[:, c0:c0 + CF] = dhub
            dh2 = dh2 + _dot_nt(dhub, wup[:, c0:c0 + CF])
        acc_ref[4:5, :] += _colsum(dh2 * xh0)
        acc_ref[5:6, :] += _colsum(dh2)
        dx1_ref[...] = (ALPHA * dr2 + _ln_bwd(dh2 * (1.0 + scale2), xh0, rstd0)).astype(BF16)

        @pl.when(i == n_tiles - 1)
        def _():
            tot = jnp.sum(acc_ref[6:7, :], axis=-1, keepdims=True) * (0.5 / D)
            acc_ref[6:7, :] = jnp.broadcast_to(tot, (1, D))

    tile = pl.BlockSpec((TM, D), lambda i: (i, 0))
    wide = pl.BlockSpec((TM, FF), lambda i: (i, 0))
    vec = _full((1, D))
    return pl.pallas_call(
        body, name="ffn", grid=(n_tiles,),
        in_specs=[tile, tile, _full(mod.shape), vec, vec, ANY, _full((1, FF)), ANY, vec, vec, vec],
        out_specs=[tile, tile, wide, wide, tile, _full((8, D)), _full((1, FF))],
        out_shape=[jax.ShapeDtypeStruct((S, D), BF16), jax.ShapeDtypeStruct((S, D), BF16), jax.ShapeDtypeStruct((S, FF), BF16),
                   jax.ShapeDtypeStruct((S, FF), BF16), jax.ShapeDtypeStruct((S, D), BF16),
                   jax.ShapeDtypeStruct((8, D), F32), jax.ShapeDtypeStruct((1, FF), F32)],
        scratch_shapes=[pltpu.VMEM((D, FF), BF16), pltpu.VMEM((FF, D), BF16), pltpu.VMEM((TM, FF), F32),
                        pltpu.SemaphoreType.DMA((nb,)), pltpu.SemaphoreType.DMA((1,))],
        compiler_params=_cparams(1),
    )(r1, tgt, mod, ln1_g, ln1_b, w_up_g, b_up, w_down, b_down, ln2_g, ln2_b)


def _dw(a, b, name, msplit=1, nsplit=1, wire=False, exchange=None):
    S, M = a.shape
    N = b.shape[1]
    TK = min(DW_TILE, S)
    mb, nbk = M // msplit, N // nsplit
    nk = S // TK
    srcs, kinds = exchange if exchange else ((), ())
    ns = len(srcs)

    def body(*refs):
        a_ref, b_ref = refs[:2]
        ins = refs[2:2 + ns]
        o_ref = refs[2 + ns]
        o16_ref = refs[3 + ns] if wire else None
        rest = refs[3 + ns + (1 if wire else 0):]
        k = pl.program_id(2)
        if ns:
            outs, (send_sems, recv_sems) = rest[:ns], rest[ns:]
            sends, arrivals = _peer_copies(lambda q, p: _shard_of(ins[q], kinds[q], p), outs, send_sems, recv_sems)
            first = (pl.program_id(0) == 0) & (pl.program_id(1) == 0) & (k == 0)
            last = (pl.program_id(0) == msplit - 1) & (pl.program_id(1) == nsplit - 1) & (k == nk - 1)

            @pl.when(first)
            def _():
                for cp in sends:
                    cp.start()

        @pl.when(k == 0)
        def _():
            o_ref[...] = jnp.zeros(o_ref.shape, F32)

        o_ref[...] += _dot_tn(a_ref[...], b_ref[...])

        if wire:
            @pl.when(k == nk - 1)
            def _():
                o16_ref[...] = o_ref[...].astype(BF16)

        if ns:
            @pl.when(last)
            def _():
                for cp in arrivals:
                    cp.wait_recv()
                for cp in sends:
                    cp.wait_send()

    oblk = pl.BlockSpec((mb, nbk), lambda i, j, k: (i, j))
    res = pl.pallas_call(
        body, name=name, grid=(msplit, nsplit, nk),
        in_specs=[pl.BlockSpec((TK, mb), lambda i, j, k: (k, i)), pl.BlockSpec((TK, nbk), lambda i, j, k: (k, j))] + [ANY] * ns,
        out_specs=[oblk] + ([oblk] if wire else []) + [ANY] * ns,
        out_shape=[jax.ShapeDtypeStruct((M, N), F32)] + ([jax.ShapeDtypeStruct((M, N), BF16)] if wire else [])
        + [jax.ShapeDtypeStruct((N_DEV,) + _shard_shape(g.shape, kd), g.dtype) for g, kd in zip(srcs, kinds)],
        scratch_shapes=_peer_sems(ns) if ns else [],
        compiler_params=_cparams(3),
    )(a, b, *srcs)
    return res if (wire or ns) else res[0]


def _bwd_mix1(dx1, r1, out, ya, yb, q, u1, zg, mod, ln1_g, lag, lab, w_o, w_a_out, w_b_out):
    S, D = dx1.shape
    TM = TOKEN_TILE

    def body(dx1_ref, r1_ref, out_ref, ya_ref, yb_ref, q_ref, u1_ref, zg_ref, mod_ref, g1_ref, lag_ref, lab_ref,
             wo_hbm, wao_hbm, wbo_hbm,
             dxp_ref, du1_ref, dq_ref, dzc_ref, dout_ref, dya_ref, dyb_ref, acc_ref,
             wo, wao, wbo, sems):
        @pl.when(pl.program_id(0) == 0)
        def _():
            _load_whole([(wo_hbm, wo), (wao_hbm, wao), (wbo_hbm, wbo)], sems)
            acc_ref[...] = jnp.zeros(acc_ref.shape, F32)

        dx1v = dx1_ref[...].astype(F32)
        xhat1, rstd1 = _ln(r1_ref[...])
        acc_ref[0:1, :] += _colsum(dx1v * xhat1)
        acc_ref[1:2, :] += _colsum(dx1v)
        dr1 = _ln_bwd(dx1v * g1_ref[...], xhat1, rstd1)
        dxp_ref[...] = (ALPHA * dr1).astype(BF16)
        acc_ref[2:3, :] += _colsum(dr1 * out_ref[...].astype(F32))
        dout = (1.0 + mod_ref[:, 2 * D:3 * D]) * dr1
        acc_ref[3:4, :] += _colsum(dout)
        doutb = dout.astype(BF16)
        dout_ref[...] = doutb
        dmg = _dot_nt(doutb, wo[...])
        sga, sgb = _sigmoid(zg_ref[:, D:2 * D].astype(F32)), _sigmoid(zg_ref[:, 2 * D:3 * D].astype(F32))
        dga = dmg * ya_ref[...].astype(F32) * sga * (1.0 - sga)
        dgb = dmg * yb_ref[...].astype(F32) * sgb * (1.0 - sgb)
        acc_ref[9:10, :] += _colsum(dga)
        acc_ref[10:11, :] += _colsum(dgb)
        dzc_ref[:, D:2 * D] = dga.astype(BF16)
        dzc_ref[:, 2 * D:3 * D] = dgb.astype(BF16)
        dya = dmg * sga
        acc_ref[4:5, :] += _colsum(dya)
        dyab = dya.astype(BF16)
        dya_ref[...] = dyab
        dybb = (dmg * sgb).astype(BF16)
        dyb_ref[...] = dybb
        du2 = _dot_nt(dyab, wao[...])
        xa, rstda = _ln(u1_ref[...])
        l = xa * lag_ref[...] + lab_ref[...]
        sl = _sigmoid(l)
        dl = du2 * (sl * (1.0 + l * (1.0 - sl)))
        acc_ref[5:6, :] += _colsum(dl * xa)
        acc_ref[6:7, :] += _colsum(dl)
        du1 = _ln_bwd(dl * lag_ref[...], xa, rstda)
        acc_ref[7:8, :] += _colsum(du1)
        du1_ref[...] = du1.astype(BF16)
        dv = _dot_nt(dybb, wbo[...])
        dgbk = dv * q_ref[...].astype(F32)
        acc_ref[8:9, :] += _colsum(dgbk)
        dzc_ref[:, 0:D] = dgbk.astype(BF16)
        dq_ref[...] = (dv * zg_ref[:, 0:D].astype(F32)).astype(BF16)

    tile = pl.BlockSpec((TM, D), lambda i: (i, 0))
    vec = _full((1, D))
    return pl.pallas_call(
        body, name="bwd_mix1", grid=(S // TM,),
        in_specs=[tile] * 7 + [pl.BlockSpec((TM, 3 * D), lambda i: (i, 0)), _full(mod.shape), vec, vec, vec, ANY, ANY, ANY],
        out_specs=[tile, tile, tile, pl.BlockSpec((TM, 3 * D), lambda i: (i, 0)), tile, tile, tile, _full((16, D))],
        out_shape=[jax.ShapeDtypeStruct((S, D), BF16)] * 3 + [jax.ShapeDtypeStruct((S, 3 * D), BF16)]
        + [jax.ShapeDtypeStruct((S, D), BF16)] * 3 + [jax.ShapeDtypeStruct((16, D), F32)],
        scratch_shapes=[pltpu.VMEM((D, D), BF16)] * 3 + [pltpu.SemaphoreType.DMA((3,))],
        compiler_params=_cparams(1),
    )(dx1, r1, out, ya, yb, q, u1, zg, mod, ln1_g, lag, lab, w_o, w_a_out, w_b_out)


def _anticausal_taps(dbuf, u_ref, w_ref, o_ref, dwacc, K, halo, TM, D):
    R, C = CONV_ROWS, CONV_LANES

    for r0 in range(0, TM, R):
        for c0 in range(0, D, C):
            win = dbuf[r0:r0 + R + halo, c0:c0 + C]
            uc = u_ref[r0:r0 + R, c0:c0 + C]
            acc = jnp.zeros((R, C), F32)
            for j in range(K):
                sh = _rows_from(win, j, R)
                k = K - 1 - j
                acc = acc + w_ref[k:k + 1, c0:c0 + C] * sh
                pr = uc * sh
                part = pr[0:8, :]
                for s in range(8, R, 8):
                    part = part + pr[s:s + 8, :]
                dwacc[k, :, c0:c0 + C] += part
            o_ref[r0:r0 + R, c0:c0 + C] = acc


def _bwd_mix2(du1, dq, z, dzc, x, dxp, mod, wa, wb, w_in_g):
    S, D = x.shape
    N = z.shape[1]
    nb, _, wbk = w_in_g.shape
    TM = TOKEN_TILE
    KA, KB = wa.shape[0], wb.shape[0]
    n_tiles = S // TM

    def body(du1_ref, dq_ref, zav_ref, zc_ref, zx_ref, dzc_ref, x_ref, dxp_ref, mod_ref, wa_ref, wb_ref, win_hbm,
             dz_ref, gx_ref, dwa_ref, dwb_ref, acc_ref, dbin_ref,
             dbuf, qbuf, ubuf, pbuf, obuf, dwa_acc, dwb_acc, w_v, sems):
        i = pl.program_id(0)

        @pl.when(i == 0)
        def _():
            _load_cols(win_hbm, w_v, sems)
            dbuf[TM:TM + HALO_A, :] = jnp.zeros((HALO_A, D), F32)
            qbuf[TM:TM + HALO_B, :] = jnp.zeros((HALO_B, D), F32)
            dwa_acc[...] = jnp.zeros(dwa_acc.shape, F32)
            dwb_acc[...] = jnp.zeros(dwb_acc.shape, F32)
            acc_ref[...] = jnp.zeros(acc_ref.shape, F32)
            dbin_ref[...] = jnp.zeros(dbin_ref.shape, F32)

        a_val = zav_ref[:, 0:D]
        sa = _sigmoid(zav_ref[:, D:2 * D])
        ubuf[...] = a_val * sa
        dbuf[0:TM, :] = du1_ref[...].astype(F32)
        _anticausal_taps(dbuf, ubuf, wa_ref, obuf, dwa_acc, KA, HALO_A, TM, D)
        dbuf[TM:TM + HALO_A, :] = dbuf[0:HALO_A, :]
        du0 = obuf[...]
        dav = du0 * sa
        dag = du0 * a_val * sa * (1.0 - sa)
        dbin_ref[:, 0:D] += _colsum(dav)
        dbin_ref[:, D:2 * D] += _colsum(dag)
        dz_ref[:, 0:D] = dav.astype(BF16)
        dz_ref[:, D:2 * D] = dag.astype(BF16)
        pbuf[...] = zc_ref[...] * zx_ref[...]
        qbuf[0:TM, :] = dq_ref[...].astype(F32)
        _anticausal_taps(qbuf, pbuf, wb_ref, obuf, dwb_acc, KB, HALO_B, TM, D)
        qbuf[TM:TM + HALO_B, :] = qbuf[0:HALO_B, :]
        dp = obuf[...]
        dgc = dp * zx_ref[...]
        dgx = dp * zc_ref[...]
        dbin_ref[:, 3 * D:4 * D] += _colsum(dgc)
        dbin_ref[:, 4 * D:5 * D] += _colsum(dgx)
        dz_ref[:, 3 * D:4 * D] = dgc.astype(BF16)
        dz_ref[:, 4 * D:5 * D] = dgx.astype(BF16)
        dz_ref[:, 2 * D:3 * D] = dzc_ref[:, 0:D]
        dz_ref[:, 5 * D:7 * D] = dzc_ref[:, D:3 * D]
        dh = _dot_nt(dz_ref[...], w_v[...])
        xhat, rstd = _ln(x_ref[...])
        acc_ref[0:1, :] += _colsum(dh * xhat)
        acc_ref[1:2, :] += _colsum(dh)
        gx_ref[...] = dxp_ref[...].astype(F32) + _ln_bwd(dh * (1.0 + mod_ref[:, D:2 * D]), xhat, rstd)

        @pl.when(i == n_tiles - 1)
        def _():
            for k in range(KA):
                dwa_ref[k:k + 1, :] = jnp.sum(dwa_acc[k], axis=0, keepdims=True)
            for k in range(KB):
                dwb_ref[k:k + 1, :] = jnp.sum(dwb_acc[k], axis=0, keepdims=True)

    rev = lambda i: n_tiles - 1 - i
    tile = pl.BlockSpec((TM, D), lambda i: (rev(i), 0))
    zcol = lambda k: pl.BlockSpec((TM, D), lambda i: (rev(i), k))
    return pl.pallas_call(
        body, name="bwd_mix2", grid=(n_tiles,),
        in_specs=[tile, tile, pl.BlockSpec((TM, 2 * D), lambda i: (rev(i), 0)), zcol(3), zcol(4),
                  pl.BlockSpec((TM, 3 * D), lambda i: (rev(i), 0)), tile, tile, _full(mod.shape), _full((KA, D)), _full((KB, D)), ANY],
        out_specs=[pl.BlockSpec((TM, N), lambda i: (rev(i), 0)), tile, _full((KA, D)), _full((KB, D)), _full((8, D)), _full((1, N))],
        out_shape=[jax.ShapeDtypeStruct((S, N), BF16), jax.ShapeDtypeStruct((S, D), F32), jax.ShapeDtypeStruct((KA, D), F32),
                   jax.ShapeDtypeStruct((KB, D), F32), jax.ShapeDtypeStruct((8, D), F32), jax.ShapeDtypeStruct((1, N), F32)],
        scratch_shapes=[pltpu.VMEM((TM + HALO_A, D), F32), pltpu.VMEM((TM + HALO_B, D), F32), pltpu.VMEM((TM, D), F32),
                        pltpu.VMEM((TM, D), F32), pltpu.VMEM((TM, D), F32), pltpu.VMEM((KA, 8, D), F32), pltpu.VMEM((KB, 8, D), F32),
                        pltpu.VMEM((D, N), BF16), pltpu.SemaphoreType.DMA((nb,))],
        compiler_params=_cparams(1),
    )(du1, dq, z, z, z, dzc, x, dxp, mod, wa, wb, w_in_g)


PEER_REDUCED = ("w_up", "w_down", "w_o", "w_a_out", "w_b_out", "conv_a_w", "conv_b_w")


def _local_grads(me, x, c, c_all, tgt, w_ada16, w_in_g, shards, vecs):
    D = x.shape[1]
    rows = lambda a: a.reshape(a.shape[0] * a.shape[1], a.shape[2])
    cols = lambda a: jnp.transpose(a, (1, 0, 2)).reshape(a.shape[1], a.shape[0] * a.shape[2])
    mod = _ada_mod(me, c_all, w_ada16, vecs["b_ada"])
    z, h1, zg, gathered = _fwd_in(x, mod, w_in_g, vecs["b_in"], shards)
    w_a_out, w_b_out, w_o, w_up_g, w_down = rows(gathered[0]), rows(gathered[1]), rows(gathered[2]), gathered[3], rows(gathered[4])
    wa, wb = cols(gathered[5]), cols(gathered[6])
    u1, ya, yb, q, out, r1, u2, v, mg = _fwd_mix(
        z, x, mod, wa, vecs["conv_a_b"], vecs["ln_a_g"], vecs["ln_a_b"], w_a_out, vecs["b_a_out"], wb, w_b_out, w_o, vecs["b_o"])
    dx1, h2, f, dhu, do2, acc_f, acc_up = _ffn(
        r1, tgt, mod, vecs["ln1_g"], vecs["ln1_b"], w_up_g, vecs["b_up"], w_down, vecs["b_down"], vecs["ln2_g"], vecs["ln2_b"])
    g_up, g_up16 = _dw(h2, dhu, "dw_up", nsplit=2, wire=True)
    g_down, g_down16 = _dw(f, do2, "dw_down", msplit=2, wire=True)
    dxp, du1, dq, dzc, doutb, dyab, dybb, acc_1 = _bwd_mix1(
        dx1, r1, out, ya, yb, q, u1, zg, mod, vecs["ln1_g"], vecs["ln_a_g"], vecs["ln_a_b"], w_o, w_a_out, w_b_out)
    g_o, g_o16 = _dw(mg, doutb, "dw_o", wire=True)
    g_a_out, g_a_out16 = _dw(u2, dyab, "dw_a_out", wire=True)
    g_b_out, g_b_out16 = _dw(v, dybb, "dw_b_out", wire=True)
    dz, gx, g_wa, g_wb, acc_2, db_in = _bwd_mix2(du1, dq, z, dzc, x, dxp, mod, wa, wb, w_in_g)
    full = dict(zip(PEER_REDUCED, (g_up, g_down, g_o, g_a_out, g_b_out, g_wa, g_wb)))
    res = _dw(h1, dz, "dw_in", nsplit=4,
              exchange=((g_up16, g_down16, g_o16, g_a_out16, g_b_out16, g_wa, g_wb), [SHARD_KIND[n] for n in PEER_REDUCED]))
    g_in = res[0]
    peers = {n: (full[n], r) for n, r in zip(PEER_REDUCED, res[1:])}
    row = lambda acc, k: (acc, k, 0, D)
    pieces = [
        row(acc_2, 1), row(acc_2, 0), row(acc_1, 2), row(acc_f, 5), row(acc_f, 4), row(acc_f, 3),
        (db_in, 0, 0, 2 * D), row(acc_1, 8), (db_in, 0, 3 * D, 2 * D), row(acc_1, 9), row(acc_1, 10),
        row(acc_1, 7), row(acc_1, 5), row(acc_1, 6), row(acc_1, 4), row(acc_1, 3), row(acc_1, 0), row(acc_1, 1),
        (acc_up, 0, 0, acc_up.shape[1]), row(acc_f, 0), row(acc_f, 1), row(acc_f, 2), (c, 0, 0, D),
        (acc_f, 6, 0, LOSS_LANES)]
    return acc_f[6, 0], gx, g_in, peers, pieces


WEIGHTS = ("w_ada", "b_ada", "w_in", "b_in", "conv_a_w", "conv_a_b", "ln_a_g", "ln_a_b", "w_a_out", "b_a_out", "conv_b_w",
           "w_b_out", "w_o", "b_o", "ln1_g", "ln1_b", "w_up", "b_up", "w_down", "b_down", "ln2_g", "ln2_b")
VECTORS = ("b_ada", "b_in", "conv_a_b", "ln_a_g", "ln_a_b", "b_a_out", "b_o", "ln1_g", "ln1_b", "b_up", "b_down", "ln2_g", "ln2_b")
SHARD_KIND = {"w_in": "col", "w_a_out": "row", "w_b_out": "row", "w_o": "row", "w_up": "col", "w_down": "row",
              "conv_a_w": "col", "conv_b_w": "col"}


def kernel(x, c, w_ada, b_ada, w_in, b_in, conv_a_w, conv_a_b, ln_a_g, ln_a_b, w_a_out, b_a_out, conv_b_w, w_b_out, w_o, b_o, ln1_g, ln1_b, w_up, b_up, w_down, b_down, ln2_g, ln2_b, loss_target, m_w_ada, m_b_ada, m_w_in, m_b_in, m_conv_a_w, m_conv_a_b, m_ln_a_g, m_ln_a_b, m_w_a_out, m_b_a_out, m_conv_b_w, m_w_b_out, m_w_o, m_b_o, m_ln1_g, m_ln1_b, m_w_up, m_b_up, m_w_down, m_b_down, m_ln2_g, m_ln2_b, v_w_ada, v_b_ada, v_w_in, v_b_in, v_conv_a_w, v_conv_a_b, v_ln_a_g, v_ln_a_b, v_w_a_out, v_b_a_out, v_conv_b_w, v_w_b_out, v_w_o, v_b_o, v_ln1_g, v_ln1_b, v_w_up, v_b_up, v_w_down, v_b_down, v_ln2_g, v_ln2_b):
    args = dict(locals())
    w = {n: args[n][0] for n in WEIGHTS}
    m = {n: args["m_" + n][0] for n in WEIGHTS}
    v = {n: args["v_" + n][0] for n in WEIGHTS}
    w = {n: (a[None, :] if a.ndim == 1 else a) for n, a in w.items()}
    m = {n: (a[None, :] if a.ndim == 1 else a) for n, a in m.items()}
    v = {n: (a[None, :] if a.ndim == 1 else a) for n, a in v.items()}
    xs, tgt = x[0], loss_target[0]
    S, D = xs.shape
    me = (4 * lax.axis_index("x") + 2 * lax.axis_index("y") + lax.axis_index("c")).astype(jnp.int32).reshape(1)
    core = lax.axis_index("c").astype(jnp.int32).reshape(1)
    chip = (2 * lax.axis_index("x") + lax.axis_index("y")).astype(jnp.int32).reshape(1)

    bf = lambda n: w[n].astype(BF16)
    w_in_g, c_all = _all_gather([bf("w_in"), c], "gather_weights")
    shards = [bf("w_a_out"), bf("w_b_out"), bf("w_o"), bf("w_up"), bf("w_down"), w["conv_a_w"], w["conv_b_w"]]
    vecs = {n: w[n] for n in VECTORS}

    loss, gx, g_in, peers, pieces = _local_grads(me, xs, c, c_all.reshape(N_DEV, D), tgt, bf("w_ada"), w_in_g, shards, vecs)

    (from_sibling,) = _exchange_cores([g_in], [SHARD_KIND["w_in"]], "reduce_cores")
    own_in, wire_in = _add_own(core, g_in, from_sibling, SHARD_KIND["w_in"], "add_w_in")
    sems, wire_thru, landing, after = _chips_start(wire_in)

    res = {}
    for n, (g_full, recv) in peers.items():
        res[n] = _sum_peers_adamw(me, g_full, recv, SHARD_KIND[n], w[n], m[n], v[n], "adamw_" + n, after)
        after = res[n][0]
    small = _pack_rows(pieces, "pack_vectors", after)
    (small_g,) = _all_gather([small], "gather_vectors")
    small_g = small_g.reshape(N_DEV, small.shape[1])
    W = w["w_ada"].shape[1]
    n_vec = small.shape[1] - D - LOSS_LANES
    g_ada = _ada_bwd(me, small_g, D, W, n_vec // D)
    res["w_ada"] = _sum_adamw(g_ada[None], w["w_ada"], m["w_ada"], v["w_ada"], "adamw_w_ada")
    by_kind, loss = _adamw_vectors(small_g, [w[n] for n in VECTORS], [m[n] for n in VECTORS], [v[n] for n in VECTORS], n_vec + D)
    for i, n in enumerate(VECTORS):
        res[n] = tuple(by_kind[t][i] for t in range(4))
    from_chips = _chips_wait(sems, wire_thru, landing, res["w_ada"][0])
    res["w_in"] = _sum_chips_adamw(chip, own_in, from_chips, w["w_in"], m["w_in"], v["w_in"], "adamw_w_in")

    outs = [loss, gx[None]]
    for t in range(4):
        outs += [res[n][t].reshape(args[n].shape) for n in WEIGHTS]
    return tuple(outs)
```

```python
import functools

import jax
import jax.numpy as jnp
from jax import lax
from jax.experimental import pallas as pl
from jax.experimental.pallas import tpu as pltpu

F32 = jnp.float32
BF16 = jnp.bfloat16
MESH = pl.DeviceIdType.MESH

N_DEV = 8
LN_EPS = 1e-5
DEPTH = 1
ALPHA = (2.0 * DEPTH) ** 0.25
ADAM_LR, ADAM_B1, ADAM_B2, ADAM_EPS, ADAM_WD, ADAM_STEP = 0.001, 0.9, 0.999, 1e-08, 0.01, 10

VMEM_LIMIT = 60 * 1024 * 1024
TOKEN_TILE = 256
DW_TILE = 2048
HALO_A = 32
HALO_B = 8
CONV_ROWS, CONV_LANES = 256, 256
FFN_CHUNK = 1024
LOSS_LANES = 128


def _cparams(n_grid):
    return pltpu.CompilerParams(dimension_semantics=("arbitrary",) * n_grid, vmem_limit_bytes=VMEM_LIMIT)


def _full(shape):
    return pl.BlockSpec(shape, lambda *_: (0,) * len(shape))


ANY = pl.BlockSpec(memory_space=pl.ANY)


def _ln(x):
    mu = jnp.mean(x, axis=-1, keepdims=True)
    xc = x - mu
    var = jnp.mean(xc * xc, axis=-1, keepdims=True)
    rstd = lax.rsqrt(var + LN_EPS)
    return xc * rstd, rstd


def _ln_bwd(dxhat, xhat, rstd):
    m1 = jnp.mean(dxhat, axis=-1, keepdims=True)
    m2 = jnp.mean(dxhat * xhat, axis=-1, keepdims=True)
    return rstd * (dxhat - m1 - xhat * m2)


def _sigmoid(x):
    return 0.5 * jnp.tanh(0.5 * x) + 0.5


def _colsum(a):
    return jnp.sum(a, axis=0, keepdims=True)


def _dot(a, b):
    return jnp.dot(a, b, preferred_element_type=F32)


def _dot_nt(a, b):
    return lax.dot_general(a, b, (((1,), (1,)), ((), ())), preferred_element_type=F32)


def _dot_tn(a, b):
    return lax.dot_general(a, b, (((0,), (0,)), ((), ())), preferred_element_type=F32)


def _load_cols(src_hbm, dst_vmem, sems):
    nblk, _, w = src_hbm.shape
    cps = [pltpu.make_async_copy(src_hbm.at[j], dst_vmem.at[:, pl.ds(j * w, w)], sems.at[j]) for j in range(nblk)]
    for cp in cps:
        cp.start()
    for cp in cps:
        cp.wait()


def _load_whole(pairs, sems):
    cps = [pltpu.make_async_copy(s, d, sems.at[k]) for k, (s, d) in enumerate(pairs)]
    for cp in cps:
        cp.start()
    for cp in cps:
        cp.wait()


def _mesh_pos():
    return lax.axis_index("x"), lax.axis_index("y"), lax.axis_index("c")


def _all_gather(arrs, name):
    n = len(arrs)

    def body(*refs):
        ins, outs = refs[:n], refs[n:2 * n]
        send_sems, recv_sems, local_sems = refs[2 * n:]
        x, y, c = _mesh_pos()
        me, sibling = (x, y, c), (x, y, 1 - c)
        chips = [(1 - x, y), (x, 1 - y), (1 - x, 1 - y)]

        def slot(a, px, py, pc):
            return outs[a].at[4 * px + 2 * py + pc]

        def copy(a, k, block, to, src=None):
            return pltpu.make_async_remote_copy(
                src_ref=slot(a, *block) if src is None else src, dst_ref=slot(a, *block),
                send_sem=send_sems.at[a, k], recv_sem=recv_sems.at[a, k], device_id=to, device_id_type=MESH)

        mine = [pltpu.make_async_copy(ins[a], slot(a, *me), local_sems.at[a]) for a in range(n)]
        for cp in mine:
            cp.start()
        first = []
        for a in range(n):
            first.append(copy(a, 0, me, sibling, src=ins[a]))
            first += [copy(a, 1 + j, me, (*chip, c), src=ins[a]) for j, chip in enumerate(chips)]
        for cp in first:
            cp.start()
        passed = []
        for a in range(n):
            for j, chip in enumerate(chips):
                copy(a, 1 + j, (*chip, c), me).wait_recv()
                fwd = copy(a, 4 + j, (*chip, c), sibling)
                fwd.start()
                passed.append(fwd)
        for a in range(n):
            copy(a, 0, sibling, me).wait_recv()
            for j, chip in enumerate(chips):
                copy(a, 4 + j, (*chip, 1 - c), me).wait_recv()
        for cp in first + passed:
            cp.wait_send()
        for cp in mine:
            cp.wait()

    outs = pl.pallas_call(
        body, name=name,
        out_shape=[jax.ShapeDtypeStruct((N_DEV,) + a.shape, a.dtype) for a in arrs],
        in_specs=[ANY] * n, out_specs=[ANY] * n,
        scratch_shapes=[pltpu.SemaphoreType.DMA((n, 7)), pltpu.SemaphoreType.DMA((n, 7)), pltpu.SemaphoreType.DMA((n,))],
    )(*arrs)
    return list(outs)


def _peer_copies(src_of, dsts, send_sems, recv_sems):
    x, y, c = _mesh_pos()
    me = 4 * x + 2 * y + c
    sends, arrivals = [], []
    for a in range(len(dsts)):
        for k in range(1, N_DEV):
            px, py, pc = (1 - x if k & 4 else x), (1 - y if k & 2 else y), (1 - c if k & 1 else c)
            p = 4 * px + 2 * py + pc
            common = dict(send_sem=send_sems.at[a, k - 1], recv_sem=recv_sems.at[a, k - 1],
                          device_id=(px, py, pc), device_id_type=MESH)
            sends.append(pltpu.make_async_remote_copy(src_ref=src_of(a, p), dst_ref=dsts[a].at[me], **common))
            arrivals.append(pltpu.make_async_remote_copy(src_ref=dsts[a].at[p], dst_ref=dsts[a].at[p], **common))
    return sends, arrivals


def _peer_sems(n):
    return [pltpu.SemaphoreType.DMA((n, N_DEV - 1)), pltpu.SemaphoreType.DMA((n, N_DEV - 1))]


def _shard_of(ref, kind, j):
    if kind == "col":
        w = ref.shape[1] // N_DEV
        return ref.at[:, pl.ds(j * w, w)]
    h = ref.shape[0] // N_DEV
    return ref.at[pl.ds(j * h, h), :]


def _shard_shape(shape, kind):
    return (shape[0], shape[1] // N_DEV) if kind == "col" else (shape[0] // N_DEV, shape[1])


def _exchange_cores(grads, kinds, name):
    n = len(grads)

    def body(*refs):
        ins, outs = refs[:n], refs[n:2 * n]
        send_sems, recv_sems = refs[2 * n:]
        x, y, c = _mesh_pos()
        sibling = (x, y, 1 - c)
        sends = []
        for a in range(n):
            for s in range(4):
                sends.append(pltpu.make_async_remote_copy(
                    src_ref=_shard_of(ins[a], kinds[a], 2 * s + (1 - c)), dst_ref=outs[a].at[s],
                    send_sem=send_sems.at[a, s], recv_sem=recv_sems.at[a, s], device_id=sibling, device_id_type=MESH))
        for cp in sends:
            cp.start()
        for a in range(n):
            for s in range(4):
                pltpu.make_async_remote_copy(
                    src_ref=outs[a].at[s], dst_ref=outs[a].at[s],
                    send_sem=send_sems.at[a, s], recv_sem=recv_sems.at[a, s], device_id=sibling, device_id_type=MESH).wait_recv()
        for cp in sends:
            cp.wait_send()

    outs = pl.pallas_call(
        body, name=name,
        out_shape=[jax.ShapeDtypeStruct((4,) + _shard_shape(g.shape, k), F32) for g, k in zip(grads, kinds)],
        in_specs=[ANY] * n, out_specs=[ANY] * n,
        scratch_shapes=[pltpu.SemaphoreType.DMA((n, 4)), pltpu.SemaphoreType.DMA((n, 4))],
    )(*grads)
    return list(outs)


HBM = pl.BlockSpec(memory_space=pltpu.HBM)
SEM = pl.BlockSpec(memory_space=pltpu.SEMAPHORE)
DATAFLOW = pltpu.SideEffectType.DATAFLOW_SIDE_EFFECTING


def _chip_copies(src_ref, land_ref, send_sems, recv_sems):
    x, y, c = _mesh_pos()
    my_slot = 2 * x + y
    sends, arrivals = [], []
    for j, (px, py) in enumerate([(1 - x, y), (x, 1 - y), (1 - x, 1 - y)]):
        common = dict(send_sem=send_sems[j], recv_sem=recv_sems[j], device_id=(px, py, c), device_id_type=MESH)
        sends.append(pltpu.make_async_remote_copy(src_ref=src_ref.at[2 * px + py], dst_ref=land_ref.at[my_slot], **common))
        arrivals.append(pltpu.make_async_remote_copy(src_ref=src_ref.at[2 * px + py], dst_ref=land_ref.at[2 * px + py], **common))
    return sends, arrivals


def _chips_start(part):
    def body(src_ref, land_ref, s0, s1, s2, r0, r1, r2, src_thru, land_thru, token):
        sends, _ = _chip_copies(src_ref, land_ref, (s0, s1, s2), (r0, r1, r2))
        for cp in sends:
            cp.start()
        token[...] = jnp.zeros_like(token)

    res = pl.pallas_call(
        body, name="reduce_chips_start",
        out_shape=(pltpu.SemaphoreType.DMA(()),) * 6 + (pltpu.HBM(part.shape, part.dtype), pltpu.HBM(part.shape, part.dtype),
                                                        jax.ShapeDtypeStruct((8, 128), F32)),
        in_specs=(HBM, HBM), out_specs=(SEM,) * 6 + (HBM, HBM, pl.BlockSpec(memory_space=pltpu.VMEM)),
        input_output_aliases={0: 6, 1: 7}, compiler_params=pltpu.CompilerParams(has_side_effects=DATAFLOW),
    )(pltpu.with_memory_space_constraint(part, pltpu.HBM),
      pltpu.with_memory_space_constraint(lax.empty(part.shape, part.dtype), pltpu.HBM))
    return res[:6], res[6], res[7], res[8]


def _chips_wait(sems, src_thru, land_thru, after):
    def body(src_ref, land_ref, s0, s1, s2, r0, r1, r2, after_ref, src_dead, got_ref):
        sends, arrivals = _chip_copies(src_ref, land_ref, (s0, s1, s2), (r0, r1, r2))
        for cp in sends:
            cp.wait_send()
        for cp in arrivals:
            cp.wait_recv()

    return pl.pallas_call(
        body, name="reduce_chips_wait",
        out_shape=(pltpu.HBM(src_thru.shape, src_thru.dtype), pltpu.HBM(land_thru.shape, land_thru.dtype)),
        in_specs=(HBM, HBM) + (SEM,) * 6 + (ANY,), out_specs=(HBM, HBM), input_output_aliases={0: 0, 1: 1},
        compiler_params=pltpu.CompilerParams(has_side_effects=DATAFLOW),
    )(src_thru, land_thru, *sems, after)[1]


def _row_tile(rows):
    for t in (256, 128, 64, 32, 16, 8):
        if rows % t == 0:
            return t
    return rows


def _wire_dtype(rows):
    return BF16 if rows % 16 == 0 else F32


def _add_own(core, g, recv, kind, name):
    ns, R, C = recv.shape
    tr = _row_tile(R)
    nr = R // tr
    if kind == "col":
        g_spec = pl.BlockSpec((tr, C), lambda s, r, c_ref: (r, 2 * s + c_ref[0]))
    else:
        g_spec = pl.BlockSpec((tr, C), lambda s, r, c_ref: ((2 * s + c_ref[0]) * nr + r, 0))
    slab = pl.BlockSpec((None, tr, C), lambda s, r, c_ref: (s, r, 0))

    def body(c_ref, g_ref, r_ref, o32_ref, o16_ref):
        t = g_ref[...] + r_ref[...]
        o32_ref[...] = t
        o16_ref[...] = t.astype(o16_ref.dtype)

    return pl.pallas_call(
        body, name=name,
        grid_spec=pltpu.PrefetchScalarGridSpec(num_scalar_prefetch=1, grid=(ns, nr), in_specs=[g_spec, slab], out_specs=[slab, slab]),
        out_shape=[jax.ShapeDtypeStruct((ns, R, C), F32), jax.ShapeDtypeStruct((ns, R, C), _wire_dtype(R))],
        compiler_params=_cparams(2),
    )(core, g, recv)


def _adamw_math(w, g, m, v):
    m2 = ADAM_B1 * m + (1.0 - ADAM_B1) * g
    v2 = ADAM_B2 * v + (1.0 - ADAM_B2) * (g * g)
    m_hat = m2 / (1.0 - ADAM_B1 ** ADAM_STEP)
    v_hat = v2 / (1.0 - ADAM_B2 ** ADAM_STEP)
    delta = -ADAM_LR * (m_hat / (jnp.sqrt(v_hat) + ADAM_EPS) + ADAM_WD * w)
    return delta, m2, v2


def _sum_adamw(parts, w, m, v, name):
    n, R, C = parts.shape
    tr = _row_tile(R)

    def body(p_ref, w_ref, m_ref, v_ref, g_ref, d_ref, m_out, v_out):
        g = p_ref[0]
        for k in range(1, n):
            g = g + p_ref[k]
        g_ref[...] = g
        d_ref[...], m_out[...], v_out[...] = _adamw_math(w_ref[...], g, m_ref[...], v_ref[...])

    blk = pl.BlockSpec((tr, C), lambda r: (r, 0))
    return pl.pallas_call(
        body, name=name, grid=(R // tr,),
        in_specs=[pl.BlockSpec((n, tr, C), lambda r: (0, r, 0)), blk, blk, blk],
        out_specs=[blk] * 4, out_shape=[jax.ShapeDtypeStruct((R, C), F32)] * 4, compiler_params=_cparams(1),
    )(parts, w, m, v)


def _sum_chips_adamw(chip, own, recv, w, m, v, name):
    n, R, C = recv.shape
    tr = _row_tile(R)

    def body(chip_ref, own_ref, r_ref, w_ref, m_ref, v_ref, g_ref, d_ref, m_out, v_out):
        g = None
        for k in range(n):
            term = jnp.where(chip_ref[0] == k, own_ref[...], r_ref[k].astype(F32))
            g = term if g is None else g + term
        g_ref[...] = g
        d_ref[...], m_out[...], v_out[...] = _adamw_math(w_ref[...], g, m_ref[...], v_ref[...])

    blk = pl.BlockSpec((tr, C), lambda r, chip_ref: (r, 0))
    return pl.pallas_call(
        body, name=name,
        grid_spec=pltpu.PrefetchScalarGridSpec(
            num_scalar_prefetch=1, grid=(R // tr,),
            in_specs=[pl.BlockSpec((None, tr, C), lambda r, chip_ref: (chip_ref[0], r, 0)),
                      pl.BlockSpec((n, tr, C), lambda r, chip_ref: (0, r, 0)), blk, blk, blk],
            out_specs=[blk] * 4),
        out_shape=[jax.ShapeDtypeStruct((R, C), F32)] * 4, compiler_params=_cparams(1),
    )(chip, own, recv, w, m, v)


def _pack_rows(pieces, name, after):
    arrs = []
    for a, _, _, _ in pieces:
        if not any(a is b for b in arrs):
            arrs.append(a)
    which = [next(i for i, b in enumerate(arrs) if b is a) for a, _, _, _ in pieces]
    total = sum(n for _, _, _, n in pieces)

    def body(*refs):
        o_ref = refs[len(arrs) + 1]
        off = 0
        for (_, r, c0, n), i in zip(pieces, which):
            o_ref[:, off:off + n] = refs[i][r:r + 1, c0:c0 + n]
            off += n

    return pl.pallas_call(
        body, name=name, in_specs=[_full(a.shape) for a in arrs] + [ANY], out_specs=_full((1, total)),
        out_shape=jax.ShapeDtypeStruct((1, total), F32), grid=(1,), compiler_params=_cparams(1),
    )(*arrs, after)


def _adamw_vectors(parts, ws, ms, vs, loss_at):
    nv = len(ws)
    widths = [a.shape[1] for a in ws]

    def body(*refs):
        p_ref = refs[0]
        w_refs, m_refs, v_refs = refs[1:1 + nv], refs[1 + nv:1 + 2 * nv], refs[1 + 2 * nv:1 + 3 * nv]
        outs = refs[1 + 3 * nv:]

        def total(off, n):
            g = p_ref[0:1, off:off + n]
            for k in range(1, p_ref.shape[0]):
                g = g + p_ref[k:k + 1, off:off + n]
            return g

        off = 0
        for i, n in enumerate(widths):
            g = total(off, n)
            outs[i][...] = g
            outs[nv + i][...], outs[2 * nv + i][...], outs[3 * nv + i][...] = _adamw_math(w_refs[i][...], g, m_refs[i][...], v_refs[i][...])
            off += n
        outs[4 * nv][...] = total(loss_at, LOSS_LANES)

    vec_specs = [_full((1, n)) for n in widths]
    res = pl.pallas_call(
        body, name="adamw_vectors", grid=(1,),
        in_specs=[_full(parts.shape)] + vec_specs * 3, out_specs=vec_specs * 4 + [_full((1, LOSS_LANES))],
        out_shape=[jax.ShapeDtypeStruct((1, n), F32) for n in widths] * 4 + [jax.ShapeDtypeStruct((1, LOSS_LANES), F32)],
        compiler_params=_cparams(1),
    )(parts, *ws, *ms, *vs)
    return [res[t * nv:(t + 1) * nv] for t in range(4)], res[4 * nv][0, 0]


def _sum_peers_adamw(me, g, recv, kind, w, m, v, name, after):
    n, R, C = recv.shape
    tr = _row_tile(R)
    nr = R // tr
    if kind == "col":
        g_spec = pl.BlockSpec((tr, C), lambda r, me_ref: (r, me_ref[0]))
    else:
        g_spec = pl.BlockSpec((tr, C), lambda r, me_ref: (me_ref[0] * nr + r, 0))

    def body(me_ref, own_ref, r_ref, w_ref, m_ref, v_ref, after_ref, g_ref, d_ref, m_out, v_out):
        acc = None
        for k in range(n):
            term = jnp.where(me_ref[0] == k, own_ref[...], r_ref[k].astype(F32))
            acc = term if acc is None else acc + term
        g_ref[...] = acc
        d_ref[...], m_out[...], v_out[...] = _adamw_math(w_ref[...], acc, m_ref[...], v_ref[...])

    blk = pl.BlockSpec((tr, C), lambda r, me_ref: (r, 0))
    return pl.pallas_call(
        body, name=name,
        grid_spec=pltpu.PrefetchScalarGridSpec(
            num_scalar_prefetch=1, grid=(nr,),
            in_specs=[g_spec, pl.BlockSpec((n, tr, C), lambda r, me_ref: (0, r, 0)), blk, blk, blk, ANY], out_specs=[blk] * 4),
        out_shape=[jax.ShapeDtypeStruct((R, C), F32)] * 4, compiler_params=_cparams(1),
    )(me, g, recv, w, m, v, after)


def _ada_cols(me, c_all, w_ada16, b_ada):
    nd, D = c_all.shape
    W = w_ada16.shape[1]

    def body(me_ref, c_ref, w_ref, b_ref, o_ref):
        cv = c_ref[...]
        r = _dot((cv * _sigmoid(cv)).astype(BF16), w_ref[...]) + b_ref[...]
        for b in range(nd):
            o_ref[b] = r[b:b + 1, :]

    return pl.pallas_call(
        body, name="ada_cols",
        grid_spec=pltpu.PrefetchScalarGridSpec(
            num_scalar_prefetch=1, grid=(1,),
            in_specs=[_full((nd, D)), _full((D, W)), pl.BlockSpec((1, W), lambda i, me_ref: (0, me_ref[0]))],
            out_specs=_full((nd, 1, W))),
        out_shape=jax.ShapeDtypeStruct((nd, 1, W), F32), compiler_params=_cparams(1),
    )(me, c_all, w_ada16, b_ada)


def _rows_to_owners(part, name):
    def body(p_ref, o_ref, send_sems, recv_sems, local_sem):
        x, y, c = _mesh_pos()
        me = 4 * x + 2 * y + c
        mine = pltpu.make_async_copy(p_ref.at[me], o_ref.at[me], local_sem.at[0])
        sends, arrivals = _peer_copies(lambda a, p: p_ref.at[p], [o_ref], send_sems, recv_sems)
        mine.start()
        for cp in sends:
            cp.start()
        for cp in arrivals:
            cp.wait_recv()
        for cp in sends:
            cp.wait_send()
        mine.wait()

    return pl.pallas_call(
        body, name=name, in_specs=[ANY], out_specs=ANY, out_shape=jax.ShapeDtypeStruct(part.shape, part.dtype),
        scratch_shapes=_peer_sems(1) + [pltpu.SemaphoreType.DMA((1,))],
    )(part)


def _ada_mod(me, c_all, w_ada16, b_ada):
    part = _ada_cols(me, c_all, w_ada16, b_ada)
    return _rows_to_owners(part, "scatter_mod").reshape(1, N_DEV * w_ada16.shape[1])


def _ada_bwd(me, small_g, D, W, c_block):
    def body(me_ref, c_ref, dm_ref, o_ref):
        cv = c_ref[...]
        ca = (cv * _sigmoid(cv)).astype(BF16).astype(F32)
        dm = dm_ref[...].astype(BF16).astype(F32)
        o_ref[...] = lax.dot_general(ca, dm, (((0,), (0,)), ((), ())), precision=lax.Precision.HIGHEST,
                                     preferred_element_type=F32)

    return pl.pallas_call(
        body, name="ada_bwd",
        grid_spec=pltpu.PrefetchScalarGridSpec(
            num_scalar_prefetch=1, grid=(1,),
            in_specs=[pl.BlockSpec((N_DEV, D), lambda i, me_ref: (0, c_block)),
                      pl.BlockSpec((N_DEV, W), lambda i, me_ref: (0, me_ref[0]))],
            out_specs=pl.BlockSpec((D, W), lambda i, me_ref: (0, 0))),
        out_shape=jax.ShapeDtypeStruct((D, W), F32), compiler_params=_cparams(1),
    )(me, small_g, small_g)


def _fwd_in(x, mod, w_in_g, b_in, shards):
    S, D = x.shape
    nb, _, wb = w_in_g.shape
    N = nb * wb
    TM = TOKEN_TILE
    n_tiles = S // TM
    ns = len(shards)

    def body(*refs):
        x_ref, mod_ref, w_hbm, b_ref = refs[:4]
        ins = refs[4:4 + ns]
        z_ref, h_ref, zg_ref = refs[4 + ns:7 + ns]
        outs = refs[7 + ns:7 + 2 * ns]
        w_v, sems, send_sems, recv_sems, local_sems = refs[7 + 2 * ns:]
        i = pl.program_id(0)
        x_, y_, c_ = _mesh_pos()
        me = 4 * x_ + 2 * y_ + c_
        mine = [pltpu.make_async_copy(ins[a], outs[a].at[me], local_sems.at[a]) for a in range(ns)]
        sends, arrivals = _peer_copies(lambda a, p: ins[a], outs, send_sems, recv_sems)

        @pl.when(i == 0)
        def _():
            _load_cols(w_hbm, w_v, sems)
            for cp in sends + mine:
                cp.start()

        xhat, _ = _ln(x_ref[...])
        hb = (xhat * (1.0 + mod_ref[:, D:2 * D]) + mod_ref[:, 0:D]).astype(BF16)
        h_ref[...] = hb
        z_ref[...] = _dot(hb, w_v[...]) + b_ref[...]
        zg_ref[:, 0:D] = z_ref[:, 2 * D:3 * D].astype(BF16)
        zg_ref[:, D:3 * D] = z_ref[:, 5 * D:7 * D].astype(BF16)

        @pl.when(i == n_tiles - 1)
        def _():
            for cp in arrivals:
                cp.wait_recv()
            for cp in sends:
                cp.wait_send()
            for cp in mine:
                cp.wait()

    res = pl.pallas_call(
        body, name="fwd_in", grid=(n_tiles,),
        in_specs=[pl.BlockSpec((TM, D), lambda i: (i, 0)), _full(mod.shape), ANY, _full((1, N))] + [ANY] * ns,
        out_specs=[pl.BlockSpec((TM, N), lambda i: (i, 0)), pl.BlockSpec((TM, D), lambda i: (i, 0)),
                   pl.BlockSpec((TM, 3 * D), lambda i: (i, 0))] + [ANY] * ns,
        out_shape=[jax.ShapeDtypeStruct((S, N), F32), jax.ShapeDtypeStruct((S, D), BF16), jax.ShapeDtypeStruct((S, 3 * D), BF16)]
        + [jax.ShapeDtypeStruct((N_DEV,) + a.shape, a.dtype) for a in shards],
        scratch_shapes=[pltpu.VMEM((D, N), BF16), pltpu.SemaphoreType.DMA((nb,))] + _peer_sems(ns) + [pltpu.SemaphoreType.DMA((ns,))],
        compiler_params=_cparams(1),
    )(x, mod, w_in_g, b_in, *shards)
    return res[0], res[1], res[2], list(res[3:])


def _rows_from(win, o, R):
    if o % 8 == 0:
        return win[o:o + R, :]
    return pltpu.roll(win, win.shape[0] - o, axis=0)[0:R, :]


def _causal_taps(buf, w_ref, o_ref, K, halo, TM, D):
    R, C = CONV_ROWS, CONV_LANES
    base = halo - (K - 1)

    for r0 in range(0, TM, R):
        for c0 in range(0, D, C):
            win = buf[r0:r0 + R + halo, c0:c0 + C]
            acc = jnp.zeros((R, C), F32)
            for k in range(K):
                acc = acc + w_ref[k:k + 1, c0:c0 + C] * _rows_from(win, base + k, R)
            o_ref[r0:r0 + R, c0:c0 + C] = acc


def _fwd_mix(z, x, mod, wa, ba, lag, lab, w_a_out, b_a_out, wb, w_b_out, w_o, b_o):
    S, D = x.shape
    N = z.shape[1]
    TM = TOKEN_TILE
    KA, KB = wa.shape[0], wb.shape[0]

    def body(z_ref, x_ref, mod_ref, wa_ref, ba_ref, lag_ref, lab_ref, wao_hbm, bao_ref, wb_ref, wbo_hbm, wo_hbm, bo_ref,
             u1_ref, ya_ref, yb_ref, q_ref, out_ref, r1_ref, u2_ref, v_ref, mg_ref,
             ubuf, pbuf, qbuf, wao, wbo, wo, sems):
        i = pl.program_id(0)

        @pl.when(i == 0)
        def _():
            _load_whole([(wao_hbm, wao), (wbo_hbm, wbo), (wo_hbm, wo)], sems)
            ubuf[0:HALO_A, :] = jnp.zeros((HALO_A, D), F32)
            pbuf[0:HALO_B, :] = jnp.zeros((HALO_B, D), F32)

        ubuf[HALO_A:HALO_A + TM, :] = z_ref[:, 0:D] * _sigmoid(z_ref[:, D:2 * D])
        _causal_taps(ubuf, wa_ref, u1_ref, KA, HALO_A, TM, D)
        ubuf[0:HALO_A, :] = ubuf[TM:TM + HALO_A, :]
        u1 = u1_ref[...] + ba_ref[...]
        u1_ref[...] = u1
        xa, _ = _ln(u1)
        l = xa * lag_ref[...] + lab_ref[...]
        u2 = (l * _sigmoid(l)).astype(BF16)
        u2_ref[...] = u2
        ya = _dot(u2, wao[...]) + bao_ref[...]
        ya_ref[...] = ya.astype(BF16)
        pbuf[HALO_B:HALO_B + TM, :] = z_ref[:, 3 * D:4 * D] * z_ref[:, 4 * D:5 * D]
        _causal_taps(pbuf, wb_ref, qbuf, KB, HALO_B, TM, D)
        pbuf[0:HALO_B, :] = pbuf[TM:TM + HALO_B, :]
        q_ref[...] = qbuf[...].astype(BF16)
        v = (z_ref[:, 2 * D:3 * D] * qbuf[...]).astype(BF16)
        v_ref[...] = v
        yb = _dot(v, wbo[...])
        yb_ref[...] = yb.astype(BF16)
        mg = (_sigmoid(z_ref[:, 5 * D:6 * D]) * ya + _sigmoid(z_ref[:, 6 * D:7 * D]) * yb).astype(BF16)
        mg_ref[...] = mg
        out = _dot(mg, wo[...]) + bo_ref[...]
        out_ref[...] = out.astype(BF16)
        r1_ref[...] = ALPHA * x_ref[...] + (1.0 + mod_ref[:, 2 * D:3 * D]) * out

    tile = pl.BlockSpec((TM, D), lambda i: (i, 0))
    vec = _full((1, D))
    return pl.pallas_call(
        body, name="fwd_mix", grid=(S // TM,),
        in_specs=[pl.BlockSpec((TM, N), lambda i: (i, 0)), tile, _full(mod.shape), _full((KA, D)), vec, vec, vec, ANY, vec,
                  _full((KB, D)), ANY, ANY, vec],
        out_specs=[tile] * 9,
        out_shape=[jax.ShapeDtypeStruct((S, D), dt) for dt in (F32, BF16, BF16, BF16, BF16, F32, BF16, BF16, BF16)],
        scratch_shapes=[pltpu.VMEM((TM + HALO_A, D), F32), pltpu.VMEM((TM + HALO_B, D), F32), pltpu.VMEM((TM, D), F32),
                        pltpu.VMEM((D, D), BF16), pltpu.VMEM((D, D), BF16), pltpu.VMEM((D, D), BF16),
                        pltpu.SemaphoreType.DMA((3,))],
        compiler_params=_cparams(1),
    )(z, x, mod, wa, ba, lag, lab, w_a_out, b_a_out, wb, w_b_out, w_o, b_o)


def _ffn(r1, tgt, mod, ln1_g, ln1_b, w_up_g, b_up, w_down, b_down, ln2_g, ln2_b):
    S, D = r1.shape
    nb, _, wb = w_up_g.shape
    FF = nb * wb
    TM = TOKEN_TILE
    CF = FFN_CHUNK
    n_tiles = S // TM

    def body(r1_ref, t_ref, mod_ref, g1_ref, b1_ref, wup_hbm, bup_ref, wdn_hbm, bdn_ref, g2_ref, b2_ref,
             dx1_ref, h2_ref, f_ref, dhu_ref, do2_ref, acc_ref, accup_ref,
             wup, wdn, relu_buf, sems, sem2):
        i = pl.program_id(0)

        @pl.when(i == 0)
        def _():
            _load_cols(wup_hbm, wup, sems)
            _load_whole([(wdn_hbm, wdn)], sem2)
            acc_ref[...] = jnp.zeros(acc_ref.shape, F32)
            accup_ref[...] = jnp.zeros(accup_ref.shape, F32)

        scale2, shift2, gate2 = mod_ref[:, 4 * D:5 * D], mod_ref[:, 3 * D:4 * D], mod_ref[:, 5 * D:6 * D]
        xhat1, _ = _ln(r1_ref[...])
        x1 = xhat1 * g1_ref[...] + b1_ref[...]
        xh0, rstd0 = _ln(x1)
        h2 = (xh0 * (1.0 + scale2) + shift2).astype(BF16)
        h2_ref[...] = h2
        out2 = jnp.zeros((TM, D), F32) + bdn_ref[...]
        for c0 in range(0, FF, CF):
            hu = _dot(h2, wup[:, c0:c0 + CF]) + bup_ref[:, c0:c0 + CF]
            rl = jnp.maximum(hu, 0.0)
            relu_buf[:, c0:c0 + CF] = rl
            fb = (rl * rl).astype(BF16)
            f_ref[:, c0:c0 + CF] = fb
            out2 = out2 + _dot(fb, wdn[c0:c0 + CF, :])
        r2 = ALPHA * x1 + (1.0 + gate2) * out2
        xh2, rstd2 = _ln(r2)
        e = xh2 * g2_ref[...] + b2_ref[...] - t_ref[...]
        acc_ref[6:7, :] += _colsum(e * e)
        dy = e * (1.0 / D)
        acc_ref[1:2, :] += _colsum(dy * xh2)
        acc_ref[2:3, :] += _colsum(dy)
        dr2 = _ln_bwd(dy * g2_ref[...], xh2, rstd2)
        acc_ref[3:4, :] += _colsum(dr2 * out2)
        do2 = (1.0 + gate2) * dr2
        acc_ref[0:1, :] += _colsum(do2)
        do2b = do2.astype(BF16)
        do2_ref[...] = do2b
        dh2 = jnp.zeros((TM, D), F32)
        for c0 in range(0, FF, CF):
            dhu = _dot_nt(do2b, wdn[c0:c0 + CF, :]) * (2.0 * relu_buf[:, c0:c0 + CF])
            accup_ref[:, c0:c0 + CF] += _colsum(dhu)
            dhub = dhu.astype(BF16)
            dhu_ref[:, c0:c0 + CF] = dhub
            dh2 = dh2 + _dot_nt(dhub, wup[:, c0:c0 + CF])
        acc_ref[4:5, :] += _colsum(dh2 * xh0)
        acc_ref[5:6, :] += _colsum(dh2)
        dx1_ref[...] = (ALPHA * dr2 + _ln_bwd(dh2 * (1.0 + scale2), xh0, rstd0)).astype(BF16)

        @pl.when(i == n_tiles - 1)
        def _():
            tot = jnp.sum(acc_ref[6:7, :], axis=-1, keepdims=True) * (0.5 / D)
            acc_ref[6:7, :] = jnp.broadcast_to(tot, (1, D))

    tile = pl.BlockSpec((TM, D), lambda i: (i, 0))
    wide = pl.BlockSpec((TM, FF), lambda i: (i, 0))
    vec = _full((1, D))
    return pl.pallas_call(
        body, name="ffn", grid=(n_tiles,),
        in_specs=[tile, tile, _full(mod.shape), vec, vec, ANY, _full((1, FF)), ANY, vec, vec, vec],
        out_specs=[tile, tile, wide, wide, tile, _full((8, D)), _full((1, FF))],
        out_shape=[jax.ShapeDtypeStruct((S, D), BF16), jax.ShapeDtypeStruct((S, D), BF16), jax.ShapeDtypeStruct((S, FF), BF16),
                   jax.ShapeDtypeStruct((S, FF), BF16), jax.ShapeDtypeStruct((S, D), BF16),
                   jax.ShapeDtypeStruct((8, D), F32), jax.ShapeDtypeStruct((1, FF), F32)],
        scratch_shapes=[pltpu.VMEM((D, FF), BF16), pltpu.VMEM((FF, D), BF16), pltpu.VMEM((TM, FF), F32),
                        pltpu.SemaphoreType.DMA((nb,)), pltpu.SemaphoreType.DMA((1,))],
        compiler_params=_cparams(1),
    )(r1, tgt, mod, ln1_g, ln1_b, w_up_g, b_up, w_down, b_down, ln2_g, ln2_b)


def _dw(a, b, name, msplit=1, nsplit=1, wire=False, exchange=None):
    S, M = a.shape
    N = b.shape[1]
    TK = min(DW_TILE, S)
    mb, nbk = M // msplit, N // nsplit
    nk = S // TK
    srcs, kinds = exchange if exchange else ((), ())
    ns = len(srcs)

    def body(*refs):
        a_ref, b_ref = refs[:2]
        ins = refs[2:2 + ns]
        o_ref = refs[2 + ns]
        o16_ref = refs[3 + ns] if wire else None
        rest = refs[3 + ns + (1 if wire else 0):]
        k = pl.program_id(2)
        if ns:
            outs, (send_sems, recv_sems) = rest[:ns], rest[ns:]
            sends, arrivals = _peer_copies(lambda q, p: _shard_of(ins[q], kinds[q], p), outs, send_sems, recv_sems)
            first = (pl.program_id(0) == 0) & (pl.program_id(1) == 0) & (k == 0)
            last = (pl.program_id(0) == msplit - 1) & (pl.program_id(1) == nsplit - 1) & (k == nk - 1)

            @pl.when(first)
            def _():
                for cp in sends:
                    cp.start()

        @pl.when(k == 0)
        def _():
            o_ref[...] = jnp.zeros(o_ref.shape, F32)

        o_ref[...] += _dot_tn(a_ref[...], b_ref[...])

        if wire:
            @pl.when(k == nk - 1)
            def _():
                o16_ref[...] = o_ref[...].astype(BF16)

        if ns:
            @pl.when(last)
            def _():
                for cp in arrivals:
                    cp.wait_recv()
                for cp in sends:
                    cp.wait_send()

    oblk = pl.BlockSpec((mb, nbk), lambda i, j, k: (i, j))
    res = pl.pallas_call(
        body, name=name, grid=(msplit, nsplit, nk),
        in_specs=[pl.BlockSpec((TK, mb), lambda i, j, k: (k, i)), pl.BlockSpec((TK, nbk), lambda i, j, k: (k, j))] + [ANY] * ns,
        out_specs=[oblk] + ([oblk] if wire else []) + [ANY] * ns,
        out_shape=[jax.ShapeDtypeStruct((M, N), F32)] + ([jax.ShapeDtypeStruct((M, N), BF16)] if wire else [])
        + [jax.ShapeDtypeStruct((N_DEV,) + _shard_shape(g.shape, kd), g.dtype) for g, kd in zip(srcs, kinds)],
        scratch_shapes=_peer_sems(ns) if ns else [],
        compiler_params=_cparams(3),
    )(a, b, *srcs)
    return res if (wire or ns) else res[0]


def _bwd_mix1(dx1, r1, out, ya, yb, q, u1, zg, mod, ln1_g, lag, lab, w_o, w_a_out, w_b_out):
    S, D = dx1.shape
    TM = TOKEN_TILE

    def body(dx1_ref, r1_ref, out_ref, ya_ref, yb_ref, q_ref, u1_ref, zg_ref, mod_ref, g1_ref, lag_ref, lab_ref,
             wo_hbm, wao_hbm, wbo_hbm,
             dxp_ref, du1_ref, dq_ref, dzc_ref, dout_ref, dya_ref, dyb_ref, acc_ref,
             wo, wao, wbo, sems):
        @pl.when(pl.program_id(0) == 0)
        def _():
            _load_whole([(wo_hbm, wo), (wao_hbm, wao), (wbo_hbm, wbo)], sems)
            acc_ref[...] = jnp.zeros(acc_ref.shape, F32)

        dx1v = dx1_ref[...].astype(F32)
        xhat1, rstd1 = _ln(r1_ref[...])
        acc_ref[0:1, :] += _colsum(dx1v * xhat1)
        acc_ref[1:2, :] += _colsum(dx1v)
        dr1 = _ln_bwd(dx1v * g1_ref[...], xhat1, rstd1)
        dxp_ref[...] = (ALPHA * dr1).astype(BF16)
        acc_ref[2:3, :] += _colsum(dr1 * out_ref[...].astype(F32))
        dout = (1.0 + mod_ref[:, 2 * D:3 * D]) * dr1
        acc_ref[3:4, :] += _colsum(dout)
        doutb = dout.astype(BF16)
        dout_ref[...] = doutb
        dmg = _dot_nt(doutb, wo[...])
        sga, sgb = _sigmoid(zg_ref[:, D:2 * D].astype(F32)), _sigmoid(zg_ref[:, 2 * D:3 * D].astype(F32))
        dga = dmg * ya_ref[...].astype(F32) * sga * (1.0 - sga)
        dgb = dmg * yb_ref[...].astype(F32) * sgb * (1.0 - sgb)
        acc_ref[9:10, :] += _colsum(dga)
        acc_ref[10:11, :] += _colsum(dgb)
        dzc_ref[:, D:2 * D] = dga.astype(BF16)
        dzc_ref[:, 2 * D:3 * D] = dgb.astype(BF16)
        dya = dmg * sga
        acc_ref[4:5, :] += _colsum(dya)
        dyab = dya.astype(BF16)
        dya_ref[...] = dyab
        dybb = (dmg * sgb).astype(BF16)
        dyb_ref[...] = dybb
        du2 = _dot_nt(dyab, wao[...])
        xa, rstda = _ln(u1_ref[...])
        l = xa * lag_ref[...] + lab_ref[...]
        sl = _sigmoid(l)
        dl = du2 * (sl * (1.0 + l * (1.0 - sl)))
        acc_ref[5:6, :] += _colsum(dl * xa)
        acc_ref[6:7, :] += _colsum(dl)
        du1 = _ln_bwd(dl * lag_ref[...], xa, rstda)
        acc_ref[7:8, :] += _colsum(du1)
        du1_ref[...] = du1.astype(BF16)
        dv = _dot_nt(dybb, wbo[...])
        dgbk = dv * q_ref[...].astype(F32)
        acc_ref[8:9, :] += _colsum(dgbk)
        dzc_ref[:, 0:D] = dgbk.astype(BF16)
        dq_ref[...] = (dv * zg_ref[:, 0:D].astype(F32)).astype(BF16)

    tile = pl.BlockSpec((TM, D), lambda i: (i, 0))
    vec = _full((1, D))
    return pl.pallas_call(
        body, name="bwd_mix1", grid=(S // TM,),
        in_specs=[tile] * 7 + [pl.BlockSpec((TM, 3 * D), lambda i: (i, 0)), _full(mod.shape), vec, vec, vec, ANY, ANY, ANY],
        out_specs=[tile, tile, tile, pl.BlockSpec((TM, 3 * D), lambda i: (i, 0)), tile, tile, tile, _full((16, D))],
        out_shape=[jax.ShapeDtypeStruct((S, D), BF16)] * 3 + [jax.ShapeDtypeStruct((S, 3 * D), BF16)]
        + [jax.ShapeDtypeStruct((S, D), BF16)] * 3 + [jax.ShapeDtypeStruct((16, D), F32)],
        scratch_shapes=[pltpu.VMEM((D, D), BF16)] * 3 + [pltpu.SemaphoreType.DMA((3,))],
        compiler_params=_cparams(1),
    )(dx1, r1, out, ya, yb, q, u1, zg, mod, ln1_g, lag, lab, w_o, w_a_out, w_b_out)


def _anticausal_taps(dbuf, u_ref, w_ref, o_ref, dwacc, K, halo, TM, D):
    R, C = CONV_ROWS, CONV_LANES

    for r0 in range(0, TM, R):
        for c0 in range(0, D, C):
            win = dbuf[r0:r0 + R + halo, c0:c0 + C]
            uc = u_ref[r0:r0 + R, c0:c0 + C]
            acc = jnp.zeros((R, C), F32)
            for j in range(K):
                sh = _rows_from(win, j, R)
                k = K - 1 - j
                acc = acc + w_ref[k:k + 1, c0:c0 + C] * sh
                pr = uc * sh
                part = pr[0:8, :]
                for s in range(8, R, 8):
                    part = part + pr[s:s + 8, :]
                dwacc[k, :, c0:c0 + C] += part
            o_ref[r0:r0 + R, c0:c0 + C] = acc


def _bwd_mix2(du1, dq, z, dzc, x, dxp, mod, wa, wb, w_in_g):
    S, D = x.shape
    N = z.shape[1]
    nb, _, wbk = w_in_g.shape
    TM = TOKEN_TILE
    KA, KB = wa.shape[0], wb.shape[0]
    n_tiles = S // TM

    def body(du1_ref, dq_ref, zav_ref, zc_ref, zx_ref, dzc_ref, x_ref, dxp_ref, mod_ref, wa_ref, wb_ref, win_hbm,
             dz_ref, gx_ref, dwa_ref, dwb_ref, acc_ref, dbin_ref,
             dbuf, qbuf, ubuf, pbuf, obuf, dwa_acc, dwb_acc, w_v, sems):
        i = pl.program_id(0)

        @pl.when(i == 0)
        def _():
            _load_cols(win_hbm, w_v, sems)
            dbuf[TM:TM + HALO_A, :] = jnp.zeros((HALO_A, D), F32)
            qbuf[TM:TM + HALO_B, :] = jnp.zeros((HALO_B, D), F32)
            dwa_acc[...] = jnp.zeros(dwa_acc.shape, F32)
            dwb_acc[...] = jnp.zeros(dwb_acc.shape, F32)
            acc_ref[...] = jnp.zeros(acc_ref.shape, F32)
            dbin_ref[...] = jnp.zeros(dbin_ref.shape, F32)

        a_val = zav_ref[:, 0:D]
        sa = _sigmoid(zav_ref[:, D:2 * D])
        ubuf[...] = a_val * sa
        dbuf[0:TM, :] = du1_ref[...].astype(F32)
        _anticausal_taps(dbuf, ubuf, wa_ref, obuf, dwa_acc, KA, HALO_A, TM, D)
        dbuf[TM:TM + HALO_A, :] = dbuf[0:HALO_A, :]
        du0 = obuf[...]
        dav = du0 * sa
        dag = du0 * a_val * sa * (1.0 - sa)
        dbin_ref[:, 0:D] += _colsum(dav)
        dbin_ref[:, D:2 * D] += _colsum(dag)
        dz_ref[:, 0:D] = dav.astype(BF16)
        dz_ref[:, D:2 * D] = dag.astype(BF16)
        pbuf[...] = zc_ref[...] * zx_ref[...]
        qbuf[0:TM, :] = dq_ref[...].astype(F32)
        _anticausal_taps(qbuf, pbuf, wb_ref, obuf, dwb_acc, KB, HALO_B, TM, D)
        qbuf[TM:TM + HALO_B, :] = qbuf[0:HALO_B, :]
        dp = obuf[...]
        dgc = dp * zx_ref[...]
        dgx = dp * zc_ref[...]
        dbin_ref[:, 3 * D:4 * D] += _colsum(dgc)
        dbin_ref[:, 4 * D:5 * D] += _colsum(dgx)
        dz_ref[:, 3 * D:4 * D] = dgc.astype(BF16)
        dz_ref[:, 4 * D:5 * D] = dgx.astype(BF16)
        dz_ref[:, 2 * D:3 * D] = dzc_ref[:, 0:D]
        dz_ref[:, 5 * D:7 * D] = dzc_ref[:, D:3 * D]
        dh = _dot_nt(dz_ref[...], w_v[...])
        xhat, rstd = _ln(x_ref[...])
        acc_ref[0:1, :] += _colsum(dh * xhat)
        acc_ref[1:2, :] += _colsum(dh)
        gx_ref[...] = dxp_ref[...].astype(F32) + _ln_bwd(dh * (1.0 + mod_ref[:, D:2 * D]), xhat, rstd)

        @pl.when(i == n_tiles - 1)
        def _():
            for k in range(KA):
                dwa_ref[k:k + 1, :] = jnp.sum(dwa_acc[k], axis=0, keepdims=True)
            for k in range(KB):
                dwb_ref[k:k + 1, :] = jnp.sum(dwb_acc[k], axis=0, keepdims=True)

    rev = lambda i: n_tiles - 1 - i
    tile = pl.BlockSpec((TM, D), lambda i: (rev(i), 0))
    zcol = lambda k: pl.BlockSpec((TM, D), lambda i: (rev(i), k))
    return pl.pallas_call(
        body, name="bwd_mix2", grid=(n_tiles,),
        in_specs=[tile, tile, pl.BlockSpec((TM, 2 * D), lambda i: (rev(i), 0)), zcol(3), zcol(4),
                  pl.BlockSpec((TM, 3 * D), lambda i: (rev(i), 0)), tile, tile, _full(mod.shape), _full((KA, D)), _full((KB, D)), ANY],
        out_specs=[pl.BlockSpec((TM, N), lambda i: (rev(i), 0)), tile, _full((KA, D)), _full((KB, D)), _full((8, D)), _full((1, N))],
        out_shape=[jax.ShapeDtypeStruct((S, N), BF16), jax.ShapeDtypeStruct((S, D), F32), jax.ShapeDtypeStruct((KA, D), F32),
                   jax.ShapeDtypeStruct((KB, D), F32), jax.ShapeDtypeStruct((8, D), F32), jax.ShapeDtypeStruct((1, N), F32)],
        scratch_shapes=[pltpu.VMEM((TM + HALO_A, D), F32), pltpu.VMEM((TM + HALO_B, D), F32), pltpu.VMEM((TM, D), F32),
                        pltpu.VMEM((TM, D), F32), pltpu.VMEM((TM, D), F32), pltpu.VMEM((KA, 8, D), F32), pltpu.VMEM((KB, 8, D), F32),
                        pltpu.VMEM((D, N), BF16), pltpu.SemaphoreType.DMA((nb,))],
        compiler_params=_cparams(1),
    )(du1, dq, z, z, z, dzc, x, dxp, mod, wa, wb, w_in_g)


PEER_REDUCED = ("w_up", "w_down", "w_o", "w_a_out", "w_b_out", "conv_a_w", "conv_b_w")


def _local_grads(me, x, c, c_all, tgt, w_ada16, w_in_g, shards, vecs):
    D = x.shape[1]
    rows = lambda a: a.reshape(a.shape[0] * a.shape[1], a.shape[2])
    cols = lambda a: jnp.transpose(a, (1, 0, 2)).reshape(a.shape[1], a.shape[0] * a.shape[2])
    mod = _ada_mod(me, c_all, w_ada16, vecs["b_ada"])
    z, h1, zg, gathered = _fwd_in(x, mod, w_in_g, vecs["b_in"], shards)
    w_a_out, w_b_out, w_o, w_up_g, w_down = rows(gathered[0]), rows(gathered[1]), rows(gathered[2]), gathered[3], rows(gathered[4])
    wa, wb = cols(gathered[5]), cols(gathered[6])
    u1, ya, yb, q, out, r1, u2, v, mg = _fwd_mix(
        z, x, mod, wa, vecs["conv_a_b"], vecs["ln_a_g"], vecs["ln_a_b"], w_a_out, vecs["b_a_out"], wb, w_b_out, w_o, vecs["b_o"])
    dx1, h2, f, dhu, do2, acc_f, acc_up = _ffn(
        r1, tgt, mod, vecs["ln1_g"], vecs["ln1_b"], w_up_g, vecs["b_up"], w_down, vecs["b_down"], vecs["ln2_g"], vecs["ln2_b"])
    g_up, g_up16 = _dw(h2, dhu, "dw_up", nsplit=2, wire=True)
    g_down, g_down16 = _dw(f, do2, "dw_down", msplit=2, wire=True)
    dxp, du1, dq, dzc, doutb, dyab, dybb, acc_1 = _bwd_mix1(
        dx1, r1, out, ya, yb, q, u1, zg, mod, vecs["ln1_g"], vecs["ln_a_g"], vecs["ln_a_b"], w_o, w_a_out, w_b_out)
    g_o, g_o16 = _dw(mg, doutb, "dw_o", wire=True)
    g_a_out, g_a_out16 = _dw(u2, dyab, "dw_a_out", wire=True)
    g_b_out, g_b_out16 = _dw(v, dybb, "dw_b_out", wire=True)
    dz, gx, g_wa, g_wb, acc_2, db_in = _bwd_mix2(du1, dq, z, dzc, x, dxp, mod, wa, wb, w_in_g)
    full = dict(zip(PEER_REDUCED, (g_up, g_down, g_o, g_a_out, g_b_out, g_wa, g_wb)))
    res = _dw(h1, dz, "dw_in", nsplit=4,
              exchange=((g_up16, g_down16, g_o16, g_a_out16, g_b_out16, g_wa, g_wb), [SHARD_KIND[n] for n in PEER_REDUCED]))
    g_in = res[0]
    peers = {n: (full[n], r) for n, r in zip(PEER_REDUCED, res[1:])}
    row = lambda acc, k: (acc, k, 0, D)
    pieces = [
        row(acc_2, 1), row(acc_2, 0), row(acc_1, 2), row(acc_f, 5), row(acc_f, 4), row(acc_f, 3),
        (db_in, 0, 0, 2 * D), row(acc_1, 8), (db_in, 0, 3 * D, 2 * D), row(acc_1, 9), row(acc_1, 10),
        row(acc_1, 7), row(acc_1, 5), row(acc_1, 6), row(acc_1, 4), row(acc_1, 3), row(acc_1, 0), row(acc_1, 1),
        (acc_up, 0, 0, acc_up.shape[1]), row(acc_f, 0), row(acc_f, 1), row(acc_f, 2), (c, 0, 0, D),
        (acc_f, 6, 0, LOSS_LANES)]
    return acc_f[6, 0], gx, g_in, peers, pieces


WEIGHTS = ("w_ada", "b_ada", "w_in", "b_in", "conv_a_w", "conv_a_b", "ln_a_g", "ln_a_b", "w_a_out", "b_a_out", "conv_b_w",
           "w_b_out", "w_o", "b_o", "ln1_g", "ln1_b", "w_up", "b_up", "w_down", "b_down", "ln2_g", "ln2_b")
VECTORS = ("b_ada", "b_in", "conv_a_b", "ln_a_g", "ln_a_b", "b_a_out", "b_o", "ln1_g", "ln1_b", "b_up", "b_down", "ln2_g", "ln2_b")
SHARD_KIND = {"w_in": "col", "w_a_out": "row", "w_b_out": "row", "w_o": "row", "w_up": "col", "w_down": "row",
              "conv_a_w": "col", "conv_b_w": "col"}


def kernel(x, c, w_ada, b_ada, w_in, b_in, conv_a_w, conv_a_b, ln_a_g, ln_a_b, w_a_out, b_a_out, conv_b_w, w_b_out, w_o, b_o, ln1_g, ln1_b, w_up, b_up, w_down, b_down, ln2_g, ln2_b, loss_target, m_w_ada, m_b_ada, m_w_in, m_b_in, m_conv_a_w, m_conv_a_b, m_ln_a_g, m_ln_a_b, m_w_a_out, m_b_a_out, m_conv_b_w, m_w_b_out, m_w_o, m_b_o, m_ln1_g, m_ln1_b, m_w_up, m_b_up, m_w_down, m_b_down, m_ln2_g, m_ln2_b, v_w_ada, v_b_ada, v_w_in, v_b_in, v_conv_a_w, v_conv_a_b, v_ln_a_g, v_ln_a_b, v_w_a_out, v_b_a_out, v_conv_b_w, v_w_b_out, v_w_o, v_b_o, v_ln1_g, v_ln1_b, v_w_up, v_b_up, v_w_down, v_b_down, v_ln2_g, v_ln2_b):
    args = dict(locals())
    w = {n: args[n][0] for n in WEIGHTS}
    m = {n: args["m_" + n][0] for n in WEIGHTS}
    v = {n: args["v_" + n][0] for n in WEIGHTS}
    w = {n: (a[None, :] if a.ndim == 1 else a) for n, a in w.items()}
    m = {n: (a[None, :] if a.ndim == 1 else a) for n, a in m.items()}
    v = {n: (a[None, :] if a.ndim == 1 else a) for n, a in v.items()}
    xs, tgt = x[0], loss_target[0]
    S, D = xs.shape
    me = (4 * lax.axis_index("x") + 2 * lax.axis_index("y") + lax.axis_index("c")).astype(jnp.int32).reshape(1)
    core = lax.axis_index("c").astype(jnp.int32).reshape(1)
    chip = (2 * lax.axis_index("x") + lax.axis_index("y")).astype(jnp.int32).reshape(1)

    bf = lambda n: w[n].astype(BF16)
    w_in_g, c_all = _all_gather([bf("w_in"), c], "gather_weights")
    shards = [bf("w_a_out"), bf("w_b_out"), bf("w_o"), bf("w_up"), bf("w_down"), w["conv_a_w"], w["conv_b_w"]]
    vecs = {n: w[n] for n in VECTORS}

    loss, gx, g_in, peers, pieces = _local_grads(me, xs, c, c_all.reshape(N_DEV, D), tgt, bf("w_ada"), w_in_g, shards, vecs)

    (from_sibling,) = _exchange_cores([g_in], [SHARD_KIND["w_in"]], "reduce_cores")
    own_in, wire_in = _add_own(core, g_in, from_sibling, SHARD_KIND["w_in"], "add_w_in")
    sems, wire_thru, landing, after = _chips_start(wire_in)

    res = {}
    for n, (g_full, recv) in peers.items():
        res[n] = _sum_peers_adamw(me, g_full, recv, SHARD_KIND[n], w[n], m[n], v[n], "adamw_" + n, after)
        after = res[n][0]
    small = _pack_rows(pieces, "pack_vectors", after)
    (small_g,) = _all_gather([small], "gather_vectors")
    small_g = small_g.reshape(N_DEV, small.shape[1])
    W = w["w_ada"].shape[1]
    n_vec = small.shape[1] - D - LOSS_LANES
    g_ada = _ada_bwd(me, small_g, D, W, n_vec // D)
    res["w_ada"] = _sum_adamw(g_ada[None], w["w_ada"], m["w_ada"], v["w_ada"], "adamw_w_ada")
    by_kind, loss = _adamw_vectors(small_g, [w[n] for n in VECTORS], [m[n] for n in VECTORS], [v[n] for n in VECTORS], n_vec + D)
    for i, n in enumerate(VECTORS):
        res[n] = tuple(by_kind[t][i] for t in range(4))
    from_chips = _chips_wait(sems, wire_thru, landing, res["w_ada"][0])
    res["w_in"] = _sum_chips_adamw(chip, own_in, from_chips, w["w_in"], m["w_in"], v["w_in"], "adamw_w_in")

    outs = [loss, gx[None]]
    for t in range(4):
        outs += [res[n][t].reshape(args[n].shape) for n in WEIGHTS]
    return tuple(outs)
```
